```python
import math
import jax
import jax.numpy as jnp
from jax import lax
import numpy as np

D_MODEL = 1024
BATCH = 16
SEQ = 2048
DEPTH = 1
DEC_BATCH = 16
DEC_SEQ = 4096
PAST_LEN = 128

N_META = 16
HEAD_DIM = 64
BLOCK = 128
WINDOW = 128
A_Q_HEADS = 8
A_KV_HEADS = 2
A_REP = A_Q_HEADS // A_KV_HEADS
B_HEADS = 4
B_VDIM = 2 * HEAD_DIM
A_WIDTH = A_Q_HEADS * HEAD_DIM
B_WIDTH = B_HEADS * B_VDIM
MIX_WIDTH = A_WIDTH + B_WIDTH
A_KV_COLS = A_KV_HEADS * HEAD_DIM
B_QK_COLS = B_HEADS * 2 * HEAD_DIM
IN_COLS = A_WIDTH + 2 * A_KV_COLS + 2 * B_QK_COLS + B_WIDTH
N_ATT_HEADS = A_Q_HEADS + B_HEADS
N_BUCKETS = 32
MAX_DISTANCE = 128
N_EXPERTS = 256
TOP_K = 8
N_GROUPS = 8
TOPK_GROUPS = 4
D_EXPERT = 256
D_SHARED = 256
ROUTED_SCALE = 2.5
MOE_BLOCK = 128
LN_EPS = 1e-5
ALPHA = (2 * DEPTH) ** 0.25
BETA = (8 * DEPTH) ** -0.25
NEG = -1e30

kernel_name = 'hymba_swa_diffattn_moe_encoder'


def _lambda_init(layer):
    return 0.8 - 0.6 * math.exp(-0.3 * layer)


def _layernorm(x, g, b):
    xf = x.astype(jnp.float32)
    mu = jnp.mean(xf, axis=-1, keepdims=True)
    var = jnp.mean(jnp.square(xf - mu), axis=-1, keepdims=True)
    y = (xf - mu) * lax.rsqrt(var + LN_EPS) * g.astype(jnp.float32) + b.astype(jnp.float32)
    return y.astype(x.dtype)


def _rel_bucket(rel):
    nb = N_BUCKETS // 2
    max_exact = nb // 2
    ret = jnp.where(rel > 0, nb, 0)
    n = jnp.abs(rel)
    nf = jnp.maximum(n, 1).astype(jnp.float32)
    large = max_exact + (jnp.log(nf / max_exact) / math.log(MAX_DISTANCE / max_exact) * (nb - max_exact)).astype(jnp.int32)
    large = jnp.minimum(large, nb - 1)
    return ret + jnp.where(n < max_exact, n, large)


def _sink_window_attn(q, k, v, qpos, kpos, valid, bias_tab, sink):
    N, Q, K = valid.shape
    G, R = q.shape[3], q.shape[4]
    s = jnp.einsum('bnqgrd,bnkgd->bngrqk', q, k).astype(jnp.float32) * (HEAD_DIM ** -0.5)
    bias = bias_tab[_rel_bucket(kpos[:, None, :] - qpos[:, :, None])].astype(jnp.float32)
    bias = jnp.transpose(bias.reshape(N, Q, K, G, R), (0, 3, 4, 1, 2))
    s = jnp.where(valid[None, :, None, None], s + bias[None], NEG)
    sk = sink.astype(jnp.float32).reshape(1, 1, G, R, 1, 1)
    m = jnp.maximum(jnp.max(s, axis=-1, keepdims=True), sk)
    p = jnp.exp(s - m)
    a = p / (jnp.sum(p, axis=-1, keepdims=True) + jnp.exp(sk - m))
    return jnp.einsum('bngrqk,bnkgd->bnqgrd', a.astype(v.dtype), v)


def _mixer_a(q, k, v, bias_tab, sink):
    B, L, G, R, d = q.shape
    S = L - N_META
    nb = S // BLOCK
    sink = sink.reshape(G, R)
    qm, qr = q[:, :N_META], q[:, N_META:]
    km, kr = k[:, :N_META], k[:, N_META:]
    vm, vr = v[:, :N_META], v[:, N_META:]

    def band(t):
        tp = jnp.pad(t, ((0, 0), (BLOCK, BLOCK), (0, 0), (0, 0))).reshape(B, nb + 2, BLOCK, G, d)
        return jnp.concatenate([tp[:, :-2], tp[:, 1:-1], tp[:, 2:]], axis=2)

    def with_meta(tm, tb):
        return jnp.concatenate([jnp.broadcast_to(tm[:, None], (B, nb, N_META, G, d)), tb], axis=2)

    kb = with_meta(km, band(kr))
    vb = with_meta(vm, band(vr))
    ridx = jnp.arange(S, dtype=jnp.int32).reshape(nb, BLOCK)
    kidx = (jnp.arange(nb, dtype=jnp.int32)[:, None] - 1) * BLOCK + jnp.arange(3 * BLOCK, dtype=jnp.int32)[None]
    qpos = N_META + ridx
    kpos = jnp.concatenate([jnp.broadcast_to(jnp.arange(N_META, dtype=jnp.int32)[None], (nb, N_META)), N_META + kidx], axis=1)
    kk = kidx[:, None, :]
    band_ok = (jnp.abs(kk - ridx[:, :, None]) <= WINDOW) & (kk >= 0) & (kk < S)
    valid = jnp.concatenate([jnp.ones((nb, BLOCK, N_META), bool), band_ok], axis=2)
    o_real = _sink_window_attn(qr.reshape(B, nb, BLOCK, G, R, d), kb, vb, qpos, kpos, valid, bias_tab, sink)
    o_real = o_real.reshape(B, S, G * R * d)
    km2 = jnp.concatenate([km, kr[:, :BLOCK]], axis=1)[:, None]
    vm2 = jnp.concatenate([vm, vr[:, :BLOCK]], axis=1)[:, None]
    qpos_m = jnp.arange(N_META, dtype=jnp.int32)[None]
    kpos_m = jnp.arange(N_META + BLOCK, dtype=jnp.int32)[None]
    valid_m = jnp.abs(kpos_m[:, None, :] - qpos_m[:, :, None]) <= WINDOW
    o_meta = _sink_window_attn(qm[:, None], km2, vm2, qpos_m, kpos_m, valid_m, bias_tab, sink)
    o_meta = o_meta.reshape(B, N_META, G * R * d)
    return jnp.concatenate([o_meta, o_real], axis=1)


def _diff_block(q, k, v, qpos, kpos, bias_tab, lam):
    s = jnp.einsum('bqhcd,bkhcd->bhcqk', q, k).astype(jnp.float32) * (HEAD_DIM ** -0.5)
    bias = bias_tab[_rel_bucket(kpos[None, :] - qpos[:, None])].astype(jnp.float32)
    s = s + jnp.transpose(bias, (2, 0, 1))[None, :, None]
    p = jax.nn.softmax(s, axis=-1)
    a = p[:, :, 0] - lam * p[:, :, 1]
    return jnp.einsum('bhqk,bkhe->bqhe', a.astype(v.dtype), v)


def _mixer_b(q, k, v, bias_tab, lq1, lk1, lq2, lk2, subln_g, lam_init):
    B, L = q.shape[:2]
    S = L - N_META
    nb = S // BLOCK
    f32 = jnp.float32
    lam = (jnp.exp(jnp.sum(lq1.astype(f32) * lk1.astype(f32))) - jnp.exp(jnp.sum(lq2.astype(f32) * lk2.astype(f32))) + lam_init)
    kpos = jnp.arange(L, dtype=jnp.int32)
    o_meta = _diff_block(q[:, :N_META], k, v, jnp.arange(N_META, dtype=jnp.int32), kpos, bias_tab, lam)
    qr = jnp.transpose(q[:, N_META:].reshape(B, nb, BLOCK, B_HEADS, 2, HEAD_DIM), (1, 0, 2, 3, 4, 5))
    qpos = (N_META + jnp.arange(S, dtype=jnp.int32)).reshape(nb, BLOCK)
    o_real = lax.map(lambda a: _diff_block(a[0], k, v, a[1], kpos, bias_tab, lam), (qr, qpos))
    o_real = jnp.transpose(o_real, (1, 0, 2, 3, 4)).reshape(B, S, B_HEADS, B_VDIM)
    o = jnp.concatenate([o_meta, o_real], axis=1).astype(f32)
    o = o * lax.rsqrt(jnp.mean(jnp.square(o), axis=-1, keepdims=True) + LN_EPS) * subln_g.astype(f32) * (1.0 - lam_init)
    return o.astype(q.dtype).reshape(B, L, B_WIDTH)


def _route(x, w_router, router_bias):
    T = x.shape[0]
    f32 = jnp.float32
    scores = jax.nn.sigmoid(x.astype(f32) @ w_router.astype(f32))
    biased = scores + router_bias.astype(f32)
    grp = biased.reshape(T, N_GROUPS, N_EXPERTS // N_GROUPS)
    gscore = jnp.sum(lax.top_k(grp, 2)[0], axis=-1)
    _, gidx = lax.top_k(gscore, TOPK_GROUPS)
    gmask = jnp.sum(jax.nn.one_hot(gidx, N_GROUPS, dtype=f32), axis=1) > 0
    emask = jnp.repeat(gmask, N_EXPERTS // N_GROUPS, axis=1)
    _, eidx = lax.top_k(jnp.where(emask, biased, NEG), TOP_K)
    w = jnp.take_along_axis(scores, eidx, axis=1)
    w = w / jnp.sum(w, axis=-1, keepdims=True) * ROUTED_SCALE
    return eidx, w


def _routed_experts(x, eidx, w, wg, wu, wd):
    T, D = x.shape
    A = T * TOP_K
    nblk = -(-A // MOE_BLOCK) + N_EXPERTS
    e_flat = eidx.reshape(A).astype(jnp.int32)
    t_flat = jnp.repeat(jnp.arange(T, dtype=jnp.int32), TOP_K)
    w_flat = w.reshape(A)
    order = jnp.argsort(e_flat)
    e_s = e_flat[order]
    counts = jnp.bincount(e_flat, length=N_EXPERTS).astype(jnp.int32)
    starts = jnp.cumsum(counts) - counts
    padded = (counts + MOE_BLOCK - 1) // MOE_BLOCK * MOE_BLOCK
    pend = jnp.cumsum(padded)
    pstart = pend - padded
    dest = pstart[e_s] + jnp.arange(A, dtype=jnp.int32) - starts[e_s]
    rows_tok = jnp.full((nblk * MOE_BLOCK,), T, jnp.int32).at[dest].set(t_flat[order])
    rows_w = jnp.zeros((nblk * MOE_BLOCK,), jnp.float32).at[dest].set(w_flat[order])
    blk_e = jnp.minimum(jnp.searchsorted(pend, jnp.arange(nblk, dtype=jnp.int32) * MOE_BLOCK, side='right'), N_EXPERTS - 1)
    x_pad = jnp.concatenate([x, jnp.zeros((1, D), x.dtype)], axis=0)

    def step(acc, inp):
        rt, rw, e = inp
        xb = x_pad[rt]
        hid = jax.nn.silu(xb @ wg[e]) * (xb @ wu[e])
        y = (hid @ wd[e]).astype(jnp.float32) * rw[:, None]
        return acc.at[rt].add(y), None

    acc, _ = lax.scan(step, jnp.zeros((T + 1, D), jnp.float32),
                      (rows_tok.reshape(nblk, MOE_BLOCK), rows_w.reshape(nblk, MOE_BLOCK), blk_e))
    return acc[:T]


def _trunk(x, meta_tokens, ln_emb_g, ln_emb_b, rel_bias, w_in, attn_sink, lambda_q1, lambda_k1, lambda_q2, lambda_k2,
           subln_g, w_out, ln1_g, ln1_b, w_router, router_bias, w_gate, w_up, w_down, ws_gate, ws_up, ws_down, ln2_g, ln2_b):
    B, S, D = x.shape
    L = S + N_META
    h = jnp.concatenate([jnp.broadcast_to(meta_tokens.astype(x.dtype)[None], (B, N_META, D)), x], axis=1)
    h = _layernorm(h, ln_emb_g, ln_emb_b)
    bias_a = rel_bias[:, :A_Q_HEADS]
    bias_b = rel_bias[:, A_Q_HEADS:]
    cuts = np.cumsum([A_WIDTH, A_KV_COLS, A_KV_COLS, B_QK_COLS, B_QK_COLS]).tolist()
    for l in range(DEPTH):
        proj = h @ w_in[l]
        qa, ka, va, qb, kb, vb = jnp.split(proj, cuts, axis=-1)
        oa = _mixer_a(qa.reshape(B, L, A_KV_HEADS, A_REP, HEAD_DIM),
                      ka.reshape(B, L, A_KV_HEADS, HEAD_DIM), va.reshape(B, L, A_KV_HEADS, HEAD_DIM),
                      bias_a, attn_sink[l])
        ob = _mixer_b(qb.reshape(B, L, B_HEADS, 2, HEAD_DIM), kb.reshape(B, L, B_HEADS, 2, HEAD_DIM),
                      vb.reshape(B, L, B_HEADS, B_VDIM), bias_b, lambda_q1[l], lambda_k1[l], lambda_q2[l], lambda_k2[l],
                      subln_g[l], _lambda_init(l))
        mix = jnp.concatenate([oa, ob], axis=-1) @ w_out[l]
        h = _layernorm(ALPHA * h + mix, ln1_g[l], ln1_b[l])
        flat = h.reshape(B * L, D)
        eidx, gw = _route(flat, w_router[l], router_bias[l])
        routed = _routed_experts(flat, eidx, gw, w_gate[l], w_up[l], w_down[l])
        shared = (jax.nn.silu(flat @ ws_gate[l]) * (flat @ ws_up[l])) @ ws_down[l]
        ffn = (routed + shared.astype(jnp.float32)).astype(h.dtype).reshape(B, L, D)
        h = _layernorm(ALPHA * h + ffn, ln2_g[l], ln2_b[l])
    return h[:, N_META:]


def setup_inputs(seed: int = 0) -> dict:
    key = jax.random.key(seed)
    ks = jax.random.split(key, 26)
    f32 = jnp.float32

    def nrm(k, shape, s):
        return jax.random.normal(k, shape, f32) * s

    D = D_MODEL
    return {
        'x_prompt': nrm(ks[0], (BATCH, SEQ, D), 1.0),
        'x_sample': nrm(ks[1], (DEC_BATCH, DEC_SEQ, D), 1.0),
        'meta_tokens': nrm(ks[2], (N_META, D), 1.0),
        'ln_emb_g': 1.0 + nrm(ks[3], (D,), 0.02),
        'ln_emb_b': nrm(ks[4], (D,), 0.02),
        'rel_bias': nrm(ks[5], (N_BUCKETS, N_ATT_HEADS), 0.2),
        'w_in': nrm(ks[6], (DEPTH, D, IN_COLS), D ** -0.5),
        'attn_sink': nrm(ks[7], (DEPTH, A_Q_HEADS), 0.5),
        'lambda_q1': nrm(ks[8], (DEPTH, HEAD_DIM), 0.1),
        'lambda_k1': nrm(ks[9], (DEPTH, HEAD_DIM), 0.1),
        'lambda_q2': nrm(ks[10], (DEPTH, HEAD_DIM), 0.1),
        'lambda_k2': nrm(ks[11], (DEPTH, HEAD_DIM), 0.1),
        'subln_g': 1.0 + nrm(ks[12], (DEPTH, B_VDIM), 0.02),
        'w_out': nrm(ks[13], (DEPTH, MIX_WIDTH, D), MIX_WIDTH ** -0.5 * BETA),
        'ln1_g': 1.0 + nrm(ks[14], (DEPTH, D), 0.02),
        'ln1_b': nrm(ks[15], (DEPTH, D), 0.02),
        'w_router': nrm(ks[16], (DEPTH, D, N_EXPERTS), D ** -0.5),
        'router_bias': nrm(ks[17], (DEPTH, N_EXPERTS), 0.01),
        'w_gate': nrm(ks[18], (DEPTH, N_EXPERTS, D, D_EXPERT), D ** -0.5),
        'w_up': nrm(ks[19], (DEPTH, N_EXPERTS, D, D_EXPERT), D ** -0.5),
        'w_down': nrm(ks[20], (DEPTH, N_EXPERTS, D_EXPERT, D), D_EXPERT ** -0.5 * BETA),
        'ws_gate': nrm(ks[21], (DEPTH, D, D_SHARED), D ** -0.5),
        'ws_up': nrm(ks[22], (DEPTH, D, D_SHARED), D ** -0.5),
        'ws_down': nrm(ks[23], (DEPTH, D_SHARED, D), D_SHARED ** -0.5 * BETA),
        'ln2_g': 1.0 + nrm(ks[24], (DEPTH, D), 0.02),
        'ln2_b': nrm(ks[25], (DEPTH, D), 0.02),
    }


def reference(x_prompt, x_sample, meta_tokens, ln_emb_g, ln_emb_b, rel_bias, w_in, attn_sink, lambda_q1, lambda_k1,
              lambda_q2, lambda_k2, subln_g, w_out, ln1_g, ln1_b, w_router, router_bias, w_gate, w_up, w_down,
              ws_gate, ws_up, ws_down, ln2_g, ln2_b):
    params = (meta_tokens, ln_emb_g, ln_emb_b, rel_bias, w_in, attn_sink, lambda_q1, lambda_k1, lambda_q2, lambda_k2,
              subln_g, w_out, ln1_g, ln1_b, w_router, router_bias, w_gate, w_up, w_down, ws_gate, ws_up, ws_down,
              ln2_g, ln2_b)
    y_prompt = _trunk(x_prompt, *params)
    y_sample = _trunk(x_sample, *params)
    return (y_prompt, y_sample)
```

```python
import functools
import math

import jax
import jax.numpy as jnp
from jax import lax
from jax.experimental import pallas as pl
from jax.experimental.pallas import tpu as pltpu

N_META = 16
HEAD_DIM = 64
WINDOW = 128
A_Q_HEADS = 8
A_KV_HEADS = 2
B_HEADS = 4
N_BUCKETS = 32
MAX_DISTANCE = 128
TOP_K = 8
N_GROUPS = 8
TOPK_GROUPS = 4
ROUTED_SCALE = 2.5
LN_EPS = 1e-5
DEPTH = 1
ALPHA = (2 * DEPTH) ** 0.25
NEG = -1e30
LAMBDA_INIT = 0.8 - 0.6 * math.exp(-0.3 * 0)

LANES = 128
VMEM_LIMIT = 48 * 1024 * 1024

QA_BLK = 0
KA_BLK = 4
VA_BLK = 6
QB_BLK = 8
KB_BLK = 12
VB_BLK = 16
PROJ_COLS = 20 * LANES

TQ_A = 128
TQ_B = 512
TK_B = 512
TM_MOE = 256


def _cparams(sem):
    return pltpu.CompilerParams(dimension_semantics=sem, vmem_limit_bytes=VMEM_LIMIT)


def _layernorm_f32(x, g, b):
    mu = jnp.mean(x, axis=-1, keepdims=True)
    xc = x - mu
    var = jnp.mean(xc * xc, axis=-1, keepdims=True)
    return xc * lax.rsqrt(var + LN_EPS) * g + b


def _dot_nt(a, b):
    return lax.dot_general(a, b, (((1,), (1,)), ((), ())), preferred_element_type=jnp.float32)


def _dot(a, b):
    return jnp.dot(a, b, preferred_element_type=jnp.float32)


def _ln_inproj_kernel(x_ref, g_ref, b_ref, w_ref, o_ref):
    h = _layernorm_f32(x_ref[...], g_ref[...], b_ref[...]).astype(jnp.bfloat16)
    o_ref[...] = _dot(h, w_ref[...]).astype(o_ref.dtype)


def _ln_inproj(x2d, g, b, w):
    t, d = x2d.shape
    n = w.shape[1]
    tm = min(512, t)
    return pl.pallas_call(
        _ln_inproj_kernel,
        out_shape=jax.ShapeDtypeStruct((t, n), jnp.bfloat16),
        grid=(t // tm,),
        in_specs=[
            pl.BlockSpec((tm, d), lambda i: (i, 0)),
            pl.BlockSpec((1, d), lambda i: (0, 0)),
            pl.BlockSpec((1, d), lambda i: (0, 0)),
            pl.BlockSpec((d, n), lambda i: (0, 0)),
        ],
        out_specs=pl.BlockSpec((tm, n), lambda i: (i, 0)),
        compiler_params=_cparams(("parallel",)),
        name="ln_inproj",
    )(x2d, g, b, w)


def _mixer_a_kernel(sink_ref, q_ref, k_ref, v_ref, km_ref, vm_ref, tab_ref, tabm_ref, o_ref, *, nblk, sub):
    i = pl.program_id(1)
    lane = lax.broadcasted_iota(jnp.int32, (1, LANES), 1)
    lo = lane < HEAD_DIM
    for j in range(sub):
        gi = i * sub + j
        sp = pl.multiple_of(jnp.maximum(gi - 1, 0) * TQ_A, TQ_A)
        sc = pl.multiple_of(gi * TQ_A, TQ_A)
        sn = pl.multiple_of(jnp.minimum(gi + 1, nblk - 1) * TQ_A, TQ_A)
        neg_p = jnp.where(gi > 0, 0.0, NEG).astype(jnp.float32)
        neg_n = jnp.where(gi < nblk - 1, 0.0, NEG).astype(jnp.float32)
        for hp in range(A_Q_HEADS // 2):
            g = hp // 2
            gs = slice(g * LANES, (g + 1) * LANES)
            qc = q_ref[0, j * TQ_A:(j + 1) * TQ_A, hp * LANES:(hp + 1) * LANES]
            kpieces = (km_ref[:, gs], k_ref[0, pl.ds(sp, TQ_A), gs], k_ref[0, pl.ds(sc, TQ_A), gs],
                       k_ref[0, pl.ds(sn, TQ_A), gs])
            vpieces = (vm_ref[:, gs], v_ref[0, pl.ds(sp, TQ_A), gs], v_ref[0, pl.ds(sc, TQ_A), gs],
                       v_ref[0, pl.ds(sn, TQ_A), gs])
            o_pair = jnp.zeros((TQ_A, LANES), jnp.float32)
            for half in range(2):
                h = 2 * hp + half
                msk = lo if half == 0 else jnp.logical_not(lo)
                qm = jnp.where(msk, qc, jnp.zeros_like(qc))
                sink = sink_ref[h]
                s = [
                    _dot_nt(qm, kpieces[0]) + tabm_ref[0, h, j * TQ_A:(j + 1) * TQ_A, :],
                    _dot_nt(qm, kpieces[1]) + (tab_ref[h, :, 0:TQ_A] + neg_p),
                    _dot_nt(qm, kpieces[2]) + tab_ref[h, :, TQ_A:2 * TQ_A],
                    _dot_nt(qm, kpieces[3]) + (tab_ref[h, :, 2 * TQ_A:3 * TQ_A] + neg_n),
                ]
                m = jnp.maximum(jnp.maximum(jnp.max(s[0], axis=-1, keepdims=True), jnp.max(s[1], axis=-1, keepdims=True)),
                                jnp.maximum(jnp.max(s[2], axis=-1, keepdims=True), jnp.max(s[3], axis=-1, keepdims=True)))
                m = jnp.maximum(m, sink)
                l = jnp.exp(sink - m)
                o_h = jnp.zeros((TQ_A, LANES), jnp.float32)
                for sx, vx in zip(s, vpieces):
                    p = jnp.exp(sx - m)
                    l = l + jnp.sum(p, axis=-1, keepdims=True)
                    vmk = jnp.where(msk, vx, jnp.zeros_like(vx))
                    o_h = o_h + _dot(p.astype(jnp.bfloat16), vmk)
                o_pair = o_pair + o_h / l
            o_ref[0, j * TQ_A:(j + 1) * TQ_A, hp * LANES:(hp + 1) * LANES] = o_pair.astype(o_ref.dtype)


def _mixer_a(proj3, proj_meta, tab, tabm, sink):
    bsz, s, _ = proj3.shape
    nblk = s // TQ_A
    sub = 4 if nblk % 4 == 0 else 1
    nq = nblk // sub
    tq = sub * TQ_A
    kern = functools.partial(_mixer_a_kernel, nblk=nblk, sub=sub)
    return pl.pallas_call(
        kern,
        out_shape=jax.ShapeDtypeStruct((bsz, s, A_Q_HEADS * HEAD_DIM), jnp.bfloat16),
        grid=(bsz, nq),
        in_specs=[
            pl.BlockSpec(memory_space=pltpu.SMEM),
            pl.BlockSpec((1, tq, 4 * LANES), lambda b, i: (b, i, QA_BLK // 4)),
            pl.BlockSpec((1, s, 2 * LANES), lambda b, i: (b, 0, KA_BLK // 2)),
            pl.BlockSpec((1, s, 2 * LANES), lambda b, i: (b, 0, VA_BLK // 2)),
            pl.BlockSpec((N_META, 2 * LANES), lambda b, i: (0, KA_BLK // 2)),
            pl.BlockSpec((N_META, 2 * LANES), lambda b, i: (0, VA_BLK // 2)),
            pl.BlockSpec((A_Q_HEADS, TQ_A, 3 * TQ_A), lambda b, i: (0, 0, 0)),
            pl.BlockSpec((1, A_Q_HEADS, tq, N_META), lambda b, i: (jnp.minimum(i, 1), 0, 0, 0)),
        ],
        out_specs=pl.BlockSpec((1, tq, 4 * LANES), lambda b, i: (b, i, 0)),
        compiler_params=_cparams(("parallel", "arbitrary")),
        name="mixer_a",
    )(sink, proj3, proj3, proj3, proj_meta, proj_meta, tab, tabm)


def _mixer_b_kernel(sc_ref, q_ref, k_ref, v_ref, km_ref, vm_ref, tab_ref, tabm_ref, g_ref, o_ref,
                    m_ref, l_ref, acc_ref, *, nk):
    h = pl.program_id(1)
    i = pl.program_id(2)
    lane = lax.broadcasted_iota(jnp.int32, (1, LANES), 1)
    lo = lane < HEAD_DIM
    q = q_ref[0]
    qs = (jnp.where(lo, q, jnp.zeros_like(q)), jnp.where(lo, jnp.zeros_like(q), q))
    c_left = sc_ref[h]
    c_right = sc_ref[B_HEADS + h]
    lam = sc_ref[2 * B_HEADS]

    km = km_ref[...]
    vm = vm_ref[...]
    for c in range(2):
        s = _dot_nt(qs[c], km) + tabm_ref[0]
        m = jnp.max(s, axis=-1, keepdims=True)
        p = jnp.exp(s - m)
        m_ref[c] = m
        l_ref[c] = jnp.sum(p, axis=-1, keepdims=True)
        acc_ref[c] = _dot(p.astype(jnp.bfloat16), vm)

    def update(j, bias, const):
        start = pl.multiple_of(j * TK_B, TK_B)
        kj = k_ref[0, pl.ds(start, TK_B), :]
        vj = v_ref[0, pl.ds(start, TK_B), :]
        for c in range(2):
            s = _dot_nt(qs[c], kj)
            if bias is not None:
                s = s + bias
            m_prev = m_ref[c]
            m_cur = jnp.max(s, axis=-1, keepdims=True)
            if const is not None:
                m_cur = m_cur + const
            m_new = jnp.maximum(m_prev, m_cur)
            shift = m_new if const is None else m_new - const
            a = jnp.exp(m_prev - m_new)
            p = jnp.exp(s - shift)
            m_ref[c] = m_new
            l_ref[c] = a * l_ref[c] + jnp.sum(p, axis=-1, keepdims=True)
            acc_ref[c] = a * acc_ref[c] + _dot(p.astype(jnp.bfloat16), vj)

    def far_left(j, carry):
        update(j, None, c_left)
        return carry

    def far_right(j, carry):
        update(j, None, c_right)
        return carry

    lax.fori_loop(0, jnp.maximum(i - 1, 0), far_left, 0)
    for d in range(3):
        j = i + d - 1

        @pl.when(jnp.logical_and(j >= 0, j < nk))
        def _():
            update(j, tab_ref[0, d], None)

    lax.fori_loop(jnp.minimum(i + 2, nk), nk, far_right, 0)

    o0 = acc_ref[0] / l_ref[0]
    o1 = acc_ref[1] / l_ref[1]
    o = o0 - lam * o1
    ms = jnp.mean(o * o, axis=-1, keepdims=True)
    o = o * lax.rsqrt(ms + LN_EPS) * g_ref[...] * (1.0 - LAMBDA_INIT)
    o_ref[0] = o.astype(o_ref.dtype)


def _mixer_b(proj3, proj_meta, tab, tabm, scal, subln_g):
    bsz, s, _ = proj3.shape
    nq = s // TQ_B
    nk = s // TK_B
    kern = functools.partial(_mixer_b_kernel, nk=nk)
    return pl.pallas_call(
        kern,
        out_shape=jax.ShapeDtypeStruct((bsz, s, B_HEADS * LANES), jnp.bfloat16),
        grid=(bsz, B_HEADS, nq),
        in_specs=[
            pl.BlockSpec(memory_space=pltpu.SMEM),
            pl.BlockSpec((1, TQ_B, LANES), lambda b, h, i: (b, i, QB_BLK + h)),
            pl.BlockSpec((1, s, LANES), lambda b, h, i: (b, 0, KB_BLK + h)),
            pl.BlockSpec((1, s, LANES), lambda b, h, i: (b, 0, VB_BLK + h)),
            pl.BlockSpec((N_META, LANES), lambda b, h, i: (0, KB_BLK + h)),
            pl.BlockSpec((N_META, LANES), lambda b, h, i: (0, VB_BLK + h)),
            pl.BlockSpec((1, 3, TQ_B, TK_B), lambda b, h, i: (h, 0, 0, 0)),
            pl.BlockSpec((1, TQ_B, N_META), lambda b, h, i: (h, i, 0)),
            pl.BlockSpec((1, LANES), lambda b, h, i: (0, 0)),
        ],
        out_specs=pl.BlockSpec((1, TQ_B, LANES), lambda b, h, i: (b, i, h)),
        scratch_shapes=[
            pltpu.VMEM((2, TQ_B, 1), jnp.float32),
            pltpu.VMEM((2, TQ_B, 1), jnp.float32),
            pltpu.VMEM((2, TQ_B, LANES), jnp.float32),
        ],
        compiler_params=_cparams(("parallel", "parallel", "arbitrary")),
        name="mixer_b",
    )(scal, proj3, proj3, proj3, proj_meta, proj_meta, tab, tabm, subln_g)


def _outproj_kernel(x_ref, oa_ref, ob_ref, w_ref, eg_ref, eb_ref, g_ref, b_ref, o_ref):
    h0 = _layernorm_f32(x_ref[...], eg_ref[...], eb_ref[...])
    half = oa_ref.shape[1]
    mix = _dot(oa_ref[...], w_ref[0:half, :]) + _dot(ob_ref[...], w_ref[half:, :])
    o_ref[...] = _layernorm_f32(ALPHA * h0 + mix, g_ref[...], b_ref[...])


def _outproj_ln1(x2d, oa, ob, w_out, eg, eb, g, b):
    t, d = x2d.shape
    tm = min(512, t)
    wa = oa.shape[1]
    wb = ob.shape[1]
    vec = pl.BlockSpec((1, d), lambda i: (0, 0))
    return pl.pallas_call(
        _outproj_kernel,
        out_shape=jax.ShapeDtypeStruct((t, d), jnp.float32),
        grid=(t // tm,),
        in_specs=[
            pl.BlockSpec((tm, d), lambda i: (i, 0)),
            pl.BlockSpec((tm, wa), lambda i: (i, 0)),
            pl.BlockSpec((tm, wb), lambda i: (i, 0)),
            pl.BlockSpec((wa + wb, d), lambda i: (0, 0)),
            vec, vec, vec, vec,
        ],
        out_specs=pl.BlockSpec((tm, d), lambda i: (i, 0)),
        compiler_params=_cparams(("parallel",)),
        name="outproj_ln1",
    )(x2d, oa, ob, w_out, eg, eb, g, b)


def _router_kernel(h_ref, wh_ref, wl_ref, rb_ref, e_ref, w_ref, r_ref, cnt_ref, carry_ref, *, n_exp):
    i = pl.program_id(0)
    tn = h_ref.shape[0]
    gsz = n_exp // N_GROUPS

    @pl.when(i == 0)
    def _():
        carry_ref[...] = jnp.zeros_like(carry_ref)

    x = h_ref[...]
    xh = x.astype(jnp.bfloat16)
    xl = (x - xh.astype(jnp.float32)).astype(jnp.bfloat16)
    logits = _dot_nt(wh_ref[...], xh) + (_dot_nt(wh_ref[...], xl) + _dot_nt(wl_ref[...], xh))
    scores = 1.0 / (1.0 + jnp.exp(-logits))
    biased = scores + rb_ref[...]

    g3 = biased.reshape(N_GROUPS, gsz, tn)
    it3 = lax.broadcasted_iota(jnp.int32, (N_GROUPS, gsz, tn), 1)
    mx1 = jnp.max(g3, axis=1, keepdims=True)
    first = jnp.min(jnp.where(g3 == mx1, it3, gsz), axis=1, keepdims=True)
    mx2 = jnp.max(jnp.where(it3 == first, -jnp.inf, g3), axis=1, keepdims=True)
    gscore = (mx1 + mx2).reshape(N_GROUPS, tn)

    itg = lax.broadcasted_iota(jnp.int32, (N_GROUPS, tn), 0)
    gsel = jnp.zeros((N_GROUPS, tn), jnp.bool_)
    cur = gscore
    for _ in range(TOPK_GROUPS):
        mx = jnp.max(cur, axis=0, keepdims=True)
        fi = jnp.min(jnp.where(cur == mx, itg, N_GROUPS), axis=0, keepdims=True)
        hit = itg == fi
        gsel = jnp.logical_or(gsel, hit)
        cur = jnp.where(hit, -jnp.inf, cur)
    emask = jnp.broadcast_to(gsel.reshape(N_GROUPS, 1, tn), (N_GROUPS, gsz, tn)).reshape(n_exp, tn)
    cur = jnp.where(emask, biased, NEG)

    ite = lax.broadcasted_iota(jnp.int32, (n_exp, tn), 0)
    hits = []
    eidx = []
    wsel = []
    for _ in range(TOP_K):
        mx = jnp.max(cur, axis=0, keepdims=True)
        fi = jnp.min(jnp.where(cur == mx, ite, n_exp), axis=0, keepdims=True)
        hit = ite == fi
        hits.append(hit)
        eidx.append(fi)
        wsel.append(jnp.sum(jnp.where(hit, scores, 0.0), axis=0, keepdims=True))
        cur = jnp.where(hit, -jnp.inf, cur)
    sel = hits[0]
    for hit in hits[1:]:
        sel = jnp.logical_or(sel, hit)
    self32 = jnp.where(sel, 1.0, 0.0)

    r_io = lax.broadcasted_iota(jnp.int32, (tn, tn), 0)
    c_io = lax.broadcasted_iota(jnp.int32, (tn, tn), 1)
    upper = jnp.where(r_io < c_io, 1.0, 0.0).astype(jnp.bfloat16)
    before = _dot(self32.astype(jnp.bfloat16), upper) + carry_ref[...]
    ranks = [jnp.sum(jnp.where(hit, before, 0.0), axis=0, keepdims=True) for hit in hits]
    carry_ref[...] = carry_ref[...] + jnp.sum(self32, axis=1, keepdims=True)
    cnt_ref[...] = carry_ref[...]

    wcat = jnp.concatenate(wsel, axis=0)
    wcat = wcat / jnp.sum(wcat, axis=0, keepdims=True) * ROUTED_SCALE
    e_ref[...] = jnp.concatenate(eidx, axis=0)
    w_ref[...] = wcat
    r_ref[...] = jnp.concatenate(ranks, axis=0).astype(jnp.int32)


def _router(h1, wr_hi, wr_lo, rbias):
    t, d = h1.shape
    n_exp = wr_hi.shape[0]
    tn = min(512, t)
    kern = functools.partial(_router_kernel, n_exp=n_exp)
    row = pl.BlockSpec((TOP_K, tn), lambda i: (0, i))
    return pl.pallas_call(
        kern,
        out_shape=(
            jax.ShapeDtypeStruct((TOP_K, t), jnp.int32),
            jax.ShapeDtypeStruct((TOP_K, t), jnp.float32),
            jax.ShapeDtypeStruct((TOP_K, t), jnp.int32),
            jax.ShapeDtypeStruct((n_exp, 1), jnp.float32),
        ),
        grid=(t // tn,),
        in_specs=[
            pl.BlockSpec((tn, d), lambda i: (i, 0)),
            pl.BlockSpec((n_exp, d), lambda i: (0, 0)),
            pl.BlockSpec((n_exp, d), lambda i: (0, 0)),
            pl.BlockSpec((n_exp, 1), lambda i: (0, 0)),
        ],
        out_specs=(row, row, row, pl.BlockSpec((n_exp, 1), lambda i: (0, 0))),
        scratch_shapes=[pltpu.VMEM((n_exp, 1), jnp.float32)],
        compiler_params=_cparams(("arbitrary",)),
        name="router",
    )(h1, wr_hi, wr_lo, rbias)


def _scatter_kernel(dest_ref, h_ref, xs_ref, sem):
    tn = h_ref.shape[0]

    def row_copy(t, d):
        return pltpu.make_async_copy(h_ref.at[pl.ds(t, 1), :], xs_ref.at[pl.ds(d, 1), :], sem)

    def issue(t, carry):
        for k in range(TOP_K):
            row_copy(t, dest_ref[k, t]).start()
        return carry

    lax.fori_loop(0, tn, issue, 0)

    def drain(t, carry):
        for k in range(TOP_K):
            row_copy(t, dest_ref[k, t]).wait()
        return carry

    lax.fori_loop(0, tn, drain, 0)


def _scatter_rows(h1, dest, n_rows):
    t, d = h1.shape
    tn = min(512, t)
    return pl.pallas_call(
        _scatter_kernel,
        out_shape=jax.ShapeDtypeStruct((n_rows, d), jnp.float32),
        grid=(t // tn,),
        in_specs=[
            pl.BlockSpec((TOP_K, tn), lambda i: (0, i), memory_space=pltpu.SMEM),
            pl.BlockSpec((tn, d), lambda i: (i, 0)),
        ],
        out_specs=pl.BlockSpec(memory_space=pl.ANY),
        scratch_shapes=[pltpu.SemaphoreType.DMA(())],
        compiler_params=_cparams(("arbitrary",)),
        name="scatter_rows",
    )(dest, h1)


def _gmm_kernel(te_ref, tv_ref, tb_ref, x_ref, wgu_ref, wd_ref, y_ref):
    i = pl.program_id(0)
    nvalid = tv_ref[i]

    @pl.when(nvalid > 0)
    def _():
        tm = x_ref.shape[0]
        rows = lax.broadcasted_iota(jnp.int32, (tm, 1), 0)
        x = jnp.where(rows < nvalid, x_ref[...], 0.0).astype(jnp.bfloat16)
        gu = _dot(x, wgu_ref[0])
        de = gu.shape[1] // 2
        gate = gu[:, :de]
        hid = (gate / (1.0 + jnp.exp(-gate))) * gu[:, de:]
        y_ref[...] = _dot(hid.astype(jnp.bfloat16), wd_ref[0])


def _gmm(xs, wgu, wd, tile_e, tile_valid, tile_blk):
    n_rows, d = xs.shape
    nt = n_rows // TM_MOE
    de2 = wgu.shape[2]
    return pl.pallas_call(
        _gmm_kernel,
        out_shape=jax.ShapeDtypeStruct((n_rows, d), jnp.float32),
        grid_spec=pltpu.PrefetchScalarGridSpec(
            num_scalar_prefetch=3,
            grid=(nt,),
            in_specs=[
                pl.BlockSpec((TM_MOE, d), lambda i, te, tv, tb: (tb[i], 0)),
                pl.BlockSpec((1, d, de2), lambda i, te, tv, tb: (te[i], 0, 0)),
                pl.BlockSpec((1, de2 // 2, d), lambda i, te, tv, tb: (te[i], 0, 0)),
            ],
            out_specs=pl.BlockSpec((TM_MOE, d), lambda i, te, tv, tb: (tb[i], 0)),
        ),
        compiler_params=_cparams(("arbitrary",)),
        name="expert_gmm",
    )(tile_e, tile_valid, tile_blk, xs, wgu, wd)


def _combine_kernel(dest_ref, h_ref, w_ref, ys_ref, sgu_ref, sd_ref, g_ref, b_ref, o_ref, buf, sem):
    tn = h_ref.shape[0]

    def row_copy(k, t):
        return pltpu.make_async_copy(ys_ref.at[pl.ds(dest_ref[k, t], 1), :], buf.at[k, pl.ds(t, 1), :], sem)

    def issue(t, carry):
        for k in range(TOP_K):
            row_copy(k, t).start()
        return carry

    lax.fori_loop(0, tn, issue, 0)

    h = h_ref[...]
    gu = _dot(h.astype(jnp.bfloat16), sgu_ref[...])
    ds = gu.shape[1] // 2
    gate = gu[:, :ds]
    hid = (gate / (1.0 + jnp.exp(-gate))) * gu[:, ds:]
    shared = _dot(hid.astype(jnp.bfloat16), sd_ref[...])

    def drain(t, carry):
        for k in range(TOP_K):
            row_copy(k, t).wait()
        return carry

    lax.fori_loop(0, tn, drain, 0)

    w = w_ref[...]
    routed = buf[0] * w[:, 0:1]
    for k in range(1, TOP_K):
        routed = routed + buf[k] * w[:, k:k + 1]
    o_ref[...] = _layernorm_f32(ALPHA * h + (routed + shared), g_ref[...], b_ref[...])


def _combine(h1, dest, w_tok, ys, sgu, sd, g, b):
    t, d = h1.shape
    tn = min(256, t)
    ds2 = sgu.shape[1]
    vec = pl.BlockSpec((1, d), lambda i: (0, 0))
    return pl.pallas_call(
        _combine_kernel,
        out_shape=jax.ShapeDtypeStruct((t, d), jnp.float32),
        grid=(t // tn,),
        in_specs=[
            pl.BlockSpec((TOP_K, tn), lambda i: (0, i), memory_space=pltpu.SMEM),
            pl.BlockSpec((tn, d), lambda i: (i, 0)),
            pl.BlockSpec((tn, TOP_K), lambda i: (i, 0)),
            pl.BlockSpec(memory_space=pl.ANY),
            pl.BlockSpec((d, ds2), lambda i: (0, 0)),
            pl.BlockSpec((ds2 // 2, d), lambda i: (0, 0)),
            vec, vec,
        ],
        out_specs=pl.BlockSpec((tn, d), lambda i: (i, 0)),
        scratch_shapes=[pltpu.VMEM((TOP_K, tn, d), jnp.float32), pltpu.SemaphoreType.DMA(())],
        compiler_params=_cparams(("arbitrary",)),
        name="combine_ln2",
    )(dest, h1, w_tok, ys, sgu, sd, g, b)


def _rel_bucket(rel):
    nb = N_BUCKETS // 2
    max_exact = nb // 2
    ret = jnp.where(rel > 0, nb, 0)
    n = jnp.abs(rel)
    nf = jnp.maximum(n, 1).astype(jnp.float32)
    large = max_exact + (jnp.log(nf / max_exact) / math.log(MAX_DISTANCE / max_exact) * (nb - max_exact)).astype(jnp.int32)
    large = jnp.minimum(large, nb - 1)
    return ret + jnp.where(n < max_exact, n, large)


def _bias_tables_a(rel_bias):
    bias_a = rel_bias[:, :A_Q_HEADS].astype(jnp.float32)
    ql = jnp.arange(TQ_A, dtype=jnp.int32)[:, None]
    rel = jnp.arange(3 * TQ_A, dtype=jnp.int32)[None, :] - TQ_A - ql
    band = jnp.where((jnp.abs(rel) <= WINDOW)[..., None], bias_a[_rel_bucket(rel)], NEG)
    band = jnp.transpose(band, (2, 0, 1))
    return band


def _meta_table_a(rel_bias, rows):
    bias_a = rel_bias[:, :A_Q_HEADS].astype(jnp.float32)
    kpos = jnp.arange(N_META, dtype=jnp.int32)[None, :]
    tabs = []
    for base in (0, rows):
        qpos = N_META + base + jnp.arange(rows, dtype=jnp.int32)[:, None]
        tabs.append(jnp.transpose(bias_a[_rel_bucket(kpos - qpos)], (2, 0, 1)))
    return jnp.stack(tabs)


def _bias_tables_b(rel_bias, s):
    bias_b = rel_bias[:, A_Q_HEADS:].astype(jnp.float32)
    ql = jnp.arange(TQ_B, dtype=jnp.int32)[:, None]
    kl = jnp.arange(TK_B, dtype=jnp.int32)[None, :]
    near = jnp.stack([bias_b[_rel_bucket(TK_B * d + kl - ql)] for d in (-1, 0, 1)])
    near = jnp.transpose(near, (3, 0, 1, 2))
    qpos = N_META + jnp.arange(s, dtype=jnp.int32)[:, None]
    kpos = jnp.arange(N_META, dtype=jnp.int32)[None, :]
    meta = jnp.transpose(bias_b[_rel_bucket(kpos - qpos)], (2, 0, 1))
    far = jnp.concatenate([bias_b[_rel_bucket(jnp.int32(-2 * MAX_DISTANCE))], bias_b[_rel_bucket(jnp.int32(2 * MAX_DISTANCE))]])
    return near, meta, far


def _prep_w_in(w_in):
    a_w = A_Q_HEADS * HEAD_DIM
    kv = A_KV_HEADS * HEAD_DIM
    bqk = B_HEADS * 2 * HEAD_DIM
    scale = HEAD_DIM ** -0.5
    qa = w_in[:, :a_w] * scale
    ka = w_in[:, a_w:a_w + kv]
    va = w_in[:, a_w + kv:a_w + 2 * kv]
    o = a_w + 2 * kv
    qb = w_in[:, o:o + bqk] * scale
    kb = w_in[:, o + bqk:o + 2 * bqk]
    vb = w_in[:, o + 2 * bqk:]

    def dup(w):
        return jnp.concatenate([w[:, g * HEAD_DIM:(g + 1) * HEAD_DIM] for g in range(A_KV_HEADS) for _ in range(2)], axis=1)

    return jnp.concatenate([qa, dup(ka), dup(va), qb, kb, vb], axis=1).astype(jnp.bfloat16)


def _trunk(x, prm):
    bsz, s, d = x.shape
    t = bsz * s
    x2d = x.reshape(t, d)
    proj = _ln_inproj(x2d, prm["ln_emb_g"], prm["ln_emb_b"], prm["w_in"])
    proj3 = proj.reshape(bsz, s, PROJ_COLS)
    sub = 4 if (s // TQ_A) % 4 == 0 else 1
    tabm_a = _meta_table_a(prm["rel_bias"], sub * TQ_A)
    oa = _mixer_a(proj3, prm["proj_meta"], prm["tab_a"], tabm_a, prm["sink"])
    near, meta_b, far = _bias_tables_b(prm["rel_bias"], s)
    scal = jnp.concatenate([far, prm["lam"][None]]).astype(jnp.float32)
    ob = _mixer_b(proj3, prm["proj_meta"], near, meta_b, scal, prm["subln_g"])
    h1 = _outproj_ln1(x2d, oa.reshape(t, -1), ob.reshape(t, -1), prm["w_out"], prm["ln_emb_g"], prm["ln_emb_b"],
                      prm["ln1_g"], prm["ln1_b"])

    eidx, wts, rank, counts = _router(h1, prm["wr_hi"], prm["wr_lo"], prm["router_bias"])
    n_exp = counts.shape[0]
    counts = counts[:, 0].astype(jnp.int32)
    tiles_e = (counts + TM_MOE - 1) // TM_MOE
    tend = jnp.cumsum(tiles_e)
    tstart = tend - tiles_e
    dest = tstart[eidx] * TM_MOE + rank
    nt = (t * TOP_K) // TM_MOE + n_exp
    tid = jnp.arange(nt, dtype=jnp.int32)
    tile_e = jnp.minimum(jnp.searchsorted(tend, tid, side="right"), n_exp - 1).astype(jnp.int32)
    tile_valid = jnp.clip(counts[tile_e] - (tid - tstart[tile_e]) * TM_MOE, 0, TM_MOE)
    tile_valid = jnp.where(tid < tend[-1], tile_valid, 0).astype(jnp.int32)
    tile_blk = jnp.minimum(tid, tend[-1] - 1).astype(jnp.int32)

    xs = _scatter_rows(h1, dest, nt * TM_MOE)
    ys = _gmm(xs, prm["w_gu"], prm["w_down"], tile_e, tile_valid, tile_blk)
    out = _combine(h1, dest, wts.T, ys, prm["ws_gu"], prm["ws_down"], prm["ln2_g"], prm["ln2_b"])
    return out.reshape(bsz, s, d)


def kernel(x_prompt, x_sample, meta_tokens, ln_emb_g, ln_emb_b, rel_bias, w_in, attn_sink, lambda_q1, lambda_k1, lambda_q2, lambda_k2, subln_g, w_out, ln1_g, ln1_b, w_router, router_bias, w_gate, w_up, w_down, ws_gate, ws_up, ws_down, ln2_g, ln2_b):
    f32 = jnp.float32
    bf16 = jnp.bfloat16
    d = x_prompt.shape[-1]
    l = 0
    row = lambda v: v.reshape(1, -1).astype(f32)
    lam = (jnp.exp(jnp.sum(lambda_q1[l].astype(f32) * lambda_k1[l].astype(f32)))
           - jnp.exp(jnp.sum(lambda_q2[l].astype(f32) * lambda_k2[l].astype(f32))) + LAMBDA_INIT)
    wr_t = w_router[l].astype(f32).T
    wr_hi = wr_t.astype(bf16)
    prm = {
        "ln_emb_g": row(ln_emb_g), "ln_emb_b": row(ln_emb_b),
        "rel_bias": rel_bias,
        "w_in": _prep_w_in(w_in[l]),
        "sink": attn_sink[l].astype(f32),
        "lam": lam,
        "subln_g": row(subln_g[l]),
        "w_out": w_out[l].astype(bf16),
        "ln1_g": row(ln1_g[l]), "ln1_b": row(ln1_b[l]),
        "wr_hi": wr_hi, "wr_lo": (wr_t - wr_hi.astype(f32)).astype(bf16),
        "router_bias": router_bias[l].astype(f32).reshape(-1, 1),
        "w_gu": jnp.concatenate([w_gate[l].astype(bf16), w_up[l].astype(bf16)], axis=-1),
        "w_down": w_down[l].astype(bf16),
        "ws_gu": jnp.concatenate([ws_gate[l], ws_up[l]], axis=-1).astype(bf16),
        "ws_down": ws_down[l].astype(bf16),
        "ln2_g": row(ln2_g[l]), "ln2_b": row(ln2_b[l]),
        "tab_a": _bias_tables_a(rel_bias),
    }
    prm["proj_meta"] = _ln_inproj(meta_tokens.astype(f32), prm["ln_emb_g"], prm["ln_emb_b"], prm["w_in"])
    return (_trunk(x_prompt, prm), _trunk(x_sample, prm))
```

```python
import functools
import math

import jax
import jax.numpy as jnp
from jax import lax
from jax.experimental import pallas as pl
from jax.experimental.pallas import tpu as pltpu

N_META = 16
HEAD_DIM = 64
WINDOW = 128
A_Q_HEADS = 8
A_KV_HEADS = 2
B_HEADS = 4
N_BUCKETS = 32
MAX_DISTANCE = 128
TOP_K = 8
N_GROUPS = 8
TOPK_GROUPS = 4
ROUTED_SCALE = 2.5
LN_EPS = 1e-5
DEPTH = 1
ALPHA = (2 * DEPTH) ** 0.25
NEG = -1e30
LAMBDA_INIT = 0.8 - 0.6 * math.exp(-0.3 * 0)

LANES = 128
VMEM_LIMIT = 48 * 1024 * 1024

QA_BLK = 0
KA_BLK = 4
VA_BLK = 6
QB_BLK = 8
KB_BLK = 12
PROJ_COLS = 16 * LANES
VT_ROWS = LANES + 16
LOG2E = 1.4426950408889634

TQ_A = 128
TQ_B = 512
TK_B = 512
TM_MOE = 256


def _cparams(sem):
    return pltpu.CompilerParams(dimension_semantics=sem, vmem_limit_bytes=VMEM_LIMIT)


def _layernorm_f32(x, g, b):
    mu = jnp.mean(x, axis=-1, keepdims=True)
    xc = x - mu
    var = jnp.mean(xc * xc, axis=-1, keepdims=True)
    return xc * lax.rsqrt(var + LN_EPS) * g + b


def _dot_nt(a, b):
    return lax.dot_general(a, b, (((1,), (1,)), ((), ())), preferred_element_type=jnp.float32)


def _dot(a, b):
    return jnp.dot(a, b, preferred_element_type=jnp.float32)


def _ln_inproj_kernel(x_ref, g_ref, b_ref, w_ref, wvt_ref, o_ref, vt_ref):
    h = _layernorm_f32(x_ref[...], g_ref[...], b_ref[...]).astype(jnp.bfloat16)
    o_ref[...] = _dot(h, w_ref[...]).astype(o_ref.dtype)
    vt = _dot_nt(wvt_ref[...], h).astype(vt_ref.dtype)
    ones = jnp.ones((VT_ROWS - LANES, vt.shape[1]), vt_ref.dtype)
    for hd in range(B_HEADS):
        vt_ref[hd * VT_ROWS:hd * VT_ROWS + LANES, :] = vt[hd * LANES:(hd + 1) * LANES, :]
        vt_ref[hd * VT_ROWS + LANES:(hd + 1) * VT_ROWS, :] = ones


def _ln_inproj(x2d, g, b, w, wvt):
    t, d = x2d.shape
    n = w.shape[1]
    tm = min(512, t)
    return pl.pallas_call(
        _ln_inproj_kernel,
        out_shape=(jax.ShapeDtypeStruct((t, n), jnp.bfloat16),
                   jax.ShapeDtypeStruct((B_HEADS * VT_ROWS, t), jnp.bfloat16)),
        grid=(t // tm,),
        in_specs=[
            pl.BlockSpec((tm, d), lambda i: (i, 0)),
            pl.BlockSpec((1, d), lambda i: (0, 0)),
            pl.BlockSpec((1, d), lambda i: (0, 0)),
            pl.BlockSpec((d, n), lambda i: (0, 0)),
            pl.BlockSpec((B_HEADS * LANES, d), lambda i: (0, 0)),
        ],
        out_specs=(pl.BlockSpec((tm, n), lambda i: (i, 0)),
                   pl.BlockSpec((B_HEADS * VT_ROWS, tm), lambda i: (0, i))),
        compiler_params=_cparams(("parallel",)),
        name="ln_inproj",
    )(x2d, g, b, w, wvt)


def _mixer_a_kernel(sink_ref, q_ref, k_ref, v_ref, km_ref, vm_ref, tab_ref, tabm_ref, o_ref, *, nblk, sub):
    i = pl.program_id(1)
    lane = lax.broadcasted_iota(jnp.int32, (1, LANES), 1)
    lo = lane < HEAD_DIM
    for j in range(sub):
        gi = i * sub + j
        sp = pl.multiple_of(jnp.maximum(gi - 1, 0) * TQ_A, TQ_A)
        sc = pl.multiple_of(gi * TQ_A, TQ_A)
        sn = pl.multiple_of(jnp.minimum(gi + 1, nblk - 1) * TQ_A, TQ_A)
        neg_p = jnp.where(gi > 0, 0.0, NEG).astype(jnp.float32)
        neg_n = jnp.where(gi < nblk - 1, 0.0, NEG).astype(jnp.float32)
        for hp in range(A_Q_HEADS // 2):
            g = hp // 2
            gs = slice(g * LANES, (g + 1) * LANES)
            qc = q_ref[0, j * TQ_A:(j + 1) * TQ_A, hp * LANES:(hp + 1) * LANES]
            kpieces = (km_ref[:, gs], k_ref[0, pl.ds(sp, TQ_A), gs], k_ref[0, pl.ds(sc, TQ_A), gs],
                       k_ref[0, pl.ds(sn, TQ_A), gs])
            vpieces = (vm_ref[:, gs], v_ref[0, pl.ds(sp, TQ_A), gs], v_ref[0, pl.ds(sc, TQ_A), gs],
                       v_ref[0, pl.ds(sn, TQ_A), gs])
            o_pair = jnp.zeros((TQ_A, LANES), jnp.float32)
            for half in range(2):
                h = 2 * hp + half
                msk = lo if half == 0 else jnp.logical_not(lo)
                qm = jnp.where(msk, qc, jnp.zeros_like(qc))
                sink = sink_ref[h]
                s = [
                    _dot_nt(qm, kpieces[0]) + tabm_ref[0, h, j * TQ_A:(j + 1) * TQ_A, :],
                    _dot_nt(qm, kpieces[1]) + (tab_ref[h, :, 0:TQ_A] + neg_p),
                    _dot_nt(qm, kpieces[2]) + tab_ref[h, :, TQ_A:2 * TQ_A],
                    _dot_nt(qm, kpieces[3]) + (tab_ref[h, :, 2 * TQ_A:3 * TQ_A] + neg_n),
                ]
                m = jnp.maximum(jnp.maximum(jnp.max(s[0], axis=-1, keepdims=True), jnp.max(s[1], axis=-1, keepdims=True)),
                                jnp.maximum(jnp.max(s[2], axis=-1, keepdims=True), jnp.max(s[3], axis=-1, keepdims=True)))
                m = jnp.maximum(m, sink)
                l = jnp.exp(sink - m)
                o_h = jnp.zeros((TQ_A, LANES), jnp.float32)
                for sx, vx in zip(s, vpieces):
                    p = jnp.exp(sx - m)
                    l = l + jnp.sum(p, axis=-1, keepdims=True)
                    vmk = jnp.where(msk, vx, jnp.zeros_like(vx))
                    o_h = o_h + _dot(p.astype(jnp.bfloat16), vmk)
                o_pair = o_pair + o_h / l
            o_ref[0, j * TQ_A:(j + 1) * TQ_A, hp * LANES:(hp + 1) * LANES] = o_pair.astype(o_ref.dtype)


def _mixer_a(proj3, proj_meta, tab, tabm, sink):
    bsz, s, _ = proj3.shape
    nblk = s // TQ_A
    sub = 4 if nblk % 4 == 0 else 1
    nq = nblk // sub
    tq = sub * TQ_A
    kern = functools.partial(_mixer_a_kernel, nblk=nblk, sub=sub)
    return pl.pallas_call(
        kern,
        out_shape=jax.ShapeDtypeStruct((bsz, s, A_Q_HEADS * HEAD_DIM), jnp.bfloat16),
        grid=(bsz, nq),
        in_specs=[
            pl.BlockSpec(memory_space=pltpu.SMEM),
            pl.BlockSpec((1, tq, 4 * LANES), lambda b, i: (b, i, QA_BLK // 4)),
            pl.BlockSpec((1, s, 2 * LANES), lambda b, i: (b, 0, KA_BLK // 2)),
            pl.BlockSpec((1, s, 2 * LANES), lambda b, i: (b, 0, VA_BLK // 2)),
            pl.BlockSpec((N_META, 2 * LANES), lambda b, i: (0, KA_BLK // 2)),
            pl.BlockSpec((N_META, 2 * LANES), lambda b, i: (0, VA_BLK // 2)),
            pl.BlockSpec((A_Q_HEADS, TQ_A, 3 * TQ_A), lambda b, i: (0, 0, 0)),
            pl.BlockSpec((1, A_Q_HEADS, tq, N_META), lambda b, i: (jnp.minimum(i, 1), 0, 0, 0)),
        ],
        out_specs=pl.BlockSpec((1, tq, 4 * LANES), lambda b, i: (b, i, 0)),
        compiler_params=_cparams(("parallel", "arbitrary")),
        name="mixer_a",
    )(sink, proj3, proj3, proj3, proj_meta, proj_meta, tab, tabm)


def _mixer_b_kernel(sc_ref, q_ref, k_ref, vt_ref, km_ref, vtm_ref, tab_ref, tabm_ref, g_ref, o_ref,
                    m_ref, acc_ref, *, nk):
    h = pl.program_id(1)
    i = pl.program_id(2)
    lane = lax.broadcasted_iota(jnp.int32, (1, LANES), 1)
    lo = lane < HEAD_DIM
    q = q_ref[0]
    qs = (jnp.where(lo, q, jnp.zeros_like(q)), jnp.where(lo, jnp.zeros_like(q), q))
    c_left = sc_ref[h]
    c_right = sc_ref[B_HEADS + h]
    lam = sc_ref[2 * B_HEADS]

    km = km_ref[...]
    vtm = vtm_ref[...]
    for c in range(2):
        s = _dot_nt(km, qs[c]) + tabm_ref[0]
        m = jnp.max(s, axis=0, keepdims=True)
        p = jnp.exp2(s - m)
        m_ref[c] = m
        acc_ref[c] = _dot(vtm, p.astype(jnp.bfloat16))

    def update(j, bias, const):
        start = pl.multiple_of(j * TK_B, TK_B)
        kj = k_ref[0, pl.ds(start, TK_B), :]
        vtj = vt_ref[:, pl.ds(start, TK_B)]
        for c in range(2):
            s = _dot_nt(kj, qs[c])
            if bias is not None:
                s = s + bias
            m_prev = m_ref[c]
            m_cur = jnp.max(s, axis=0, keepdims=True)
            if const is not None:
                m_cur = m_cur + const
            m_new = jnp.maximum(m_prev, m_cur)
            shift = m_new if const is None else m_new - const
            a = jnp.exp2(m_prev - m_new)
            p = jnp.exp2(s - shift)
            m_ref[c] = m_new
            acc_ref[c] = a * acc_ref[c] + _dot(vtj, p.astype(jnp.bfloat16))

    def far_left(j, carry):
        update(j, None, c_left)
        return carry

    def far_right(j, carry):
        update(j, None, c_right)
        return carry

    lax.fori_loop(0, jnp.maximum(i - 1, 0), far_left, 0)
    for d in range(3):
        j = i + d - 1

        @pl.when(jnp.logical_and(j >= 0, j < nk))
        def _():
            update(j, tab_ref[0, d], None)

    lax.fori_loop(jnp.minimum(i + 2, nk), nk, far_right, 0)

    o0 = acc_ref[0, 0:LANES, :] / acc_ref[0, LANES:LANES + 1, :]
    o1 = acc_ref[1, 0:LANES, :] / acc_ref[1, LANES:LANES + 1, :]
    o = o0 - lam * o1
    ms = jnp.mean(o * o, axis=0, keepdims=True)
    o = o * lax.rsqrt(ms + LN_EPS) * (g_ref[...] * (1.0 - LAMBDA_INIT))
    o_ref[0] = o.T.astype(o_ref.dtype)


def _mixer_b(proj3, vt, proj_meta, vt_meta, tab, tabm, scal, subln_g):
    bsz, s, _ = proj3.shape
    nq = s // TQ_B
    nk = s // TK_B
    kern = functools.partial(_mixer_b_kernel, nk=nk)
    return pl.pallas_call(
        kern,
        out_shape=jax.ShapeDtypeStruct((bsz, s, B_HEADS * LANES), jnp.bfloat16),
        grid=(bsz, B_HEADS, nq),
        in_specs=[
            pl.BlockSpec(memory_space=pltpu.SMEM),
            pl.BlockSpec((1, TQ_B, LANES), lambda b, h, i: (b, i, QB_BLK + h)),
            pl.BlockSpec((1, s, LANES), lambda b, h, i: (b, 0, KB_BLK + h)),
            pl.BlockSpec((VT_ROWS, s), lambda b, h, i: (h, b)),
            pl.BlockSpec((N_META, LANES), lambda b, h, i: (0, KB_BLK + h)),
            pl.BlockSpec((VT_ROWS, N_META), lambda b, h, i: (h, 0)),
            pl.BlockSpec((1, 3, TK_B, TQ_B), lambda b, h, i: (h, 0, 0, 0)),
            pl.BlockSpec((1, N_META, TQ_B), lambda b, h, i: (h, 0, i)),
            pl.BlockSpec((LANES, 1), lambda b, h, i: (0, 0)),
        ],
        out_specs=pl.BlockSpec((1, TQ_B, LANES), lambda b, h, i: (b, i, h)),
        scratch_shapes=[
            pltpu.VMEM((2, 1, TQ_B), jnp.float32),
            pltpu.VMEM((2, VT_ROWS, TQ_B), jnp.float32),
        ],
        compiler_params=_cparams(("parallel", "parallel", "arbitrary")),
        name="mixer_b",
    )(scal, proj3, proj3, vt, proj_meta, vt_meta, tab, tabm, subln_g)


def _outproj_kernel(x_ref, oa_ref, ob_ref, w_ref, eg_ref, eb_ref, g_ref, b_ref, o_ref):
    h0 = _layernorm_f32(x_ref[...], eg_ref[...], eb_ref[...])
    half = oa_ref.shape[1]
    mix = _dot(oa_ref[...], w_ref[0:half, :]) + _dot(ob_ref[...], w_ref[half:, :])
    o_ref[...] = _layernorm_f32(ALPHA * h0 + mix, g_ref[...], b_ref[...])


def _outproj_ln1(x2d, oa, ob, w_out, eg, eb, g, b):
    t, d = x2d.shape
    tm = min(512, t)
    wa = oa.shape[1]
    wb = ob.shape[1]
    vec = pl.BlockSpec((1, d), lambda i: (0, 0))
    return pl.pallas_call(
        _outproj_kernel,
        out_shape=jax.ShapeDtypeStruct((t, d), jnp.float32),
        grid=(t // tm,),
        in_specs=[
            pl.BlockSpec((tm, d), lambda i: (i, 0)),
            pl.BlockSpec((tm, wa), lambda i: (i, 0)),
            pl.BlockSpec((tm, wb), lambda i: (i, 0)),
            pl.BlockSpec((wa + wb, d), lambda i: (0, 0)),
            vec, vec, vec, vec,
        ],
        out_specs=pl.BlockSpec((tm, d), lambda i: (i, 0)),
        compiler_params=_cparams(("parallel",)),
        name="outproj_ln1",
    )(x2d, oa, ob, w_out, eg, eb, g, b)


def _router_kernel(h_ref, wh_ref, wl_ref, rb_ref, e_ref, w_ref, r_ref, cnt_ref, carry_ref, *, n_exp):
    i = pl.program_id(0)
    tn = h_ref.shape[0]
    gsz = n_exp // N_GROUPS

    @pl.when(i == 0)
    def _():
        carry_ref[...] = jnp.zeros_like(carry_ref)

    x = h_ref[...]
    xh = x.astype(jnp.bfloat16)
    xl = (x - xh.astype(jnp.float32)).astype(jnp.bfloat16)
    logits = _dot_nt(wh_ref[...], xh) + (_dot_nt(wh_ref[...], xl) + _dot_nt(wl_ref[...], xh))
    scores = 1.0 / (1.0 + jnp.exp(-logits))
    biased = scores + rb_ref[...]

    g3 = biased.reshape(N_GROUPS, gsz, tn)
    it3 = lax.broadcasted_iota(jnp.int32, (N_GROUPS, gsz, tn), 1)
    mx1 = jnp.max(g3, axis=1, keepdims=True)
    first = jnp.min(jnp.where(g3 == mx1, it3, gsz), axis=1, keepdims=True)
    mx2 = jnp.max(jnp.where(it3 == first, -jnp.inf, g3), axis=1, keepdims=True)
    gscore = (mx1 + mx2).reshape(N_GROUPS, tn)

    itg = lax.broadcasted_iota(jnp.int32, (N_GROUPS, tn), 0)
    gsel = jnp.zeros((N_GROUPS, tn), jnp.bool_)
    cur = gscore
    for _ in range(TOPK_GROUPS):
        mx = jnp.max(cur, axis=0, keepdims=True)
        fi = jnp.min(jnp.where(cur == mx, itg, N_GROUPS), axis=0, keepdims=True)
        hit = itg == fi
        gsel = jnp.logical_or(gsel, hit)
        cur = jnp.where(hit, -jnp.inf, cur)
    emask = jnp.broadcast_to(gsel.reshape(N_GROUPS, 1, tn), (N_GROUPS, gsz, tn)).reshape(n_exp, tn)
    cur = jnp.where(emask, biased, NEG)

    ite = lax.broadcasted_iota(jnp.int32, (n_exp, tn), 0)
    hits = []
    eidx = []
    wsel = []
    for _ in range(TOP_K):
        mx = jnp.max(cur, axis=0, keepdims=True)
        fi = jnp.min(jnp.where(cur == mx, ite, n_exp), axis=0, keepdims=True)
        hit = ite == fi
        hits.append(hit)
        eidx.append(fi)
        wsel.append(jnp.sum(jnp.where(hit, scores, 0.0), axis=0, keepdims=True))
        cur = jnp.where(hit, -jnp.inf, cur)
    sel = hits[0]
    for hit in hits[1:]:
        sel = jnp.logical_or(sel, hit)
    self32 = jnp.where(sel, 1.0, 0.0)

    r_io = lax.broadcasted_iota(jnp.int32, (tn, tn), 0)
    c_io = lax.broadcasted_iota(jnp.int32, (tn, tn), 1)
    upper = jnp.where(r_io < c_io, 1.0, 0.0).astype(jnp.bfloat16)
    before = _dot(self32.astype(jnp.bfloat16), upper) + carry_ref[...]
    ranks = [jnp.sum(jnp.where(hit, before, 0.0), axis=0, keepdims=True) for hit in hits]
    carry_ref[...] = carry_ref[...] + jnp.sum(self32, axis=1, keepdims=True)
    cnt_ref[...] = carry_ref[...]

    wcat = jnp.concatenate(wsel, axis=0)
    wcat = wcat / jnp.sum(wcat, axis=0, keepdims=True) * ROUTED_SCALE
    e_ref[...] = jnp.concatenate(eidx, axis=0)
    w_ref[...] = wcat
    r_ref[...] = jnp.concatenate(ranks, axis=0).astype(jnp.int32)


def _router(h1, wr_hi, wr_lo, rbias):
    t, d = h1.shape
    n_exp = wr_hi.shape[0]
    tn = min(512, t)
    kern = functools.partial(_router_kernel, n_exp=n_exp)
    row = pl.BlockSpec((TOP_K, tn), lambda i: (0, i))
    return pl.pallas_call(
        kern,
        out_shape=(
            jax.ShapeDtypeStruct((TOP_K, t), jnp.int32),
            jax.ShapeDtypeStruct((TOP_K, t), jnp.float32),
            jax.ShapeDtypeStruct((TOP_K, t), jnp.int32),
            jax.ShapeDtypeStruct((n_exp, 1), jnp.float32),
        ),
        grid=(t // tn,),
        in_specs=[
            pl.BlockSpec((tn, d), lambda i: (i, 0)),
            pl.BlockSpec((n_exp, d), lambda i: (0, 0)),
            pl.BlockSpec((n_exp, d), lambda i: (0, 0)),
            pl.BlockSpec((n_exp, 1), lambda i: (0, 0)),
        ],
        out_specs=(row, row, row, pl.BlockSpec((n_exp, 1), lambda i: (0, 0))),
        scratch_shapes=[pltpu.VMEM((n_exp, 1), jnp.float32)],
        compiler_params=_cparams(("arbitrary",)),
        name="router",
    )(h1, wr_hi, wr_lo, rbias)


def _scatter_kernel(ts_ref, e_ref, r_ref, h_ref, xs_ref, sem):
    tn = h_ref.shape[0]

    def row_copy(k, t):
        d = ts_ref[e_ref[k, t]] + r_ref[k, t]
        return pltpu.make_async_copy(h_ref.at[pl.ds(t, 1), :], xs_ref.at[pl.ds(d, 1), :], sem)

    def issue(t, carry):
        for k in range(TOP_K):
            row_copy(k, t).start()
        return carry

    lax.fori_loop(0, tn, issue, 0)

    def drain(t, carry):
        for k in range(TOP_K):
            row_copy(k, t).wait()
        return carry

    lax.fori_loop(0, tn, drain, 0)


def _scatter_rows(h1, row_start, eidx, rank, n_rows):
    t, d = h1.shape
    tn = min(512, t)
    idx = pl.BlockSpec((TOP_K, tn), lambda i: (0, i), memory_space=pltpu.SMEM)
    return pl.pallas_call(
        _scatter_kernel,
        out_shape=jax.ShapeDtypeStruct((n_rows, d), jnp.float32),
        grid=(t // tn,),
        in_specs=[
            pl.BlockSpec(memory_space=pltpu.SMEM),
            idx, idx,
            pl.BlockSpec((tn, d), lambda i: (i, 0)),
        ],
        out_specs=pl.BlockSpec(memory_space=pl.ANY),
        scratch_shapes=[pltpu.SemaphoreType.DMA(())],
        compiler_params=_cparams(("arbitrary",)),
        name="scatter_rows",
    )(row_start, eidx, rank, h1)


def _gmm_kernel(te_ref, tv_ref, tb_ref, x_ref, wgu_ref, wd_ref, y_ref):
    i = pl.program_id(0)
    nvalid = tv_ref[i]

    @pl.when(nvalid > 0)
    def _():
        tm = x_ref.shape[0]
        rows = lax.broadcasted_iota(jnp.int32, (tm, 1), 0)
        x = jnp.where(rows < nvalid, x_ref[...], 0.0).astype(jnp.bfloat16)
        gu = _dot(x, wgu_ref[0])
        de = gu.shape[1] // 2
        gate = gu[:, :de]
        hid = (gate / (1.0 + jnp.exp(-gate))) * gu[:, de:]
        y_ref[...] = _dot(hid.astype(jnp.bfloat16), wd_ref[0])


def _gmm(xs, wgu, wd, tile_e, tile_valid, tile_blk):
    n_rows, d = xs.shape
    nt = n_rows // TM_MOE
    de2 = wgu.shape[2]
    return pl.pallas_call(
        _gmm_kernel,
        out_shape=jax.ShapeDtypeStruct((n_rows, d), jnp.float32),
        grid_spec=pltpu.PrefetchScalarGridSpec(
            num_scalar_prefetch=3,
            grid=(nt,),
            in_specs=[
                pl.BlockSpec((TM_MOE, d), lambda i, te, tv, tb: (tb[i], 0)),
                pl.BlockSpec((1, d, de2), lambda i, te, tv, tb: (te[i], 0, 0)),
                pl.BlockSpec((1, de2 // 2, d), lambda i, te, tv, tb: (te[i], 0, 0)),
            ],
            out_specs=pl.BlockSpec((TM_MOE, d), lambda i, te, tv, tb: (tb[i], 0)),
        ),
        compiler_params=_cparams(("arbitrary",)),
        name="expert_gmm",
    )(tile_e, tile_valid, tile_blk, xs, wgu, wd)


def _combine_kernel(ts_ref, e_ref, r_ref, h_ref, w_ref, ys_ref, sgu_ref, sd_ref, g_ref, b_ref, o_ref, buf, sem):
    tn = h_ref.shape[0]

    def row_copy(k, t):
        d = ts_ref[e_ref[k, t]] + r_ref[k, t]
        return pltpu.make_async_copy(ys_ref.at[pl.ds(d, 1), :], buf.at[k, pl.ds(t, 1), :], sem)

    def issue(t, carry):
        for k in range(TOP_K):
            row_copy(k, t).start()
        return carry

    lax.fori_loop(0, tn, issue, 0)

    h = h_ref[...]
    gu = _dot(h.astype(jnp.bfloat16), sgu_ref[...])
    ds = gu.shape[1] // 2
    gate = gu[:, :ds]
    hid = (gate / (1.0 + jnp.exp(-gate))) * gu[:, ds:]
    shared = _dot(hid.astype(jnp.bfloat16), sd_ref[...])

    def drain(t, carry):
        for k in range(TOP_K):
            row_copy(k, t).wait()
        return carry

    lax.fori_loop(0, tn, drain, 0)

    w = w_ref[...]
    routed = buf[0] * w[:, 0:1]
    for k in range(1, TOP_K):
        routed = routed + buf[k] * w[:, k:k + 1]
    o_ref[...] = _layernorm_f32(ALPHA * h + (routed + shared), g_ref[...], b_ref[...])


def _combine(h1, row_start, eidx, rank, w_tok, ys, sgu, sd, g, b):
    t, d = h1.shape
    tn = min(256, t)
    ds2 = sgu.shape[1]
    vec = pl.BlockSpec((1, d), lambda i: (0, 0))
    idx = pl.BlockSpec((TOP_K, tn), lambda i: (0, i), memory_space=pltpu.SMEM)
    return pl.pallas_call(
        _combine_kernel,
        out_shape=jax.ShapeDtypeStruct((t, d), jnp.float32),
        grid=(t // tn,),
        in_specs=[
            pl.BlockSpec(memory_space=pltpu.SMEM),
            idx, idx,
            pl.BlockSpec((tn, d), lambda i: (i, 0)),
            pl.BlockSpec((tn, TOP_K), lambda i: (i, 0)),
            pl.BlockSpec(memory_space=pl.ANY),
            pl.BlockSpec((d, ds2), lambda i: (0, 0)),
            pl.BlockSpec((ds2 // 2, d), lambda i: (0, 0)),
            vec, vec,
        ],
        out_specs=pl.BlockSpec((tn, d), lambda i: (i, 0)),
        scratch_shapes=[pltpu.VMEM((TOP_K, tn, d), jnp.float32), pltpu.SemaphoreType.DMA(())],
        compiler_params=_cparams(("arbitrary",)),
        name="combine_ln2",
    )(row_start, eidx, rank, h1, w_tok, ys, sgu, sd, g, b)


def _rel_bucket(rel):
    nb = N_BUCKETS // 2
    max_exact = nb // 2
    ret = jnp.where(rel > 0, nb, 0)
    n = jnp.abs(rel)
    nf = jnp.maximum(n, 1).astype(jnp.float32)
    large = max_exact + (jnp.log(nf / max_exact) / math.log(MAX_DISTANCE / max_exact) * (nb - max_exact)).astype(jnp.int32)
    large = jnp.minimum(large, nb - 1)
    return ret + jnp.where(n < max_exact, n, large)


def _bias_of_rel(rel_bias, rel):
    return rel_bias.astype(jnp.float32)[_rel_bucket(rel)]


def _toeplitz(vec, nrow, ncol, off):
    lo = off - (nrow - 1)
    v = vec[lo:off + ncol]
    p = v.shape[0] + 1
    v = jnp.concatenate([v, v[:1]], axis=0)
    flat = jnp.tile(v, (nrow + 1, 1))
    base = off - lo
    out = flat[base:base + nrow * (p - 1)].reshape(nrow, p - 1, vec.shape[1])
    return out[:, :ncol]


def _bias_tables_a(rel_bias, rows):
    bias_a = rel_bias[:, :A_Q_HEADS]
    span = 2 * TQ_A
    rel = jnp.arange(-span, span + 1, dtype=jnp.int32)
    vec = jnp.where((jnp.abs(rel) <= WINDOW)[:, None], _bias_of_rel(bias_a, rel), NEG)
    band = jnp.transpose(_toeplitz(vec, TQ_A, 3 * TQ_A, span - TQ_A), (2, 0, 1))
    tabs = []
    for base in (0, rows):
        lo = -(N_META + base + rows - 1)
        relm = jnp.arange(lo, N_META, dtype=jnp.int32)
        vm = _bias_of_rel(bias_a, relm)
        tabs.append(jnp.transpose(_toeplitz(vm, rows, N_META, -(N_META + base) - lo), (2, 0, 1)))
    return band, jnp.stack(tabs)


def _bias_tables_b(rel_bias, s):
    bias_b = rel_bias[:, A_Q_HEADS:]
    near = []
    for d in (-1, 0, 1):
        m = jnp.arange(TQ_B + TK_B - 1, dtype=jnp.int32)
        vec = _bias_of_rel(bias_b, TK_B * d + TK_B - 1 - m)
        near.append(_toeplitz(vec, TK_B, TQ_B, TK_B - 1))
    near = jnp.transpose(jnp.stack(near), (3, 0, 1, 2)) * LOG2E
    m = jnp.arange(s + N_META - 1, dtype=jnp.int32)
    vec = _bias_of_rel(bias_b, -1 - m)
    meta = jnp.transpose(_toeplitz(vec, N_META, s, N_META - 1), (2, 0, 1)) * LOG2E
    far = jnp.concatenate([_bias_of_rel(bias_b, jnp.int32(-2 * MAX_DISTANCE)),
                           _bias_of_rel(bias_b, jnp.int32(2 * MAX_DISTANCE))]) * LOG2E
    return near, meta, far


def _prep_w_in(w_in):
    a_w = A_Q_HEADS * HEAD_DIM
    kv = A_KV_HEADS * HEAD_DIM
    bqk = B_HEADS * 2 * HEAD_DIM
    scale = HEAD_DIM ** -0.5
    qa = w_in[:, :a_w] * scale
    ka = w_in[:, a_w:a_w + kv]
    va = w_in[:, a_w + kv:a_w + 2 * kv]
    o = a_w + 2 * kv
    qb = w_in[:, o:o + bqk] * (scale * LOG2E)
    kb = w_in[:, o + bqk:o + 2 * bqk]
    vb = w_in[:, o + 2 * bqk:]

    def dup(w):
        return jnp.concatenate([w[:, g * HEAD_DIM:(g + 1) * HEAD_DIM] for g in range(A_KV_HEADS) for _ in range(2)], axis=1)

    w = jnp.concatenate([qa, dup(ka), dup(va), qb, kb], axis=1).astype(jnp.bfloat16)
    return w, vb.T.astype(jnp.bfloat16)


def _trunk(x, prm):
    bsz, s, d = x.shape
    t = bsz * s
    x2d = x.reshape(t, d)
    proj, vt = _ln_inproj(x2d, prm["ln_emb_g"], prm["ln_emb_b"], prm["w_in"], prm["w_vt"])
    proj3 = proj.reshape(bsz, s, PROJ_COLS)
    sub = 4 if (s // TQ_A) % 4 == 0 else 1
    tab_a, tabm_a = _bias_tables_a(prm["rel_bias"], sub * TQ_A)
    oa = _mixer_a(proj3, prm["proj_meta"], tab_a, tabm_a, prm["sink"])
    near, meta_b, far = _bias_tables_b(prm["rel_bias"], s)
    scal = jnp.concatenate([far, prm["lam"][None]]).astype(jnp.float32)
    ob = _mixer_b(proj3, vt, prm["proj_meta"], prm["vt_meta"], near, meta_b, scal, prm["subln_g"])
    h1 = _outproj_ln1(x2d, oa.reshape(t, -1), ob.reshape(t, -1), prm["w_out"], prm["ln_emb_g"], prm["ln_emb_b"],
                      prm["ln1_g"], prm["ln1_b"])

    eidx, wts, rank, counts = _router(h1, prm["wr_hi"], prm["wr_lo"], prm["router_bias"])
    n_exp = counts.shape[0]
    counts = counts[:, 0].astype(jnp.int32)
    tiles_e = (counts + TM_MOE - 1) // TM_MOE
    tend = jnp.cumsum(tiles_e)
    tstart = tend - tiles_e
    nt = (t * TOP_K) // TM_MOE + n_exp
    tid = jnp.arange(nt, dtype=jnp.int32)
    tile_e = jnp.minimum(jnp.sum((tend[None, :] <= tid[:, None]).astype(jnp.int32), axis=1), n_exp - 1)
    onehot = (tile_e[:, None] == jnp.arange(n_exp, dtype=jnp.int32)[None, :]).astype(jnp.int32)
    left = jnp.sum(onehot * counts[None, :], axis=1) - (tid - jnp.sum(onehot * tstart[None, :], axis=1)) * TM_MOE
    tile_valid = jnp.where(tid < tend[-1], jnp.clip(left, 0, TM_MOE), 0).astype(jnp.int32)
    tile_blk = jnp.minimum(tid, tend[-1] - 1).astype(jnp.int32)
    row_start = (tstart * TM_MOE).astype(jnp.int32)

    xs = _scatter_rows(h1, row_start, eidx, rank, nt * TM_MOE)
    ys = _gmm(xs, prm["w_gu"], prm["w_down"], tile_e.astype(jnp.int32), tile_valid, tile_blk)
    out = _combine(h1, row_start, eidx, rank, wts.T, ys, prm["ws_gu"], prm["ws_down"], prm["ln2_g"], prm["ln2_b"])
    return out.reshape(bsz, s, d)


def kernel(x_prompt, x_sample, meta_tokens, ln_emb_g, ln_emb_b, rel_bias, w_in, attn_sink, lambda_q1, lambda_k1, lambda_q2, lambda_k2, subln_g, w_out, ln1_g, ln1_b, w_router, router_bias, w_gate, w_up, w_down, ws_gate, ws_up, ws_down, ln2_g, ln2_b):
    f32 = jnp.float32
    bf16 = jnp.bfloat16
    l = 0
    row = lambda v: v.reshape(1, -1).astype(f32)
    lam = (jnp.exp(jnp.sum(lambda_q1[l].astype(f32) * lambda_k1[l].astype(f32)))
           - jnp.exp(jnp.sum(lambda_q2[l].astype(f32) * lambda_k2[l].astype(f32))) + LAMBDA_INIT)
    wr_t = w_router[l].astype(f32).T
    wr_hi = wr_t.astype(bf16)
    prm = {
        "ln_emb_g": row(ln_emb_g), "ln_emb_b": row(ln_emb_b),
        "rel_bias": rel_bias,
        "sink": attn_sink[l].astype(f32),
        "lam": lam,
        "subln_g": subln_g[l].astype(f32).reshape(-1, 1),
        "w_out": w_out[l].astype(bf16),
        "ln1_g": row(ln1_g[l]), "ln1_b": row(ln1_b[l]),
        "wr_hi": wr_hi, "wr_lo": (wr_t - wr_hi.astype(f32)).astype(bf16),
        "router_bias": router_bias[l].astype(f32).reshape(-1, 1),
        "w_gu": jnp.concatenate([w_gate[l].astype(bf16), w_up[l].astype(bf16)], axis=-1),
        "w_down": w_down[l].astype(bf16),
        "ws_gu": jnp.concatenate([ws_gate[l], ws_up[l]], axis=-1).astype(bf16),
        "ws_down": ws_down[l].astype(bf16),
        "ln2_g": row(ln2_g[l]), "ln2_b": row(ln2_b[l]),
    }
    prm["w_in"], prm["w_vt"] = _prep_w_in(w_in[l])
    prm["proj_meta"], prm["vt_meta"] = _ln_inproj(meta_tokens.astype(f32), prm["ln_emb_g"], prm["ln_emb_b"], prm["w_in"],
                                                 prm["w_vt"])
    return (_trunk(x_prompt, prm), _trunk(x_sample, prm))
```

```python
import functools
import math

import jax
import jax.numpy as jnp
from jax import lax
from jax.experimental import pallas as pl
from jax.experimental.pallas import tpu as pltpu

N_META = 16
HEAD_DIM = 64
WINDOW = 128
A_Q_HEADS = 8
A_KV_HEADS = 2
B_HEADS = 4
N_BUCKETS = 32
MAX_DISTANCE = 128
TOP_K = 8
N_GROUPS = 8
TOPK_GROUPS = 4
ROUTED_SCALE = 2.5
LN_EPS = 1e-5
DEPTH = 1
ALPHA = (2 * DEPTH) ** 0.25
NEG = -1e30
LAMBDA_INIT = 0.8 - 0.6 * math.exp(-0.3 * 0)

LANES = 128
VMEM_LIMIT = 48 * 1024 * 1024

QA_BLK = 0
KA_BLK = 4
VA_BLK = 6
QB_BLK = 8
KB_BLK = 12
PROJ_COLS = 16 * LANES
VT_ROWS = LANES + 16
LOG2E = 1.4426950408889634

TQ_A = 128
TQ_B = 512
TK_B = 512
TM_MOE = 256
SLAB = 8
KEY_SHIFT = 20
IDX_WIN = TM_MOE // LANES + 1
ISSUE_UNROLL = 8


def _cparams(sem):
    return pltpu.CompilerParams(dimension_semantics=sem, vmem_limit_bytes=VMEM_LIMIT)


def _layernorm_f32(x, g, b):
    mu = jnp.mean(x, axis=-1, keepdims=True)
    xc = x - mu
    var = jnp.mean(xc * xc, axis=-1, keepdims=True)
    return xc * lax.rsqrt(var + LN_EPS) * g + b


def _dot_nt(a, b):
    return lax.dot_general(a, b, (((1,), (1,)), ((), ())), preferred_element_type=jnp.float32)


def _dot(a, b):
    return jnp.dot(a, b, preferred_element_type=jnp.float32)


def _ln_inproj_kernel(x_ref, g_ref, b_ref, w_ref, wvt_ref, o_ref, vt_ref):
    h = _layernorm_f32(x_ref[...], g_ref[...], b_ref[...]).astype(jnp.bfloat16)
    o_ref[...] = _dot(h, w_ref[...]).astype(o_ref.dtype)
    vt = _dot_nt(wvt_ref[...], h).astype(vt_ref.dtype)
    ones = jnp.ones((VT_ROWS - LANES, vt.shape[1]), vt_ref.dtype)
    for hd in range(B_HEADS):
        vt_ref[hd * VT_ROWS:hd * VT_ROWS + LANES, :] = vt[hd * LANES:(hd + 1) * LANES, :]
        vt_ref[hd * VT_ROWS + LANES:(hd + 1) * VT_ROWS, :] = ones


def _ln_inproj(x2d, g, b, w, wvt):
    t, d = x2d.shape
    n = w.shape[1]
    tm = min(512, t)
    return pl.pallas_call(
        _ln_inproj_kernel,
        out_shape=(jax.ShapeDtypeStruct((t, n), jnp.bfloat16),
                   jax.ShapeDtypeStruct((B_HEADS * VT_ROWS, t), jnp.bfloat16)),
        grid=(t // tm,),
        in_specs=[
            pl.BlockSpec((tm, d), lambda i: (i, 0)),
            pl.BlockSpec((1, d), lambda i: (0, 0)),
            pl.BlockSpec((1, d), lambda i: (0, 0)),
            pl.BlockSpec((d, n), lambda i: (0, 0)),
            pl.BlockSpec((B_HEADS * LANES, d), lambda i: (0, 0)),
        ],
        out_specs=(pl.BlockSpec((tm, n), lambda i: (i, 0)),
                   pl.BlockSpec((B_HEADS * VT_ROWS, tm), lambda i: (0, i))),
        compiler_params=_cparams(("parallel",)),
        name="ln_inproj",
    )(x2d, g, b, w, wvt)


def _mixer_a_kernel(sink_ref, q_ref, k_ref, v_ref, km_ref, vm_ref, tab_ref, tabm_ref, o_ref, *, nblk, sub):
    i = pl.program_id(1)
    lane = lax.broadcasted_iota(jnp.int32, (1, LANES), 1)
    lo = lane < HEAD_DIM
    for j in range(sub):
        gi = i * sub + j
        sp = pl.multiple_of(jnp.maximum(gi - 1, 0) * TQ_A, TQ_A)
        sc = pl.multiple_of(gi * TQ_A, TQ_A)
        sn = pl.multiple_of(jnp.minimum(gi + 1, nblk - 1) * TQ_A, TQ_A)
        neg_p = jnp.where(gi > 0, 0.0, NEG).astype(jnp.float32)
        neg_n = jnp.where(gi < nblk - 1, 0.0, NEG).astype(jnp.float32)
        for hp in range(A_Q_HEADS // 2):
            g = hp // 2
            gs = slice(g * LANES, (g + 1) * LANES)
            qc = q_ref[0, j * TQ_A:(j + 1) * TQ_A, hp * LANES:(hp + 1) * LANES]
            kpieces = (km_ref[:, gs], k_ref[0, pl.ds(sp, TQ_A), gs], k_ref[0, pl.ds(sc, TQ_A), gs],
                       k_ref[0, pl.ds(sn, TQ_A), gs])
            vpieces = (vm_ref[:, gs], v_ref[0, pl.ds(sp, TQ_A), gs], v_ref[0, pl.ds(sc, TQ_A), gs],
                       v_ref[0, pl.ds(sn, TQ_A), gs])
            o_pair = jnp.zeros((TQ_A, LANES), jnp.float32)
            for half in range(2):
                h = 2 * hp + half
                msk = lo if half == 0 else jnp.logical_not(lo)
                qm = jnp.where(msk, qc, jnp.zeros_like(qc))
                sink = sink_ref[h]
                s = [
                    _dot_nt(qm, kpieces[0]) + tabm_ref[0, h, j * TQ_A:(j + 1) * TQ_A, :],
                    _dot_nt(qm, kpieces[1]) + (tab_ref[h, :, 0:TQ_A] + neg_p),
                    _dot_nt(qm, kpieces[2]) + tab_ref[h, :, TQ_A:2 * TQ_A],
                    _dot_nt(qm, kpieces[3]) + (tab_ref[h, :, 2 * TQ_A:3 * TQ_A] + neg_n),
                ]
                m = jnp.maximum(jnp.maximum(jnp.max(s[0], axis=-1, keepdims=True), jnp.max(s[1], axis=-1, keepdims=True)),
                                jnp.maximum(jnp.max(s[2], axis=-1, keepdims=True), jnp.max(s[3], axis=-1, keepdims=True)))
                m = jnp.maximum(m, sink)
                l = jnp.exp(sink - m)
                o_h = jnp.zeros((TQ_A, LANES), jnp.float32)
                for sx, vx in zip(s, vpieces):
                    p = jnp.exp(sx - m)
                    l = l + jnp.sum(p, axis=-1, keepdims=True)
                    vmk = jnp.where(msk, vx, jnp.zeros_like(vx))
                    o_h = o_h + _dot(p.astype(jnp.bfloat16), vmk)
                o_pair = o_pair + o_h / l
            o_ref[0, j * TQ_A:(j + 1) * TQ_A, hp * LANES:(hp + 1) * LANES] = o_pair.astype(o_ref.dtype)


def _mixer_a(proj3, proj_meta, tab, tabm, sink):
    bsz, s, _ = proj3.shape
    nblk = s // TQ_A
    sub = 4 if nblk % 4 == 0 else 1
    nq = nblk // sub
    tq = sub * TQ_A
    kern = functools.partial(_mixer_a_kernel, nblk=nblk, sub=sub)
    return pl.pallas_call(
        kern,
        out_shape=jax.ShapeDtypeStruct((bsz, s, A_Q_HEADS * HEAD_DIM), jnp.bfloat16),
        grid=(bsz, nq),
        in_specs=[
            pl.BlockSpec(memory_space=pltpu.SMEM),
            pl.BlockSpec((1, tq, 4 * LANES), lambda b, i: (b, i, QA_BLK // 4)),
            pl.BlockSpec((1, s, 2 * LANES), lambda b, i: (b, 0, KA_BLK // 2)),
            pl.BlockSpec((1, s, 2 * LANES), lambda b, i: (b, 0, VA_BLK // 2)),
            pl.BlockSpec((N_META, 2 * LANES), lambda b, i: (0, KA_BLK // 2)),
            pl.BlockSpec((N_META, 2 * LANES), lambda b, i: (0, VA_BLK // 2)),
            pl.BlockSpec((A_Q_HEADS, TQ_A, 3 * TQ_A), lambda b, i: (0, 0, 0)),
            pl.BlockSpec((1, A_Q_HEADS, tq, N_META), lambda b, i: (jnp.minimum(i, 1), 0, 0, 0)),
        ],
        out_specs=pl.BlockSpec((1, tq, 4 * LANES), lambda b, i: (b, i, 0)),
        compiler_params=_cparams(("parallel", "arbitrary")),
        name="mixer_a",
    )(sink, proj3, proj3, proj3, proj_meta, proj_meta, tab, tabm)


def _mixer_b_kernel(sc_ref, q_ref, k_ref, vt_ref, km_ref, vtm_ref, tab_ref, tabm_ref, g_ref, o_ref,
                    m_ref, acc_ref, *, nk):
    h = pl.program_id(1)
    i = pl.program_id(2)
    lane = lax.broadcasted_iota(jnp.int32, (1, LANES), 1)
    lo = lane < HEAD_DIM
    q = q_ref[0]
    qs = (jnp.where(lo, q, jnp.zeros_like(q)), jnp.where(lo, jnp.zeros_like(q), q))
    c_left = sc_ref[h]
    c_right = sc_ref[B_HEADS + h]
    lam = sc_ref[2 * B_HEADS]

    km = km_ref[...]
    vtm = vtm_ref[...]
    for c in range(2):
        s = _dot_nt(km, qs[c]) + tabm_ref[0]
        m = jnp.max(s, axis=0, keepdims=True)
        p = jnp.exp2(s - m)
        m_ref[c] = m
        acc_ref[c] = _dot(vtm, p.astype(jnp.bfloat16))

    def update(j, bias, const):
        start = pl.multiple_of(j * TK_B, TK_B)
        kj = k_ref[0, pl.ds(start, TK_B), :]
        vtj = vt_ref[:, pl.ds(start, TK_B)]
        for c in range(2):
            s = _dot_nt(kj, qs[c])
            if bias is not None:
                s = s + bias
            m_prev = m_ref[c]
            m_cur = jnp.max(s, axis=0, keepdims=True)
            if const is not None:
                m_cur = m_cur + const
            m_new = jnp.maximum(m_prev, m_cur)
            shift = m_new if const is None else m_new - const
            a = jnp.exp2(m_prev - m_new)
            p = jnp.exp2(s - shift)
            m_ref[c] = m_new
            acc_ref[c] = a * acc_ref[c] + _dot(vtj, p.astype(jnp.bfloat16))

    def far_left(j, carry):
        update(j, None, c_left)
        return carry

    def far_right(j, carry):
        update(j, None, c_right)
        return carry

    lax.fori_loop(0, jnp.maximum(i - 1, 0), far_left, 0)
    for d in range(3):
        j = i + d - 1

        @pl.when(jnp.logical_and(j >= 0, j < nk))
        def _():
            update(j, tab_ref[0, d], None)

    lax.fori_loop(jnp.minimum(i + 2, nk), nk, far_right, 0)

    o0 = acc_ref[0, 0:LANES, :] / acc_ref[0, LANES:LANES + 1, :]
    o1 = acc_ref[1, 0:LANES, :] / acc_ref[1, LANES:LANES + 1, :]
    o = o0 - lam * o1
    ms = jnp.mean(o * o, axis=0, keepdims=True)
    o = o * lax.rsqrt(ms + LN_EPS) * (g_ref[...] * (1.0 - LAMBDA_INIT))
    o_ref[0] = o.T.astype(o_ref.dtype)


def _mixer_b(proj3, vt, proj_meta, vt_meta, tab, tabm, scal, subln_g):
    bsz, s, _ = proj3.shape
    nq = s // TQ_B
    nk = s // TK_B
    kern = functools.partial(_mixer_b_kernel, nk=nk)
    return pl.pallas_call(
        kern,
        out_shape=jax.ShapeDtypeStruct((bsz, s, B_HEADS * LANES), jnp.bfloat16),
        grid=(bsz, B_HEADS, nq),
        in_specs=[
            pl.BlockSpec(memory_space=pltpu.SMEM),
            pl.BlockSpec((1, TQ_B, LANES), lambda b, h, i: (b, i, QB_BLK + h)),
            pl.BlockSpec((1, s, LANES), lambda b, h, i: (b, 0, KB_BLK + h)),
            pl.BlockSpec((VT_ROWS, s), lambda b, h, i: (h, b)),
            pl.BlockSpec((N_META, LANES), lambda b, h, i: (0, KB_BLK + h)),
            pl.BlockSpec((VT_ROWS, N_META), lambda b, h, i: (h, 0)),
            pl.BlockSpec((1, 3, TK_B, TQ_B), lambda b, h, i: (h, 0, 0, 0)),
            pl.BlockSpec((1, N_META, TQ_B), lambda b, h, i: (h, 0, i)),
            pl.BlockSpec((LANES, 1), lambda b, h, i: (0, 0)),
        ],
        out_specs=pl.BlockSpec((1, TQ_B, LANES), lambda b, h, i: (b, i, h)),
        scratch_shapes=[
            pltpu.VMEM((2, 1, TQ_B), jnp.float32),
            pltpu.VMEM((2, VT_ROWS, TQ_B), jnp.float32),
        ],
        compiler_params=_cparams(("parallel", "parallel", "arbitrary")),
        name="mixer_b",
    )(scal, proj3, proj3, vt, proj_meta, vt_meta, tab, tabm, subln_g)


def _slab_load(ref, lead, rows):
    return jnp.concatenate([ref[lead + (pl.ds(s, rows, stride=SLAB), slice(None))] for s in range(SLAB)], axis=1)


def _slab_store(ref, lead, val):
    rows = val.shape[0]
    for s in range(SLAB):
        ref[lead + (pl.ds(s, rows, stride=SLAB), slice(None))] = val[:, s * LANES:(s + 1) * LANES]


def _outproj_kernel(x_ref, oa_ref, ob_ref, w_ref, eg_ref, eb_ref, g_ref, b_ref, o_ref, os_ref):
    h0 = _layernorm_f32(x_ref[...], eg_ref[...], eb_ref[...])
    half = oa_ref.shape[1]
    mix = _dot(oa_ref[...], w_ref[0:half, :]) + _dot(ob_ref[...], w_ref[half:, :])
    h1 = _layernorm_f32(ALPHA * h0 + mix, g_ref[...], b_ref[...])
    o_ref[...] = h1
    _slab_store(os_ref, (), h1)


def _outproj_ln1(x2d, oa, ob, w_out, eg, eb, g, b):
    t, d = x2d.shape
    tm = min(512, t)
    wa = oa.shape[1]
    wb = ob.shape[1]
    vec = pl.BlockSpec((1, d), lambda i: (0, 0))
    return pl.pallas_call(
        _outproj_kernel,
        out_shape=(jax.ShapeDtypeStruct((t, d), jnp.float32),
                   jax.ShapeDtypeStruct((t * SLAB, LANES), jnp.float32)),
        grid=(t // tm,),
        in_specs=[
            pl.BlockSpec((tm, d), lambda i: (i, 0)),
            pl.BlockSpec((tm, wa), lambda i: (i, 0)),
            pl.BlockSpec((tm, wb), lambda i: (i, 0)),
            pl.BlockSpec((wa + wb, d), lambda i: (0, 0)),
            vec, vec, vec, vec,
        ],
        out_specs=(pl.BlockSpec((tm, d), lambda i: (i, 0)),
                   pl.BlockSpec((tm * SLAB, LANES), lambda i: (i, 0))),
        compiler_params=_cparams(("parallel",)),
        name="outproj_ln1",
    )(x2d, oa, ob, w_out, eg, eb, g, b)


def _router_kernel(h_ref, wh_ref, wl_ref, rb_ref, e_ref, w_ref, cnt_ref, carry_ref, *, n_exp):
    i = pl.program_id(0)
    tn = h_ref.shape[0]
    gsz = n_exp // N_GROUPS

    @pl.when(i == 0)
    def _():
        carry_ref[...] = jnp.zeros_like(carry_ref)

    x = h_ref[...]
    xh = x.astype(jnp.bfloat16)
    xl = (x - xh.astype(jnp.float32)).astype(jnp.bfloat16)
    logits = _dot_nt(wh_ref[...], xh) + (_dot_nt(wh_ref[...], xl) + _dot_nt(wl_ref[...], xh))
    scores = 1.0 / (1.0 + jnp.exp(-logits))
    biased = scores + rb_ref[...]

    g3 = biased.reshape(N_GROUPS, gsz, tn)
    it3 = lax.broadcasted_iota(jnp.int32, (N_GROUPS, gsz, tn), 1)
    mx1 = jnp.max(g3, axis=1, keepdims=True)
    first = jnp.min(jnp.where(g3 == mx1, it3, gsz), axis=1, keepdims=True)
    mx2 = jnp.max(jnp.where(it3 == first, -jnp.inf, g3), axis=1, keepdims=True)
    gscore = (mx1 + mx2).reshape(N_GROUPS, tn)

    itg = lax.broadcasted_iota(jnp.int32, (N_GROUPS, tn), 0)
    gsel = jnp.zeros((N_GROUPS, tn), jnp.bool_)
    cur = gscore
    for _ in range(TOPK_GROUPS):
        mx = jnp.max(cur, axis=0, keepdims=True)
        fi = jnp.min(jnp.where(cur == mx, itg, N_GROUPS), axis=0, keepdims=True)
        hit = itg == fi
        gsel = jnp.logical_or(gsel, hit)
        cur = jnp.where(hit, -jnp.inf, cur)
    emask = jnp.broadcast_to(gsel.reshape(N_GROUPS, 1, tn), (N_GROUPS, gsz, tn)).reshape(n_exp, tn)
    cur = jnp.where(emask, biased, NEG)

    ite = lax.broadcasted_iota(jnp.int32, (n_exp, tn), 0)
    hits = []
    eidx = []
    wsel = []
    for _ in range(TOP_K):
        mx = jnp.max(cur, axis=0, keepdims=True)
        fi = jnp.min(jnp.where(cur == mx, ite, n_exp), axis=0, keepdims=True)
        hit = ite == fi
        hits.append(hit)
        eidx.append(fi)
        wsel.append(jnp.sum(jnp.where(hit, scores, 0.0), axis=0, keepdims=True))
        cur = jnp.where(hit, -jnp.inf, cur)
    sel = hits[0]
    for hit in hits[1:]:
        sel = jnp.logical_or(sel, hit)
    self32 = jnp.where(sel, 1.0, 0.0)

    carry_ref[...] = carry_ref[...] + jnp.sum(self32, axis=1, keepdims=True)
    cnt_ref[...] = carry_ref[...]

    wcat = jnp.concatenate(wsel, axis=0)
    wcat = wcat / jnp.sum(wcat, axis=0, keepdims=True) * ROUTED_SCALE
    tok = i * tn + lax.broadcasted_iota(jnp.int32, (TOP_K, tn), 1)
    slot = lax.broadcasted_iota(jnp.int32, (TOP_K, tn), 0)
    e_ref[...] = jnp.concatenate(eidx, axis=0) * (1 << KEY_SHIFT) + (tok * TOP_K + slot)
    w_ref[...] = wcat


def _router(h1, wr_hi, wr_lo, rbias):
    t, d = h1.shape
    n_exp = wr_hi.shape[0]
    tn = min(512, t)
    kern = functools.partial(_router_kernel, n_exp=n_exp)
    row = pl.BlockSpec((TOP_K, tn), lambda i: (0, i))
    return pl.pallas_call(
        kern,
        out_shape=(
            jax.ShapeDtypeStruct((TOP_K, t), jnp.int32),
            jax.ShapeDtypeStruct((TOP_K, t), jnp.float32),
            jax.ShapeDtypeStruct((n_exp, 1), jnp.float32),
        ),
        grid=(t // tn,),
        in_specs=[
            pl.BlockSpec((tn, d), lambda i: (i, 0)),
            pl.BlockSpec((n_exp, d), lambda i: (0, 0)),
            pl.BlockSpec((n_exp, d), lambda i: (0, 0)),
            pl.BlockSpec((n_exp, 1), lambda i: (0, 0)),
        ],
        out_specs=(row, row, pl.BlockSpec((n_exp, 1), lambda i: (0, 0))),
        scratch_shapes=[pltpu.VMEM((n_exp, 1), jnp.float32)],
        compiler_params=_cparams(("arbitrary",)),
        name="router",
    )(h1, wr_hi, wr_lo, rbias)


def _experts_kernel(te_ref, tv_ref, tu_ref, na_ref, f0, f1, f2, c0, c1, c2, n0, n1, n2, h_hbm, wgu_ref, wd_ref,
                    y_hbm, xbuf, ybuf, gsem, ssem):
    i = pl.program_id(0)
    nt = pl.num_programs(0)
    nv = tv_ref[i]
    first = (f0, f1, f2)
    cur = (c0, c1, c2)
    nxt = (n0, n1, n2)

    def for_entries(wins, off, n, fn):
        for r, win in enumerate(wins):
            def one(j, r=r, win=win):
                fn(j, win[0, 0, off + j - r * LANES])

            def group(g, lo):
                for u in range(ISSUE_UNROLL):
                    one(lo + g * ISSUE_UNROLL + u)
                return lo

            def single(j, carry):
                one(j)
                return carry

            lo = jnp.clip(r * LANES - off, 0, n)
            hi = jnp.clip((r + 1) * LANES - off, 0, n)
            ngrp = (hi - lo) // ISSUE_UNROLL
            lax.fori_loop(0, ngrp, group, lo)
            lax.fori_loop(lo + ngrp * ISSUE_UNROLL, hi, single, 0)

    def gather_copy(slot, j, src, rows=1):
        src = pl.multiple_of(src, SLAB)
        return pltpu.make_async_copy(h_hbm.at[pl.ds(src, rows * SLAB), :],
                                     xbuf.at[slot, pl.ds(pl.multiple_of(j * SLAB, SLAB), rows * SLAB), :],
                                     gsem.at[slot])

    def scatter_copy(slot, j, dst, rows=1):
        dst = pl.multiple_of(dst, SLAB)
        return pltpu.make_async_copy(ybuf.at[slot, pl.ds(pl.multiple_of(j * SLAB, SLAB), rows * SLAB), :],
                                     y_hbm.at[pl.ds(dst, rows * SLAB), :], ssem.at[slot])

    def wait_rows(copy_fn, slot, n):
        rows = TM_MOE
        while rows >= 1:
            @pl.when((n & rows) != 0)
            def _(rows=rows):
                copy_fn(slot, 0, 0, rows).wait()
            rows //= 2

    def step(slot):
        other = 1 - slot

        @pl.when(i == 0)
        def _():
            xbuf[...] = jnp.zeros_like(xbuf)
            for_entries(first, 0, nv, lambda j, f: gather_copy(0, j, f).start())

        inext = jnp.minimum(i + 1, nt - 1)
        nv_next = jnp.where(i + 1 < nt, tv_ref[inext], 0)
        for_entries(nxt, tu_ref[inext] & (LANES - 1), nv_next, lambda j, f: gather_copy(other, j, f).start())

        wait_rows(gather_copy, slot, nv)
        x = _slab_load(xbuf, (slot,), TM_MOE).astype(jnp.bfloat16)
        gu = _dot(x, wgu_ref[0])
        de = gu.shape[1] // 2
        gate = gu[:, :de]
        hid = (gate / (1.0 + jnp.exp(-gate))) * gu[:, de:]
        y = _dot(hid.astype(jnp.bfloat16), wd_ref[0])

        wait_rows(scatter_copy, slot, jnp.where(i >= 2, tv_ref[jnp.maximum(i - 2, 0)], 0))
        _slab_store(ybuf, (slot,), y)
        for_entries(cur, tu_ref[i] & (LANES - 1), nv, lambda j, f: scatter_copy(slot, j, f).start())

        @pl.when(i == na_ref[0] - 1)
        def _():
            wait_rows(scatter_copy, other, jnp.where(i >= 1, tv_ref[jnp.maximum(i - 1, 0)], 0))
            wait_rows(scatter_copy, slot, nv)

    for slot in range(2):
        @pl.when(jnp.logical_and(nv > 0, (i & 1) == slot))
        def _(slot=slot):
            step(slot)


def _experts(h_slab, src3d, dst3d, wgu, wd, tile_e, tile_valid, tile_u, n_active, n_tok):
    nt = tile_e.shape[0]
    d = wgu.shape[1]
    de2 = wgu.shape[2]
    last = nt - 1

    def win(j, shift):
        return pl.BlockSpec((1, 1, LANES),
                            lambda i, te, tv, tu, na: (tu[jnp.minimum(i + shift, last)] // LANES + j, 0, 0),
                            memory_space=pltpu.SMEM)

    def win0(j):
        return pl.BlockSpec((1, 1, LANES), lambda i, te, tv, tu, na: (j, 0, 0), memory_space=pltpu.SMEM)

    wins = [win0(j) for j in range(IDX_WIN)] + [win(j, 0) for j in range(IDX_WIN)] + [win(j, 1) for j in range(IDX_WIN)]
    return pl.pallas_call(
        _experts_kernel,
        out_shape=jax.ShapeDtypeStruct((TOP_K * n_tok * SLAB, LANES), jnp.float32),
        grid_spec=pltpu.PrefetchScalarGridSpec(
            num_scalar_prefetch=4,
            grid=(nt,),
            in_specs=wins + [
                pl.BlockSpec(memory_space=pl.ANY),
                pl.BlockSpec((1, d, de2), lambda i, te, tv, tu, na: (te[i], 0, 0)),
                pl.BlockSpec((1, de2 // 2, d), lambda i, te, tv, tu, na: (te[i], 0, 0)),
            ],
            out_specs=pl.BlockSpec(memory_space=pl.ANY),
            scratch_shapes=[
                pltpu.VMEM((2, TM_MOE * SLAB, LANES), jnp.float32),
                pltpu.VMEM((2, TM_MOE * SLAB, LANES), jnp.float32),
                pltpu.SemaphoreType.DMA((2,)),
                pltpu.SemaphoreType.DMA((2,)),
            ],
        ),
        compiler_params=_cparams(("arbitrary",)),
        name="experts",
    )(tile_e, tile_valid, tile_u, n_active, *([src3d] * IDX_WIN), *([dst3d] * IDX_WIN), *([src3d] * IDX_WIN),
      h_slab, wgu, wd)


def _combine_kernel(h_ref, w_ref, y_ref, sgu_ref, sd_ref, g_ref, b_ref, o_ref):
    tn = h_ref.shape[0]
    h = h_ref[...]
    gu = _dot(h.astype(jnp.bfloat16), sgu_ref[...])
    ds = gu.shape[1] // 2
    gate = gu[:, :ds]
    hid = (gate / (1.0 + jnp.exp(-gate))) * gu[:, ds:]
    shared = _dot(hid.astype(jnp.bfloat16), sd_ref[...])

    w = w_ref[...]
    routed = _slab_load(y_ref, (0,), tn) * w[:, 0:1]
    for k in range(1, TOP_K):
        routed = routed + _slab_load(y_ref, (k,), tn) * w[:, k:k + 1]
    o_ref[...] = _layernorm_f32(ALPHA * h + (routed + shared), g_ref[...], b_ref[...])


def _combine(h1, w_tok, y_slab, sgu, sd, g, b):
    t, d = h1.shape
    tn = min(256, t)
    ds2 = sgu.shape[1]
    vec = pl.BlockSpec((1, d), lambda i: (0, 0))
    return pl.pallas_call(
        _combine_kernel,
        out_shape=jax.ShapeDtypeStruct((t, d), jnp.float32),
        grid=(t // tn,),
        in_specs=[
            pl.BlockSpec((tn, d), lambda i: (i, 0)),
            pl.BlockSpec((tn, TOP_K), lambda i: (i, 0)),
            pl.BlockSpec((TOP_K, tn * SLAB, LANES), lambda i: (0, i, 0)),
            pl.BlockSpec((d, ds2), lambda i: (0, 0)),
            pl.BlockSpec((ds2 // 2, d), lambda i: (0, 0)),
            vec, vec,
        ],
        out_specs=pl.BlockSpec((tn, d), lambda i: (i, 0)),
        compiler_params=_cparams(("parallel",)),
        name="combine_ln2",
    )(h1, w_tok, y_slab.reshape(TOP_K, t * SLAB, LANES), sgu, sd, g, b)


def _rel_bucket(rel):
    nb = N_BUCKETS // 2
    max_exact = nb // 2
    ret = jnp.where(rel > 0, nb, 0)
    n = jnp.abs(rel)
    nf = jnp.maximum(n, 1).astype(jnp.float32)
    large = max_exact + (jnp.log(nf / max_exact) / math.log(MAX_DISTANCE / max_exact) * (nb - max_exact)).astype(jnp.int32)
    large = jnp.minimum(large, nb - 1)
    return ret + jnp.where(n < max_exact, n, large)


def _bias_of_rel(rel_bias, rel):
    return rel_bias.astype(jnp.float32)[_rel_bucket(rel)]


def _toeplitz(vec, nrow, ncol, off):
    lo = off - (nrow - 1)
    v = vec[lo:off + ncol]
    p = v.shape[0] + 1
    v = jnp.concatenate([v, v[:1]], axis=0)
    flat = jnp.tile(v, (nrow + 1, 1))
    base = off - lo
    out = flat[base:base + nrow * (p - 1)].reshape(nrow, p - 1, vec.shape[1])
    return out[:, :ncol]


def _bias_tables_a(rel_bias, rows):
    bias_a = rel_bias[:, :A_Q_HEADS]
    span = 2 * TQ_A
    rel = jnp.arange(-span, span + 1, dtype=jnp.int32)
    vec = jnp.where((jnp.abs(rel) <= WINDOW)[:, None], _bias_of_rel(bias_a, rel), NEG)
    band = jnp.transpose(_toeplitz(vec, TQ_A, 3 * TQ_A, span - TQ_A), (2, 0, 1))
    tabs = []
    for base in (0, rows):
        lo = -(N_META + base + rows - 1)
        relm = jnp.arange(lo, N_META, dtype=jnp.int32)
        vm = _bias_of_rel(bias_a, relm)
        tabs.append(jnp.transpose(_toeplitz(vm, rows, N_META, -(N_META + base) - lo), (2, 0, 1)))
    return band, jnp.stack(tabs)


def _bias_tables_b(rel_bias, s):
    bias_b = rel_bias[:, A_Q_HEADS:]
    near = []
    for d in (-1, 0, 1):
        m = jnp.arange(TQ_B + TK_B - 1, dtype=jnp.int32)
        vec = _bias_of_rel(bias_b, TK_B * d + TK_B - 1 - m)
        near.append(_toeplitz(vec, TK_B, TQ_B, TK_B - 1))
    near = jnp.transpose(jnp.stack(near), (3, 0, 1, 2)) * LOG2E
    m = jnp.arange(s + N_META - 1, dtype=jnp.int32)
    vec = _bias_of_rel(bias_b, -1 - m)
    meta = jnp.transpose(_toeplitz(vec, N_META, s, N_META - 1), (2, 0, 1)) * LOG2E
    far = jnp.concatenate([_bias_of_rel(bias_b, jnp.int32(-2 * MAX_DISTANCE)),
                           _bias_of_rel(bias_b, jnp.int32(2 * MAX_DISTANCE))]) * LOG2E
    return near, meta, far


def _prep_w_in(w_in):
    a_w = A_Q_HEADS * HEAD_DIM
    kv = A_KV_HEADS * HEAD_DIM
    bqk = B_HEADS * 2 * HEAD_DIM
    scale = HEAD_DIM ** -0.5
    qa = w_in[:, :a_w] * scale
    ka = w_in[:, a_w:a_w + kv]
    va = w_in[:, a_w + kv:a_w + 2 * kv]
    o = a_w + 2 * kv
    qb = w_in[:, o:o + bqk] * (scale * LOG2E)
    kb = w_in[:, o + bqk:o + 2 * bqk]
    vb = w_in[:, o + 2 * bqk:]

    def dup(w):
        return jnp.concatenate([w[:, g * HEAD_DIM:(g + 1) * HEAD_DIM] for g in range(A_KV_HEADS) for _ in range(2)], axis=1)

    w = jnp.concatenate([qa, dup(ka), dup(va), qb, kb], axis=1).astype(jnp.bfloat16)
    return w, vb.T.astype(jnp.bfloat16)


def _trunk(x, prm):
    bsz, s, d = x.shape
    t = bsz * s
    x2d = x.reshape(t, d)
    proj, vt = _ln_inproj(x2d, prm["ln_emb_g"], prm["ln_emb_b"], prm["w_in"], prm["w_vt"])
    proj3 = proj.reshape(bsz, s, PROJ_COLS)
    sub = 4 if (s // TQ_A) % 4 == 0 else 1
    tab_a, tabm_a = _bias_tables_a(prm["rel_bias"], sub * TQ_A)
    oa = _mixer_a(proj3, prm["proj_meta"], tab_a, tabm_a, prm["sink"])
    near, meta_b, far = _bias_tables_b(prm["rel_bias"], s)
    scal = jnp.concatenate([far, prm["lam"][None]]).astype(jnp.float32)
    ob = _mixer_b(proj3, vt, prm["proj_meta"], prm["vt_meta"], near, meta_b, scal, prm["subln_g"])
    h1, h1_slab = _outproj_ln1(x2d, oa.reshape(t, -1), ob.reshape(t, -1), prm["w_out"], prm["ln_emb_g"],
                               prm["ln_emb_b"], prm["ln1_g"], prm["ln1_b"])

    keys, wts, counts = _router(h1, prm["wr_hi"], prm["wr_lo"], prm["router_bias"])
    n_exp = counts.shape[0]
    n_asg = t * TOP_K
    assert n_asg <= (1 << KEY_SHIFT) and n_asg % LANES == 0
    order = jnp.sort(keys.reshape(n_asg)) & ((1 << KEY_SHIFT) - 1)
    order = jnp.concatenate([order, jnp.zeros((IDX_WIN * LANES,), jnp.int32)]).reshape(-1, 1, LANES)
    tok = order >> 3
    src3d = tok * SLAB
    dst3d = ((order & (TOP_K - 1)) * t + tok) * SLAB
    counts = counts[:, 0].astype(jnp.int32)
    tiles_e = (counts + TM_MOE - 1) // TM_MOE
    tend = jnp.cumsum(tiles_e)
    tstart = tend - tiles_e
    ustart = jnp.cumsum(counts) - counts
    nt = n_asg // TM_MOE + n_exp
    tid = jnp.arange(nt, dtype=jnp.int32)
    tile_e = jnp.minimum(jnp.sum((tend[None, :] <= tid[:, None]).astype(jnp.int32), axis=1), n_exp - 1)
    onehot = (tile_e[:, None] == jnp.arange(n_exp, dtype=jnp.int32)[None, :]).astype(jnp.int32)
    in_e = (tid - jnp.sum(onehot * tstart[None, :], axis=1)) * TM_MOE
    active = tid < tend[-1]
    tile_valid = jnp.where(active, jnp.clip(jnp.sum(onehot * counts[None, :], axis=1) - in_e, 0, TM_MOE), 0)
    tile_u = jnp.where(active, jnp.sum(onehot * ustart[None, :], axis=1) + in_e, 0)

    y_slab = _experts(h1_slab, src3d, dst3d, prm["w_gu"], prm["w_down"], tile_e.astype(jnp.int32),
                      tile_valid.astype(jnp.int32), tile_u.astype(jnp.int32), tend[-1:].astype(jnp.int32), t)
    out = _combine(h1, wts.T, y_slab, prm["ws_gu"], prm["ws_down"], prm["ln2_g"], prm["ln2_b"])
    return out.reshape(bsz, s, d)


def kernel(x_prompt, x_sample, meta_tokens, ln_emb_g, ln_emb_b, rel_bias, w_in, attn_sink, lambda_q1, lambda_k1, lambda_q2, lambda_k2, subln_g, w_out, ln1_g, ln1_b, w_router, router_bias, w_gate, w_up, w_down, ws_gate, ws_up, ws_down, ln2_g, ln2_b):
    f32 = jnp.float32
    bf16 = jnp.bfloat16
    l = 0
    row = lambda v: v.reshape(1, -1).astype(f32)
    lam = (jnp.exp(jnp.sum(lambda_q1[l].astype(f32) * lambda_k1[l].astype(f32)))
           - jnp.exp(jnp.sum(lambda_q2[l].astype(f32) * lambda_k2[l].astype(f32))) + LAMBDA_INIT)
    wr_t = w_router[l].astype(f32).T
    wr_hi = wr_t.astype(bf16)
    prm = {
        "ln_emb_g": row(ln_emb_g), "ln_emb_b": row(ln_emb_b),
        "rel_bias": rel_bias,
        "sink": attn_sink[l].astype(f32),
        "lam": lam,
        "subln_g": subln_g[l].astype(f32).reshape(-1, 1),
        "w_out": w_out[l].astype(bf16),
        "ln1_g": row(ln1_g[l]), "ln1_b": row(ln1_b[l]),
        "wr_hi": wr_hi, "wr_lo": (wr_t - wr_hi.astype(f32)).astype(bf16),
        "router_bias": router_bias[l].astype(f32).reshape(-1, 1),
        "w_gu": jnp.concatenate([w_gate[l].astype(bf16), w_up[l].astype(bf16)], axis=-1),
        "w_down": w_down[l].astype(bf16),
        "ws_gu": jnp.concatenate([ws_gate[l], ws_up[l]], axis=-1).astype(bf16),
        "ws_down": ws_down[l].astype(bf16),
        "ln2_g": row(ln2_g[l]), "ln2_b": row(ln2_b[l]),
    }
    prm["w_in"], prm["w_vt"] = _prep_w_in(w_in[l])
    prm["proj_meta"], prm["vt_meta"] = _ln_inproj(meta_tokens.astype(f32), prm["ln_emb_g"], prm["ln_emb_b"], prm["w_in"],
                                                 prm["w_vt"])
    return (_trunk(x_prompt, prm), _trunk(x_sample, prm))
```

```python
import functools
import math

import jax
import jax.numpy as jnp
from jax import lax
from jax.experimental import pallas as pl
from jax.experimental.pallas import tpu as pltpu

N_META = 16
HEAD_DIM = 64
WINDOW = 128
A_Q_HEADS = 8
A_KV_HEADS = 2
B_HEADS = 4
N_BUCKETS = 32
MAX_DISTANCE = 128
TOP_K = 8
N_GROUPS = 8
TOPK_GROUPS = 4
ROUTED_SCALE = 2.5
LN_EPS = 1e-5
DEPTH = 1
ALPHA = (2 * DEPTH) ** 0.25
NEG = -1e30
LAMBDA_INIT = 0.8 - 0.6 * math.exp(-0.3 * 0)

LANES = 128
VMEM_LIMIT = 48 * 1024 * 1024

QA_BLK = 0
KA_BLK = 4
VA_BLK = 6
QB_BLK = 8
KB_BLK = 12
PROJ_COLS = 16 * LANES
VT_ROWS = LANES + 16
LOG2E = 1.4426950408889634

TQ_A = 128
TQ_B = 512
TK_B = 512
TM_MOE = 256
SLAB = 8
KEY_SHIFT = 20
IDX_WIN = TM_MOE // LANES + 1
ISSUE_UNROLL = 8


def _cparams(sem):
    return pltpu.CompilerParams(dimension_semantics=sem, vmem_limit_bytes=VMEM_LIMIT)


def _layernorm_f32(x, g, b):
    mu = jnp.mean(x, axis=-1, keepdims=True)
    xc = x - mu
    var = jnp.mean(xc * xc, axis=-1, keepdims=True)
    return xc * lax.rsqrt(var + LN_EPS) * g + b


def _dot_nt(a, b):
    return lax.dot_general(a, b, (((1,), (1,)), ((), ())), preferred_element_type=jnp.float32)


def _dot(a, b):
    return jnp.dot(a, b, preferred_element_type=jnp.float32)


def _ln_inproj_kernel(x_ref, g_ref, b_ref, w_ref, wvt_ref, o_ref, vt_ref):
    h = _layernorm_f32(x_ref[...], g_ref[...], b_ref[...]).astype(jnp.bfloat16)
    o_ref[...] = _dot(h, w_ref[...]).astype(o_ref.dtype)
    vt = _dot_nt(wvt_ref[...], h).astype(vt_ref.dtype)
    ones = jnp.ones((VT_ROWS - LANES, vt.shape[1]), vt_ref.dtype)
    for hd in range(B_HEADS):
        vt_ref[hd * VT_ROWS:hd * VT_ROWS + LANES, :] = vt[hd * LANES:(hd + 1) * LANES, :]
        vt_ref[hd * VT_ROWS + LANES:(hd + 1) * VT_ROWS, :] = ones


def _ln_inproj(x2d, g, b, w, wvt):
    t, d = x2d.shape
    n = w.shape[1]
    tm = min(512, t)
    return pl.pallas_call(
        _ln_inproj_kernel,
        out_shape=(jax.ShapeDtypeStruct((t, n), jnp.bfloat16),
                   jax.ShapeDtypeStruct((B_HEADS * VT_ROWS, t), jnp.bfloat16)),
        grid=(t // tm,),
        in_specs=[
            pl.BlockSpec((tm, d), lambda i: (i, 0)),
            pl.BlockSpec((1, d), lambda i: (0, 0)),
            pl.BlockSpec((1, d), lambda i: (0, 0)),
            pl.BlockSpec((d, n), lambda i: (0, 0)),
            pl.BlockSpec((B_HEADS * LANES, d), lambda i: (0, 0)),
        ],
        out_specs=(pl.BlockSpec((tm, n), lambda i: (i, 0)),
                   pl.BlockSpec((B_HEADS * VT_ROWS, tm), lambda i: (0, i))),
        compiler_params=_cparams(("parallel",)),
        name="ln_inproj",
    )(x2d, g, b, w, wvt)


def _mixer_a_kernel(sink_ref, q_ref, k_ref, v_ref, km_ref, vm_ref, tab_ref, tabm_ref, o_ref, *, nblk, sub):
    i = pl.program_id(1)
    lane = lax.broadcasted_iota(jnp.int32, (1, LANES), 1)
    lo = lane < HEAD_DIM
    for j in range(sub):
        gi = i * sub + j
        sp = pl.multiple_of(jnp.maximum(gi - 1, 0) * TQ_A, TQ_A)
        sc = pl.multiple_of(gi * TQ_A, TQ_A)
        sn = pl.multiple_of(jnp.minimum(gi + 1, nblk - 1) * TQ_A, TQ_A)
        neg_p = jnp.where(gi > 0, 0.0, NEG).astype(jnp.float32)
        neg_n = jnp.where(gi < nblk - 1, 0.0, NEG).astype(jnp.float32)
        for hp in range(A_Q_HEADS // 2):
            g = hp // 2
            gs = slice(g * LANES, (g + 1) * LANES)
            qc = q_ref[0, j * TQ_A:(j + 1) * TQ_A, hp * LANES:(hp + 1) * LANES]
            kpieces = (km_ref[:, gs], k_ref[0, pl.ds(sp, TQ_A), gs], k_ref[0, pl.ds(sc, TQ_A), gs],
                       k_ref[0, pl.ds(sn, TQ_A), gs])
            vpieces = (vm_ref[:, gs], v_ref[0, pl.ds(sp, TQ_A), gs], v_ref[0, pl.ds(sc, TQ_A), gs],
                       v_ref[0, pl.ds(sn, TQ_A), gs])
            o_pair = jnp.zeros((TQ_A, LANES), jnp.float32)
            for half in range(2):
                h = 2 * hp + half
                msk = lo if half == 0 else jnp.logical_not(lo)
                qm = jnp.where(msk, qc, jnp.zeros_like(qc))
                sink = sink_ref[h]
                s = [
                    _dot_nt(qm, kpieces[0]) + tabm_ref[0, h, j * TQ_A:(j + 1) * TQ_A, :],
                    _dot_nt(qm, kpieces[1]) + (tab_ref[h, :, 0:TQ_A] + neg_p),
                    _dot_nt(qm, kpieces[2]) + tab_ref[h, :, TQ_A:2 * TQ_A],
                    _dot_nt(qm, kpieces[3]) + (tab_ref[h, :, 2 * TQ_A:3 * TQ_A] + neg_n),
                ]
                m = jnp.maximum(jnp.maximum(jnp.max(s[0], axis=-1, keepdims=True), jnp.max(s[1], axis=-1, keepdims=True)),
                                jnp.maximum(jnp.max(s[2], axis=-1, keepdims=True), jnp.max(s[3], axis=-1, keepdims=True)))
                m = jnp.maximum(m, sink)
                l = jnp.exp(sink - m)
                o_h = jnp.zeros((TQ_A, LANES), jnp.float32)
                for sx, vx in zip(s, vpieces):
                    p = jnp.exp(sx - m)
                    l = l + jnp.sum(p, axis=-1, keepdims=True)
                    vmk = jnp.where(msk, vx, jnp.zeros_like(vx))
                    o_h = o_h + _dot(p.astype(jnp.bfloat16), vmk)
                o_pair = o_pair + o_h / l
            o_ref[0, j * TQ_A:(j + 1) * TQ_A, hp * LANES:(hp + 1) * LANES] = o_pair.astype(o_ref.dtype)


def _mixer_a(proj3, proj_meta, tab, tabm, sink):
    bsz, s, _ = proj3.shape
    nblk = s // TQ_A
    sub = 4 if nblk % 4 == 0 else 1
    nq = nblk // sub
    tq = sub * TQ_A
    kern = functools.partial(_mixer_a_kernel, nblk=nblk, sub=sub)
    return pl.pallas_call(
        kern,
        out_shape=jax.ShapeDtypeStruct((bsz, s, A_Q_HEADS * HEAD_DIM), jnp.bfloat16),
        grid=(bsz, nq),
        in_specs=[
            pl.BlockSpec(memory_space=pltpu.SMEM),
            pl.BlockSpec((1, tq, 4 * LANES), lambda b, i: (b, i, QA_BLK // 4)),
            pl.BlockSpec((1, s, 2 * LANES), lambda b, i: (b, 0, KA_BLK // 2)),
            pl.BlockSpec((1, s, 2 * LANES), lambda b, i: (b, 0, VA_BLK // 2)),
            pl.BlockSpec((N_META, 2 * LANES), lambda b, i: (0, KA_BLK // 2)),
            pl.BlockSpec((N_META, 2 * LANES), lambda b, i: (0, VA_BLK // 2)),
            pl.BlockSpec((A_Q_HEADS, TQ_A, 3 * TQ_A), lambda b, i: (0, 0, 0)),
            pl.BlockSpec((1, A_Q_HEADS, tq, N_META), lambda b, i: (jnp.minimum(i, 1), 0, 0, 0)),
        ],
        out_specs=pl.BlockSpec((1, tq, 4 * LANES), lambda b, i: (b, i, 0)),
        compiler_params=_cparams(("parallel", "arbitrary")),
        name="mixer_a",
    )(sink, proj3, proj3, proj3, proj_meta, proj_meta, tab, tabm)


def _mixer_b_kernel(sc_ref, q_ref, k_ref, vt_ref, km_ref, vtm_ref, tab_ref, tabm_ref, g_ref, o_ref,
                    m_ref, acc_ref, s0_ref, s1_ref, *, nk):
    h = pl.program_id(1)
    i = pl.program_id(2)
    lane = lax.broadcasted_iota(jnp.int32, (1, LANES), 1)
    lo = lane < HEAD_DIM
    q = q_ref[0]
    qs = (jnp.where(lo, q, jnp.zeros_like(q)), jnp.where(lo, jnp.zeros_like(q), q))
    lam = sc_ref[0]
    sbufs = (s0_ref, s1_ref)

    km = km_ref[...]
    vtm = vtm_ref[...]
    for c in range(2):
        s = _dot_nt(km, qs[c]) + tabm_ref[0]
        m = jnp.max(s, axis=0, keepdims=True)
        p = jnp.exp2(s - m)
        m_ref[c] = m
        acc_ref[c] = _dot(vtm, p.astype(jnp.bfloat16))

    def scores(j, slot):
        start = pl.multiple_of(j * TK_B, TK_B)
        kj = k_ref[0, pl.ds(start, TK_B), :]
        bias = tab_ref[0, jnp.clip(j - i, -2, 2) + 2]
        for c in range(2):
            sbufs[slot][c] = _dot_nt(kj, qs[c]) + bias

    def accumulate(j, slot):
        start = pl.multiple_of(j * TK_B, TK_B)
        vtj = vt_ref[:, pl.ds(start, TK_B)]
        for c in range(2):
            s = sbufs[slot][c]
            m_prev = m_ref[c]
            m_new = jnp.maximum(m_prev, jnp.max(s, axis=0, keepdims=True))
            a = jnp.exp2(m_prev - m_new)
            p = jnp.exp2(s - m_new)
            m_ref[c] = m_new
            acc_ref[c] = a * acc_ref[c] + _dot(vtj, p.astype(jnp.bfloat16))

    scores(0, 0)

    def pair(jj, carry):
        j = 2 * jj
        scores(j + 1, 1)
        accumulate(j, 0)
        scores(j + 2, 0)
        accumulate(j + 1, 1)
        return carry

    lax.fori_loop(0, nk // 2 - 1, pair, 0)
    scores(nk - 1, 1)
    accumulate(nk - 2, 0)
    accumulate(nk - 1, 1)

    o0 = acc_ref[0, 0:LANES, :] / acc_ref[0, LANES:LANES + 1, :]
    o1 = acc_ref[1, 0:LANES, :] / acc_ref[1, LANES:LANES + 1, :]
    o = o0 - lam * o1
    ms = jnp.mean(o * o, axis=0, keepdims=True)
    o = o * lax.rsqrt(ms + LN_EPS) * (g_ref[...] * (1.0 - LAMBDA_INIT))
    o_ref[0] = o.T.astype(o_ref.dtype)


def _mixer_b(proj3, vt, proj_meta, vt_meta, tab, tabm, scal, subln_g):
    bsz, s, _ = proj3.shape
    nq = s // TQ_B
    nk = s // TK_B
    assert nk % 2 == 0
    kern = functools.partial(_mixer_b_kernel, nk=nk)
    return pl.pallas_call(
        kern,
        out_shape=jax.ShapeDtypeStruct((bsz, s, B_HEADS * LANES), jnp.bfloat16),
        grid=(bsz, B_HEADS, nq),
        in_specs=[
            pl.BlockSpec(memory_space=pltpu.SMEM),
            pl.BlockSpec((1, TQ_B, LANES), lambda b, h, i: (b, i, QB_BLK + h)),
            pl.BlockSpec((1, s, LANES), lambda b, h, i: (b, 0, KB_BLK + h)),
            pl.BlockSpec((VT_ROWS, s), lambda b, h, i: (h, b)),
            pl.BlockSpec((N_META, LANES), lambda b, h, i: (0, KB_BLK + h)),
            pl.BlockSpec((VT_ROWS, N_META), lambda b, h, i: (h, 0)),
            pl.BlockSpec((1, 5, TK_B, TQ_B), lambda b, h, i: (h, 0, 0, 0)),
            pl.BlockSpec((1, N_META, TQ_B), lambda b, h, i: (h, 0, i)),
            pl.BlockSpec((LANES, 1), lambda b, h, i: (0, 0)),
        ],
        out_specs=pl.BlockSpec((1, TQ_B, LANES), lambda b, h, i: (b, i, h)),
        scratch_shapes=[
            pltpu.VMEM((2, 1, TQ_B), jnp.float32),
            pltpu.VMEM((2, VT_ROWS, TQ_B), jnp.float32),
            pltpu.VMEM((2, TK_B, TQ_B), jnp.float32),
            pltpu.VMEM((2, TK_B, TQ_B), jnp.float32),
        ],
        compiler_params=_cparams(("parallel", "parallel", "arbitrary")),
        name="mixer_b",
    )(scal, proj3, proj3, vt, proj_meta, vt_meta, tab, tabm, subln_g)


def _slab_load(ref, lead, rows):
    return jnp.concatenate([ref[lead + (pl.ds(s, rows, stride=SLAB), slice(None))] for s in range(SLAB)], axis=1)


def _slab_store(ref, lead, val):
    rows = val.shape[0]
    for s in range(SLAB):
        ref[lead + (pl.ds(s, rows, stride=SLAB), slice(None))] = val[:, s * LANES:(s + 1) * LANES]


def _outproj_kernel(x_ref, oa_ref, ob_ref, w_ref, eg_ref, eb_ref, g_ref, b_ref, o_ref, os_ref):
    h0 = _layernorm_f32(x_ref[...], eg_ref[...], eb_ref[...])
    half = oa_ref.shape[1]
    mix = _dot(oa_ref[...], w_ref[0:half, :]) + _dot(ob_ref[...], w_ref[half:, :])
    h1 = _layernorm_f32(ALPHA * h0 + mix, g_ref[...], b_ref[...])
    o_ref[...] = h1
    _slab_store(os_ref, (), h1)


def _outproj_ln1(x2d, oa, ob, w_out, eg, eb, g, b):
    t, d = x2d.shape
    tm = min(512, t)
    wa = oa.shape[1]
    wb = ob.shape[1]
    vec = pl.BlockSpec((1, d), lambda i: (0, 0))
    return pl.pallas_call(
        _outproj_kernel,
        out_shape=(jax.ShapeDtypeStruct((t, d), jnp.float32),
                   jax.ShapeDtypeStruct((t * SLAB, LANES), jnp.float32)),
        grid=(t // tm,),
        in_specs=[
            pl.BlockSpec((tm, d), lambda i: (i, 0)),
            pl.BlockSpec((tm, wa), lambda i: (i, 0)),
            pl.BlockSpec((tm, wb), lambda i: (i, 0)),
            pl.BlockSpec((wa + wb, d), lambda i: (0, 0)),
            vec, vec, vec, vec,
        ],
        out_specs=(pl.BlockSpec((tm, d), lambda i: (i, 0)),
                   pl.BlockSpec((tm * SLAB, LANES), lambda i: (i, 0))),
        compiler_params=_cparams(("parallel",)),
        name="outproj_ln1",
    )(x2d, oa, ob, w_out, eg, eb, g, b)


def _router_kernel(h_ref, wh_ref, wl_ref, rb_ref, e_ref, w_ref, cnt_ref, carry_ref, *, n_exp):
    i = pl.program_id(0)
    tn = h_ref.shape[0]
    gsz = n_exp // N_GROUPS

    @pl.when(i == 0)
    def _():
        carry_ref[...] = jnp.zeros_like(carry_ref)

    x = h_ref[...]
    xh = x.astype(jnp.bfloat16)
    xl = (x - xh.astype(jnp.float32)).astype(jnp.bfloat16)
    logits = _dot_nt(wh_ref[...], xh) + (_dot_nt(wh_ref[...], xl) + _dot_nt(wl_ref[...], xh))
    scores = 1.0 / (1.0 + jnp.exp(-logits))
    biased = scores + rb_ref[...]

    g3 = biased.reshape(N_GROUPS, gsz, tn)
    it3 = lax.broadcasted_iota(jnp.int32, (N_GROUPS, gsz, tn), 1)
    mx1 = jnp.max(g3, axis=1, keepdims=True)
    first = jnp.min(jnp.where(g3 == mx1, it3, gsz), axis=1, keepdims=True)
    mx2 = jnp.max(jnp.where(it3 == first, -jnp.inf, g3), axis=1, keepdims=True)
    gscore = (mx1 + mx2).reshape(N_GROUPS, tn)

    itg = lax.broadcasted_iota(jnp.int32, (N_GROUPS, tn), 0)
    gsel = jnp.zeros((N_GROUPS, tn), jnp.bool_)
    cur = gscore
    for _ in range(TOPK_GROUPS):
        mx = jnp.max(cur, axis=0, keepdims=True)
        fi = jnp.min(jnp.where(cur == mx, itg, N_GROUPS), axis=0, keepdims=True)
        hit = itg == fi
        gsel = jnp.logical_or(gsel, hit)
        cur = jnp.where(hit, -jnp.inf, cur)
    emask = jnp.broadcast_to(gsel.reshape(N_GROUPS, 1, tn), (N_GROUPS, gsz, tn)).reshape(n_exp, tn)
    cur = jnp.where(emask, biased, NEG)

    ite = lax.broadcasted_iota(jnp.int32, (n_exp, tn), 0)
    hits = []
    eidx = []
    wsel = []
    for _ in range(TOP_K):
        mx = jnp.max(cur, axis=0, keepdims=True)
        fi = jnp.min(jnp.where(cur == mx, ite, n_exp), axis=0, keepdims=True)
        hit = ite == fi
        hits.append(hit)
        eidx.append(fi)
        wsel.append(jnp.sum(jnp.where(hit, scores, 0.0), axis=0, keepdims=True))
        cur = jnp.where(hit, -jnp.inf, cur)
    sel = hits[0]
    for hit in hits[1:]:
        sel = jnp.logical_or(sel, hit)
    self32 = jnp.where(sel, 1.0, 0.0)

    carry_ref[...] = carry_ref[...] + jnp.sum(self32, axis=1, keepdims=True)
    cnt_ref[...] = carry_ref[...]

    wcat = jnp.concatenate(wsel, axis=0)
    wcat = wcat / jnp.sum(wcat, axis=0, keepdims=True) * ROUTED_SCALE
    tok = i * tn + lax.broadcasted_iota(jnp.int32, (TOP_K, tn), 1)
    slot = lax.broadcasted_iota(jnp.int32, (TOP_K, tn), 0)
    e_ref[...] = jnp.concatenate(eidx, axis=0) * (1 << KEY_SHIFT) + (tok * TOP_K + slot)
    w_ref[...] = wcat


def _router(h1, wr_hi, wr_lo, rbias):
    t, d = h1.shape
    n_exp = wr_hi.shape[0]
    tn = min(512, t)
    kern = functools.partial(_router_kernel, n_exp=n_exp)
    row = pl.BlockSpec((TOP_K, tn), lambda i: (0, i))
    return pl.pallas_call(
        kern,
        out_shape=(
            jax.ShapeDtypeStruct((TOP_K, t), jnp.int32),
            jax.ShapeDtypeStruct((TOP_K, t), jnp.float32),
            jax.ShapeDtypeStruct((n_exp, 1), jnp.float32),
        ),
        grid=(t // tn,),
        in_specs=[
            pl.BlockSpec((tn, d), lambda i: (i, 0)),
            pl.BlockSpec((n_exp, d), lambda i: (0, 0)),
            pl.BlockSpec((n_exp, d), lambda i: (0, 0)),
            pl.BlockSpec((n_exp, 1), lambda i: (0, 0)),
        ],
        out_specs=(row, row, pl.BlockSpec((n_exp, 1), lambda i: (0, 0))),
        scratch_shapes=[pltpu.VMEM((n_exp, 1), jnp.float32)],
        compiler_params=_cparams(("arbitrary",)),
        name="router",
    )(h1, wr_hi, wr_lo, rbias)


def _experts_kernel(te_ref, tv_ref, tu_ref, na_ref, f0, f1, f2, c0, c1, c2, n0, n1, n2, h_hbm, wgu_ref, wd_ref,
                    y_hbm, xbuf, ybuf, gsem, ssem):
    i = pl.program_id(0)
    nt = pl.num_programs(0)
    nv = tv_ref[i]
    first = (f0, f1, f2)
    cur = (c0, c1, c2)
    nxt = (n0, n1, n2)

    def for_entries(wins, off, n, fn):
        for r, win in enumerate(wins):
            def one(j, r=r, win=win):
                fn(j, win[0, 0, off + j - r * LANES])

            def group(g, lo):
                for u in range(ISSUE_UNROLL):
                    one(lo + g * ISSUE_UNROLL + u)
                return lo

            def single(j, carry):
                one(j)
                return carry

            lo = jnp.clip(r * LANES - off, 0, n)
            hi = jnp.clip((r + 1) * LANES - off, 0, n)
            ngrp = (hi - lo) // ISSUE_UNROLL
            lax.fori_loop(0, ngrp, group, lo)
            lax.fori_loop(lo + ngrp * ISSUE_UNROLL, hi, single, 0)

    def gather_copy(slot, j, src, rows=1):
        src = pl.multiple_of(src, SLAB)
        return pltpu.make_async_copy(h_hbm.at[pl.ds(src, rows * SLAB), :],
                                     xbuf.at[slot, pl.ds(pl.multiple_of(j * SLAB, SLAB), rows * SLAB), :],
                                     gsem.at[slot])

    def scatter_copy(slot, j, dst, rows=1):
        dst = pl.multiple_of(dst, SLAB)
        return pltpu.make_async_copy(ybuf.at[slot, pl.ds(pl.multiple_of(j * SLAB, SLAB), rows * SLAB), :],
                                     y_hbm.at[pl.ds(dst, rows * SLAB), :], ssem.at[slot])

    def wait_rows(copy_fn, slot, n):
        rows = TM_MOE
        while rows >= 1:
            @pl.when((n & rows) != 0)
            def _(rows=rows):
                copy_fn(slot, 0, 0, rows).wait()
            rows //= 2

    def step(slot):
        other = 1 - slot

        @pl.when(i == 0)
        def _():
            xbuf[...] = jnp.zeros_like(xbuf)
            for_entries(first, 0, nv, lambda j, f: gather_copy(0, j, f).start())

        inext = jnp.minimum(i + 1, nt - 1)
        nv_next = jnp.where(i + 1 < nt, tv_ref[inext], 0)
        for_entries(nxt, tu_ref[inext] & (LANES - 1), nv_next, lambda j, f: gather_copy(other, j, f).start())

        wait_rows(gather_copy, slot, nv)
        x = _slab_load(xbuf, (slot,), TM_MOE).astype(jnp.bfloat16)
        gu = _dot(x, wgu_ref[0])
        de = gu.shape[1] // 2
        gate = gu[:, :de]
        hid = (gate / (1.0 + jnp.exp(-gate))) * gu[:, de:]
        y = _dot(hid.astype(jnp.bfloat16), wd_ref[0])

        wait_rows(scatter_copy, slot, jnp.where(i >= 2, tv_ref[jnp.maximum(i - 2, 0)], 0))
        _slab_store(ybuf, (slot,), y)
        for_entries(cur, tu_ref[i] & (LANES - 1), nv, lambda j, f: scatter_copy(slot, j, f).start())

        @pl.when(i == na_ref[0] - 1)
        def _():
            wait_rows(scatter_copy, other, jnp.where(i >= 1, tv_ref[jnp.maximum(i - 1, 0)], 0))
            wait_rows(scatter_copy, slot, nv)

    for slot in range(2):
        @pl.when(jnp.logical_and(nv > 0, (i & 1) == slot))
        def _(slot=slot):
            step(slot)


def _experts(h_slab, src3d, dst3d, wgu, wd, tile_e, tile_valid, tile_u, n_active, n_tok):
    nt = tile_e.shape[0]
    d = wgu.shape[1]
    de2 = wgu.shape[2]
    last = nt - 1

    def win(j, shift):
        return pl.BlockSpec((1, 1, LANES),
                            lambda i, te, tv, tu, na: (tu[jnp.minimum(i + shift, last)] // LANES + j, 0, 0),
                            memory_space=pltpu.SMEM)

    def win0(j):
        return pl.BlockSpec((1, 1, LANES), lambda i, te, tv, tu, na: (j, 0, 0), memory_space=pltpu.SMEM)

    wins = [win0(j) for j in range(IDX_WIN)] + [win(j, 0) for j in range(IDX_WIN)] + [win(j, 1) for j in range(IDX_WIN)]
    return pl.pallas_call(
        _experts_kernel,
        out_shape=jax.ShapeDtypeStruct((TOP_K * n_tok * SLAB, LANES), jnp.float32),
        grid_spec=pltpu.PrefetchScalarGridSpec(
            num_scalar_prefetch=4,
            grid=(nt,),
            in_specs=wins + [
                pl.BlockSpec(memory_space=pl.ANY),
                pl.BlockSpec((1, d, de2), lambda i, te, tv, tu, na: (te[i], 0, 0)),
                pl.BlockSpec((1, de2 // 2, d), lambda i, te, tv, tu, na: (te[i], 0, 0)),
            ],
            out_specs=pl.BlockSpec(memory_space=pl.ANY),
            scratch_shapes=[
                pltpu.VMEM((2, TM_MOE * SLAB, LANES), jnp.float32),
                pltpu.VMEM((2, TM_MOE * SLAB, LANES), jnp.float32),
                pltpu.SemaphoreType.DMA((2,)),
                pltpu.SemaphoreType.DMA((2,)),
            ],
        ),
        compiler_params=_cparams(("arbitrary",)),
        name="experts",
    )(tile_e, tile_valid, tile_u, n_active, *([src3d] * IDX_WIN), *([dst3d] * IDX_WIN), *([src3d] * IDX_WIN),
      h_slab, wgu, wd)


def _combine_kernel(h_ref, w_ref, y_ref, sgu_ref, sd_ref, g_ref, b_ref, o_ref):
    tn = h_ref.shape[0]
    h = h_ref[...]
    gu = _dot(h.astype(jnp.bfloat16), sgu_ref[...])
    ds = gu.shape[1] // 2
    gate = gu[:, :ds]
    hid = (gate / (1.0 + jnp.exp(-gate))) * gu[:, ds:]
    shared = _dot(hid.astype(jnp.bfloat16), sd_ref[...])

    w = w_ref[...]
    routed = _slab_load(y_ref, (0,), tn) * w[:, 0:1]
    for k in range(1, TOP_K):
        routed = routed + _slab_load(y_ref, (k,), tn) * w[:, k:k + 1]
    o_ref[...] = _layernorm_f32(ALPHA * h + (routed + shared), g_ref[...], b_ref[...])


def _combine(h1, w_tok, y_slab, sgu, sd, g, b):
    t, d = h1.shape
    tn = min(256, t)
    ds2 = sgu.shape[1]
    vec = pl.BlockSpec((1, d), lambda i: (0, 0))
    return pl.pallas_call(
        _combine_kernel,
        out_shape=jax.ShapeDtypeStruct((t, d), jnp.float32),
        grid=(t // tn,),
        in_specs=[
            pl.BlockSpec((tn, d), lambda i: (i, 0)),
            pl.BlockSpec((tn, TOP_K), lambda i: (i, 0)),
            pl.BlockSpec((TOP_K, tn * SLAB, LANES), lambda i: (0, i, 0)),
            pl.BlockSpec((d, ds2), lambda i: (0, 0)),
            pl.BlockSpec((ds2 // 2, d), lambda i: (0, 0)),
            vec, vec,
        ],
        out_specs=pl.BlockSpec((tn, d), lambda i: (i, 0)),
        compiler_params=_cparams(("parallel",)),
        name="combine_ln2",
    )(h1, w_tok, y_slab.reshape(TOP_K, t * SLAB, LANES), sgu, sd, g, b)


def _rel_bucket(rel):
    nb = N_BUCKETS // 2
    max_exact = nb // 2
    ret = jnp.where(rel > 0, nb, 0)
    n = jnp.abs(rel)
    nf = jnp.maximum(n, 1).astype(jnp.float32)
    large = max_exact + (jnp.log(nf / max_exact) / math.log(MAX_DISTANCE / max_exact) * (nb - max_exact)).astype(jnp.int32)
    large = jnp.minimum(large, nb - 1)
    return ret + jnp.where(n < max_exact, n, large)


def _bias_of_rel(rel_bias, rel):
    return rel_bias.astype(jnp.float32)[_rel_bucket(rel)]


def _toeplitz(vec, nrow, ncol, off):
    lo = off - (nrow - 1)
    v = vec[lo:off + ncol]
    p = v.shape[0] + 1
    v = jnp.concatenate([v, v[:1]], axis=0)
    flat = jnp.tile(v, (nrow + 1, 1))
    base = off - lo
    out = flat[base:base + nrow * (p - 1)].reshape(nrow, p - 1, vec.shape[1])
    return out[:, :ncol]


def _bias_tables_a(rel_bias, rows):
    bias_a = rel_bias[:, :A_Q_HEADS]
    span = 2 * TQ_A
    rel = jnp.arange(-span, span + 1, dtype=jnp.int32)
    vec = jnp.where((jnp.abs(rel) <= WINDOW)[:, None], _bias_of_rel(bias_a, rel), NEG)
    band = jnp.transpose(_toeplitz(vec, TQ_A, 3 * TQ_A, span - TQ_A), (2, 0, 1))
    tabs = []
    for base in (0, rows):
        lo = -(N_META + base + rows - 1)
        relm = jnp.arange(lo, N_META, dtype=jnp.int32)
        vm = _bias_of_rel(bias_a, relm)
        tabs.append(jnp.transpose(_toeplitz(vm, rows, N_META, -(N_META + base) - lo), (2, 0, 1)))
    return band, jnp.stack(tabs)


def _bias_tables_b(rel_bias, s):
    bias_b = rel_bias[:, A_Q_HEADS:]
    near = []
    for d in (-1, 0, 1):
        m = jnp.arange(TQ_B + TK_B - 1, dtype=jnp.int32)
        vec = _bias_of_rel(bias_b, TK_B * d + TK_B - 1 - m)
        near.append(_toeplitz(vec, TK_B, TQ_B, TK_B - 1))
    far_l = jnp.broadcast_to(_bias_of_rel(bias_b, jnp.int32(-TK_B - 1)), (TK_B, TQ_B, B_HEADS))
    far_r = jnp.broadcast_to(_bias_of_rel(bias_b, jnp.int32(TK_B + 1)), (TK_B, TQ_B, B_HEADS))
    tabs = jnp.transpose(jnp.stack([far_l] + near + [far_r]), (3, 0, 1, 2)) * LOG2E
    m = jnp.arange(s + N_META - 1, dtype=jnp.int32)
    vec = _bias_of_rel(bias_b, -1 - m)
    meta = jnp.transpose(_toeplitz(vec, N_META, s, N_META - 1), (2, 0, 1)) * LOG2E
    return tabs, meta


def _prep_w_in(w_in):
    a_w = A_Q_HEADS * HEAD_DIM
    kv = A_KV_HEADS * HEAD_DIM
    bqk = B_HEADS * 2 * HEAD_DIM
    scale = HEAD_DIM ** -0.5
    qa = w_in[:, :a_w] * scale
    ka = w_in[:, a_w:a_w + kv]
    va = w_in[:, a_w + kv:a_w + 2 * kv]
    o = a_w + 2 * kv
    qb = w_in[:, o:o + bqk] * (scale * LOG2E)
    kb = w_in[:, o + bqk:o + 2 * bqk]
    vb = w_in[:, o + 2 * bqk:]

    def dup(w):
        return jnp.concatenate([w[:, g * HEAD_DIM:(g + 1) * HEAD_DIM] for g in range(A_KV_HEADS) for _ in range(2)], axis=1)

    w = jnp.concatenate([qa, dup(ka), dup(va), qb, kb], axis=1).astype(jnp.bfloat16)
    return w, vb.T.astype(jnp.bfloat16)


def _trunk(x, prm):
    bsz, s, d = x.shape
    t = bsz * s
    x2d = x.reshape(t, d)
    proj, vt = _ln_inproj(x2d, prm["ln_emb_g"], prm["ln_emb_b"], prm["w_in"], prm["w_vt"])
    proj3 = proj.reshape(bsz, s, PROJ_COLS)
    sub = 4 if (s // TQ_A) % 4 == 0 else 1
    tab_a, tabm_a = _bias_tables_a(prm["rel_bias"], sub * TQ_A)
    oa = _mixer_a(proj3, prm["proj_meta"], tab_a, tabm_a, prm["sink"])
    tabs_b, meta_b = _bias_tables_b(prm["rel_bias"], s)
    ob = _mixer_b(proj3, vt, prm["proj_meta"], prm["vt_meta"], tabs_b, meta_b, prm["lam"].reshape(1).astype(jnp.float32),
                  prm["subln_g"])
    h1, h1_slab = _outproj_ln1(x2d, oa.reshape(t, -1), ob.reshape(t, -1), prm["w_out"], prm["ln_emb_g"],
                               prm["ln_emb_b"], prm["ln1_g"], prm["ln1_b"])

    keys, wts, counts = _router(h1, prm["wr_hi"], prm["wr_lo"], prm["router_bias"])
    n_exp = counts.shape[0]
    n_asg = t * TOP_K
    assert n_asg <= (1 << KEY_SHIFT) and n_asg % LANES == 0
    order = jnp.sort(keys.reshape(n_asg)) & ((1 << KEY_SHIFT) - 1)
    order = jnp.concatenate([order, jnp.zeros((IDX_WIN * LANES,), jnp.int32)]).reshape(-1, 1, LANES)
    tok = order >> 3
    src3d = tok * SLAB
    dst3d = ((order & (TOP_K - 1)) * t + tok) * SLAB
    counts = counts[:, 0].astype(jnp.int32)
    tiles_e = (counts + TM_MOE - 1) // TM_MOE
    tend = jnp.cumsum(tiles_e)
    tstart = tend - tiles_e
    ustart = jnp.cumsum(counts) - counts
    nt = n_asg // TM_MOE + n_exp
    tid = jnp.arange(nt, dtype=jnp.int32)
    tile_e = jnp.minimum(jnp.sum((tend[None, :] <= tid[:, None]).astype(jnp.int32), axis=1), n_exp - 1)
    onehot = (tile_e[:, None] == jnp.arange(n_exp, dtype=jnp.int32)[None, :]).astype(jnp.int32)
    in_e = (tid - jnp.sum(onehot * tstart[None, :], axis=1)) * TM_MOE
    active = tid < tend[-1]
    tile_valid = jnp.where(active, jnp.clip(jnp.sum(onehot * counts[None, :], axis=1) - in_e, 0, TM_MOE), 0)
    tile_u = jnp.where(active, jnp.sum(onehot * ustart[None, :], axis=1) + in_e, 0)

    y_slab = _experts(h1_slab, src3d, dst3d, prm["w_gu"], prm["w_down"], tile_e.astype(jnp.int32),
                      tile_valid.astype(jnp.int32), tile_u.astype(jnp.int32), tend[-1:].astype(jnp.int32), t)
    out = _combine(h1, wts.T, y_slab, prm["ws_gu"], prm["ws_down"], prm["ln2_g"], prm["ln2_b"])
    return out.reshape(bsz, s, d)


def kernel(x_prompt, x_sample, meta_tokens, ln_emb_g, ln_emb_b, rel_bias, w_in, attn_sink, lambda_q1, lambda_k1, lambda_q2, lambda_k2, subln_g, w_out, ln1_g, ln1_b, w_router, router_bias, w_gate, w_up, w_down, ws_gate, ws_up, ws_down, ln2_g, ln2_b):
    f32 = jnp.float32
    bf16 = jnp.bfloat16
    l = 0
    row = lambda v: v.reshape(1, -1).astype(f32)
    lam = (jnp.exp(jnp.sum(lambda_q1[l].astype(f32) * lambda_k1[l].astype(f32)))
           - jnp.exp(jnp.sum(lambda_q2[l].astype(f32) * lambda_k2[l].astype(f32))) + LAMBDA_INIT)
    wr_t = w_router[l].astype(f32).T
    wr_hi = wr_t.astype(bf16)
    prm = {
        "ln_emb_g": row(ln_emb_g), "ln_emb_b": row(ln_emb_b),
        "rel_bias": rel_bias,
        "sink": attn_sink[l].astype(f32),
        "lam": lam,
        "subln_g": subln_g[l].astype(f32).reshape(-1, 1),
        "w_out": w_out[l].astype(bf16),
        "ln1_g": row(ln1_g[l]), "ln1_b": row(ln1_b[l]),
        "wr_hi": wr_hi, "wr_lo": (wr_t - wr_hi.astype(f32)).astype(bf16),
        "router_bias": router_bias[l].astype(f32).reshape(-1, 1),
        "w_gu": jnp.concatenate([w_gate[l].astype(bf16), w_up[l].astype(bf16)], axis=-1),
        "w_down": w_down[l].astype(bf16),
        "ws_gu": jnp.concatenate([ws_gate[l], ws_up[l]], axis=-1).astype(bf16),
        "ws_down": ws_down[l].astype(bf16),
        "ln2_g": row(ln2_g[l]), "ln2_b": row(ln2_b[l]),
    }
    prm["w_in"], prm["w_vt"] = _prep_w_in(w_in[l])
    prm["proj_meta"], prm["vt_meta"] = _ln_inproj(meta_tokens.astype(f32), prm["ln_emb_g"], prm["ln_emb_b"], prm["w_in"],
                                                 prm["w_vt"])
    return (_trunk(x_prompt, prm), _trunk(x_sample, prm))
```

```python
import functools
import math

import jax
import jax.numpy as jnp
from jax import lax
from jax.experimental import pallas as pl
from jax.experimental.pallas import tpu as pltpu

N_META = 16
HEAD_DIM = 64
WINDOW = 128
A_Q_HEADS = 8
A_KV_HEADS = 2
B_HEADS = 4
N_BUCKETS = 32
MAX_DISTANCE = 128
TOP_K = 8
N_GROUPS = 8
TOPK_GROUPS = 4
ROUTED_SCALE = 2.5
LN_EPS = 1e-5
DEPTH = 1
ALPHA = (2 * DEPTH) ** 0.25
NEG = -1e30
LAMBDA_INIT = 0.8 - 0.6 * math.exp(-0.3 * 0)

LANES = 128
VMEM_LIMIT = 48 * 1024 * 1024

QA_BLK = 0
KA_BLK = 4
VA_BLK = 6
QB_BLK = 8
KB_BLK = 12
PROJ_COLS = 16 * LANES
VT_ROWS = LANES + 16
LOG2E = 1.4426950408889634

TQ_A = 128
TQ_B = 512
TK_B = 512
TM_MOE = 256
SLAB = 8
KEY_SHIFT = 20


def _cparams(sem):
    return pltpu.CompilerParams(dimension_semantics=sem, vmem_limit_bytes=VMEM_LIMIT)


def _layernorm_f32(x, g, b):
    mu = jnp.mean(x, axis=-1, keepdims=True)
    xc = x - mu
    var = jnp.mean(xc * xc, axis=-1, keepdims=True)
    return xc * lax.rsqrt(var + LN_EPS) * g + b


def _dot_nt(a, b):
    return lax.dot_general(a, b, (((1,), (1,)), ((), ())), preferred_element_type=jnp.float32)


def _dot(a, b):
    return jnp.dot(a, b, preferred_element_type=jnp.float32)


def _ln_inproj_kernel(x_ref, g_ref, b_ref, w_ref, wvt_ref, o_ref, vt_ref):
    h = _layernorm_f32(x_ref[...], g_ref[...], b_ref[...]).astype(jnp.bfloat16)
    o_ref[...] = _dot(h, w_ref[...]).astype(o_ref.dtype)
    vt = _dot_nt(wvt_ref[...], h).astype(vt_ref.dtype)
    ones = jnp.ones((VT_ROWS - LANES, vt.shape[1]), vt_ref.dtype)
    for hd in range(B_HEADS):
        vt_ref[hd * VT_ROWS:hd * VT_ROWS + LANES, :] = vt[hd * LANES:(hd + 1) * LANES, :]
        vt_ref[hd * VT_ROWS + LANES:(hd + 1) * VT_ROWS, :] = ones


def _ln_inproj(x2d, g, b, w, wvt):
    t, d = x2d.shape
    n = w.shape[1]
    tm = min(512, t)
    return pl.pallas_call(
        _ln_inproj_kernel,
        out_shape=(jax.ShapeDtypeStruct((t, n), jnp.bfloat16),
                   jax.ShapeDtypeStruct((B_HEADS * VT_ROWS, t), jnp.bfloat16)),
        grid=(t // tm,),
        in_specs=[
            pl.BlockSpec((tm, d), lambda i: (i, 0)),
            pl.BlockSpec((1, d), lambda i: (0, 0)),
            pl.BlockSpec((1, d), lambda i: (0, 0)),
            pl.BlockSpec((d, n), lambda i: (0, 0)),
            pl.BlockSpec((B_HEADS * LANES, d), lambda i: (0, 0)),
        ],
        out_specs=(pl.BlockSpec((tm, n), lambda i: (i, 0)),
                   pl.BlockSpec((B_HEADS * VT_ROWS, tm), lambda i: (0, i))),
        compiler_params=_cparams(("parallel",)),
        name="ln_inproj",
    )(x2d, g, b, w, wvt)


def _mixer_a_kernel(sink_ref, q_ref, k_ref, v_ref, km_ref, vm_ref, tab_ref, tabm_ref, o_ref, *, nblk, sub):
    i = pl.program_id(1)
    lane = lax.broadcasted_iota(jnp.int32, (1, LANES), 1)
    lo = lane < HEAD_DIM
    for j in range(sub):
        gi = i * sub + j
        sp = pl.multiple_of(jnp.maximum(gi - 1, 0) * TQ_A, TQ_A)
        sc = pl.multiple_of(gi * TQ_A, TQ_A)
        sn = pl.multiple_of(jnp.minimum(gi + 1, nblk - 1) * TQ_A, TQ_A)
        neg_p = jnp.where(gi > 0, 0.0, NEG).astype(jnp.float32)
        neg_n = jnp.where(gi < nblk - 1, 0.0, NEG).astype(jnp.float32)
        for hp in range(A_Q_HEADS // 2):
            g = hp // 2
            gs = slice(g * LANES, (g + 1) * LANES)
            qc = q_ref[0, j * TQ_A:(j + 1) * TQ_A, hp * LANES:(hp + 1) * LANES]
            kpieces = (km_ref[:, gs], k_ref[0, pl.ds(sp, TQ_A), gs], k_ref[0, pl.ds(sc, TQ_A), gs],
                       k_ref[0, pl.ds(sn, TQ_A), gs])
            vpieces = (vm_ref[:, gs], v_ref[0, pl.ds(sp, TQ_A), gs], v_ref[0, pl.ds(sc, TQ_A), gs],
                       v_ref[0, pl.ds(sn, TQ_A), gs])
            o_pair = jnp.zeros((TQ_A, LANES), jnp.float32)
            for half in range(2):
                h = 2 * hp + half
                msk = lo if half == 0 else jnp.logical_not(lo)
                qm = jnp.where(msk, qc, jnp.zeros_like(qc))
                sink = sink_ref[h]
                s = [
                    _dot_nt(qm, kpieces[0]) + tabm_ref[0, h, j * TQ_A:(j + 1) * TQ_A, :],
                    _dot_nt(qm, kpieces[1]) + (tab_ref[h, :, 0:TQ_A] + neg_p),
                    _dot_nt(qm, kpieces[2]) + tab_ref[h, :, TQ_A:2 * TQ_A],
                    _dot_nt(qm, kpieces[3]) + (tab_ref[h, :, 2 * TQ_A:3 * TQ_A] + neg_n),
                ]
                m = jnp.maximum(jnp.maximum(jnp.max(s[0], axis=-1, keepdims=True), jnp.max(s[1], axis=-1, keepdims=True)),
                                jnp.maximum(jnp.max(s[2], axis=-1, keepdims=True), jnp.max(s[3], axis=-1, keepdims=True)))
                m = jnp.maximum(m, sink)
                l = jnp.exp(sink - m)
                o_h = jnp.zeros((TQ_A, LANES), jnp.float32)
                for sx, vx in zip(s, vpieces):
                    p = jnp.exp(sx - m)
                    l = l + jnp.sum(p, axis=-1, keepdims=True)
                    vmk = jnp.where(msk, vx, jnp.zeros_like(vx))
                    o_h = o_h + _dot(p.astype(jnp.bfloat16), vmk)
                o_pair = o_pair + o_h / l
            o_ref[0, j * TQ_A:(j + 1) * TQ_A, hp * LANES:(hp + 1) * LANES] = o_pair.astype(o_ref.dtype)


def _mixer_a(proj3, proj_meta, tab, tabm, sink):
    bsz, s, _ = proj3.shape
    nblk = s // TQ_A
    sub = 4 if nblk % 4 == 0 else 1
    nq = nblk // sub
    tq = sub * TQ_A
    kern = functools.partial(_mixer_a_kernel, nblk=nblk, sub=sub)
    return pl.pallas_call(
        kern,
        out_shape=jax.ShapeDtypeStruct((bsz, s, A_Q_HEADS * HEAD_DIM), jnp.bfloat16),
        grid=(bsz, nq),
        in_specs=[
            pl.BlockSpec(memory_space=pltpu.SMEM),
            pl.BlockSpec((1, tq, 4 * LANES), lambda b, i: (b, i, QA_BLK // 4)),
            pl.BlockSpec((1, s, 2 * LANES), lambda b, i: (b, 0, KA_BLK // 2)),
            pl.BlockSpec((1, s, 2 * LANES), lambda b, i: (b, 0, VA_BLK // 2)),
            pl.BlockSpec((N_META, 2 * LANES), lambda b, i: (0, KA_BLK // 2)),
            pl.BlockSpec((N_META, 2 * LANES), lambda b, i: (0, VA_BLK // 2)),
            pl.BlockSpec((A_Q_HEADS, TQ_A, 3 * TQ_A), lambda b, i: (0, 0, 0)),
            pl.BlockSpec((1, A_Q_HEADS, tq, N_META), lambda b, i: (jnp.minimum(i, 1), 0, 0, 0)),
        ],
        out_specs=pl.BlockSpec((1, tq, 4 * LANES), lambda b, i: (b, i, 0)),
        compiler_params=_cparams(("parallel", "arbitrary")),
        name="mixer_a",
    )(sink, proj3, proj3, proj3, proj_meta, proj_meta, tab, tabm)


def _mixer_b_kernel(sc_ref, q_ref, k_ref, vt_ref, km_ref, vtm_ref, tab_ref, tabm_ref, g_ref, o_ref,
                    m_ref, acc_ref, s0_ref, s1_ref, *, nk):
    h = pl.program_id(1)
    i = pl.program_id(2)
    lane = lax.broadcasted_iota(jnp.int32, (1, LANES), 1)
    lo = lane < HEAD_DIM
    q = q_ref[0]
    qs = (jnp.where(lo, q, jnp.zeros_like(q)), jnp.where(lo, jnp.zeros_like(q), q))
    lam = sc_ref[0]
    sbufs = (s0_ref, s1_ref)

    km = km_ref[...]
    vtm = vtm_ref[...]
    for c in range(2):
        s = _dot_nt(km, qs[c]) + tabm_ref[0]
        m = jnp.max(s, axis=0, keepdims=True)
        p = jnp.exp2(s - m)
        m_ref[c] = m
        acc_ref[c] = _dot(vtm, p.astype(jnp.bfloat16))

    def scores(j, slot):
        start = pl.multiple_of(j * TK_B, TK_B)
        kj = k_ref[0, pl.ds(start, TK_B), :]
        bias = tab_ref[0, jnp.clip(j - i, -2, 2) + 2]
        for c in range(2):
            sbufs[slot][c] = _dot_nt(kj, qs[c]) + bias

    def accumulate(j, slot):
        start = pl.multiple_of(j * TK_B, TK_B)
        vtj = vt_ref[:, pl.ds(start, TK_B)]
        for c in range(2):
            s = sbufs[slot][c]
            m_prev = m_ref[c]
            m_new = jnp.maximum(m_prev, jnp.max(s, axis=0, keepdims=True))
            a = jnp.exp2(m_prev - m_new)
            p = jnp.exp2(s - m_new)
            m_ref[c] = m_new
            acc_ref[c] = a * acc_ref[c] + _dot(vtj, p.astype(jnp.bfloat16))

    scores(0, 0)

    def pair(jj, carry):
        j = 2 * jj
        scores(j + 1, 1)
        accumulate(j, 0)
        scores(j + 2, 0)
        accumulate(j + 1, 1)
        return carry

    lax.fori_loop(0, nk // 2 - 1, pair, 0)
    scores(nk - 1, 1)
    accumulate(nk - 2, 0)
    accumulate(nk - 1, 1)

    o0 = acc_ref[0, 0:LANES, :] / acc_ref[0, LANES:LANES + 1, :]
    o1 = acc_ref[1, 0:LANES, :] / acc_ref[1, LANES:LANES + 1, :]
    o = o0 - lam * o1
    ms = jnp.mean(o * o, axis=0, keepdims=True)
    o = o * lax.rsqrt(ms + LN_EPS) * (g_ref[...] * (1.0 - LAMBDA_INIT))
    o_ref[0] = o.T.astype(o_ref.dtype)


def _mixer_b(proj3, vt, proj_meta, vt_meta, tab, tabm, scal, subln_g):
    bsz, s, _ = proj3.shape
    nq = s // TQ_B
    nk = s // TK_B
    assert nk % 2 == 0
    kern = functools.partial(_mixer_b_kernel, nk=nk)
    return pl.pallas_call(
        kern,
        out_shape=jax.ShapeDtypeStruct((bsz, s, B_HEADS * LANES), jnp.bfloat16),
        grid=(bsz, B_HEADS, nq),
        in_specs=[
            pl.BlockSpec(memory_space=pltpu.SMEM),
            pl.BlockSpec((1, TQ_B, LANES), lambda b, h, i: (b, i, QB_BLK + h)),
            pl.BlockSpec((1, s, LANES), lambda b, h, i: (b, 0, KB_BLK + h)),
            pl.BlockSpec((VT_ROWS, s), lambda b, h, i: (h, b)),
            pl.BlockSpec((N_META, LANES), lambda b, h, i: (0, KB_BLK + h)),
            pl.BlockSpec((VT_ROWS, N_META), lambda b, h, i: (h, 0)),
            pl.BlockSpec((1, 5, TK_B, TQ_B), lambda b, h, i: (h, 0, 0, 0)),
            pl.BlockSpec((1, N_META, TQ_B), lambda b, h, i: (h, 0, i)),
            pl.BlockSpec((LANES, 1), lambda b, h, i: (0, 0)),
        ],
        out_specs=pl.BlockSpec((1, TQ_B, LANES), lambda b, h, i: (b, i, h)),
        scratch_shapes=[
            pltpu.VMEM((2, 1, TQ_B), jnp.float32),
            pltpu.VMEM((2, VT_ROWS, TQ_B), jnp.float32),
            pltpu.VMEM((2, TK_B, TQ_B), jnp.float32),
            pltpu.VMEM((2, TK_B, TQ_B), jnp.float32),
        ],
        compiler_params=_cparams(("parallel", "parallel", "arbitrary")),
        name="mixer_b",
    )(scal, proj3, proj3, vt, proj_meta, vt_meta, tab, tabm, subln_g)


def _slab_load(ref, lead, rows):
    return jnp.concatenate([ref[lead + (pl.ds(s, rows, stride=SLAB), slice(None))] for s in range(SLAB)], axis=1)


def _slab_store(ref, lead, val):
    rows = val.shape[0]
    for s in range(SLAB):
        ref[lead + (pl.ds(s, rows, stride=SLAB), slice(None))] = val[:, s * LANES:(s + 1) * LANES]


def _outproj_kernel(x_ref, oa_ref, ob_ref, w_ref, eg_ref, eb_ref, g_ref, b_ref, o_ref, os_ref):
    h0 = _layernorm_f32(x_ref[...], eg_ref[...], eb_ref[...])
    half = oa_ref.shape[1]
    mix = _dot(oa_ref[...], w_ref[0:half, :]) + _dot(ob_ref[...], w_ref[half:, :])
    h1 = _layernorm_f32(ALPHA * h0 + mix, g_ref[...], b_ref[...])
    o_ref[...] = h1
    _slab_store(os_ref, (), h1)


def _outproj_ln1(x2d, oa, ob, w_out, eg, eb, g, b):
    t, d = x2d.shape
    tm = min(512, t)
    wa = oa.shape[1]
    wb = ob.shape[1]
    vec = pl.BlockSpec((1, d), lambda i: (0, 0))
    return pl.pallas_call(
        _outproj_kernel,
        out_shape=(jax.ShapeDtypeStruct((t, d), jnp.float32),
                   jax.ShapeDtypeStruct((t * SLAB, LANES), jnp.float32)),
        grid=(t // tm,),
        in_specs=[
            pl.BlockSpec((tm, d), lambda i: (i, 0)),
            pl.BlockSpec((tm, wa), lambda i: (i, 0)),
            pl.BlockSpec((tm, wb), lambda i: (i, 0)),
            pl.BlockSpec((wa + wb, d), lambda i: (0, 0)),
            vec, vec, vec, vec,
        ],
        out_specs=(pl.BlockSpec((tm, d), lambda i: (i, 0)),
                   pl.BlockSpec((tm * SLAB, LANES), lambda i: (i, 0))),
        compiler_params=_cparams(("parallel",)),
        name="outproj_ln1",
    )(x2d, oa, ob, w_out, eg, eb, g, b)


def _router_kernel(h_ref, wh_ref, wl_ref, rb_ref, e_ref, w_ref, cnt_ref, carry_ref, *, n_exp):
    i = pl.program_id(0)
    tn = h_ref.shape[0]
    gsz = n_exp // N_GROUPS

    @pl.when(i == 0)
    def _():
        carry_ref[...] = jnp.zeros_like(carry_ref)

    x = h_ref[...]
    xh = x.astype(jnp.bfloat16)
    xl = (x - xh.astype(jnp.float32)).astype(jnp.bfloat16)
    logits = _dot_nt(wh_ref[...], xh) + (_dot_nt(wh_ref[...], xl) + _dot_nt(wl_ref[...], xh))
    scores = 1.0 / (1.0 + jnp.exp(-logits))
    biased = scores + rb_ref[...]

    g3 = biased.reshape(N_GROUPS, gsz, tn)
    it3 = lax.broadcasted_iota(jnp.int32, (N_GROUPS, gsz, tn), 1)
    mx1 = jnp.max(g3, axis=1, keepdims=True)
    first = jnp.min(jnp.where(g3 == mx1, it3, gsz), axis=1, keepdims=True)
    mx2 = jnp.max(jnp.where(it3 == first, -jnp.inf, g3), axis=1, keepdims=True)
    gscore = (mx1 + mx2).reshape(N_GROUPS, tn)

    itg = lax.broadcasted_iota(jnp.int32, (N_GROUPS, tn), 0)
    gsel = jnp.zeros((N_GROUPS, tn), jnp.bool_)
    cur = gscore
    for _ in range(TOPK_GROUPS):
        mx = jnp.max(cur, axis=0, keepdims=True)
        fi = jnp.min(jnp.where(cur == mx, itg, N_GROUPS), axis=0, keepdims=True)
        hit = itg == fi
        gsel = jnp.logical_or(gsel, hit)
        cur = jnp.where(hit, -jnp.inf, cur)
    emask = jnp.broadcast_to(gsel.reshape(N_GROUPS, 1, tn), (N_GROUPS, gsz, tn)).reshape(n_exp, tn)
    cur = jnp.where(emask, biased, NEG)

    ite = lax.broadcasted_iota(jnp.int32, (n_exp, tn), 0)
    hits = []
    eidx = []
    wsel = []
    for _ in range(TOP_K):
        mx = jnp.max(cur, axis=0, keepdims=True)
        fi = jnp.min(jnp.where(cur == mx, ite, n_exp), axis=0, keepdims=True)
        hit = ite == fi
        hits.append(hit)
        eidx.append(fi)
        wsel.append(jnp.sum(jnp.where(hit, scores, 0.0), axis=0, keepdims=True))
        cur = jnp.where(hit, -jnp.inf, cur)
    sel = hits[0]
    for hit in hits[1:]:
        sel = jnp.logical_or(sel, hit)
    self32 = jnp.where(sel, 1.0, 0.0)

    carry_ref[...] = carry_ref[...] + jnp.sum(self32, axis=1, keepdims=True)
    cnt_ref[...] = carry_ref[...]

    wcat = jnp.concatenate(wsel, axis=0)
    wcat = wcat / jnp.sum(wcat, axis=0, keepdims=True) * ROUTED_SCALE
    tok = i * tn + lax.broadcasted_iota(jnp.int32, (TOP_K, tn), 1)
    slot = lax.broadcasted_iota(jnp.int32, (TOP_K, tn), 0)
    e_ref[...] = jnp.concatenate(eidx, axis=0) * (1 << KEY_SHIFT) + (tok * TOP_K + slot)
    w_ref[...] = wcat


def _router(h1, wr_hi, wr_lo, rbias):
    t, d = h1.shape
    n_exp = wr_hi.shape[0]
    tn = min(512, t)
    kern = functools.partial(_router_kernel, n_exp=n_exp)
    row = pl.BlockSpec((TOP_K, tn), lambda i: (0, i))
    return pl.pallas_call(
        kern,
        out_shape=(
            jax.ShapeDtypeStruct((TOP_K, t), jnp.int32),
            jax.ShapeDtypeStruct((TOP_K, t), jnp.float32),
            jax.ShapeDtypeStruct((n_exp, 1), jnp.float32),
        ),
        grid=(t // tn,),
        in_specs=[
            pl.BlockSpec((tn, d), lambda i: (i, 0)),
            pl.BlockSpec((n_exp, d), lambda i: (0, 0)),
            pl.BlockSpec((n_exp, d), lambda i: (0, 0)),
            pl.BlockSpec((n_exp, 1), lambda i: (0, 0)),
        ],
        out_specs=(row, row, pl.BlockSpec((n_exp, 1), lambda i: (0, 0))),
        scratch_shapes=[pltpu.VMEM((n_exp, 1), jnp.float32)],
        compiler_params=_cparams(("arbitrary",)),
        name="router",
    )(h1, wr_hi, wr_lo, rbias)


def _experts_kernel(te_ref, tv_ref, tu_ref, na_ref, first_w, prev_w, next_w, h_hbm, wgu_ref, wd_ref, y_hbm,
                    xb0, xb1, yb0, yb1, gsem, ssem, *, trash_row):
    i = pl.program_id(0)
    nt = pl.num_programs(0)
    n_act = na_ref[0]
    xbufs = (xb0, xb1)
    ybufs = (yb0, yb1)
    tile_rows = TM_MOE * SLAB

    def gather(slot, j, src):
        return pltpu.make_async_copy(h_hbm.at[pl.ds(pl.multiple_of(src, SLAB), SLAB), :],
                                     xbufs[slot].at[pl.ds(j * SLAB, SLAB), :], gsem.at[slot])

    def scatter(slot, j, dst):
        return pltpu.make_async_copy(ybufs[slot].at[pl.ds(j * SLAB, SLAB), :],
                                     y_hbm.at[pl.ds(pl.multiple_of(dst, SLAB), SLAB), :], ssem.at[slot])

    def wait_gathers(slot):
        pltpu.make_async_copy(h_hbm.at[pl.ds(0, tile_rows), :], xbufs[slot], gsem.at[slot]).wait()

    def wait_scatters(slot):
        pltpu.make_async_copy(ybufs[slot], y_hbm.at[pl.ds(0, tile_rows), :], ssem.at[slot]).wait()

    def issue_gathers(win, off, slot):
        for j in range(TM_MOE):
            gather(slot, j, win[0, 0, off + j]).start()

    def issue_scatters(win, off, nvalid, slot):
        for j in range(TM_MOE):
            dst = jnp.where(j < nvalid, win[0, 0, off + j], trash_row + j * SLAB)
            scatter(slot, j, dst).start()

    @pl.when(i == 0)
    def _():
        yb1[...] = jnp.zeros_like(yb1)
        issue_gathers(first_w, 0, 0)

    iprev = jnp.maximum(i - 1, 0)
    nv_prev = jnp.where(i >= 1, tv_ref[iprev], 0)
    off_prev = tu_ref[iprev] & (LANES - 1)
    off_next = tu_ref[jnp.minimum(i + 1, nt - 1)] & (LANES - 1)

    for slot in range(2):
        other = 1 - slot

        @pl.when(jnp.logical_and(i < n_act, (i & 1) == slot))
        def _(slot=slot, other=other):
            wait_gathers(slot)
            issue_gathers(next_w, off_next, other)
            x = _slab_load(xbufs[slot], (), TM_MOE).astype(jnp.bfloat16)
            gu = _dot(x, wgu_ref[0])
            de = gu.shape[1] // 2
            gate = gu[:, :de]
            hid = (gate / (1.0 + jnp.exp(-gate))) * gu[:, de:]
            y = _dot(hid.astype(jnp.bfloat16), wd_ref[0])
            issue_scatters(prev_w, off_prev, nv_prev, other)

            @pl.when(i >= 1)
            def _():
                wait_scatters(slot)

            _slab_store(ybufs[slot], (), y)

        @pl.when(jnp.logical_and(i == n_act, (i & 1) == slot))
        def _(slot=slot, other=other):
            wait_gathers(slot)
            issue_scatters(prev_w, off_prev, nv_prev, other)
            wait_scatters(slot)
            wait_scatters(other)


def _overlapped_windows(v, n_rows, width):
    v2 = jnp.concatenate([v, jnp.zeros((width,), v.dtype)]).reshape(-1, LANES)
    out = jnp.concatenate([v2[q:q + n_rows] for q in range(width // LANES)], axis=1)
    return out.reshape(n_rows, 1, width)


def _experts(h_slab, src, dst, wgu, wd, tile_e, tile_valid, tile_u, n_active, n_tok):
    nt = tile_e.shape[0]
    d = wgu.shape[1]
    de2 = wgu.shape[2]
    last = nt - 1
    width = TM_MOE + LANES
    n_win = src.shape[0] // LANES
    src_w = _overlapped_windows(src, n_win, width)
    dst_w = _overlapped_windows(dst, n_win, width)
    out_rows = TOP_K * n_tok * SLAB

    def win(shift):
        return pl.BlockSpec((1, 1, width),
                            lambda i, te, tv, tu, na: (tu[jnp.clip(i + shift, 0, last)] // LANES, 0, 0),
                            memory_space=pltpu.SMEM)

    first = pl.BlockSpec((1, 1, width), lambda i, te, tv, tu, na: (0, 0, 0), memory_space=pltpu.SMEM)
    kern = functools.partial(_experts_kernel, trash_row=out_rows)
    buf = pltpu.VMEM((TM_MOE * SLAB, LANES), jnp.float32)
    return pl.pallas_call(
        kern,
        out_shape=jax.ShapeDtypeStruct((out_rows + TM_MOE * SLAB, LANES), jnp.float32),
        grid_spec=pltpu.PrefetchScalarGridSpec(
            num_scalar_prefetch=4,
            grid=(nt,),
            in_specs=[
                first, win(-1), win(1),
                pl.BlockSpec(memory_space=pl.ANY),
                pl.BlockSpec((1, d, de2), lambda i, te, tv, tu, na: (te[i], 0, 0)),
                pl.BlockSpec((1, de2 // 2, d), lambda i, te, tv, tu, na: (te[i], 0, 0)),
            ],
            out_specs=pl.BlockSpec(memory_space=pl.ANY),
            scratch_shapes=[buf, buf, buf, buf, pltpu.SemaphoreType.DMA((2,)), pltpu.SemaphoreType.DMA((2,))],
        ),
        compiler_params=_cparams(("arbitrary",)),
        name="experts",
    )(tile_e, tile_valid, tile_u, n_active, src_w, dst_w, src_w, h_slab, wgu, wd)


def _combine_kernel(h_ref, w_ref, *rest):
    y_refs = rest[:TOP_K]
    sgu_ref, sd_ref, g_ref, b_ref, o_ref = rest[TOP_K:]
    tn = h_ref.shape[0]
    h = h_ref[...]
    gu = _dot(h.astype(jnp.bfloat16), sgu_ref[...])
    ds = gu.shape[1] // 2
    gate = gu[:, :ds]
    hid = (gate / (1.0 + jnp.exp(-gate))) * gu[:, ds:]
    shared = _dot(hid.astype(jnp.bfloat16), sd_ref[...])

    w = w_ref[...]
    routed = _slab_load(y_refs[0], (), tn) * w[:, 0:1]
    for k in range(1, TOP_K):
        routed = routed + _slab_load(y_refs[k], (), tn) * w[:, k:k + 1]
    o_ref[...] = _layernorm_f32(ALPHA * h + (routed + shared), g_ref[...], b_ref[...])


def _combine(h1, w_tok, y_slab, sgu, sd, g, b):
    t, d = h1.shape
    tn = min(256, t)
    nblk = t // tn
    ds2 = sgu.shape[1]
    vec = pl.BlockSpec((1, d), lambda i: (0, 0))

    def slot_spec(k):
        return pl.BlockSpec((tn * SLAB, LANES), lambda i: (k * nblk + i, 0))

    return pl.pallas_call(
        _combine_kernel,
        out_shape=jax.ShapeDtypeStruct((t, d), jnp.float32),
        grid=(nblk,),
        in_specs=[
            pl.BlockSpec((tn, d), lambda i: (i, 0)),
            pl.BlockSpec((tn, TOP_K), lambda i: (i, 0)),
            *[slot_spec(k) for k in range(TOP_K)],
            pl.BlockSpec((d, ds2), lambda i: (0, 0)),
            pl.BlockSpec((ds2 // 2, d), lambda i: (0, 0)),
            vec, vec,
        ],
        out_specs=pl.BlockSpec((tn, d), lambda i: (i, 0)),
        compiler_params=_cparams(("parallel",)),
        name="combine_ln2",
    )(h1, w_tok, *([y_slab] * TOP_K), sgu, sd, g, b)


def _rel_bucket(rel):
    nb = N_BUCKETS // 2
    max_exact = nb // 2
    ret = jnp.where(rel > 0, nb, 0)
    n = jnp.abs(rel)
    nf = jnp.maximum(n, 1).astype(jnp.float32)
    large = max_exact + (jnp.log(nf / max_exact) / math.log(MAX_DISTANCE / max_exact) * (nb - max_exact)).astype(jnp.int32)
    large = jnp.minimum(large, nb - 1)
    return ret + jnp.where(n < max_exact, n, large)


def _bias_of_rel(rel_bias, rel):
    return rel_bias.astype(jnp.float32)[_rel_bucket(rel)]


def _toeplitz(vec, nrow, ncol, off):
    lo = off - (nrow - 1)
    v = vec[lo:off + ncol]
    p = v.shape[0] + 1
    v = jnp.concatenate([v, v[:1]], axis=0)
    flat = jnp.tile(v, (nrow + 1, 1))
    base = off - lo
    out = flat[base:base + nrow * (p - 1)].reshape(nrow, p - 1, vec.shape[1])
    return out[:, :ncol]


def _bias_tables_a(rel_bias, rows):
    bias_a = rel_bias[:, :A_Q_HEADS]
    span = 2 * TQ_A
    rel = jnp.arange(-span, span + 1, dtype=jnp.int32)
    vec = jnp.where((jnp.abs(rel) <= WINDOW)[:, None], _bias_of_rel(bias_a, rel), NEG)
    band = jnp.transpose(_toeplitz(vec, TQ_A, 3 * TQ_A, span - TQ_A), (2, 0, 1))
    tabs = []
    for base in (0, rows):
        lo = -(N_META + base + rows - 1)
        relm = jnp.arange(lo, N_META, dtype=jnp.int32)
        vm = _bias_of_rel(bias_a, relm)
        tabs.append(jnp.transpose(_toeplitz(vm, rows, N_META, -(N_META + base) - lo), (2, 0, 1)))
    return band, jnp.stack(tabs)


def _bias_tables_b(rel_bias, s):
    bias_b = rel_bias[:, A_Q_HEADS:]
    near = []
    for d in (-1, 0, 1):
        m = jnp.arange(TQ_B + TK_B - 1, dtype=jnp.int32)
        vec = _bias_of_rel(bias_b, TK_B * d + TK_B - 1 - m)
        near.append(_toeplitz(vec, TK_B, TQ_B, TK_B - 1))
    far_l = jnp.broadcast_to(_bias_of_rel(bias_b, jnp.int32(-TK_B - 1)), (TK_B, TQ_B, B_HEADS))
    far_r = jnp.broadcast_to(_bias_of_rel(bias_b, jnp.int32(TK_B + 1)), (TK_B, TQ_B, B_HEADS))
    tabs = jnp.transpose(jnp.stack([far_l] + near + [far_r]), (3, 0, 1, 2)) * LOG2E
    m = jnp.arange(s + N_META - 1, dtype=jnp.int32)
    vec = _bias_of_rel(bias_b, -1 - m)
    meta = jnp.transpose(_toeplitz(vec, N_META, s, N_META - 1), (2, 0, 1)) * LOG2E
    return tabs, meta


def _prep_w_in(w_in):
    a_w = A_Q_HEADS * HEAD_DIM
    kv = A_KV_HEADS * HEAD_DIM
    bqk = B_HEADS * 2 * HEAD_DIM
    scale = HEAD_DIM ** -0.5
    qa = w_in[:, :a_w] * scale
    ka = w_in[:, a_w:a_w + kv]
    va = w_in[:, a_w + kv:a_w + 2 * kv]
    o = a_w + 2 * kv
    qb = w_in[:, o:o + bqk] * (scale * LOG2E)
    kb = w_in[:, o + bqk:o + 2 * bqk]
    vb = w_in[:, o + 2 * bqk:]

    def dup(w):
        return jnp.concatenate([w[:, g * HEAD_DIM:(g + 1) * HEAD_DIM] for g in range(A_KV_HEADS) for _ in range(2)], axis=1)

    w = jnp.concatenate([qa, dup(ka), dup(va), qb, kb], axis=1).astype(jnp.bfloat16)
    return w, vb.T.astype(jnp.bfloat16)


def _trunk(x, prm):
    bsz, s, d = x.shape
    t = bsz * s
    x2d = x.reshape(t, d)
    proj, vt = _ln_inproj(x2d, prm["ln_emb_g"], prm["ln_emb_b"], prm["w_in"], prm["w_vt"])
    proj3 = proj.reshape(bsz, s, PROJ_COLS)
    sub = 4 if (s // TQ_A) % 4 == 0 else 1
    tab_a, tabm_a = _bias_tables_a(prm["rel_bias"], sub * TQ_A)
    oa = _mixer_a(proj3, prm["proj_meta"], tab_a, tabm_a, prm["sink"])
    tabs_b, meta_b = _bias_tables_b(prm["rel_bias"], s)
    ob = _mixer_b(proj3, vt, prm["proj_meta"], prm["vt_meta"], tabs_b, meta_b, prm["lam"].reshape(1).astype(jnp.float32),
                  prm["subln_g"])
    h1, h1_slab = _outproj_ln1(x2d, oa.reshape(t, -1), ob.reshape(t, -1), prm["w_out"], prm["ln_emb_g"],
                               prm["ln_emb_b"], prm["ln1_g"], prm["ln1_b"])

    keys, wts, counts = _router(h1, prm["wr_hi"], prm["wr_lo"], prm["router_bias"])
    n_exp = counts.shape[0]
    n_asg = t * TOP_K
    assert n_asg <= (1 << KEY_SHIFT) and n_asg % LANES == 0
    order = jnp.sort(keys.reshape(n_asg)) & ((1 << KEY_SHIFT) - 1)
    tok = order >> 3
    src = tok * SLAB
    dst = ((order & (TOP_K - 1)) * t + tok) * SLAB
    counts = counts[:, 0].astype(jnp.int32)
    tiles_e = (counts + TM_MOE - 1) // TM_MOE
    tend = jnp.cumsum(tiles_e)
    tstart = tend - tiles_e
    ustart = jnp.cumsum(counts) - counts
    nt = n_asg // TM_MOE + n_exp + 1
    tid = jnp.arange(nt, dtype=jnp.int32)
    tile_e = jnp.minimum(jnp.sum((tend[None, :] <= tid[:, None]).astype(jnp.int32), axis=1), n_exp - 1)
    onehot = (tile_e[:, None] == jnp.arange(n_exp, dtype=jnp.int32)[None, :]).astype(jnp.int32)
    in_e = (tid - jnp.sum(onehot * tstart[None, :], axis=1)) * TM_MOE
    active = tid < tend[-1]
    tile_valid = jnp.where(active, jnp.clip(jnp.sum(onehot * counts[None, :], axis=1) - in_e, 0, TM_MOE), 0)
    tile_u = jnp.where(active, jnp.sum(onehot * ustart[None, :], axis=1) + in_e, 0)

    y_slab = _experts(h1_slab, src, dst, prm["w_gu"], prm["w_down"], tile_e.astype(jnp.int32),
                      tile_valid.astype(jnp.int32), tile_u.astype(jnp.int32), tend[-1:].astype(jnp.int32), t)
    out = _combine(h1, wts.T, y_slab, prm["ws_gu"], prm["ws_down"], prm["ln2_g"], prm["ln2_b"])
    return out.reshape(bsz, s, d)


def kernel(x_prompt, x_sample, meta_tokens, ln_emb_g, ln_emb_b, rel_bias, w_in, attn_sink, lambda_q1, lambda_k1, lambda_q2, lambda_k2, subln_g, w_out, ln1_g, ln1_b, w_router, router_bias, w_gate, w_up, w_down, ws_gate, ws_up, ws_down, ln2_g, ln2_b):
    f32 = jnp.float32
    bf16 = jnp.bfloat16
    l = 0
    row = lambda v: v.reshape(1, -1).astype(f32)
    lam = (jnp.exp(jnp.sum(lambda_q1[l].astype(f32) * lambda_k1[l].astype(f32)))
           - jnp.exp(jnp.sum(lambda_q2[l].astype(f32) * lambda_k2[l].astype(f32))) + LAMBDA_INIT)
    wr_t = w_router[l].astype(f32).T
    wr_hi = wr_t.astype(bf16)
    prm = {
        "ln_emb_g": row(ln_emb_g), "ln_emb_b": row(ln_emb_b),
        "rel_bias": rel_bias,
        "sink": attn_sink[l].astype(f32),
        "lam": lam,
        "subln_g": subln_g[l].astype(f32).reshape(-1, 1),
        "w_out": w_out[l].astype(bf16),
        "ln1_g": row(ln1_g[l]), "ln1_b": row(ln1_b[l]),
        "wr_hi": wr_hi, "wr_lo": (wr_t - wr_hi.astype(f32)).astype(bf16),
        "router_bias": router_bias[l].astype(f32).reshape(-1, 1),
        "w_gu": jnp.concatenate([w_gate[l].astype(bf16), w_up[l].astype(bf16)], axis=-1),
        "w_down": w_down[l].astype(bf16),
        "ws_gu": jnp.concatenate([ws_gate[l], ws_up[l]], axis=-1).astype(bf16),
        "ws_down": ws_down[l].astype(bf16),
        "ln2_g": row(ln2_g[l]), "ln2_b": row(ln2_b[l]),
    }
    prm["w_in"], prm["w_vt"] = _prep_w_in(w_in[l])
    prm["proj_meta"], prm["vt_meta"] = _ln_inproj(meta_tokens.astype(f32), prm["ln_emb_g"], prm["ln_emb_b"], prm["w_in"],
                                                 prm["w_vt"])
    return (_trunk(x_prompt, prm), _trunk(x_sample, prm))
```

```python
import functools
import math

import jax
import jax.numpy as jnp
from jax import lax
from jax.experimental import pallas as pl
from jax.experimental.pallas import tpu as pltpu

N_META = 16
HEAD_DIM = 64
WINDOW = 128
A_Q_HEADS = 8
A_KV_HEADS = 2
B_HEADS = 4
N_BUCKETS = 32
MAX_DISTANCE = 128
TOP_K = 8
N_GROUPS = 8
TOPK_GROUPS = 4
ROUTED_SCALE = 2.5
LN_EPS = 1e-5
DEPTH = 1
ALPHA = (2 * DEPTH) ** 0.25
NEG = -1e30
LAMBDA_INIT = 0.8 - 0.6 * math.exp(-0.3 * 0)

LANES = 128
VMEM_LIMIT = 48 * 1024 * 1024

QA_BLK = 0
KA_BLK = 4
QB_BLK = 6
KB_BLK = 10
PROJ_COLS = 14 * LANES
VT_ROWS = LANES + 16
VA_ROWS = HEAD_DIM + 16
LOG2E = 1.4426950408889634

TQ_A = 128
NK_A = 4 * TQ_A
TQ_B = 512
TK_B = 512
TM_MOE = 256
SLAB = 8
KEY_SHIFT = 20


def _cparams(sem):
    return pltpu.CompilerParams(dimension_semantics=sem, vmem_limit_bytes=VMEM_LIMIT)


def _layernorm_f32(x, g, b):
    mu = jnp.mean(x, axis=-1, keepdims=True)
    xc = x - mu
    var = jnp.mean(xc * xc, axis=-1, keepdims=True)
    return xc * lax.rsqrt(var + LN_EPS) * g + b


def _dot_nt(a, b):
    return lax.dot_general(a, b, (((1,), (1,)), ((), ())), preferred_element_type=jnp.float32)


def _dot(a, b):
    return jnp.dot(a, b, preferred_element_type=jnp.float32)


def _ln_inproj_kernel(x_ref, g_ref, b_ref, w_ref, wvt_ref, o_ref, vt_ref, vta_ref):
    h = _layernorm_f32(x_ref[...], g_ref[...], b_ref[...]).astype(jnp.bfloat16)
    o_ref[...] = _dot(h, w_ref[...]).astype(o_ref.dtype)
    vt = _dot_nt(wvt_ref[...], h).astype(vt_ref.dtype)
    ones = jnp.ones((16, vt.shape[1]), vt_ref.dtype)
    for hd in range(B_HEADS):
        vt_ref[hd * VT_ROWS:hd * VT_ROWS + LANES, :] = vt[hd * LANES:(hd + 1) * LANES, :]
        vt_ref[hd * VT_ROWS + LANES:(hd + 1) * VT_ROWS, :] = ones
    base = B_HEADS * LANES
    for g in range(A_KV_HEADS):
        vta_ref[g * VA_ROWS:g * VA_ROWS + HEAD_DIM, :] = vt[base + g * HEAD_DIM:base + (g + 1) * HEAD_DIM, :]
        vta_ref[g * VA_ROWS + HEAD_DIM:(g + 1) * VA_ROWS, :] = ones


def _ln_inproj(x2d, g, b, w, wvt):
    t, d = x2d.shape
    n = w.shape[1]
    tm = min(512, t)
    return pl.pallas_call(
        _ln_inproj_kernel,
        out_shape=(jax.ShapeDtypeStruct((t, n), jnp.bfloat16),
                   jax.ShapeDtypeStruct((B_HEADS * VT_ROWS, t), jnp.bfloat16),
                   jax.ShapeDtypeStruct((A_KV_HEADS * VA_ROWS, t), jnp.bfloat16)),
        grid=(t // tm,),
        in_specs=[
            pl.BlockSpec((tm, d), lambda i: (i, 0)),
            pl.BlockSpec((1, d), lambda i: (0, 0)),
            pl.BlockSpec((1, d), lambda i: (0, 0)),
            pl.BlockSpec((d, n), lambda i: (0, 0)),
            pl.BlockSpec((wvt.shape[0], d), lambda i: (0, 0)),
        ],
        out_specs=(pl.BlockSpec((tm, n), lambda i: (i, 0)),
                   pl.BlockSpec((B_HEADS * VT_ROWS, tm), lambda i: (0, i)),
                   pl.BlockSpec((A_KV_HEADS * VA_ROWS, tm), lambda i: (0, i))),
        compiler_params=_cparams(("parallel",)),
        name="ln_inproj",
    )(x2d, g, b, w, wvt)


def _mixer_a_kernel(q_ref, k_ref, vt_ref, km_ref, vtm_ref, tab_ref, sink_ref, o_ref, *, nblk, sub):
    i = pl.program_id(1)
    lane = lax.broadcasted_iota(jnp.int32, (1, LANES), 1)
    lo = lane < HEAD_DIM
    for j in range(sub):
        gi = i * sub + j
        sp = pl.multiple_of(jnp.maximum(gi - 1, 0) * TQ_A, TQ_A)
        sc = pl.multiple_of(gi * TQ_A, TQ_A)
        sn = pl.multiple_of(jnp.minimum(gi + 1, nblk - 1) * TQ_A, TQ_A)
        variant = jnp.where(gi == 0, 0, jnp.where(gi == nblk - 1, 2, 1))
        rows = slice(j * TQ_A, (j + 1) * TQ_A)
        for g in range(A_KV_HEADS):
            gs = slice(g * LANES, (g + 1) * LANES)
            vr = slice(g * VA_ROWS, (g + 1) * VA_ROWS)
            k_all = jnp.concatenate([k_ref[0, pl.ds(sp, TQ_A), gs], k_ref[0, pl.ds(sc, TQ_A), gs],
                                     k_ref[0, pl.ds(sn, TQ_A), gs], km_ref[:, gs]], axis=0)
            vt_all = jnp.concatenate([vt_ref[vr, pl.ds(sp, TQ_A)], vt_ref[vr, pl.ds(sc, TQ_A)],
                                      vt_ref[vr, pl.ds(sn, TQ_A)], vtm_ref[vr, :]], axis=1)
            for pp in range(2):
                hp = 2 * g + pp
                cols = slice(hp * LANES, (hp + 1) * LANES)
                qc = q_ref[0, rows, cols]
                q2 = jnp.concatenate([jnp.where(lo, qc, jnp.zeros_like(qc)), jnp.where(lo, jnp.zeros_like(qc), qc)],
                                     axis=0)
                s = _dot_nt(k_all, q2) + tab_ref[variant, hp]
                sink = sink_ref[hp]
                m = jnp.maximum(jnp.max(s, axis=0, keepdims=True), sink)
                p = jnp.exp2(s - m)
                acc = _dot(vt_all, p.astype(jnp.bfloat16))
                o = acc[0:HEAD_DIM, :] / (acc[HEAD_DIM:HEAD_DIM + 1, :] + jnp.exp2(sink - m))
                o2 = jnp.concatenate([o[:, :TQ_A], o[:, TQ_A:]], axis=0)
                o_ref[0, rows, cols] = o2.T.astype(o_ref.dtype)


def _mixer_a(proj3, vta, km, vtm, tab, sink):
    bsz, s, _ = proj3.shape
    nblk = s // TQ_A
    assert nblk >= 2
    sub = 4 if nblk % 4 == 0 else 1
    nq = nblk // sub
    tq = sub * TQ_A
    kern = functools.partial(_mixer_a_kernel, nblk=nblk, sub=sub)
    return pl.pallas_call(
        kern,
        out_shape=jax.ShapeDtypeStruct((bsz, s, A_Q_HEADS * HEAD_DIM), jnp.bfloat16),
        grid=(bsz, nq),
        in_specs=[
            pl.BlockSpec((1, tq, 4 * LANES), lambda b, i: (b, i, QA_BLK // 4)),
            pl.BlockSpec((1, s, 2 * LANES), lambda b, i: (b, 0, KA_BLK // 2)),
            pl.BlockSpec((A_KV_HEADS * VA_ROWS, s), lambda b, i: (0, b)),
            pl.BlockSpec(km.shape, lambda b, i: (0, 0)),
            pl.BlockSpec(vtm.shape, lambda b, i: (0, 0)),
            pl.BlockSpec(tab.shape, lambda b, i: (0, 0, 0, 0)),
            pl.BlockSpec(sink.shape, lambda b, i: (0, 0, 0)),
        ],
        out_specs=pl.BlockSpec((1, tq, 4 * LANES), lambda b, i: (b, i, 0)),
        compiler_params=_cparams(("parallel", "arbitrary")),
        name="mixer_a",
    )(proj3, proj3, vta, km, vtm, tab, sink)


def _mixer_b_kernel(sc_ref, q_ref, k_ref, vt_ref, km_ref, vtm_ref, tab_ref, tabm_ref, g_ref, o_ref,
                    m_ref, acc_ref, s0_ref, s1_ref, *, nk):
    h = pl.program_id(1)
    i = pl.program_id(2)
    lane = lax.broadcasted_iota(jnp.int32, (1, LANES), 1)
    lo = lane < HEAD_DIM
    q = q_ref[0]
    qs = (jnp.where(lo, q, jnp.zeros_like(q)), jnp.where(lo, jnp.zeros_like(q), q))
    lam = sc_ref[0]
    sbufs = (s0_ref, s1_ref)

    km = km_ref[...]
    vtm = vtm_ref[...]
    for c in range(2):
        s = _dot_nt(km, qs[c]) + tabm_ref[0]
        m = jnp.max(s, axis=0, keepdims=True)
        p = jnp.exp2(s - m)
        m_ref[c] = m
        acc_ref[c] = _dot(vtm, p.astype(jnp.bfloat16))

    def scores(j, slot):
        start = pl.multiple_of(j * TK_B, TK_B)
        kj = k_ref[0, pl.ds(start, TK_B), :]
        bias = tab_ref[0, jnp.clip(j - i, -2, 2) + 2]
        for c in range(2):
            sbufs[slot][c] = _dot_nt(kj, qs[c]) + bias

    def accumulate(j, slot):
        start = pl.multiple_of(j * TK_B, TK_B)
        vtj = vt_ref[:, pl.ds(start, TK_B)]
        for c in range(2):
            s = sbufs[slot][c]
            m_prev = m_ref[c]
            m_new = jnp.maximum(m_prev, jnp.max(s, axis=0, keepdims=True))
            a = jnp.exp2(m_prev - m_new)
            p = jnp.exp2(s - m_new)
            m_ref[c] = m_new
            acc_ref[c] = a * acc_ref[c] + _dot(vtj, p.astype(jnp.bfloat16))

    scores(0, 0)

    def pair(jj, carry):
        j = 2 * jj
        scores(j + 1, 1)
        accumulate(j, 0)
        scores(j + 2, 0)
        accumulate(j + 1, 1)
        return carry

    lax.fori_loop(0, nk // 2 - 1, pair, 0)
    scores(nk - 1, 1)
    accumulate(nk - 2, 0)
    accumulate(nk - 1, 1)

    o0 = acc_ref[0, 0:LANES, :] / acc_ref[0, LANES:LANES + 1, :]
    o1 = acc_ref[1, 0:LANES, :] / acc_ref[1, LANES:LANES + 1, :]
    o = o0 - lam * o1
    ms = jnp.mean(o * o, axis=0, keepdims=True)
    o = o * lax.rsqrt(ms + LN_EPS) * (g_ref[...] * (1.0 - LAMBDA_INIT))
    o_ref[0] = o.T.astype(o_ref.dtype)


def _mixer_b(proj3, vt, proj_meta, vt_meta, tab, tabm, scal, subln_g):
    bsz, s, _ = proj3.shape
    nq = s // TQ_B
    nk = s // TK_B
    assert nk % 2 == 0
    kern = functools.partial(_mixer_b_kernel, nk=nk)
    return pl.pallas_call(
        kern,
        out_shape=jax.ShapeDtypeStruct((bsz, s, B_HEADS * LANES), jnp.bfloat16),
        grid=(bsz, B_HEADS, nq),
        in_specs=[
            pl.BlockSpec(memory_space=pltpu.SMEM),
            pl.BlockSpec((1, TQ_B, LANES), lambda b, h, i: (b, i, QB_BLK + h)),
            pl.BlockSpec((1, s, LANES), lambda b, h, i: (b, 0, KB_BLK + h)),
            pl.BlockSpec((VT_ROWS, s), lambda b, h, i: (h, b)),
            pl.BlockSpec((N_META, LANES), lambda b, h, i: (0, KB_BLK + h)),
            pl.BlockSpec((VT_ROWS, N_META), lambda b, h, i: (h, 0)),
            pl.BlockSpec((1, 5, TK_B, TQ_B), lambda b, h, i: (h, 0, 0, 0)),
            pl.BlockSpec((1, N_META, TQ_B), lambda b, h, i: (h, 0, i)),
            pl.BlockSpec((LANES, 1), lambda b, h, i: (0, 0)),
        ],
        out_specs=pl.BlockSpec((1, TQ_B, LANES), lambda b, h, i: (b, i, h)),
        scratch_shapes=[
            pltpu.VMEM((2, 1, TQ_B), jnp.float32),
            pltpu.VMEM((2, VT_ROWS, TQ_B), jnp.float32),
            pltpu.VMEM((2, TK_B, TQ_B), jnp.float32),
            pltpu.VMEM((2, TK_B, TQ_B), jnp.float32),
        ],
        compiler_params=_cparams(("parallel", "parallel", "arbitrary")),
        name="mixer_b",
    )(scal, proj3, proj3, vt, proj_meta, vt_meta, tab, tabm, subln_g)


def _slab_load(ref, lead, rows):
    return jnp.concatenate([ref[lead + (pl.ds(s, rows, stride=SLAB), slice(None))] for s in range(SLAB)], axis=1)


def _slab_store(ref, lead, val):
    rows = val.shape[0]
    for s in range(SLAB):
        ref[lead + (pl.ds(s, rows, stride=SLAB), slice(None))] = val[:, s * LANES:(s + 1) * LANES]


def _outproj_kernel(x_ref, oa_ref, ob_ref, w_ref, eg_ref, eb_ref, g_ref, b_ref, o_ref, os_ref):
    h0 = _layernorm_f32(x_ref[...], eg_ref[...], eb_ref[...])
    half = oa_ref.shape[1]
    mix = _dot(oa_ref[...], w_ref[0:half, :]) + _dot(ob_ref[...], w_ref[half:, :])
    h1 = _layernorm_f32(ALPHA * h0 + mix, g_ref[...], b_ref[...])
    o_ref[...] = h1
    _slab_store(os_ref, (), h1)


def _outproj_ln1(x2d, oa, ob, w_out, eg, eb, g, b):
    t, d = x2d.shape
    tm = min(512, t)
    wa = oa.shape[1]
    wb = ob.shape[1]
    vec = pl.BlockSpec((1, d), lambda i: (0, 0))
    return pl.pallas_call(
        _outproj_kernel,
        out_shape=(jax.ShapeDtypeStruct((t, d), jnp.float32),
                   jax.ShapeDtypeStruct((t * SLAB, LANES), jnp.float32)),
        grid=(t // tm,),
        in_specs=[
            pl.BlockSpec((tm, d), lambda i: (i, 0)),
            pl.BlockSpec((tm, wa), lambda i: (i, 0)),
            pl.BlockSpec((tm, wb), lambda i: (i, 0)),
            pl.BlockSpec((wa + wb, d), lambda i: (0, 0)),
            vec, vec, vec, vec,
        ],
        out_specs=(pl.BlockSpec((tm, d), lambda i: (i, 0)),
                   pl.BlockSpec((tm * SLAB, LANES), lambda i: (i, 0))),
        compiler_params=_cparams(("parallel",)),
        name="outproj_ln1",
    )(x2d, oa, ob, w_out, eg, eb, g, b)


def _router_kernel(h_ref, wh_ref, wl_ref, rb_ref, e_ref, w_ref, cnt_ref, carry_ref, *, n_exp):
    i = pl.program_id(0)
    tn = h_ref.shape[0]
    gsz = n_exp // N_GROUPS

    @pl.when(i == 0)
    def _():
        carry_ref[...] = jnp.zeros_like(carry_ref)

    x = h_ref[...]
    xh = x.astype(jnp.bfloat16)
    xl = (x - xh.astype(jnp.float32)).astype(jnp.bfloat16)
    logits = _dot_nt(wh_ref[...], xh) + (_dot_nt(wh_ref[...], xl) + _dot_nt(wl_ref[...], xh))
    scores = 1.0 / (1.0 + jnp.exp(-logits))
    biased = scores + rb_ref[...]

    g3 = biased.reshape(N_GROUPS, gsz, tn)
    it3 = lax.broadcasted_iota(jnp.int32, (N_GROUPS, gsz, tn), 1)
    mx1 = jnp.max(g3, axis=1, keepdims=True)
    first = jnp.min(jnp.where(g3 == mx1, it3, gsz), axis=1, keepdims=True)
    mx2 = jnp.max(jnp.where(it3 == first, -jnp.inf, g3), axis=1, keepdims=True)
    gscore = (mx1 + mx2).reshape(N_GROUPS, tn)

    itg = lax.broadcasted_iota(jnp.int32, (N_GROUPS, tn), 0)
    gsel = jnp.zeros((N_GROUPS, tn), jnp.bool_)
    cur = gscore
    for _ in range(TOPK_GROUPS):
        mx = jnp.max(cur, axis=0, keepdims=True)
        fi = jnp.min(jnp.where(cur == mx, itg, N_GROUPS), axis=0, keepdims=True)
        hit = itg == fi
        gsel = jnp.logical_or(gsel, hit)
        cur = jnp.where(hit, -jnp.inf, cur)
    emask = jnp.broadcast_to(gsel.reshape(N_GROUPS, 1, tn), (N_GROUPS, gsz, tn)).reshape(n_exp, tn)
    cur = jnp.where(emask, biased, NEG)

    ite = lax.broadcasted_iota(jnp.int32, (n_exp, tn), 0)
    hits = []
    eidx = []
    wsel = []
    for _ in range(TOP_K):
        mx = jnp.max(cur, axis=0, keepdims=True)
        fi = jnp.min(jnp.where(cur == mx, ite, n_exp), axis=0, keepdims=True)
        hit = ite == fi
        hits.append(hit)
        eidx.append(fi)
        wsel.append(jnp.sum(jnp.where(hit, scores, 0.0), axis=0, keepdims=True))
        cur = jnp.where(hit, -jnp.inf, cur)
    sel = hits[0]
    for hit in hits[1:]:
        sel = jnp.logical_or(sel, hit)
    self32 = jnp.where(sel, 1.0, 0.0)

    carry_ref[...] = carry_ref[...] + jnp.sum(self32, axis=1, keepdims=True)
    cnt_ref[...] = carry_ref[...]

    wcat = jnp.concatenate(wsel, axis=0)
    wcat = wcat / jnp.sum(wcat, axis=0, keepdims=True) * ROUTED_SCALE
    tok = i * tn + lax.broadcasted_iota(jnp.int32, (TOP_K, tn), 1)
    slot = lax.broadcasted_iota(jnp.int32, (TOP_K, tn), 0)
    e_ref[...] = jnp.concatenate(eidx, axis=0) * (1 << KEY_SHIFT) + (tok * TOP_K + slot)
    w_ref[...] = wcat


def _router(h1, wr_hi, wr_lo, rbias):
    t, d = h1.shape
    n_exp = wr_hi.shape[0]
    tn = min(512, t)
    kern = functools.partial(_router_kernel, n_exp=n_exp)
    row = pl.BlockSpec((TOP_K, tn), lambda i: (0, i))
    return pl.pallas_call(
        kern,
        out_shape=(
            jax.ShapeDtypeStruct((TOP_K, t), jnp.int32),
            jax.ShapeDtypeStruct((TOP_K, t), jnp.float32),
            jax.ShapeDtypeStruct((n_exp, 1), jnp.float32),
        ),
        grid=(t // tn,),
        in_specs=[
            pl.BlockSpec((tn, d), lambda i: (i, 0)),
            pl.BlockSpec((n_exp, d), lambda i: (0, 0)),
            pl.BlockSpec((n_exp, d), lambda i: (0, 0)),
            pl.BlockSpec((n_exp, 1), lambda i: (0, 0)),
        ],
        out_specs=(row, row, pl.BlockSpec((n_exp, 1), lambda i: (0, 0))),
        scratch_shapes=[pltpu.VMEM((n_exp, 1), jnp.float32)],
        compiler_params=_cparams(("arbitrary",)),
        name="router",
    )(h1, wr_hi, wr_lo, rbias)


def _experts_kernel(te_ref, tv_ref, tu_ref, na_ref, first_w, prev_w, next_w, h_hbm, wgu_ref, wd_ref, y_hbm,
                    xb0, xb1, yb0, yb1, gsem, ssem, *, trash_row):
    i = pl.program_id(0)
    nt = pl.num_programs(0)
    n_act = na_ref[0]
    xbufs = (xb0, xb1)
    ybufs = (yb0, yb1)
    tile_rows = TM_MOE * SLAB

    def gather(slot, j, src):
        return pltpu.make_async_copy(h_hbm.at[pl.ds(pl.multiple_of(src, SLAB), SLAB), :],
                                     xbufs[slot].at[pl.ds(j * SLAB, SLAB), :], gsem.at[slot])

    def scatter(slot, j, dst):
        return pltpu.make_async_copy(ybufs[slot].at[pl.ds(j * SLAB, SLAB), :],
                                     y_hbm.at[pl.ds(pl.multiple_of(dst, SLAB), SLAB), :], ssem.at[slot])

    def wait_gathers(slot):
        pltpu.make_async_copy(h_hbm.at[pl.ds(0, tile_rows), :], xbufs[slot], gsem.at[slot]).wait()

    def wait_scatters(slot):
        pltpu.make_async_copy(ybufs[slot], y_hbm.at[pl.ds(0, tile_rows), :], ssem.at[slot]).wait()

    def issue_gathers(win, off, slot):
        for j in range(TM_MOE):
            gather(slot, j, win[0, 0, off + j]).start(priority=j % 2)

    def issue_scatters(win, off, nvalid, slot):
        for j in range(TM_MOE):
            dst = jnp.where(j < nvalid, win[0, 0, off + j], trash_row + j * SLAB)
            scatter(slot, j, dst).start(priority=j % 2)

    @pl.when(i == 0)
    def _():
        yb1[...] = jnp.zeros_like(yb1)
        issue_gathers(first_w, 0, 0)

    iprev = jnp.maximum(i - 1, 0)
    nv_prev = jnp.where(i >= 1, tv_ref[iprev], 0)
    off_prev = tu_ref[iprev] & (LANES - 1)
    off_next = tu_ref[jnp.minimum(i + 1, nt - 1)] & (LANES - 1)

    for slot in range(2):
        other = 1 - slot

        @pl.when(jnp.logical_and(i < n_act, (i & 1) == slot))
        def _(slot=slot, other=other):
            wait_gathers(slot)
            issue_gathers(next_w, off_next, other)
            x = _slab_load(xbufs[slot], (), TM_MOE).astype(jnp.bfloat16)
            gu = _dot(x, wgu_ref[0])
            de = gu.shape[1] // 2
            gate = gu[:, :de]
            hid = (gate / (1.0 + jnp.exp(-gate))) * gu[:, de:]
            y = _dot(hid.astype(jnp.bfloat16), wd_ref[0])
            issue_scatters(prev_w, off_prev, nv_prev, other)

            @pl.when(i >= 1)
            def _():
                wait_scatters(slot)

            _slab_store(ybufs[slot], (), y)

        @pl.when(jnp.logical_and(i == n_act, (i & 1) == slot))
        def _(slot=slot, other=other):
            wait_gathers(slot)
            issue_scatters(prev_w, off_prev, nv_prev, other)
            wait_scatters(slot)
            wait_scatters(other)


def _overlapped_windows(v, n_rows, width):
    v2 = jnp.concatenate([v, jnp.zeros((width,), v.dtype)]).reshape(-1, LANES)
    out = jnp.concatenate([v2[q:q + n_rows] for q in range(width // LANES)], axis=1)
    return out.reshape(n_rows, 1, width)


def _experts(h_slab, src, dst, wgu, wd, tile_e, tile_valid, tile_u, n_active, n_tok):
    nt = tile_e.shape[0]
    d = wgu.shape[1]
    de2 = wgu.shape[2]
    last = nt - 1
    width = TM_MOE + LANES
    n_win = src.shape[0] // LANES
    src_w = _overlapped_windows(src, n_win, width)
    dst_w = _overlapped_windows(dst, n_win, width)
    out_rows = TOP_K * n_tok * SLAB

    def win(shift):
        return pl.BlockSpec((1, 1, width),
                            lambda i, te, tv, tu, na: (tu[jnp.clip(i + shift, 0, last)] // LANES, 0, 0),
                            memory_space=pltpu.SMEM)

    first = pl.BlockSpec((1, 1, width), lambda i, te, tv, tu, na: (0, 0, 0), memory_space=pltpu.SMEM)
    kern = functools.partial(_experts_kernel, trash_row=out_rows)
    buf = pltpu.VMEM((TM_MOE * SLAB, LANES), jnp.float32)
    return pl.pallas_call(
        kern,
        out_shape=jax.ShapeDtypeStruct((out_rows + TM_MOE * SLAB, LANES), jnp.float32),
        grid_spec=pltpu.PrefetchScalarGridSpec(
            num_scalar_prefetch=4,
            grid=(nt,),
            in_specs=[
                first, win(-1), win(1),
                pl.BlockSpec(memory_space=pl.ANY),
                pl.BlockSpec((1, d, de2), lambda i, te, tv, tu, na: (te[i], 0, 0)),
                pl.BlockSpec((1, de2 // 2, d), lambda i, te, tv, tu, na: (te[i], 0, 0)),
            ],
            out_specs=pl.BlockSpec(memory_space=pl.ANY),
            scratch_shapes=[buf, buf, buf, buf, pltpu.SemaphoreType.DMA((2,)), pltpu.SemaphoreType.DMA((2,))],
        ),
        compiler_params=_cparams(("arbitrary",)),
        name="experts",
    )(tile_e, tile_valid, tile_u, n_active, src_w, dst_w, src_w, h_slab, wgu, wd)


def _combine_kernel(h_ref, w_ref, *rest):
    y_refs = rest[:TOP_K]
    sgu_ref, sd_ref, g_ref, b_ref, o_ref = rest[TOP_K:]
    tn = h_ref.shape[0]
    h = h_ref[...]
    gu = _dot(h.astype(jnp.bfloat16), sgu_ref[...])
    ds = gu.shape[1] // 2
    gate = gu[:, :ds]
    hid = (gate / (1.0 + jnp.exp(-gate))) * gu[:, ds:]
    shared = _dot(hid.astype(jnp.bfloat16), sd_ref[...])

    w = w_ref[...]
    routed = _slab_load(y_refs[0], (), tn) * w[:, 0:1]
    for k in range(1, TOP_K):
        routed = routed + _slab_load(y_refs[k], (), tn) * w[:, k:k + 1]
    o_ref[...] = _layernorm_f32(ALPHA * h + (routed + shared), g_ref[...], b_ref[...])


def _combine(h1, w_tok, y_slab, sgu, sd, g, b):
    t, d = h1.shape
    tn = min(256, t)
    nblk = t // tn
    ds2 = sgu.shape[1]
    vec = pl.BlockSpec((1, d), lambda i: (0, 0))

    def slot_spec(k):
        return pl.BlockSpec((tn * SLAB, LANES), lambda i: (k * nblk + i, 0))

    return pl.pallas_call(
        _combine_kernel,
        out_shape=jax.ShapeDtypeStruct((t, d), jnp.float32),
        grid=(nblk,),
        in_specs=[
            pl.BlockSpec((tn, d), lambda i: (i, 0)),
            pl.BlockSpec((tn, TOP_K), lambda i: (i, 0)),
            *[slot_spec(k) for k in range(TOP_K)],
            pl.BlockSpec((d, ds2), lambda i: (0, 0)),
            pl.BlockSpec((ds2 // 2, d), lambda i: (0, 0)),
            vec, vec,
        ],
        out_specs=pl.BlockSpec((tn, d), lambda i: (i, 0)),
        compiler_params=_cparams(("parallel",)),
        name="combine_ln2",
    )(h1, w_tok, *([y_slab] * TOP_K), sgu, sd, g, b)


def _rel_bucket(rel):
    nb = N_BUCKETS // 2
    max_exact = nb // 2
    ret = jnp.where(rel > 0, nb, 0)
    n = jnp.abs(rel)
    nf = jnp.maximum(n, 1).astype(jnp.float32)
    large = max_exact + (jnp.log(nf / max_exact) / math.log(MAX_DISTANCE / max_exact) * (nb - max_exact)).astype(jnp.int32)
    large = jnp.minimum(large, nb - 1)
    return ret + jnp.where(n < max_exact, n, large)


def _bias_of_rel(rel_bias, rel):
    return rel_bias.astype(jnp.float32)[_rel_bucket(rel)]


def _toeplitz(vec, nrow, ncol, off):
    lo = off - (nrow - 1)
    v = vec[lo:off + ncol]
    p = v.shape[0] + 1
    v = jnp.concatenate([v, v[:1]], axis=0)
    flat = jnp.tile(v, (nrow + 1, 1))
    base = off - lo
    out = flat[base:base + nrow * (p - 1)].reshape(nrow, p - 1, vec.shape[1])
    return out[:, :ncol]


def _bias_tables_a(rel_bias, sink):
    bias_a = rel_bias[:, :A_Q_HEADS]
    m = jnp.arange(3 * TQ_A + TQ_A - 1, dtype=jnp.int32)
    rel = 2 * TQ_A - 1 - m
    vec = jnp.where((jnp.abs(rel) <= WINDOW)[:, None], _bias_of_rel(bias_a, rel), NEG)
    band = _toeplitz(vec, 3 * TQ_A, TQ_A, 3 * TQ_A - 1)
    mm = jnp.arange(TQ_A + N_META - 1, dtype=jnp.int32)
    meta_first = _toeplitz(_bias_of_rel(bias_a, -1 - mm), N_META, TQ_A, N_META - 1)
    meta_far = jnp.broadcast_to(_bias_of_rel(bias_a, jnp.int32(-2 * MAX_DISTANCE)), (N_META, TQ_A, A_Q_HEADS))
    pad = jnp.full((NK_A - 3 * TQ_A - N_META, TQ_A, A_Q_HEADS), NEG, jnp.float32)
    blocked = jnp.full((TQ_A, TQ_A, A_Q_HEADS), NEG, jnp.float32)
    first = jnp.concatenate([blocked, band[TQ_A:], meta_first, pad])
    middle = jnp.concatenate([band, meta_far, pad])
    last = jnp.concatenate([band[:2 * TQ_A], blocked, meta_far, pad])
    tab = jnp.stack([first, middle, last])
    tab = tab.reshape(3, NK_A, TQ_A, A_Q_HEADS // 2, 2)
    tab = jnp.transpose(tab, (0, 3, 1, 4, 2)).reshape(3, A_Q_HEADS // 2, NK_A, 2 * TQ_A) * LOG2E
    sink_rows = jnp.repeat(sink.astype(jnp.float32).reshape(A_Q_HEADS // 2, 1, 2), TQ_A, axis=2) * LOG2E
    return tab, sink_rows


def _bias_tables_b(rel_bias, s):
    bias_b = rel_bias[:, A_Q_HEADS:]
    near = []
    for d in (-1, 0, 1):
        m = jnp.arange(TQ_B + TK_B - 1, dtype=jnp.int32)
        vec = _bias_of_rel(bias_b, TK_B * d + TK_B - 1 - m)
        near.append(_toeplitz(vec, TK_B, TQ_B, TK_B - 1))
    far_l = jnp.broadcast_to(_bias_of_rel(bias_b, jnp.int32(-TK_B - 1)), (TK_B, TQ_B, B_HEADS))
    far_r = jnp.broadcast_to(_bias_of_rel(bias_b, jnp.int32(TK_B + 1)), (TK_B, TQ_B, B_HEADS))
    tabs = jnp.transpose(jnp.stack([far_l] + near + [far_r]), (3, 0, 1, 2)) * LOG2E
    m = jnp.arange(s + N_META - 1, dtype=jnp.int32)
    vec = _bias_of_rel(bias_b, -1 - m)
    meta = jnp.transpose(_toeplitz(vec, N_META, s, N_META - 1), (2, 0, 1)) * LOG2E
    return tabs, meta


def _prep_w_in(w_in):
    a_w = A_Q_HEADS * HEAD_DIM
    kv = A_KV_HEADS * HEAD_DIM
    bqk = B_HEADS * 2 * HEAD_DIM
    scale = HEAD_DIM ** -0.5
    qa = w_in[:, :a_w] * (scale * LOG2E)
    ka = w_in[:, a_w:a_w + kv]
    va = w_in[:, a_w + kv:a_w + 2 * kv]
    o = a_w + 2 * kv
    qb = w_in[:, o:o + bqk] * (scale * LOG2E)
    kb = w_in[:, o + bqk:o + 2 * bqk]
    vb = w_in[:, o + 2 * bqk:]

    def dup(w):
        return jnp.concatenate([w[:, g * HEAD_DIM:(g + 1) * HEAD_DIM] for g in range(A_KV_HEADS) for _ in range(2)], axis=1)

    w = jnp.concatenate([qa, dup(ka), qb, kb], axis=1).astype(jnp.bfloat16)
    return w, jnp.concatenate([vb, va], axis=1).T.astype(jnp.bfloat16)


def _trunk(x, prm):
    bsz, s, d = x.shape
    t = bsz * s
    x2d = x.reshape(t, d)
    proj, vt, vta = _ln_inproj(x2d, prm["ln_emb_g"], prm["ln_emb_b"], prm["w_in"], prm["w_vt"])
    proj3 = proj.reshape(bsz, s, PROJ_COLS)
    oa = _mixer_a(proj3, vta, prm["km_a"], prm["vtm_a"], prm["tab_a"], prm["sink_rows"])
    tabs_b, meta_b = _bias_tables_b(prm["rel_bias"], s)
    ob = _mixer_b(proj3, vt, prm["proj_meta"], prm["vt_meta"], tabs_b, meta_b, prm["lam"].reshape(1).astype(jnp.float32),
                  prm["subln_g"])
    h1, h1_slab = _outproj_ln1(x2d, oa.reshape(t, -1), ob.reshape(t, -1), prm["w_out"], prm["ln_emb_g"],
                               prm["ln_emb_b"], prm["ln1_g"], prm["ln1_b"])

    keys, wts, counts = _router(h1, prm["wr_hi"], prm["wr_lo"], prm["router_bias"])
    n_exp = counts.shape[0]
    n_asg = t * TOP_K
    assert n_asg <= (1 << KEY_SHIFT) and n_asg % LANES == 0
    order = jnp.sort(keys.reshape(n_asg)) & ((1 << KEY_SHIFT) - 1)
    tok = order >> 3
    src = tok * SLAB
    dst = ((order & (TOP_K - 1)) * t + tok) * SLAB
    counts = counts[:, 0].astype(jnp.int32)
    tiles_e = (counts + TM_MOE - 1) // TM_MOE
    tend = jnp.cumsum(tiles_e)
    tstart = tend - tiles_e
    ustart = jnp.cumsum(counts) - counts
    nt = n_asg // TM_MOE + n_exp + 1
    tid = jnp.arange(nt, dtype=jnp.int32)
    tile_e = jnp.minimum(jnp.sum((tend[None, :] <= tid[:, None]).astype(jnp.int32), axis=1), n_exp - 1)
    onehot = (tile_e[:, None] == jnp.arange(n_exp, dtype=jnp.int32)[None, :]).astype(jnp.int32)
    in_e = (tid - jnp.sum(onehot * tstart[None, :], axis=1)) * TM_MOE
    active = tid < tend[-1]
    tile_valid = jnp.where(active, jnp.clip(jnp.sum(onehot * counts[None, :], axis=1) - in_e, 0, TM_MOE), 0)
    tile_u = jnp.where(active, jnp.sum(onehot * ustart[None, :], axis=1) + in_e, 0)

    y_slab = _experts(h1_slab, src, dst, prm["w_gu"], prm["w_down"], tile_e.astype(jnp.int32),
                      tile_valid.astype(jnp.int32), tile_u.astype(jnp.int32), tend[-1:].astype(jnp.int32), t)
    out = _combine(h1, wts.T, y_slab, prm["ws_gu"], prm["ws_down"], prm["ln2_g"], prm["ln2_b"])
    return out.reshape(bsz, s, d)


def kernel(x_prompt, x_sample, meta_tokens, ln_emb_g, ln_emb_b, rel_bias, w_in, attn_sink, lambda_q1, lambda_k1, lambda_q2, lambda_k2, subln_g, w_out, ln1_g, ln1_b, w_router, router_bias, w_gate, w_up, w_down, ws_gate, ws_up, ws_down, ln2_g, ln2_b):
    f32 = jnp.float32
    bf16 = jnp.bfloat16
    l = 0
    row = lambda v: v.reshape(1, -1).astype(f32)
    lam = (jnp.exp(jnp.sum(lambda_q1[l].astype(f32) * lambda_k1[l].astype(f32)))
           - jnp.exp(jnp.sum(lambda_q2[l].astype(f32) * lambda_k2[l].astype(f32))) + LAMBDA_INIT)
    wr_t = w_router[l].astype(f32).T
    wr_hi = wr_t.astype(bf16)
    prm = {
        "ln_emb_g": row(ln_emb_g), "ln_emb_b": row(ln_emb_b),
        "rel_bias": rel_bias,
        "lam": lam,
        "subln_g": subln_g[l].astype(f32).reshape(-1, 1),
        "w_out": w_out[l].astype(bf16),
        "ln1_g": row(ln1_g[l]), "ln1_b": row(ln1_b[l]),
        "wr_hi": wr_hi, "wr_lo": (wr_t - wr_hi.astype(f32)).astype(bf16),
        "router_bias": router_bias[l].astype(f32).reshape(-1, 1),
        "w_gu": jnp.concatenate([w_gate[l].astype(bf16), w_up[l].astype(bf16)], axis=-1),
        "w_down": w_down[l].astype(bf16),
        "ws_gu": jnp.concatenate([ws_gate[l], ws_up[l]], axis=-1).astype(bf16),
        "ws_down": ws_down[l].astype(bf16),
        "ln2_g": row(ln2_g[l]), "ln2_b": row(ln2_b[l]),
    }
    prm["w_in"], prm["w_vt"] = _prep_w_in(w_in[l])
    prm["proj_meta"], prm["vt_meta"], vta_meta = _ln_inproj(meta_tokens.astype(f32), prm["ln_emb_g"], prm["ln_emb_b"],
                                                           prm["w_in"], prm["w_vt"])
    meta_pad = NK_A - 3 * TQ_A - N_META
    prm["km_a"] = jnp.pad(prm["proj_meta"][:, KA_BLK * LANES:(KA_BLK + 2) * LANES], ((0, meta_pad), (0, 0)))
    prm["vtm_a"] = jnp.pad(vta_meta, ((0, 0), (0, meta_pad)))
    prm["tab_a"], prm["sink_rows"] = _bias_tables_a(rel_bias, attn_sink[l])
    return (_trunk(x_prompt, prm), _trunk(x_sample, prm))
```

```python
import functools
import math

import jax
import jax.numpy as jnp
from jax import lax
from jax.experimental import pallas as pl
from jax.experimental.pallas import tpu as pltpu
from jax.experimental.pallas import tpu_sc as plsc

N_META = 16
HEAD_DIM = 64
WINDOW = 128
A_Q_HEADS = 8
A_KV_HEADS = 2
B_HEADS = 4
N_BUCKETS = 32
MAX_DISTANCE = 128
TOP_K = 8
N_GROUPS = 8
TOPK_GROUPS = 4
ROUTED_SCALE = 2.5
LN_EPS = 1e-5
DEPTH = 1
ALPHA = (2 * DEPTH) ** 0.25
NEG = -1e30
LAMBDA_INIT = 0.8 - 0.6 * math.exp(-0.3 * 0)

LANES = 128
VMEM_LIMIT = 48 * 1024 * 1024

QA_BLK = 0
KA_BLK = 4
QB_BLK = 6
KB_BLK = 10
PROJ_COLS = 14 * LANES
VT_ROWS = LANES + 16
VA_ROWS = HEAD_DIM + 16
LOG2E = 1.4426950408889634

TQ_A = 128
NK_A = 4 * TQ_A
TQ_B = 512
TK_B = 512
TM_MOE = 256
SLAB = 8
SC_GRP = 16
SC_NBUF = 4
SC_WORKERS = 32
KEY_SHIFT = 20


def _cparams(sem):
    return pltpu.CompilerParams(dimension_semantics=sem, vmem_limit_bytes=VMEM_LIMIT)


def _layernorm_f32(x, g, b):
    mu = jnp.mean(x, axis=-1, keepdims=True)
    xc = x - mu
    var = jnp.mean(xc * xc, axis=-1, keepdims=True)
    return xc * lax.rsqrt(var + LN_EPS) * g + b


def _dot_nt(a, b):
    return lax.dot_general(a, b, (((1,), (1,)), ((), ())), preferred_element_type=jnp.float32)


def _dot(a, b):
    return jnp.dot(a, b, preferred_element_type=jnp.float32)


def _ln_inproj_kernel(x_ref, g_ref, b_ref, w_ref, wvt_ref, o_ref, vt_ref, vta_ref):
    h = _layernorm_f32(x_ref[...], g_ref[...], b_ref[...]).astype(jnp.bfloat16)
    o_ref[...] = _dot(h, w_ref[...]).astype(o_ref.dtype)
    vt = _dot_nt(wvt_ref[...], h).astype(vt_ref.dtype)
    ones = jnp.ones((16, vt.shape[1]), vt_ref.dtype)
    for hd in range(B_HEADS):
        vt_ref[hd * VT_ROWS:hd * VT_ROWS + LANES, :] = vt[hd * LANES:(hd + 1) * LANES, :]
        vt_ref[hd * VT_ROWS + LANES:(hd + 1) * VT_ROWS, :] = ones
    base = B_HEADS * LANES
    for g in range(A_KV_HEADS):
        vta_ref[g * VA_ROWS:g * VA_ROWS + HEAD_DIM, :] = vt[base + g * HEAD_DIM:base + (g + 1) * HEAD_DIM, :]
        vta_ref[g * VA_ROWS + HEAD_DIM:(g + 1) * VA_ROWS, :] = ones


def _ln_inproj(x2d, g, b, w, wvt):
    t, d = x2d.shape
    n = w.shape[1]
    tm = min(512, t)
    return pl.pallas_call(
        _ln_inproj_kernel,
        out_shape=(jax.ShapeDtypeStruct((t, n), jnp.bfloat16),
                   jax.ShapeDtypeStruct((B_HEADS * VT_ROWS, t), jnp.bfloat16),
                   jax.ShapeDtypeStruct((A_KV_HEADS * VA_ROWS, t), jnp.bfloat16)),
        grid=(t // tm,),
        in_specs=[
            pl.BlockSpec((tm, d), lambda i: (i, 0)),
            pl.BlockSpec((1, d), lambda i: (0, 0)),
            pl.BlockSpec((1, d), lambda i: (0, 0)),
            pl.BlockSpec((d, n), lambda i: (0, 0)),
            pl.BlockSpec((wvt.shape[0], d), lambda i: (0, 0)),
        ],
        out_specs=(pl.BlockSpec((tm, n), lambda i: (i, 0)),
                   pl.BlockSpec((B_HEADS * VT_ROWS, tm), lambda i: (0, i)),
                   pl.BlockSpec((A_KV_HEADS * VA_ROWS, tm), lambda i: (0, i))),
        compiler_params=_cparams(("parallel",)),
        name="ln_inproj",
    )(x2d, g, b, w, wvt)


def _mixer_a_kernel(q_ref, k_ref, vt_ref, km_ref, vtm_ref, tab_ref, sink_ref, o_ref, *, nblk, sub):
    i = pl.program_id(1)
    lane = lax.broadcasted_iota(jnp.int32, (1, LANES), 1)
    lo = lane < HEAD_DIM
    for j in range(sub):
        gi = i * sub + j
        sp = pl.multiple_of(jnp.maximum(gi - 1, 0) * TQ_A, TQ_A)
        sc = pl.multiple_of(gi * TQ_A, TQ_A)
        sn = pl.multiple_of(jnp.minimum(gi + 1, nblk - 1) * TQ_A, TQ_A)
        variant = jnp.where(gi == 0, 0, jnp.where(gi == nblk - 1, 2, 1))
        rows = slice(j * TQ_A, (j + 1) * TQ_A)
        for g in range(A_KV_HEADS):
            gs = slice(g * LANES, (g + 1) * LANES)
            vr = slice(g * VA_ROWS, (g + 1) * VA_ROWS)
            k_all = jnp.concatenate([k_ref[0, pl.ds(sp, TQ_A), gs], k_ref[0, pl.ds(sc, TQ_A), gs],
                                     k_ref[0, pl.ds(sn, TQ_A), gs], km_ref[:, gs]], axis=0)
            vt_all = jnp.concatenate([vt_ref[vr, pl.ds(sp, TQ_A)], vt_ref[vr, pl.ds(sc, TQ_A)],
                                      vt_ref[vr, pl.ds(sn, TQ_A)], vtm_ref[vr, :]], axis=1)
            for pp in range(2):
                hp = 2 * g + pp
                cols = slice(hp * LANES, (hp + 1) * LANES)
                qc = q_ref[0, rows, cols]
                q2 = jnp.concatenate([jnp.where(lo, qc, jnp.zeros_like(qc)), jnp.where(lo, jnp.zeros_like(qc), qc)],
                                     axis=0)
                s = _dot_nt(k_all, q2) + tab_ref[variant, hp]
                sink = sink_ref[hp]
                m = jnp.maximum(jnp.max(s, axis=0, keepdims=True), sink)
                p = jnp.exp2(s - m)
                acc = _dot(vt_all, p.astype(jnp.bfloat16))
                o = acc[0:HEAD_DIM, :] / (acc[HEAD_DIM:HEAD_DIM + 1, :] + jnp.exp2(sink - m))
                o2 = jnp.concatenate([o[:, :TQ_A], o[:, TQ_A:]], axis=0)
                o_ref[0, rows, cols] = o2.T.astype(o_ref.dtype)


def _mixer_a(proj3, vta, km, vtm, tab, sink):
    bsz, s, _ = proj3.shape
    nblk = s // TQ_A
    assert nblk >= 2
    sub = 4 if nblk % 4 == 0 else 1
    nq = nblk // sub
    tq = sub * TQ_A
    kern = functools.partial(_mixer_a_kernel, nblk=nblk, sub=sub)
    return pl.pallas_call(
        kern,
        out_shape=jax.ShapeDtypeStruct((bsz, s, A_Q_HEADS * HEAD_DIM), jnp.bfloat16),
        grid=(bsz, nq),
        in_specs=[
            pl.BlockSpec((1, tq, 4 * LANES), lambda b, i: (b, i, QA_BLK // 4)),
            pl.BlockSpec((1, s, 2 * LANES), lambda b, i: (b, 0, KA_BLK // 2)),
            pl.BlockSpec((A_KV_HEADS * VA_ROWS, s), lambda b, i: (0, b)),
            pl.BlockSpec(km.shape, lambda b, i: (0, 0)),
            pl.BlockSpec(vtm.shape, lambda b, i: (0, 0)),
            pl.BlockSpec(tab.shape, lambda b, i: (0, 0, 0, 0)),
            pl.BlockSpec(sink.shape, lambda b, i: (0, 0, 0)),
        ],
        out_specs=pl.BlockSpec((1, tq, 4 * LANES), lambda b, i: (b, i, 0)),
        compiler_params=_cparams(("parallel", "arbitrary")),
        name="mixer_a",
    )(proj3, proj3, vta, km, vtm, tab, sink)


def _mixer_b_kernel(sc_ref, q_ref, k_ref, vt_ref, km_ref, vtm_ref, tab_ref, tabm_ref, g_ref, o_ref,
                    m_ref, acc_ref, s0_ref, s1_ref, *, nk):
    h = pl.program_id(1)
    i = pl.program_id(2)
    lane = lax.broadcasted_iota(jnp.int32, (1, LANES), 1)
    lo = lane < HEAD_DIM
    q = q_ref[0]
    qs = (jnp.where(lo, q, jnp.zeros_like(q)), jnp.where(lo, jnp.zeros_like(q), q))
    lam = sc_ref[0]
    sbufs = (s0_ref, s1_ref)

    km = km_ref[...]
    vtm = vtm_ref[...]
    for c in range(2):
        s = _dot_nt(km, qs[c]) + tabm_ref[0]
        m = jnp.max(s, axis=0, keepdims=True)
        p = jnp.exp2(s - m)
        m_ref[c] = m
        acc_ref[c] = _dot(vtm, p.astype(jnp.bfloat16))

    def scores(j, slot):
        start = pl.multiple_of(j * TK_B, TK_B)
        kj = k_ref[0, pl.ds(start, TK_B), :]
        bias = tab_ref[0, jnp.clip(j - i, -2, 2) + 2]
        for c in range(2):
            sbufs[slot][c] = _dot_nt(kj, qs[c]) + bias

    def accumulate(j, slot):
        start = pl.multiple_of(j * TK_B, TK_B)
        vtj = vt_ref[:, pl.ds(start, TK_B)]
        for c in range(2):
            s = sbufs[slot][c]
            m_prev = m_ref[c]
            m_new = jnp.maximum(m_prev, jnp.max(s, axis=0, keepdims=True))
            a = jnp.exp2(m_prev - m_new)
            p = jnp.exp2(s - m_new)
            m_ref[c] = m_new
            acc_ref[c] = a * acc_ref[c] + _dot(vtj, p.astype(jnp.bfloat16))

    scores(0, 0)

    def pair(jj, carry):
        j = 2 * jj
        scores(j + 1, 1)
        accumulate(j, 0)
        scores(j + 2, 0)
        accumulate(j + 1, 1)
        return carry

    lax.fori_loop(0, nk // 2 - 1, pair, 0)
    scores(nk - 1, 1)
    accumulate(nk - 2, 0)
    accumulate(nk - 1, 1)

    o0 = acc_ref[0, 0:LANES, :] / acc_ref[0, LANES:LANES + 1, :]
    o1 = acc_ref[1, 0:LANES, :] / acc_ref[1, LANES:LANES + 1, :]
    o = o0 - lam * o1
    ms = jnp.mean(o * o, axis=0, keepdims=True)
    o = o * lax.rsqrt(ms + LN_EPS) * (g_ref[...] * (1.0 - LAMBDA_INIT))
    o_ref[0] = o.T.astype(o_ref.dtype)


def _mixer_b(proj3, vt, proj_meta, vt_meta, tab, tabm, scal, subln_g):
    bsz, s, _ = proj3.shape
    nq = s // TQ_B
    nk = s // TK_B
    assert nk % 2 == 0
    kern = functools.partial(_mixer_b_kernel, nk=nk)
    return pl.pallas_call(
        kern,
        out_shape=jax.ShapeDtypeStruct((bsz, s, B_HEADS * LANES), jnp.bfloat16),
        grid=(bsz, B_HEADS, nq),
        in_specs=[
            pl.BlockSpec(memory_space=pltpu.SMEM),
            pl.BlockSpec((1, TQ_B, LANES), lambda b, h, i: (b, i, QB_BLK + h)),
            pl.BlockSpec((1, s, LANES), lambda b, h, i: (b, 0, KB_BLK + h)),
            pl.BlockSpec((VT_ROWS, s), lambda b, h, i: (h, b)),
            pl.BlockSpec((N_META, LANES), lambda b, h, i: (0, KB_BLK + h)),
            pl.BlockSpec((VT_ROWS, N_META), lambda b, h, i: (h, 0)),
            pl.BlockSpec((1, 5, TK_B, TQ_B), lambda b, h, i: (h, 0, 0, 0)),
            pl.BlockSpec((1, N_META, TQ_B), lambda b, h, i: (h, 0, i)),
            pl.BlockSpec((LANES, 1), lambda b, h, i: (0, 0)),
        ],
        out_specs=pl.BlockSpec((1, TQ_B, LANES), lambda b, h, i: (b, i, h)),
        scratch_shapes=[
            pltpu.VMEM((2, 1, TQ_B), jnp.float32),
            pltpu.VMEM((2, VT_ROWS, TQ_B), jnp.float32),
            pltpu.VMEM((2, TK_B, TQ_B), jnp.float32),
            pltpu.VMEM((2, TK_B, TQ_B), jnp.float32),
        ],
        compiler_params=_cparams(("parallel", "parallel", "arbitrary")),
        name="mixer_b",
    )(scal, proj3, proj3, vt, proj_meta, vt_meta, tab, tabm, subln_g)


def _slab_load(ref, lead, rows):
    return jnp.concatenate([ref[lead + (pl.ds(s, rows, stride=SLAB), slice(None))] for s in range(SLAB)], axis=1)


def _slab_store(ref, lead, val):
    rows = val.shape[0]
    for s in range(SLAB):
        ref[lead + (pl.ds(s, rows, stride=SLAB), slice(None))] = val[:, s * LANES:(s + 1) * LANES]


def _outproj_kernel(x_ref, oa_ref, ob_ref, w_ref, eg_ref, eb_ref, g_ref, b_ref, o_ref, os_ref):
    h0 = _layernorm_f32(x_ref[...], eg_ref[...], eb_ref[...])
    half = oa_ref.shape[1]
    mix = _dot(oa_ref[...], w_ref[0:half, :]) + _dot(ob_ref[...], w_ref[half:, :])
    h1 = _layernorm_f32(ALPHA * h0 + mix, g_ref[...], b_ref[...])
    o_ref[...] = h1
    _slab_store(os_ref, (), h1)


def _outproj_ln1(x2d, oa, ob, w_out, eg, eb, g, b):
    t, d = x2d.shape
    tm = min(512, t)
    wa = oa.shape[1]
    wb = ob.shape[1]
    vec = pl.BlockSpec((1, d), lambda i: (0, 0))
    return pl.pallas_call(
        _outproj_kernel,
        out_shape=(jax.ShapeDtypeStruct((t, d), jnp.float32),
                   jax.ShapeDtypeStruct((t * SLAB, LANES), jnp.float32)),
        grid=(t // tm,),
        in_specs=[
            pl.BlockSpec((tm, d), lambda i: (i, 0)),
            pl.BlockSpec((tm, wa), lambda i: (i, 0)),
            pl.BlockSpec((tm, wb), lambda i: (i, 0)),
            pl.BlockSpec((wa + wb, d), lambda i: (0, 0)),
            vec, vec, vec, vec,
        ],
        out_specs=(pl.BlockSpec((tm, d), lambda i: (i, 0)),
                   pl.BlockSpec((tm * SLAB, LANES), lambda i: (i, 0))),
        compiler_params=_cparams(("parallel",)),
        name="outproj_ln1",
    )(x2d, oa, ob, w_out, eg, eb, g, b)


def _router_kernel(h_ref, wh_ref, wl_ref, rb_ref, e_ref, w_ref, cnt_ref, carry_ref, *, n_exp):
    i = pl.program_id(0)
    tn = h_ref.shape[0]
    gsz = n_exp // N_GROUPS

    @pl.when(i == 0)
    def _():
        carry_ref[...] = jnp.zeros_like(carry_ref)

    x = h_ref[...]
    xh = x.astype(jnp.bfloat16)
    xl = (x - xh.astype(jnp.float32)).astype(jnp.bfloat16)
    logits = _dot_nt(wh_ref[...], xh) + (_dot_nt(wh_ref[...], xl) + _dot_nt(wl_ref[...], xh))
    scores = 1.0 / (1.0 + jnp.exp(-logits))
    biased = scores + rb_ref[...]

    g3 = biased.reshape(N_GROUPS, gsz, tn)
    it3 = lax.broadcasted_iota(jnp.int32, (N_GROUPS, gsz, tn), 1)
    mx1 = jnp.max(g3, axis=1, keepdims=True)
    first = jnp.min(jnp.where(g3 == mx1, it3, gsz), axis=1, keepdims=True)
    mx2 = jnp.max(jnp.where(it3 == first, -jnp.inf, g3), axis=1, keepdims=True)
    gscore = (mx1 + mx2).reshape(N_GROUPS, tn)

    itg = lax.broadcasted_iota(jnp.int32, (N_GROUPS, tn), 0)
    gsel = jnp.zeros((N_GROUPS, tn), jnp.bool_)
    cur = gscore
    for _ in range(TOPK_GROUPS):
        mx = jnp.max(cur, axis=0, keepdims=True)
        fi = jnp.min(jnp.where(cur == mx, itg, N_GROUPS), axis=0, keepdims=True)
        hit = itg == fi
        gsel = jnp.logical_or(gsel, hit)
        cur = jnp.where(hit, -jnp.inf, cur)
    emask = jnp.broadcast_to(gsel.reshape(N_GROUPS, 1, tn), (N_GROUPS, gsz, tn)).reshape(n_exp, tn)
    cur = jnp.where(emask, biased, NEG)

    ite = lax.broadcasted_iota(jnp.int32, (n_exp, tn), 0)
    hits = []
    eidx = []
    wsel = []
    for _ in range(TOP_K):
        mx = jnp.max(cur, axis=0, keepdims=True)
        fi = jnp.min(jnp.where(cur == mx, ite, n_exp), axis=0, keepdims=True)
        hit = ite == fi
        hits.append(hit)
        eidx.append(fi)
        wsel.append(jnp.sum(jnp.where(hit, scores, 0.0), axis=0, keepdims=True))
        cur = jnp.where(hit, -jnp.inf, cur)
    sel = hits[0]
    for hit in hits[1:]:
        sel = jnp.logical_or(sel, hit)
    self32 = jnp.where(sel, 1.0, 0.0)

    carry_ref[...] = carry_ref[...] + jnp.sum(self32, axis=1, keepdims=True)
    cnt_ref[...] = carry_ref[...]

    wcat = jnp.concatenate(wsel, axis=0)
    wcat = wcat / jnp.sum(wcat, axis=0, keepdims=True) * ROUTED_SCALE
    tok = i * tn + lax.broadcasted_iota(jnp.int32, (TOP_K, tn), 1)
    slot = lax.broadcasted_iota(jnp.int32, (TOP_K, tn), 0)
    e_ref[...] = jnp.concatenate(eidx, axis=0) * (1 << KEY_SHIFT) + (tok * TOP_K + slot)
    w_ref[...] = wcat


def _router(h1, wr_hi, wr_lo, rbias):
    t, d = h1.shape
    n_exp = wr_hi.shape[0]
    tn = min(512, t)
    kern = functools.partial(_router_kernel, n_exp=n_exp)
    row = pl.BlockSpec((TOP_K, tn), lambda i: (0, i))
    return pl.pallas_call(
        kern,
        out_shape=(
            jax.ShapeDtypeStruct((TOP_K, t), jnp.int32),
            jax.ShapeDtypeStruct((TOP_K, t), jnp.float32),
            jax.ShapeDtypeStruct((n_exp, 1), jnp.float32),
        ),
        grid=(t // tn,),
        in_specs=[
            pl.BlockSpec((tn, d), lambda i: (i, 0)),
            pl.BlockSpec((n_exp, d), lambda i: (0, 0)),
            pl.BlockSpec((n_exp, d), lambda i: (0, 0)),
            pl.BlockSpec((n_exp, 1), lambda i: (0, 0)),
        ],
        out_specs=(row, row, pl.BlockSpec((n_exp, 1), lambda i: (0, 0))),
        scratch_shapes=[pltpu.VMEM((n_exp, 1), jnp.float32)],
        compiler_params=_cparams(("arbitrary",)),
        name="router",
    )(h1, wr_hi, wr_lo, rbias)


def _sc_gather_rows(x_slab, idx):
    m = idx.shape[0]
    rnd = SC_GRP * SC_NBUF
    per = m // SC_WORKERS
    assert m % (SC_WORKERS * rnd) == 0
    row = x_slab.shape[1:]
    mesh = plsc.VectorSubcoreMesh(core_axis_name="c", subcore_axis_name="s")

    @pl.kernel(out_type=jax.ShapeDtypeStruct((m,) + row, x_slab.dtype), mesh=mesh,
               scratch_types=[pltpu.VMEM((rnd,), jnp.int32), pltpu.VMEM((SC_NBUF, SC_GRP) + row, x_slab.dtype),
                              pltpu.SemaphoreType.DMA((SC_NBUF,)), pltpu.SemaphoreType.DMA((SC_NBUF,))])
    def kern(x_hbm, i_hbm, o_hbm, ibuf, buf, gsem, ssem):
        worker = lax.axis_index("c") * (SC_WORKERS // 2) + lax.axis_index("s")
        base = worker * per

        @pl.loop(0, per // rnd)
        def _(r):
            off = base + r * rnd
            pltpu.sync_copy(i_hbm.at[pl.ds(off, rnd)], ibuf)
            gathers = [pltpu.make_async_copy(x_hbm.at[ibuf.at[pl.ds(b * SC_GRP, SC_GRP)]], buf.at[b], gsem.at[b])
                       for b in range(SC_NBUF)]
            for g in gathers:
                g.start()
            stores = []
            for b in range(SC_NBUF):
                gathers[b].wait()
                st = pltpu.make_async_copy(buf.at[b], o_hbm.at[pl.ds(off + b * SC_GRP, SC_GRP)], ssem.at[b])
                st.start()
                stores.append(st)
            for st in stores:
                st.wait()

    return kern(x_slab, idx)


def _experts_kernel(te_ref, tv_ref, tu_ref, na_ref, prev_w, xs_hbm, wgu_ref, wd_ref, y_hbm,
                    xb0, xb1, yb0, yb1, gsem, ssem, *, trash_row):
    i = pl.program_id(0)
    nt = pl.num_programs(0)
    n_act = na_ref[0]
    xbufs = (xb0, xb1)
    ybufs = (yb0, yb1)
    tile_rows = TM_MOE * SLAB

    def x_copy(slot, tile):
        start = pl.multiple_of(tu_ref[tile] * SLAB, SLAB)
        return pltpu.make_async_copy(xs_hbm.at[pl.ds(start, tile_rows), :], xbufs[slot], gsem.at[slot])

    def scatter(slot, j, dst):
        return pltpu.make_async_copy(ybufs[slot].at[pl.ds(j * SLAB, SLAB), :],
                                     y_hbm.at[pl.ds(pl.multiple_of(dst, SLAB), SLAB), :], ssem.at[slot])

    def wait_scatters(slot):
        pltpu.make_async_copy(ybufs[slot], y_hbm.at[pl.ds(0, tile_rows), :], ssem.at[slot]).wait()

    def issue_scatters(win, off, nvalid, slot):
        for j in range(TM_MOE):
            dst = jnp.where(j < nvalid, win[0, 0, off + j], trash_row + j * SLAB)
            scatter(slot, j, dst).start(priority=j % 2)

    @pl.when(i == 0)
    def _():
        yb1[...] = jnp.zeros_like(yb1)
        x_copy(0, 0).start()

    iprev = jnp.maximum(i - 1, 0)
    nv_prev = jnp.where(i >= 1, tv_ref[iprev], 0)
    off_prev = tu_ref[iprev] & (LANES - 1)
    inext = jnp.minimum(i + 1, nt - 1)

    for slot in range(2):
        other = 1 - slot

        @pl.when(jnp.logical_and(i < n_act, (i & 1) == slot))
        def _(slot=slot, other=other):
            x_copy(slot, i).wait()
            x_copy(other, inext).start()
            x = _slab_load(xbufs[slot], (), TM_MOE).astype(jnp.bfloat16)
            gu = _dot(x, wgu_ref[0])
            de = gu.shape[1] // 2
            gate = gu[:, :de]
            hid = (gate / (1.0 + jnp.exp(-gate))) * gu[:, de:]
            y = _dot(hid.astype(jnp.bfloat16), wd_ref[0])
            issue_scatters(prev_w, off_prev, nv_prev, other)

            @pl.when(i >= 1)
            def _():
                wait_scatters(slot)

            _slab_store(ybufs[slot], (), y)

        @pl.when(jnp.logical_and(i == n_act, (i & 1) == slot))
        def _(slot=slot, other=other):
            x_copy(slot, i).wait()
            issue_scatters(prev_w, off_prev, nv_prev, other)
            wait_scatters(slot)
            wait_scatters(other)


def _overlapped_windows(v, n_rows, width):
    v2 = jnp.concatenate([v, jnp.zeros((width,), v.dtype)]).reshape(-1, LANES)
    out = jnp.concatenate([v2[q:q + n_rows] for q in range(width // LANES)], axis=1)
    return out.reshape(n_rows, 1, width)


def _experts(xs_slab, dst, wgu, wd, tile_e, tile_valid, tile_u, n_active, n_tok):
    nt = tile_e.shape[0]
    d = wgu.shape[1]
    de2 = wgu.shape[2]
    last = nt - 1
    width = TM_MOE + LANES
    n_win = dst.shape[0] // LANES
    dst_w = _overlapped_windows(dst, n_win, width)
    out_rows = TOP_K * n_tok * SLAB

    def win(shift):
        return pl.BlockSpec((1, 1, width),
                            lambda i, te, tv, tu, na: (tu[jnp.clip(i + shift, 0, last)] // LANES, 0, 0),
                            memory_space=pltpu.SMEM)

    kern = functools.partial(_experts_kernel, trash_row=out_rows)
    buf = pltpu.VMEM((TM_MOE * SLAB, LANES), jnp.float32)
    return pl.pallas_call(
        kern,
        out_shape=jax.ShapeDtypeStruct((out_rows + TM_MOE * SLAB, LANES), jnp.float32),
        grid_spec=pltpu.PrefetchScalarGridSpec(
            num_scalar_prefetch=4,
            grid=(nt,),
            in_specs=[
                win(-1),
                pl.BlockSpec(memory_space=pl.ANY),
                pl.BlockSpec((1, d, de2), lambda i, te, tv, tu, na: (te[i], 0, 0)),
                pl.BlockSpec((1, de2 // 2, d), lambda i, te, tv, tu, na: (te[i], 0, 0)),
            ],
            out_specs=pl.BlockSpec(memory_space=pl.ANY),
            scratch_shapes=[buf, buf, buf, buf, pltpu.SemaphoreType.DMA((2,)), pltpu.SemaphoreType.DMA((2,))],
        ),
        compiler_params=_cparams(("arbitrary",)),
        name="experts",
    )(tile_e, tile_valid, tile_u, n_active, dst_w, xs_slab, wgu, wd)


def _combine_kernel(h_ref, w_ref, *rest):
    y_refs = rest[:TOP_K]
    sgu_ref, sd_ref, g_ref, b_ref, o_ref = rest[TOP_K:]
    tn = h_ref.shape[0]
    h = h_ref[...]
    gu = _dot(h.astype(jnp.bfloat16), sgu_ref[...])
    ds = gu.shape[1] // 2
    gate = gu[:, :ds]
    hid = (gate / (1.0 + jnp.exp(-gate))) * gu[:, ds:]
    shared = _dot(hid.astype(jnp.bfloat16), sd_ref[...])

    w = w_ref[...]
    routed = _slab_load(y_refs[0], (), tn) * w[:, 0:1]
    for k in range(1, TOP_K):
        routed = routed + _slab_load(y_refs[k], (), tn) * w[:, k:k + 1]
    o_ref[...] = _layernorm_f32(ALPHA * h + (routed + shared), g_ref[...], b_ref[...])


def _combine(h1, w_tok, y_slab, sgu, sd, g, b):
    t, d = h1.shape
    tn = min(256, t)
    nblk = t // tn
    ds2 = sgu.shape[1]
    vec = pl.BlockSpec((1, d), lambda i: (0, 0))

    def slot_spec(k):
        return pl.BlockSpec((tn * SLAB, LANES), lambda i: (k * nblk + i, 0))

    return pl.pallas_call(
        _combine_kernel,
        out_shape=jax.ShapeDtypeStruct((t, d), jnp.float32),
        grid=(nblk,),
        in_specs=[
            pl.BlockSpec((tn, d), lambda i: (i, 0)),
            pl.BlockSpec((tn, TOP_K), lambda i: (i, 0)),
            *[slot_spec(k) for k in range(TOP_K)],
            pl.BlockSpec((d, ds2), lambda i: (0, 0)),
            pl.BlockSpec((ds2 // 2, d), lambda i: (0, 0)),
            vec, vec,
        ],
        out_specs=pl.BlockSpec((tn, d), lambda i: (i, 0)),
        compiler_params=_cparams(("parallel",)),
        name="combine_ln2",
    )(h1, w_tok, *([y_slab] * TOP_K), sgu, sd, g, b)


def _rel_bucket(rel):
    nb = N_BUCKETS // 2
    max_exact = nb // 2
    ret = jnp.where(rel > 0, nb, 0)
    n = jnp.abs(rel)
    nf = jnp.maximum(n, 1).astype(jnp.float32)
    large = max_exact + (jnp.log(nf / max_exact) / math.log(MAX_DISTANCE / max_exact) * (nb - max_exact)).astype(jnp.int32)
    large = jnp.minimum(large, nb - 1)
    return ret + jnp.where(n < max_exact, n, large)


def _bias_of_rel(rel_bias, rel):
    return rel_bias.astype(jnp.float32)[_rel_bucket(rel)]


def _toeplitz(vec, nrow, ncol, off):
    lo = off - (nrow - 1)
    v = vec[lo:off + ncol]
    p = v.shape[0] + 1
    v = jnp.concatenate([v, v[:1]], axis=0)
    flat = jnp.tile(v, (nrow + 1, 1))
    base = off - lo
    out = flat[base:base + nrow * (p - 1)].reshape(nrow, p - 1, vec.shape[1])
    return out[:, :ncol]


def _bias_tables_a(rel_bias, sink):
    bias_a = rel_bias[:, :A_Q_HEADS]
    m = jnp.arange(3 * TQ_A + TQ_A - 1, dtype=jnp.int32)
    rel = 2 * TQ_A - 1 - m
    vec = jnp.where((jnp.abs(rel) <= WINDOW)[:, None], _bias_of_rel(bias_a, rel), NEG)
    band = _toeplitz(vec, 3 * TQ_A, TQ_A, 3 * TQ_A - 1)
    mm = jnp.arange(TQ_A + N_META - 1, dtype=jnp.int32)
    meta_first = _toeplitz(_bias_of_rel(bias_a, -1 - mm), N_META, TQ_A, N_META - 1)
    meta_far = jnp.broadcast_to(_bias_of_rel(bias_a, jnp.int32(-2 * MAX_DISTANCE)), (N_META, TQ_A, A_Q_HEADS))
    pad = jnp.full((NK_A - 3 * TQ_A - N_META, TQ_A, A_Q_HEADS), NEG, jnp.float32)
    blocked = jnp.full((TQ_A, TQ_A, A_Q_HEADS), NEG, jnp.float32)
    first = jnp.concatenate([blocked, band[TQ_A:], meta_first, pad])
    middle = jnp.concatenate([band, meta_far, pad])
    last = jnp.concatenate([band[:2 * TQ_A], blocked, meta_far, pad])
    tab = jnp.stack([first, middle, last])
    tab = tab.reshape(3, NK_A, TQ_A, A_Q_HEADS // 2, 2)
    tab = jnp.transpose(tab, (0, 3, 1, 4, 2)).reshape(3, A_Q_HEADS // 2, NK_A, 2 * TQ_A) * LOG2E
    sink_rows = jnp.repeat(sink.astype(jnp.float32).reshape(A_Q_HEADS // 2, 1, 2), TQ_A, axis=2) * LOG2E
    return tab, sink_rows


def _bias_tables_b(rel_bias, s):
    bias_b = rel_bias[:, A_Q_HEADS:]
    near = []
    for d in (-1, 0, 1):
        m = jnp.arange(TQ_B + TK_B - 1, dtype=jnp.int32)
        vec = _bias_of_rel(bias_b, TK_B * d + TK_B - 1 - m)
        near.append(_toeplitz(vec, TK_B, TQ_B, TK_B - 1))
    far_l = jnp.broadcast_to(_bias_of_rel(bias_b, jnp.int32(-TK_B - 1)), (TK_B, TQ_B, B_HEADS))
    far_r = jnp.broadcast_to(_bias_of_rel(bias_b, jnp.int32(TK_B + 1)), (TK_B, TQ_B, B_HEADS))
    tabs = jnp.transpose(jnp.stack([far_l] + near + [far_r]), (3, 0, 1, 2)) * LOG2E
    m = jnp.arange(s + N_META - 1, dtype=jnp.int32)
    vec = _bias_of_rel(bias_b, -1 - m)
    meta = jnp.transpose(_toeplitz(vec, N_META, s, N_META - 1), (2, 0, 1)) * LOG2E
    return tabs, meta


def _prep_w_in(w_in):
    a_w = A_Q_HEADS * HEAD_DIM
    kv = A_KV_HEADS * HEAD_DIM
    bqk = B_HEADS * 2 * HEAD_DIM
    scale = HEAD_DIM ** -0.5
    qa = w_in[:, :a_w] * (scale * LOG2E)
    ka = w_in[:, a_w:a_w + kv]
    va = w_in[:, a_w + kv:a_w + 2 * kv]
    o = a_w + 2 * kv
    qb = w_in[:, o:o + bqk] * (scale * LOG2E)
    kb = w_in[:, o + bqk:o + 2 * bqk]
    vb = w_in[:, o + 2 * bqk:]

    def dup(w):
        return jnp.concatenate([w[:, g * HEAD_DIM:(g + 1) * HEAD_DIM] for g in range(A_KV_HEADS) for _ in range(2)], axis=1)

    w = jnp.concatenate([qa, dup(ka), qb, kb], axis=1).astype(jnp.bfloat16)
    return w, jnp.concatenate([vb, va], axis=1).T.astype(jnp.bfloat16)


def _trunk_front(x, prm):
    bsz, s, d = x.shape
    t = bsz * s
    x2d = x.reshape(t, d)
    proj, vt, vta = _ln_inproj(x2d, prm["ln_emb_g"], prm["ln_emb_b"], prm["w_in"], prm["w_vt"])
    proj3 = proj.reshape(bsz, s, PROJ_COLS)
    oa = _mixer_a(proj3, vta, prm["km_a"], prm["vtm_a"], prm["tab_a"], prm["sink_rows"])
    tabs_b, meta_b = _bias_tables_b(prm["rel_bias"], s)
    ob = _mixer_b(proj3, vt, prm["proj_meta"], prm["vt_meta"], tabs_b, meta_b, prm["lam"].reshape(1).astype(jnp.float32),
                  prm["subln_g"])
    h1, h1_slab = _outproj_ln1(x2d, oa.reshape(t, -1), ob.reshape(t, -1), prm["w_out"], prm["ln_emb_g"],
                               prm["ln_emb_b"], prm["ln1_g"], prm["ln1_b"])

    keys, wts, counts = _router(h1, prm["wr_hi"], prm["wr_lo"], prm["router_bias"])
    n_exp = counts.shape[0]
    n_asg = t * TOP_K
    assert n_asg <= (1 << KEY_SHIFT) and n_asg % LANES == 0
    order = jnp.sort(keys.reshape(n_asg)) & ((1 << KEY_SHIFT) - 1)
    tok = order >> 3
    dst = ((order & (TOP_K - 1)) * t + tok) * SLAB
    sc_pad = SC_WORKERS * SC_GRP * SC_NBUF
    assert sc_pad >= TM_MOE and n_asg % sc_pad == 0
    xs = _sc_gather_rows(h1_slab.reshape(t, SLAB, LANES), jnp.concatenate([tok, jnp.zeros((sc_pad,), jnp.int32)]))
    xs_slab = xs.reshape((n_asg + sc_pad) * SLAB, LANES)
    counts = counts[:, 0].astype(jnp.int32)
    tiles_e = (counts + TM_MOE - 1) // TM_MOE
    tend = jnp.cumsum(tiles_e)
    tstart = tend - tiles_e
    ustart = jnp.cumsum(counts) - counts
    nt = n_asg // TM_MOE + n_exp + 1
    tid = jnp.arange(nt, dtype=jnp.int32)
    tile_e = jnp.minimum(jnp.sum((tend[None, :] <= tid[:, None]).astype(jnp.int32), axis=1), n_exp - 1)
    onehot = (tile_e[:, None] == jnp.arange(n_exp, dtype=jnp.int32)[None, :]).astype(jnp.int32)
    in_e = (tid - jnp.sum(onehot * tstart[None, :], axis=1)) * TM_MOE
    active = tid < tend[-1]
    tile_valid = jnp.where(active, jnp.clip(jnp.sum(onehot * counts[None, :], axis=1) - in_e, 0, TM_MOE), 0)
    tile_u = jnp.where(active, jnp.sum(onehot * ustart[None, :], axis=1) + in_e, 0)

    tiles = (tile_e.astype(jnp.int32), tile_valid.astype(jnp.int32), tile_u.astype(jnp.int32),
             tend[-1:].astype(jnp.int32))
    return h1, wts.T, xs_slab, dst, tiles, x.shape


def _trunk_back(front, prm):
    h1, w_tok, xs_slab, dst, tiles, shape = front
    y_slab = _experts(xs_slab, dst, prm["w_gu"], prm["w_down"], *tiles, h1.shape[0])
    out = _combine(h1, w_tok, y_slab, prm["ws_gu"], prm["ws_down"], prm["ln2_g"], prm["ln2_b"])
    return out.reshape(shape)


def kernel(x_prompt, x_sample, meta_tokens, ln_emb_g, ln_emb_b, rel_bias, w_in, attn_sink, lambda_q1, lambda_k1, lambda_q2, lambda_k2, subln_g, w_out, ln1_g, ln1_b, w_router, router_bias, w_gate, w_up, w_down, ws_gate, ws_up, ws_down, ln2_g, ln2_b):
    f32 = jnp.float32
    bf16 = jnp.bfloat16
    l = 0
    row = lambda v: v.reshape(1, -1).astype(f32)
    lam = (jnp.exp(jnp.sum(lambda_q1[l].astype(f32) * lambda_k1[l].astype(f32)))
           - jnp.exp(jnp.sum(lambda_q2[l].astype(f32) * lambda_k2[l].astype(f32))) + LAMBDA_INIT)
    wr_t = w_router[l].astype(f32).T
    wr_hi = wr_t.astype(bf16)
    prm = {
        "ln_emb_g": row(ln_emb_g), "ln_emb_b": row(ln_emb_b),
        "rel_bias": rel_bias,
        "lam": lam,
        "subln_g": subln_g[l].astype(f32).reshape(-1, 1),
        "w_out": w_out[l].astype(bf16),
        "ln1_g": row(ln1_g[l]), "ln1_b": row(ln1_b[l]),
        "wr_hi": wr_hi, "wr_lo": (wr_t - wr_hi.astype(f32)).astype(bf16),
        "router_bias": router_bias[l].astype(f32).reshape(-1, 1),
        "w_gu": jnp.concatenate([w_gate[l].astype(bf16), w_up[l].astype(bf16)], axis=-1),
        "w_down": w_down[l].astype(bf16),
        "ws_gu": jnp.concatenate([ws_gate[l], ws_up[l]], axis=-1).astype(bf16),
        "ws_down": ws_down[l].astype(bf16),
        "ln2_g": row(ln2_g[l]), "ln2_b": row(ln2_b[l]),
    }
    prm["w_in"], prm["w_vt"] = _prep_w_in(w_in[l])
    prm["proj_meta"], prm["vt_meta"], vta_meta = _ln_inproj(meta_tokens.astype(f32), prm["ln_emb_g"], prm["ln_emb_b"],
                                                           prm["w_in"], prm["w_vt"])
    meta_pad = NK_A - 3 * TQ_A - N_META
    prm["km_a"] = jnp.pad(prm["proj_meta"][:, KA_BLK * LANES:(KA_BLK + 2) * LANES], ((0, meta_pad), (0, 0)))
    prm["vtm_a"] = jnp.pad(vta_meta, ((0, 0), (0, meta_pad)))
    prm["tab_a"], prm["sink_rows"] = _bias_tables_a(rel_bias, attn_sink[l])
    front_p = _trunk_front(x_prompt, prm)
    front_s = _trunk_front(x_sample, prm)
    return (_trunk_back(front_p, prm), _trunk_back(front_s, prm))
```

```python
import functools
import math

import jax
import jax.numpy as jnp
from jax import lax
from jax.experimental import pallas as pl
from jax.experimental.pallas import tpu as pltpu
from jax.experimental.pallas import tpu_sc as plsc

N_META = 16
HEAD_DIM = 64
WINDOW = 128
A_Q_HEADS = 8
A_KV_HEADS = 2
B_HEADS = 4
N_BUCKETS = 32
MAX_DISTANCE = 128
TOP_K = 8
N_GROUPS = 8
TOPK_GROUPS = 4
ROUTED_SCALE = 2.5
LN_EPS = 1e-5
DEPTH = 1
ALPHA = (2 * DEPTH) ** 0.25
NEG = -1e30
LAMBDA_INIT = 0.8 - 0.6 * math.exp(-0.3 * 0)

LANES = 128
VMEM_LIMIT = 48 * 1024 * 1024

QA_BLK = 0
KA_BLK = 4
QB_BLK = 6
KB_BLK = 10
PROJ_COLS = 14 * LANES
VT_ROWS = LANES + 16
VA_ROWS = HEAD_DIM + 16
LOG2E = 1.4426950408889634

TQ_A = 128
NK_A = 4 * TQ_A
TQ_B = 512
TK_B = 512
TM_MOE = 256
SLAB = 8
SC_GRP = 16
SC_NBUF = 4
SC_WORKERS = 32
KEY_SHIFT = 20


def _cparams(sem):
    return pltpu.CompilerParams(dimension_semantics=sem, vmem_limit_bytes=VMEM_LIMIT)


def _layernorm_f32(x, g, b):
    mu = jnp.mean(x, axis=-1, keepdims=True)
    xc = x - mu
    var = jnp.mean(xc * xc, axis=-1, keepdims=True)
    return xc * lax.rsqrt(var + LN_EPS) * g + b


def _dot_nt(a, b):
    return lax.dot_general(a, b, (((1,), (1,)), ((), ())), preferred_element_type=jnp.float32)


def _dot(a, b):
    return jnp.dot(a, b, preferred_element_type=jnp.float32)


def _ln_inproj_kernel(x_ref, g_ref, b_ref, w_ref, wvt_ref, o_ref, vt_ref, vta_ref):
    h = _layernorm_f32(x_ref[...], g_ref[...], b_ref[...]).astype(jnp.bfloat16)
    o_ref[...] = _dot(h, w_ref[...]).astype(o_ref.dtype)
    vt = _dot_nt(wvt_ref[...], h).astype(vt_ref.dtype)
    ones = jnp.ones((16, vt.shape[1]), vt_ref.dtype)
    for hd in range(B_HEADS):
        vt_ref[hd * VT_ROWS:hd * VT_ROWS + LANES, :] = vt[hd * LANES:(hd + 1) * LANES, :]
        vt_ref[hd * VT_ROWS + LANES:(hd + 1) * VT_ROWS, :] = ones
    base = B_HEADS * LANES
    for g in range(A_KV_HEADS):
        vta_ref[g * VA_ROWS:g * VA_ROWS + HEAD_DIM, :] = vt[base + g * HEAD_DIM:base + (g + 1) * HEAD_DIM, :]
        vta_ref[g * VA_ROWS + HEAD_DIM:(g + 1) * VA_ROWS, :] = ones


def _ln_inproj(x2d, g, b, w, wvt):
    t, d = x2d.shape
    n = w.shape[1]
    tm = min(512, t)
    return pl.pallas_call(
        _ln_inproj_kernel,
        out_shape=(jax.ShapeDtypeStruct((t, n), jnp.bfloat16),
                   jax.ShapeDtypeStruct((B_HEADS * VT_ROWS, t), jnp.bfloat16),
                   jax.ShapeDtypeStruct((A_KV_HEADS * VA_ROWS, t), jnp.bfloat16)),
        grid=(t // tm,),
        in_specs=[
            pl.BlockSpec((tm, d), lambda i: (i, 0)),
            pl.BlockSpec((1, d), lambda i: (0, 0)),
            pl.BlockSpec((1, d), lambda i: (0, 0)),
            pl.BlockSpec((d, n), lambda i: (0, 0)),
            pl.BlockSpec((wvt.shape[0], d), lambda i: (0, 0)),
        ],
        out_specs=(pl.BlockSpec((tm, n), lambda i: (i, 0)),
                   pl.BlockSpec((B_HEADS * VT_ROWS, tm), lambda i: (0, i)),
                   pl.BlockSpec((A_KV_HEADS * VA_ROWS, tm), lambda i: (0, i))),
        compiler_params=_cparams(("parallel",)),
        name="ln_inproj",
    )(x2d, g, b, w, wvt)


def _mixer_a_kernel(q_ref, k_ref, vt_ref, km_ref, vtm_ref, tab_ref, sink_ref, o_ref, *, nblk, sub):
    i = pl.program_id(1)
    lane = lax.broadcasted_iota(jnp.int32, (1, LANES), 1)
    lo = lane < HEAD_DIM
    for j in range(sub):
        gi = i * sub + j
        sp = pl.multiple_of(jnp.maximum(gi - 1, 0) * TQ_A, TQ_A)
        sc = pl.multiple_of(gi * TQ_A, TQ_A)
        sn = pl.multiple_of(jnp.minimum(gi + 1, nblk - 1) * TQ_A, TQ_A)
        variant = jnp.where(gi == 0, 0, jnp.where(gi == nblk - 1, 2, 1))
        rows = slice(j * TQ_A, (j + 1) * TQ_A)
        for g in range(A_KV_HEADS):
            gs = slice(g * LANES, (g + 1) * LANES)
            vr = slice(g * VA_ROWS, (g + 1) * VA_ROWS)
            k_all = jnp.concatenate([k_ref[0, pl.ds(sp, TQ_A), gs], k_ref[0, pl.ds(sc, TQ_A), gs],
                                     k_ref[0, pl.ds(sn, TQ_A), gs], km_ref[:, gs]], axis=0)
            vt_all = jnp.concatenate([vt_ref[vr, pl.ds(sp, TQ_A)], vt_ref[vr, pl.ds(sc, TQ_A)],
                                      vt_ref[vr, pl.ds(sn, TQ_A)], vtm_ref[vr, :]], axis=1)
            for pp in range(2):
                hp = 2 * g + pp
                cols = slice(hp * LANES, (hp + 1) * LANES)
                qc = q_ref[0, rows, cols]
                q2 = jnp.concatenate([jnp.where(lo, qc, jnp.zeros_like(qc)), jnp.where(lo, jnp.zeros_like(qc), qc)],
                                     axis=0)
                s = _dot_nt(k_all, q2) + tab_ref[variant, hp]
                sink = sink_ref[hp]
                m = jnp.maximum(jnp.max(s, axis=0, keepdims=True), sink)
                p = jnp.exp2(s - m)
                acc = _dot(vt_all, p.astype(jnp.bfloat16))
                o = acc[0:HEAD_DIM, :] / (acc[HEAD_DIM:HEAD_DIM + 1, :] + jnp.exp2(sink - m))
                o2 = jnp.concatenate([o[:, :TQ_A], o[:, TQ_A:]], axis=0)
                o_ref[0, rows, cols] = o2.T.astype(o_ref.dtype)


def _mixer_a(proj3, vta, km, vtm, tab, sink):
    bsz, s, _ = proj3.shape
    nblk = s // TQ_A
    assert nblk >= 2
    sub = 4 if nblk % 4 == 0 else 1
    nq = nblk // sub
    tq = sub * TQ_A
    kern = functools.partial(_mixer_a_kernel, nblk=nblk, sub=sub)
    return pl.pallas_call(
        kern,
        out_shape=jax.ShapeDtypeStruct((bsz, s, A_Q_HEADS * HEAD_DIM), jnp.bfloat16),
        grid=(bsz, nq),
        in_specs=[
            pl.BlockSpec((1, tq, 4 * LANES), lambda b, i: (b, i, QA_BLK // 4)),
            pl.BlockSpec((1, s, 2 * LANES), lambda b, i: (b, 0, KA_BLK // 2)),
            pl.BlockSpec((A_KV_HEADS * VA_ROWS, s), lambda b, i: (0, b)),
            pl.BlockSpec(km.shape, lambda b, i: (0, 0)),
            pl.BlockSpec(vtm.shape, lambda b, i: (0, 0)),
            pl.BlockSpec(tab.shape, lambda b, i: (0, 0, 0, 0)),
            pl.BlockSpec(sink.shape, lambda b, i: (0, 0, 0)),
        ],
        out_specs=pl.BlockSpec((1, tq, 4 * LANES), lambda b, i: (b, i, 0)),
        compiler_params=_cparams(("parallel", "arbitrary")),
        name="mixer_a",
    )(proj3, proj3, vta, km, vtm, tab, sink)


def _mixer_b_kernel(sc_ref, q_ref, k_ref, vt_ref, km_ref, vtm_ref, tab_ref, tabm_ref, g_ref, o_ref,
                    m_ref, acc_ref, s0_ref, s1_ref, *, nk):
    h = pl.program_id(1)
    i = pl.program_id(2)
    lane = lax.broadcasted_iota(jnp.int32, (1, LANES), 1)
    lo = lane < HEAD_DIM
    q = q_ref[0]
    qs = (jnp.where(lo, q, jnp.zeros_like(q)), jnp.where(lo, jnp.zeros_like(q), q))
    lam = sc_ref[0]
    sbufs = (s0_ref, s1_ref)

    km = km_ref[...]
    vtm = vtm_ref[...]
    for c in range(2):
        s = _dot_nt(km, qs[c]) + tabm_ref[0]
        m = jnp.max(s, axis=0, keepdims=True)
        p = jnp.exp2(s - m)
        m_ref[c] = m
        acc_ref[c] = _dot(vtm, p.astype(jnp.bfloat16))

    def scores(j, slot):
        start = pl.multiple_of(j * TK_B, TK_B)
        kj = k_ref[0, pl.ds(start, TK_B), :]
        bias = tab_ref[0, jnp.clip(j - i, -2, 2) + 2]
        for c in range(2):
            sbufs[slot][c] = _dot_nt(kj, qs[c]) + bias

    def accumulate(j, slot):
        start = pl.multiple_of(j * TK_B, TK_B)
        vtj = vt_ref[:, pl.ds(start, TK_B)]
        for c in range(2):
            s = sbufs[slot][c]
            m_prev = m_ref[c]
            m_new = jnp.maximum(m_prev, jnp.max(s, axis=0, keepdims=True))
            a = jnp.exp2(m_prev - m_new)
            p = jnp.exp2(s - m_new)
            m_ref[c] = m_new
            acc_ref[c] = a * acc_ref[c] + _dot(vtj, p.astype(jnp.bfloat16))

    scores(0, 0)

    def pair(jj, carry):
        j = 2 * jj
        scores(j + 1, 1)
        accumulate(j, 0)
        scores(j + 2, 0)
        accumulate(j + 1, 1)
        return carry

    lax.fori_loop(0, nk // 2 - 1, pair, 0)
    scores(nk - 1, 1)
    accumulate(nk - 2, 0)
    accumulate(nk - 1, 1)

    o0 = acc_ref[0, 0:LANES, :] / acc_ref[0, LANES:LANES + 1, :]
    o1 = acc_ref[1, 0:LANES, :] / acc_ref[1, LANES:LANES + 1, :]
    o = o0 - lam * o1
    ms = jnp.mean(o * o, axis=0, keepdims=True)
    o = o * lax.rsqrt(ms + LN_EPS) * (g_ref[...] * (1.0 - LAMBDA_INIT))
    o_ref[0] = o.T.astype(o_ref.dtype)


def _mixer_b(proj3, vt, proj_meta, vt_meta, tab, tabm, scal, subln_g):
    bsz, s, _ = proj3.shape
    nq = s // TQ_B
    nk = s // TK_B
    assert nk % 2 == 0
    kern = functools.partial(_mixer_b_kernel, nk=nk)
    return pl.pallas_call(
        kern,
        out_shape=jax.ShapeDtypeStruct((bsz, s, B_HEADS * LANES), jnp.bfloat16),
        grid=(bsz, B_HEADS, nq),
        in_specs=[
            pl.BlockSpec(memory_space=pltpu.SMEM),
            pl.BlockSpec((1, TQ_B, LANES), lambda b, h, i: (b, i, QB_BLK + h)),
            pl.BlockSpec((1, s, LANES), lambda b, h, i: (b, 0, KB_BLK + h)),
            pl.BlockSpec((VT_ROWS, s), lambda b, h, i: (h, b)),
            pl.BlockSpec((N_META, LANES), lambda b, h, i: (0, KB_BLK + h)),
            pl.BlockSpec((VT_ROWS, N_META), lambda b, h, i: (h, 0)),
            pl.BlockSpec((1, 5, TK_B, TQ_B), lambda b, h, i: (h, 0, 0, 0)),
            pl.BlockSpec((1, N_META, TQ_B), lambda b, h, i: (h, 0, i)),
            pl.BlockSpec((LANES, 1), lambda b, h, i: (0, 0)),
        ],
        out_specs=pl.BlockSpec((1, TQ_B, LANES), lambda b, h, i: (b, i, h)),
        scratch_shapes=[
            pltpu.VMEM((2, 1, TQ_B), jnp.float32),
            pltpu.VMEM((2, VT_ROWS, TQ_B), jnp.float32),
            pltpu.VMEM((2, TK_B, TQ_B), jnp.float32),
            pltpu.VMEM((2, TK_B, TQ_B), jnp.float32),
        ],
        compiler_params=_cparams(("parallel", "parallel", "arbitrary")),
        name="mixer_b",
    )(scal, proj3, proj3, vt, proj_meta, vt_meta, tab, tabm, subln_g)


def _slab_load(ref, lead, rows):
    return jnp.concatenate([ref[lead + (pl.ds(s, rows, stride=SLAB), slice(None))] for s in range(SLAB)], axis=1)


def _slab_store(ref, lead, val):
    rows = val.shape[0]
    for s in range(SLAB):
        ref[lead + (pl.ds(s, rows, stride=SLAB), slice(None))] = val[:, s * LANES:(s + 1) * LANES]


def _outproj_kernel(x_ref, oa_ref, ob_ref, w_ref, eg_ref, eb_ref, g_ref, b_ref, o_ref, os_ref):
    h0 = _layernorm_f32(x_ref[...], eg_ref[...], eb_ref[...])
    half = oa_ref.shape[1]
    mix = _dot(oa_ref[...], w_ref[0:half, :]) + _dot(ob_ref[...], w_ref[half:, :])
    h1 = _layernorm_f32(ALPHA * h0 + mix, g_ref[...], b_ref[...])
    o_ref[...] = h1
    _slab_store(os_ref, (), h1)


def _outproj_ln1(x2d, oa, ob, w_out, eg, eb, g, b):
    t, d = x2d.shape
    tm = min(512, t)
    wa = oa.shape[1]
    wb = ob.shape[1]
    vec = pl.BlockSpec((1, d), lambda i: (0, 0))
    return pl.pallas_call(
        _outproj_kernel,
        out_shape=(jax.ShapeDtypeStruct((t, d), jnp.float32),
                   jax.ShapeDtypeStruct((t * SLAB, LANES), jnp.float32)),
        grid=(t // tm,),
        in_specs=[
            pl.BlockSpec((tm, d), lambda i: (i, 0)),
            pl.BlockSpec((tm, wa), lambda i: (i, 0)),
            pl.BlockSpec((tm, wb), lambda i: (i, 0)),
            pl.BlockSpec((wa + wb, d), lambda i: (0, 0)),
            vec, vec, vec, vec,
        ],
        out_specs=(pl.BlockSpec((tm, d), lambda i: (i, 0)),
                   pl.BlockSpec((tm * SLAB, LANES), lambda i: (i, 0))),
        compiler_params=_cparams(("parallel",)),
        name="outproj_ln1",
    )(x2d, oa, ob, w_out, eg, eb, g, b)


def _router_kernel(h_ref, wh_ref, wl_ref, rb_ref, e_ref, w_ref, cnt_ref, carry_ref, *, n_exp):
    i = pl.program_id(0)
    tn = h_ref.shape[0]
    gsz = n_exp // N_GROUPS

    @pl.when(i == 0)
    def _():
        carry_ref[...] = jnp.zeros_like(carry_ref)

    x = h_ref[...]
    xh = x.astype(jnp.bfloat16)
    xl = (x - xh.astype(jnp.float32)).astype(jnp.bfloat16)
    logits = _dot_nt(wh_ref[...], xh) + (_dot_nt(wh_ref[...], xl) + _dot_nt(wl_ref[...], xh))
    scores = 1.0 / (1.0 + jnp.exp(-logits))
    biased = scores + rb_ref[...]

    g3 = biased.reshape(N_GROUPS, gsz, tn)
    it3 = lax.broadcasted_iota(jnp.int32, (N_GROUPS, gsz, tn), 1)
    mx1 = jnp.max(g3, axis=1, keepdims=True)
    first = jnp.min(jnp.where(g3 == mx1, it3, gsz), axis=1, keepdims=True)
    mx2 = jnp.max(jnp.where(it3 == first, -jnp.inf, g3), axis=1, keepdims=True)
    gscore = (mx1 + mx2).reshape(N_GROUPS, tn)

    itg = lax.broadcasted_iota(jnp.int32, (N_GROUPS, tn), 0)
    gsel = jnp.zeros((N_GROUPS, tn), jnp.bool_)
    cur = gscore
    for _ in range(TOPK_GROUPS):
        mx = jnp.max(cur, axis=0, keepdims=True)
        fi = jnp.min(jnp.where(cur == mx, itg, N_GROUPS), axis=0, keepdims=True)
        hit = itg == fi
        gsel = jnp.logical_or(gsel, hit)
        cur = jnp.where(hit, -jnp.inf, cur)
    emask = jnp.broadcast_to(gsel.reshape(N_GROUPS, 1, tn), (N_GROUPS, gsz, tn)).reshape(n_exp, tn)
    cur = jnp.where(emask, biased, NEG)

    ite = lax.broadcasted_iota(jnp.int32, (n_exp, tn), 0)
    hits = []
    eidx = []
    wsel = []
    for _ in range(TOP_K):
        mx = jnp.max(cur, axis=0, keepdims=True)
        fi = jnp.min(jnp.where(cur == mx, ite, n_exp), axis=0, keepdims=True)
        hit = ite == fi
        hits.append(hit)
        eidx.append(fi)
        wsel.append(jnp.sum(jnp.where(hit, scores, 0.0), axis=0, keepdims=True))
        cur = jnp.where(hit, -jnp.inf, cur)
    sel = hits[0]
    for hit in hits[1:]:
        sel = jnp.logical_or(sel, hit)
    self32 = jnp.where(sel, 1.0, 0.0)

    carry_ref[...] = carry_ref[...] + jnp.sum(self32, axis=1, keepdims=True)
    cnt_ref[...] = carry_ref[...]

    wcat = jnp.concatenate(wsel, axis=0)
    wcat = wcat / jnp.sum(wcat, axis=0, keepdims=True) * ROUTED_SCALE
    tok = i * tn + lax.broadcasted_iota(jnp.int32, (TOP_K, tn), 1)
    slot = lax.broadcasted_iota(jnp.int32, (TOP_K, tn), 0)
    e_ref[...] = jnp.concatenate(eidx, axis=0) * (1 << KEY_SHIFT) + (tok * TOP_K + slot)
    w_ref[...] = wcat


def _router(h1, wr_hi, wr_lo, rbias):
    t, d = h1.shape
    n_exp = wr_hi.shape[0]
    tn = min(512, t)
    kern = functools.partial(_router_kernel, n_exp=n_exp)
    row = pl.BlockSpec((TOP_K, tn), lambda i: (0, i))
    return pl.pallas_call(
        kern,
        out_shape=(
            jax.ShapeDtypeStruct((TOP_K, t), jnp.int32),
            jax.ShapeDtypeStruct((TOP_K, t), jnp.float32),
            jax.ShapeDtypeStruct((n_exp, 1), jnp.float32),
        ),
        grid=(t // tn,),
        in_specs=[
            pl.BlockSpec((tn, d), lambda i: (i, 0)),
            pl.BlockSpec((n_exp, d), lambda i: (0, 0)),
            pl.BlockSpec((n_exp, d), lambda i: (0, 0)),
            pl.BlockSpec((n_exp, 1), lambda i: (0, 0)),
        ],
        out_specs=(row, row, pl.BlockSpec((n_exp, 1), lambda i: (0, 0))),
        scratch_shapes=[pltpu.VMEM((n_exp, 1), jnp.float32)],
        compiler_params=_cparams(("arbitrary",)),
        name="router",
    )(h1, wr_hi, wr_lo, rbias)


def _sc_move_rows(x_slab, idx, n_out, scatter):
    m = idx.shape[0]
    rnd = SC_GRP * SC_NBUF
    per = m // SC_WORKERS
    assert m % (SC_WORKERS * rnd) == 0
    row = x_slab.shape[1:]
    mesh = plsc.VectorSubcoreMesh(core_axis_name="c", subcore_axis_name="s")

    @pl.kernel(out_type=jax.ShapeDtypeStruct((n_out,) + row, x_slab.dtype), mesh=mesh,
               scratch_types=[pltpu.VMEM((rnd,), jnp.int32), pltpu.VMEM((SC_NBUF, SC_GRP) + row, x_slab.dtype),
                              pltpu.SemaphoreType.DMA((SC_NBUF,)), pltpu.SemaphoreType.DMA((SC_NBUF,))])
    def kern(x_hbm, i_hbm, o_hbm, ibuf, buf, lsem, ssem):
        worker = lax.axis_index("c") * (SC_WORKERS // 2) + lax.axis_index("s")
        base = worker * per

        @pl.loop(0, per // rnd)
        def _(r):
            off = base + r * rnd
            pltpu.sync_copy(i_hbm.at[pl.ds(off, rnd)], ibuf)
            loads = []
            stores = []
            for b in range(SC_NBUF):
                indexed = ibuf.at[pl.ds(b * SC_GRP, SC_GRP)]
                linear = pl.ds(off + b * SC_GRP, SC_GRP)
                src = x_hbm.at[linear] if scatter else x_hbm.at[indexed]
                dst = o_hbm.at[indexed] if scatter else o_hbm.at[linear]
                loads.append(pltpu.make_async_copy(src, buf.at[b], lsem.at[b]))
                stores.append(pltpu.make_async_copy(buf.at[b], dst, ssem.at[b]))
            for ld in loads:
                ld.start()
            for b in range(SC_NBUF):
                loads[b].wait()
                stores[b].start()
            for st in stores:
                st.wait()

    return kern(x_slab, idx)


def _experts_kernel(te_ref, tv_ref, tu_ref, na_ref, xs_hbm, wgu_ref, wd_ref, ys_hbm, xb0, xb1, yb0, yb1, xsem, ysem):
    i = pl.program_id(0)
    nt = pl.num_programs(0)
    n_act = na_ref[0]
    xbufs = (xb0, xb1)
    ybufs = (yb0, yb1)
    tile_rows = TM_MOE * SLAB

    def rows_of(ref, tile):
        return ref.at[pl.ds(pl.multiple_of(tu_ref[tile] * SLAB, SLAB), tile_rows), :]

    def x_copy(slot, tile):
        return pltpu.make_async_copy(rows_of(xs_hbm, tile), xbufs[slot], xsem.at[slot])

    def y_copy(slot, tile):
        return pltpu.make_async_copy(ybufs[slot], rows_of(ys_hbm, tile), ysem.at[slot])

    @pl.when(i == 0)
    def _():
        x_copy(0, 0).start()

    iprev = jnp.maximum(i - 1, 0)
    inext = jnp.minimum(i + 1, nt - 1)

    for slot in range(2):
        other = 1 - slot

        @pl.when(jnp.logical_and(i < n_act, (i & 1) == slot))
        def _(slot=slot, other=other):
            x_copy(slot, i).wait()
            x_copy(other, inext).start()
            x = _slab_load(xbufs[slot], (), TM_MOE).astype(jnp.bfloat16)
            gu = _dot(x, wgu_ref[0])
            de = gu.shape[1] // 2
            gate = gu[:, :de]
            hid = (gate / (1.0 + jnp.exp(-gate))) * gu[:, de:]
            y = _dot(hid.astype(jnp.bfloat16), wd_ref[0])
            _slab_store(ybufs[slot], (), y)

            @pl.when(i >= 1)
            def _():
                y_copy(other, iprev).wait()

            y_copy(slot, i).start()

        @pl.when(jnp.logical_and(i == n_act, (i & 1) == slot))
        def _(slot=slot, other=other):
            x_copy(slot, i).wait()
            y_copy(other, iprev).wait()


def _experts(xs_slab, wgu, wd, tile_e, tile_valid, tile_u, n_active):
    nt = tile_e.shape[0]
    d = wgu.shape[1]
    de2 = wgu.shape[2]
    buf = pltpu.VMEM((TM_MOE * SLAB, LANES), jnp.float32)
    return pl.pallas_call(
        _experts_kernel,
        out_shape=jax.ShapeDtypeStruct(xs_slab.shape, jnp.float32),
        grid_spec=pltpu.PrefetchScalarGridSpec(
            num_scalar_prefetch=4,
            grid=(nt,),
            in_specs=[
                pl.BlockSpec(memory_space=pl.ANY),
                pl.BlockSpec((1, d, de2), lambda i, te, tv, tu, na: (te[i], 0, 0)),
                pl.BlockSpec((1, de2 // 2, d), lambda i, te, tv, tu, na: (te[i], 0, 0)),
            ],
            out_specs=pl.BlockSpec(memory_space=pl.ANY),
            scratch_shapes=[buf, buf, buf, buf, pltpu.SemaphoreType.DMA((2,)), pltpu.SemaphoreType.DMA((2,))],
        ),
        compiler_params=_cparams(("arbitrary",)),
        name="experts",
    )(tile_e, tile_valid, tile_u, n_active, xs_slab, wgu, wd)


def _combine_kernel(h_ref, w_ref, *rest):
    y_refs = rest[:TOP_K]
    sgu_ref, sd_ref, g_ref, b_ref, o_ref = rest[TOP_K:]
    tn = h_ref.shape[0]
    h = h_ref[...]
    gu = _dot(h.astype(jnp.bfloat16), sgu_ref[...])
    ds = gu.shape[1] // 2
    gate = gu[:, :ds]
    hid = (gate / (1.0 + jnp.exp(-gate))) * gu[:, ds:]
    shared = _dot(hid.astype(jnp.bfloat16), sd_ref[...])

    w = w_ref[...]
    routed = _slab_load(y_refs[0], (), tn) * w[:, 0:1]
    for k in range(1, TOP_K):
        routed = routed + _slab_load(y_refs[k], (), tn) * w[:, k:k + 1]
    o_ref[...] = _layernorm_f32(ALPHA * h + (routed + shared), g_ref[...], b_ref[...])


def _combine(h1, w_tok, y_slab, sgu, sd, g, b):
    t, d = h1.shape
    tn = min(256, t)
    nblk = t // tn
    ds2 = sgu.shape[1]
    vec = pl.BlockSpec((1, d), lambda i: (0, 0))

    def slot_spec(k):
        return pl.BlockSpec((tn * SLAB, LANES), lambda i: (k * nblk + i, 0))

    return pl.pallas_call(
        _combine_kernel,
        out_shape=jax.ShapeDtypeStruct((t, d), jnp.float32),
        grid=(nblk,),
        in_specs=[
            pl.BlockSpec((tn, d), lambda i: (i, 0)),
            pl.BlockSpec((tn, TOP_K), lambda i: (i, 0)),
            *[slot_spec(k) for k in range(TOP_K)],
            pl.BlockSpec((d, ds2), lambda i: (0, 0)),
            pl.BlockSpec((ds2 // 2, d), lambda i: (0, 0)),
            vec, vec,
        ],
        out_specs=pl.BlockSpec((tn, d), lambda i: (i, 0)),
        compiler_params=_cparams(("parallel",)),
        name="combine_ln2",
    )(h1, w_tok, *([y_slab] * TOP_K), sgu, sd, g, b)


def _rel_bucket(rel):
    nb = N_BUCKETS // 2
    max_exact = nb // 2
    ret = jnp.where(rel > 0, nb, 0)
    n = jnp.abs(rel)
    nf = jnp.maximum(n, 1).astype(jnp.float32)
    large = max_exact + (jnp.log(nf / max_exact) / math.log(MAX_DISTANCE / max_exact) * (nb - max_exact)).astype(jnp.int32)
    large = jnp.minimum(large, nb - 1)
    return ret + jnp.where(n < max_exact, n, large)


def _bias_of_rel(rel_bias, rel):
    return rel_bias.astype(jnp.float32)[_rel_bucket(rel)]


def _toeplitz(vec, nrow, ncol, off):
    lo = off - (nrow - 1)
    v = vec[lo:off + ncol]
    p = v.shape[0] + 1
    v = jnp.concatenate([v, v[:1]], axis=0)
    flat = jnp.tile(v, (nrow + 1, 1))
    base = off - lo
    out = flat[base:base + nrow * (p - 1)].reshape(nrow, p - 1, vec.shape[1])
    return out[:, :ncol]


def _bias_tables_a(rel_bias, sink):
    bias_a = rel_bias[:, :A_Q_HEADS]
    m = jnp.arange(3 * TQ_A + TQ_A - 1, dtype=jnp.int32)
    rel = 2 * TQ_A - 1 - m
    vec = jnp.where((jnp.abs(rel) <= WINDOW)[:, None], _bias_of_rel(bias_a, rel), NEG)
    band = _toeplitz(vec, 3 * TQ_A, TQ_A, 3 * TQ_A - 1)
    mm = jnp.arange(TQ_A + N_META - 1, dtype=jnp.int32)
    meta_first = _toeplitz(_bias_of_rel(bias_a, -1 - mm), N_META, TQ_A, N_META - 1)
    meta_far = jnp.broadcast_to(_bias_of_rel(bias_a, jnp.int32(-2 * MAX_DISTANCE)), (N_META, TQ_A, A_Q_HEADS))
    pad = jnp.full((NK_A - 3 * TQ_A - N_META, TQ_A, A_Q_HEADS), NEG, jnp.float32)
    blocked = jnp.full((TQ_A, TQ_A, A_Q_HEADS), NEG, jnp.float32)
    first = jnp.concatenate([blocked, band[TQ_A:], meta_first, pad])
    middle = jnp.concatenate([band, meta_far, pad])
    last = jnp.concatenate([band[:2 * TQ_A], blocked, meta_far, pad])
    tab = jnp.stack([first, middle, last])
    tab = tab.reshape(3, NK_A, TQ_A, A_Q_HEADS // 2, 2)
    tab = jnp.transpose(tab, (0, 3, 1, 4, 2)).reshape(3, A_Q_HEADS // 2, NK_A, 2 * TQ_A) * LOG2E
    sink_rows = jnp.repeat(sink.astype(jnp.float32).reshape(A_Q_HEADS // 2, 1, 2), TQ_A, axis=2) * LOG2E
    return tab, sink_rows


def _bias_tables_b(rel_bias, s):
    bias_b = rel_bias[:, A_Q_HEADS:]
    near = []
    for d in (-1, 0, 1):
        m = jnp.arange(TQ_B + TK_B - 1, dtype=jnp.int32)
        vec = _bias_of_rel(bias_b, TK_B * d + TK_B - 1 - m)
        near.append(_toeplitz(vec, TK_B, TQ_B, TK_B - 1))
    far_l = jnp.broadcast_to(_bias_of_rel(bias_b, jnp.int32(-TK_B - 1)), (TK_B, TQ_B, B_HEADS))
    far_r = jnp.broadcast_to(_bias_of_rel(bias_b, jnp.int32(TK_B + 1)), (TK_B, TQ_B, B_HEADS))
    tabs = jnp.transpose(jnp.stack([far_l] + near + [far_r]), (3, 0, 1, 2)) * LOG2E
    m = jnp.arange(s + N_META - 1, dtype=jnp.int32)
    vec = _bias_of_rel(bias_b, -1 - m)
    meta = jnp.transpose(_toeplitz(vec, N_META, s, N_META - 1), (2, 0, 1)) * LOG2E
    return tabs, meta


def _prep_w_in(w_in):
    a_w = A_Q_HEADS * HEAD_DIM
    kv = A_KV_HEADS * HEAD_DIM
    bqk = B_HEADS * 2 * HEAD_DIM
    scale = HEAD_DIM ** -0.5
    qa = w_in[:, :a_w] * (scale * LOG2E)
    ka = w_in[:, a_w:a_w + kv]
    va = w_in[:, a_w + kv:a_w + 2 * kv]
    o = a_w + 2 * kv
    qb = w_in[:, o:o + bqk] * (scale * LOG2E)
    kb = w_in[:, o + bqk:o + 2 * bqk]
    vb = w_in[:, o + 2 * bqk:]

    def dup(w):
        return jnp.concatenate([w[:, g * HEAD_DIM:(g + 1) * HEAD_DIM] for g in range(A_KV_HEADS) for _ in range(2)], axis=1)

    w = jnp.concatenate([qa, dup(ka), qb, kb], axis=1).astype(jnp.bfloat16)
    return w, jnp.concatenate([vb, va], axis=1).T.astype(jnp.bfloat16)


def _trunk_front(x, prm):
    bsz, s, d = x.shape
    t = bsz * s
    x2d = x.reshape(t, d)
    proj, vt, vta = _ln_inproj(x2d, prm["ln_emb_g"], prm["ln_emb_b"], prm["w_in"], prm["w_vt"])
    proj3 = proj.reshape(bsz, s, PROJ_COLS)
    oa = _mixer_a(proj3, vta, prm["km_a"], prm["vtm_a"], prm["tab_a"], prm["sink_rows"])
    tabs_b, meta_b = _bias_tables_b(prm["rel_bias"], s)
    ob = _mixer_b(proj3, vt, prm["proj_meta"], prm["vt_meta"], tabs_b, meta_b, prm["lam"].reshape(1).astype(jnp.float32),
                  prm["subln_g"])
    h1, h1_slab = _outproj_ln1(x2d, oa.reshape(t, -1), ob.reshape(t, -1), prm["w_out"], prm["ln_emb_g"],
                               prm["ln_emb_b"], prm["ln1_g"], prm["ln1_b"])

    keys, wts, counts = _router(h1, prm["wr_hi"], prm["wr_lo"], prm["router_bias"])
    n_exp = counts.shape[0]
    n_asg = t * TOP_K
    assert n_asg <= (1 << KEY_SHIFT) and n_asg % LANES == 0
    order = jnp.sort(keys.reshape(n_asg)) & ((1 << KEY_SHIFT) - 1)
    tok = order >> 3
    dst = (order & (TOP_K - 1)) * t + tok
    sc_pad = SC_WORKERS * SC_GRP * SC_NBUF
    assert sc_pad >= TM_MOE and n_asg % sc_pad == 0
    xs = _sc_move_rows(h1_slab.reshape(t, SLAB, LANES), jnp.concatenate([tok, jnp.zeros((sc_pad,), jnp.int32)]),
                       n_asg + sc_pad, scatter=False)
    xs_slab = xs.reshape((n_asg + sc_pad) * SLAB, LANES)
    counts = counts[:, 0].astype(jnp.int32)
    tiles_e = (counts + TM_MOE - 1) // TM_MOE
    tend = jnp.cumsum(tiles_e)
    tstart = tend - tiles_e
    ustart = jnp.cumsum(counts) - counts
    nt = n_asg // TM_MOE + n_exp + 1
    tid = jnp.arange(nt, dtype=jnp.int32)
    tile_e = jnp.minimum(jnp.sum((tend[None, :] <= tid[:, None]).astype(jnp.int32), axis=1), n_exp - 1)
    onehot = (tile_e[:, None] == jnp.arange(n_exp, dtype=jnp.int32)[None, :]).astype(jnp.int32)
    in_e = (tid - jnp.sum(onehot * tstart[None, :], axis=1)) * TM_MOE
    active = tid < tend[-1]
    tile_valid = jnp.where(active, jnp.clip(jnp.sum(onehot * counts[None, :], axis=1) - in_e, 0, TM_MOE), 0)
    tile_u = jnp.where(active, jnp.sum(onehot * ustart[None, :], axis=1) + in_e, 0)

    tiles = (tile_e.astype(jnp.int32), tile_valid.astype(jnp.int32), tile_u.astype(jnp.int32),
             tend[-1:].astype(jnp.int32))
    return h1, wts.T, xs_slab, dst, tiles, x.shape


def _trunk_back(front, prm):
    h1, w_tok, xs_slab, dst, tiles, shape = front
    n_asg = dst.shape[0]
    ys_slab = _experts(xs_slab, prm["w_gu"], prm["w_down"], *tiles)
    y = _sc_move_rows(ys_slab.reshape(-1, SLAB, LANES), dst, n_asg, scatter=True)
    out = _combine(h1, w_tok, y.reshape(n_asg * SLAB, LANES), prm["ws_gu"], prm["ws_down"], prm["ln2_g"], prm["ln2_b"])
    return out.reshape(shape)


def kernel(x_prompt, x_sample, meta_tokens, ln_emb_g, ln_emb_b, rel_bias, w_in, attn_sink, lambda_q1, lambda_k1, lambda_q2, lambda_k2, subln_g, w_out, ln1_g, ln1_b, w_router, router_bias, w_gate, w_up, w_down, ws_gate, ws_up, ws_down, ln2_g, ln2_b):
    f32 = jnp.float32
    bf16 = jnp.bfloat16
    l = 0
    row = lambda v: v.reshape(1, -1).astype(f32)
    lam = (jnp.exp(jnp.sum(lambda_q1[l].astype(f32) * lambda_k1[l].astype(f32)))
           - jnp.exp(jnp.sum(lambda_q2[l].astype(f32) * lambda_k2[l].astype(f32))) + LAMBDA_INIT)
    wr_t = w_router[l].astype(f32).T
    wr_hi = wr_t.astype(bf16)
    prm = {
        "ln_emb_g": row(ln_emb_g), "ln_emb_b": row(ln_emb_b),
        "rel_bias": rel_bias,
        "lam": lam,
        "subln_g": subln_g[l].astype(f32).reshape(-1, 1),
        "w_out": w_out[l].astype(bf16),
        "ln1_g": row(ln1_g[l]), "ln1_b": row(ln1_b[l]),
        "wr_hi": wr_hi, "wr_lo": (wr_t - wr_hi.astype(f32)).astype(bf16),
        "router_bias": router_bias[l].astype(f32).reshape(-1, 1),
        "w_gu": jnp.concatenate([w_gate[l].astype(bf16), w_up[l].astype(bf16)], axis=-1),
        "w_down": w_down[l].astype(bf16),
        "ws_gu": jnp.concatenate([ws_gate[l], ws_up[l]], axis=-1).astype(bf16),
        "ws_down": ws_down[l].astype(bf16),
        "ln2_g": row(ln2_g[l]), "ln2_b": row(ln2_b[l]),
    }
    prm["w_in"], prm["w_vt"] = _prep_w_in(w_in[l])
    prm["proj_meta"], prm["vt_meta"], vta_meta = _ln_inproj(meta_tokens.astype(f32), prm["ln_emb_g"], prm["ln_emb_b"],
                                                           prm["w_in"], prm["w_vt"])
    meta_pad = NK_A - 3 * TQ_A - N_META
    prm["km_a"] = jnp.pad(prm["proj_meta"][:, KA_BLK * LANES:(KA_BLK + 2) * LANES], ((0, meta_pad), (0, 0)))
    prm["vtm_a"] = jnp.pad(vta_meta, ((0, 0), (0, meta_pad)))
    prm["tab_a"], prm["sink_rows"] = _bias_tables_a(rel_bias, attn_sink[l])
    front_p = _trunk_front(x_prompt, prm)
    front_s = _trunk_front(x_sample, prm)
    return (_trunk_back(front_p, prm), _trunk_back(front_s, prm))
```

```python
import functools
import math

import jax
import jax.numpy as jnp
from jax import lax
from jax.experimental import pallas as pl
from jax.experimental.pallas import tpu as pltpu
from jax.experimental.pallas import tpu_sc as plsc

N_META = 16
HEAD_DIM = 64
WINDOW = 128
A_Q_HEADS = 8
A_KV_HEADS = 2
B_HEADS = 4
N_BUCKETS = 32
MAX_DISTANCE = 128
TOP_K = 8
N_GROUPS = 8
TOPK_GROUPS = 4
ROUTED_SCALE = 2.5
LN_EPS = 1e-5
DEPTH = 1
ALPHA = (2 * DEPTH) ** 0.25
NEG = -1e30
LAMBDA_INIT = 0.8 - 0.6 * math.exp(-0.3 * 0)

LANES = 128
VMEM_LIMIT = 48 * 1024 * 1024

QA_BLK = 0
KA_BLK = 4
QB_BLK = 6
KB_BLK = 10
PROJ_COLS = 14 * LANES
VT_ROWS = LANES + 16
VA_ROWS = HEAD_DIM + 16
LOG2E = 1.4426950408889634

TQ_A = 128
NK_A = 4 * TQ_A
TQ_B = 512
TK_B = 512
TM_MOE = 256
SLAB = 8
SC_GRP = 16
SC_NBUF = 4
SC_WORKERS = 32
KEY_SHIFT = 20


def _cparams(sem):
    return pltpu.CompilerParams(dimension_semantics=sem, vmem_limit_bytes=VMEM_LIMIT)


def _layernorm_f32(x, g, b):
    mu = jnp.mean(x, axis=-1, keepdims=True)
    xc = x - mu
    var = jnp.mean(xc * xc, axis=-1, keepdims=True)
    return xc * lax.rsqrt(var + LN_EPS) * g + b


def _dot_nt(a, b):
    return lax.dot_general(a, b, (((1,), (1,)), ((), ())), preferred_element_type=jnp.float32)


def _dot(a, b):
    return jnp.dot(a, b, preferred_element_type=jnp.float32)


def _ln_inproj_kernel(x_ref, g_ref, b_ref, w_ref, wvt_ref, o_ref, vt_ref, vta_ref):
    h = _layernorm_f32(x_ref[...], g_ref[...], b_ref[...]).astype(jnp.bfloat16)
    o_ref[...] = _dot(h, w_ref[...]).astype(o_ref.dtype)
    vt = _dot_nt(wvt_ref[...], h).astype(vt_ref.dtype)
    ones = jnp.ones((16, vt.shape[1]), vt_ref.dtype)
    for hd in range(B_HEADS):
        vt_ref[hd * VT_ROWS:hd * VT_ROWS + LANES, :] = vt[hd * LANES:(hd + 1) * LANES, :]
        vt_ref[hd * VT_ROWS + LANES:(hd + 1) * VT_ROWS, :] = ones
    base = B_HEADS * LANES
    for g in range(A_KV_HEADS):
        vta_ref[g * VA_ROWS:g * VA_ROWS + HEAD_DIM, :] = vt[base + g * HEAD_DIM:base + (g + 1) * HEAD_DIM, :]
        vta_ref[g * VA_ROWS + HEAD_DIM:(g + 1) * VA_ROWS, :] = ones


def _ln_inproj(x2d, g, b, w, wvt):
    t, d = x2d.shape
    n = w.shape[1]
    tm = min(512, t)
    return pl.pallas_call(
        _ln_inproj_kernel,
        out_shape=(jax.ShapeDtypeStruct((t, n), jnp.bfloat16),
                   jax.ShapeDtypeStruct((B_HEADS * VT_ROWS, t), jnp.bfloat16),
                   jax.ShapeDtypeStruct((A_KV_HEADS * VA_ROWS, t), jnp.bfloat16)),
        grid=(t // tm,),
        in_specs=[
            pl.BlockSpec((tm, d), lambda i: (i, 0)),
            pl.BlockSpec((1, d), lambda i: (0, 0)),
            pl.BlockSpec((1, d), lambda i: (0, 0)),
            pl.BlockSpec((d, n), lambda i: (0, 0)),
            pl.BlockSpec((wvt.shape[0], d), lambda i: (0, 0)),
        ],
        out_specs=(pl.BlockSpec((tm, n), lambda i: (i, 0)),
                   pl.BlockSpec((B_HEADS * VT_ROWS, tm), lambda i: (0, i)),
                   pl.BlockSpec((A_KV_HEADS * VA_ROWS, tm), lambda i: (0, i))),
        compiler_params=_cparams(("parallel",)),
        name="ln_inproj",
    )(x2d, g, b, w, wvt)


def _mixer_a_kernel(q_ref, k_ref, vt_ref, km_ref, vtm_ref, tab_ref, sink_ref, o_ref, *, nblk, sub):
    i = pl.program_id(1)
    lane = lax.broadcasted_iota(jnp.int32, (1, LANES), 1)
    lo = lane < HEAD_DIM
    for j in range(sub):
        gi = i * sub + j
        sp = pl.multiple_of(jnp.maximum(gi - 1, 0) * TQ_A, TQ_A)
        sc = pl.multiple_of(gi * TQ_A, TQ_A)
        sn = pl.multiple_of(jnp.minimum(gi + 1, nblk - 1) * TQ_A, TQ_A)
        variant = jnp.where(gi == 0, 0, jnp.where(gi == nblk - 1, 2, 1))
        rows = slice(j * TQ_A, (j + 1) * TQ_A)
        for g in range(A_KV_HEADS):
            gs = slice(g * LANES, (g + 1) * LANES)
            vr = slice(g * VA_ROWS, (g + 1) * VA_ROWS)
            k_all = jnp.concatenate([k_ref[0, pl.ds(sp, TQ_A), gs], k_ref[0, pl.ds(sc, TQ_A), gs],
                                     k_ref[0, pl.ds(sn, TQ_A), gs], km_ref[:, gs]], axis=0)
            vt_all = jnp.concatenate([vt_ref[vr, pl.ds(sp, TQ_A)], vt_ref[vr, pl.ds(sc, TQ_A)],
                                      vt_ref[vr, pl.ds(sn, TQ_A)], vtm_ref[vr, :]], axis=1)
            for pp in range(2):
                hp = 2 * g + pp
                cols = slice(hp * LANES, (hp + 1) * LANES)
                qc = q_ref[0, rows, cols]
                q2 = jnp.concatenate([jnp.where(lo, qc, jnp.zeros_like(qc)), jnp.where(lo, jnp.zeros_like(qc), qc)],
                                     axis=0)
                s = _dot_nt(k_all, q2) + tab_ref[variant, hp]
                sink = sink_ref[hp]
                m = jnp.maximum(jnp.max(s, axis=0, keepdims=True), sink)
                p = jnp.exp2(s - m)
                acc = _dot(vt_all, p.astype(jnp.bfloat16))
                o = acc[0:HEAD_DIM, :] / (acc[HEAD_DIM:HEAD_DIM + 1, :] + jnp.exp2(sink - m))
                o2 = jnp.concatenate([o[:, :TQ_A], o[:, TQ_A:]], axis=0)
                o_ref[0, rows, cols] = o2.T.astype(o_ref.dtype)


def _mixer_a(proj3, vta, km, vtm, tab, sink):
    bsz, s, _ = proj3.shape
    nblk = s // TQ_A
    assert nblk >= 2
    sub = 4 if nblk % 4 == 0 else 1
    nq = nblk // sub
    tq = sub * TQ_A
    kern = functools.partial(_mixer_a_kernel, nblk=nblk, sub=sub)
    return pl.pallas_call(
        kern,
        out_shape=jax.ShapeDtypeStruct((bsz, s, A_Q_HEADS * HEAD_DIM), jnp.bfloat16),
        grid=(bsz, nq),
        in_specs=[
            pl.BlockSpec((1, tq, 4 * LANES), lambda b, i: (b, i, QA_BLK // 4)),
            pl.BlockSpec((1, s, 2 * LANES), lambda b, i: (b, 0, KA_BLK // 2)),
            pl.BlockSpec((A_KV_HEADS * VA_ROWS, s), lambda b, i: (0, b)),
            pl.BlockSpec(km.shape, lambda b, i: (0, 0)),
            pl.BlockSpec(vtm.shape, lambda b, i: (0, 0)),
            pl.BlockSpec(tab.shape, lambda b, i: (0, 0, 0, 0)),
            pl.BlockSpec(sink.shape, lambda b, i: (0, 0, 0)),
        ],
        out_specs=pl.BlockSpec((1, tq, 4 * LANES), lambda b, i: (b, i, 0)),
        compiler_params=_cparams(("parallel", "arbitrary")),
        name="mixer_a",
    )(proj3, proj3, vta, km, vtm, tab, sink)


def _mixer_b_kernel(sc_ref, q_ref, k_ref, vt_ref, km_ref, vtm_ref, tab_ref, tabm_ref, g_ref, o_ref,
                    m_ref, acc_ref, s0_ref, s1_ref, *, nk):
    h = pl.program_id(1)
    i = pl.program_id(2)
    lane = lax.broadcasted_iota(jnp.int32, (1, LANES), 1)
    lo = lane < HEAD_DIM
    q = q_ref[0]
    qs = (jnp.where(lo, q, jnp.zeros_like(q)), jnp.where(lo, jnp.zeros_like(q), q))
    lam = sc_ref[0]
    sbufs = (s0_ref, s1_ref)

    km = km_ref[...]
    vtm = vtm_ref[...]
    for c in range(2):
        s = _dot_nt(km, qs[c]) + tabm_ref[0]
        m = jnp.max(s, axis=0, keepdims=True)
        p = jnp.exp2(s - m)
        m_ref[c] = m
        acc_ref[c] = _dot(vtm, p.astype(jnp.bfloat16))

    def scores(j, slot):
        start = pl.multiple_of(j * TK_B, TK_B)
        kj = k_ref[0, pl.ds(start, TK_B), :]
        bias = tab_ref[0, jnp.clip(j - i, -2, 2) + 2]
        for c in range(2):
            sbufs[slot][c] = _dot_nt(kj, qs[c]) + bias

    def accumulate(j, slot):
        start = pl.multiple_of(j * TK_B, TK_B)
        vtj = vt_ref[:, pl.ds(start, TK_B)]
        for c in range(2):
            s = sbufs[slot][c]
            m_prev = m_ref[c]
            m_new = jnp.maximum(m_prev, jnp.max(s, axis=0, keepdims=True))
            a = jnp.exp2(m_prev - m_new)
            p = jnp.exp2(s - m_new)
            m_ref[c] = m_new
            acc_ref[c] = a * acc_ref[c] + _dot(vtj, p.astype(jnp.bfloat16))

    scores(0, 0)

    def pair(jj, carry):
        j = 2 * jj
        scores(j + 1, 1)
        accumulate(j, 0)
        scores(j + 2, 0)
        accumulate(j + 1, 1)
        return carry

    lax.fori_loop(0, nk // 2 - 1, pair, 0)
    scores(nk - 1, 1)
    accumulate(nk - 2, 0)
    accumulate(nk - 1, 1)

    o0 = acc_ref[0, 0:LANES, :] / acc_ref[0, LANES:LANES + 1, :]
    o1 = acc_ref[1, 0:LANES, :] / acc_ref[1, LANES:LANES + 1, :]
    o = o0 - lam * o1
    ms = jnp.mean(o * o, axis=0, keepdims=True)
    o = o * lax.rsqrt(ms + LN_EPS) * (g_ref[...] * (1.0 - LAMBDA_INIT))
    o_ref[0] = o.T.astype(o_ref.dtype)


def _mixer_b(proj3, vt, proj_meta, vt_meta, tab, tabm, scal, subln_g):
    bsz, s, _ = proj3.shape
    nq = s // TQ_B
    nk = s // TK_B
    assert nk % 2 == 0
    kern = functools.partial(_mixer_b_kernel, nk=nk)
    return pl.pallas_call(
        kern,
        out_shape=jax.ShapeDtypeStruct((bsz, s, B_HEADS * LANES), jnp.bfloat16),
        grid=(bsz, B_HEADS, nq),
        in_specs=[
            pl.BlockSpec(memory_space=pltpu.SMEM),
            pl.BlockSpec((1, TQ_B, LANES), lambda b, h, i: (b, i, QB_BLK + h)),
            pl.BlockSpec((1, s, LANES), lambda b, h, i: (b, 0, KB_BLK + h)),
            pl.BlockSpec((VT_ROWS, s), lambda b, h, i: (h, b)),
            pl.BlockSpec((N_META, LANES), lambda b, h, i: (0, KB_BLK + h)),
            pl.BlockSpec((VT_ROWS, N_META), lambda b, h, i: (h, 0)),
            pl.BlockSpec((1, 5, TK_B, TQ_B), lambda b, h, i: (h, 0, 0, 0)),
            pl.BlockSpec((1, N_META, TQ_B), lambda b, h, i: (h, 0, i)),
            pl.BlockSpec((LANES, 1), lambda b, h, i: (0, 0)),
        ],
        out_specs=pl.BlockSpec((1, TQ_B, LANES), lambda b, h, i: (b, i, h)),
        scratch_shapes=[
            pltpu.VMEM((2, 1, TQ_B), jnp.float32),
            pltpu.VMEM((2, VT_ROWS, TQ_B), jnp.float32),
            pltpu.VMEM((2, TK_B, TQ_B), jnp.float32),
            pltpu.VMEM((2, TK_B, TQ_B), jnp.float32),
        ],
        compiler_params=_cparams(("parallel", "parallel", "arbitrary")),
        name="mixer_b",
    )(scal, proj3, proj3, vt, proj_meta, vt_meta, tab, tabm, subln_g)


def _slab_load(ref, lead, rows):
    return jnp.concatenate([ref[lead + (pl.ds(s, rows, stride=SLAB), slice(None))] for s in range(SLAB)], axis=1)


def _slab_store(ref, lead, val):
    rows = val.shape[0]
    for s in range(SLAB):
        ref[lead + (pl.ds(s, rows, stride=SLAB), slice(None))] = val[:, s * LANES:(s + 1) * LANES]


def _outproj_kernel(x_ref, oa_ref, ob_ref, w_ref, eg_ref, eb_ref, g_ref, b_ref, o_ref, os_ref):
    h0 = _layernorm_f32(x_ref[...], eg_ref[...], eb_ref[...])
    half = oa_ref.shape[1]
    mix = _dot(oa_ref[...], w_ref[0:half, :]) + _dot(ob_ref[...], w_ref[half:, :])
    h1 = _layernorm_f32(ALPHA * h0 + mix, g_ref[...], b_ref[...])
    o_ref[...] = h1
    _slab_store(os_ref, (), h1)


def _outproj_ln1(x2d, oa, ob, w_out, eg, eb, g, b):
    t, d = x2d.shape
    tm = min(512, t)
    wa = oa.shape[1]
    wb = ob.shape[1]
    vec = pl.BlockSpec((1, d), lambda i: (0, 0))
    return pl.pallas_call(
        _outproj_kernel,
        out_shape=(jax.ShapeDtypeStruct((t, d), jnp.float32),
                   jax.ShapeDtypeStruct((t * SLAB, LANES), jnp.float32)),
        grid=(t // tm,),
        in_specs=[
            pl.BlockSpec((tm, d), lambda i: (i, 0)),
            pl.BlockSpec((tm, wa), lambda i: (i, 0)),
            pl.BlockSpec((tm, wb), lambda i: (i, 0)),
            pl.BlockSpec((wa + wb, d), lambda i: (0, 0)),
            vec, vec, vec, vec,
        ],
        out_specs=(pl.BlockSpec((tm, d), lambda i: (i, 0)),
                   pl.BlockSpec((tm * SLAB, LANES), lambda i: (i, 0))),
        compiler_params=_cparams(("parallel",)),
        name="outproj_ln1",
    )(x2d, oa, ob, w_out, eg, eb, g, b)


def _router_kernel(h_ref, wh_ref, wl_ref, rb_ref, e_ref, w_ref, cnt_ref, carry_ref, *, n_exp):
    i = pl.program_id(0)
    tn = h_ref.shape[0]
    gsz = n_exp // N_GROUPS

    @pl.when(i == 0)
    def _():
        carry_ref[...] = jnp.zeros_like(carry_ref)

    x = h_ref[...]
    xh = x.astype(jnp.bfloat16)
    xl = (x - xh.astype(jnp.float32)).astype(jnp.bfloat16)
    logits = _dot_nt(wh_ref[...], xh) + (_dot_nt(wh_ref[...], xl) + _dot_nt(wl_ref[...], xh))
    scores = 1.0 / (1.0 + jnp.exp(-logits))
    biased = scores + rb_ref[...]

    g3 = biased.reshape(N_GROUPS, gsz, tn)
    it3 = lax.broadcasted_iota(jnp.int32, (N_GROUPS, gsz, tn), 1)
    mx1 = jnp.max(g3, axis=1, keepdims=True)
    first = jnp.min(jnp.where(g3 == mx1, it3, gsz), axis=1, keepdims=True)
    mx2 = jnp.max(jnp.where(it3 == first, -jnp.inf, g3), axis=1, keepdims=True)
    gscore = (mx1 + mx2).reshape(N_GROUPS, tn)

    itg = lax.broadcasted_iota(jnp.int32, (N_GROUPS, tn), 0)
    gsel = jnp.zeros((N_GROUPS, tn), jnp.bool_)
    cur = gscore
    for _ in range(TOPK_GROUPS):
        mx = jnp.max(cur, axis=0, keepdims=True)
        fi = jnp.min(jnp.where(cur == mx, itg, N_GROUPS), axis=0, keepdims=True)
        hit = itg == fi
        gsel = jnp.logical_or(gsel, hit)
        cur = jnp.where(hit, -jnp.inf, cur)
    emask = jnp.broadcast_to(gsel.reshape(N_GROUPS, 1, tn), (N_GROUPS, gsz, tn)).reshape(n_exp, tn)
    cur = jnp.where(emask, biased, NEG)

    ite = lax.broadcasted_iota(jnp.int32, (n_exp, tn), 0)
    hits = []
    eidx = []
    wsel = []
    for _ in range(TOP_K):
        mx = jnp.max(cur, axis=0, keepdims=True)
        fi = jnp.min(jnp.where(cur == mx, ite, n_exp), axis=0, keepdims=True)
        hit = ite == fi
        hits.append(hit)
        eidx.append(fi)
        wsel.append(jnp.sum(jnp.where(hit, scores, 0.0), axis=0, keepdims=True))
        cur = jnp.where(hit, -jnp.inf, cur)
    sel = hits[0]
    for hit in hits[1:]:
        sel = jnp.logical_or(sel, hit)
    self32 = jnp.where(sel, 1.0, 0.0)

    carry_ref[...] = carry_ref[...] + jnp.sum(self32, axis=1, keepdims=True)
    cnt_ref[...] = carry_ref[...]

    wcat = jnp.concatenate(wsel, axis=0)
    wcat = wcat / jnp.sum(wcat, axis=0, keepdims=True) * ROUTED_SCALE
    tok = i * tn + lax.broadcasted_iota(jnp.int32, (TOP_K, tn), 1)
    slot = lax.broadcasted_iota(jnp.int32, (TOP_K, tn), 0)
    e_ref[...] = jnp.concatenate(eidx, axis=0) * (1 << KEY_SHIFT) + (tok * TOP_K + slot)
    w_ref[...] = wcat


def _router(h1, wr_hi, wr_lo, rbias):
    t, d = h1.shape
    n_exp = wr_hi.shape[0]
    tn = min(512, t)
    kern = functools.partial(_router_kernel, n_exp=n_exp)
    row = pl.BlockSpec((TOP_K, tn), lambda i: (0, i))
    return pl.pallas_call(
        kern,
        out_shape=(
            jax.ShapeDtypeStruct((TOP_K, t), jnp.int32),
            jax.ShapeDtypeStruct((TOP_K, t), jnp.float32),
            jax.ShapeDtypeStruct((n_exp, 1), jnp.float32),
        ),
        grid=(t // tn,),
        in_specs=[
            pl.BlockSpec((tn, d), lambda i: (i, 0)),
            pl.BlockSpec((n_exp, d), lambda i: (0, 0)),
            pl.BlockSpec((n_exp, d), lambda i: (0, 0)),
            pl.BlockSpec((n_exp, 1), lambda i: (0, 0)),
        ],
        out_specs=(row, row, pl.BlockSpec((n_exp, 1), lambda i: (0, 0))),
        scratch_shapes=[pltpu.VMEM((n_exp, 1), jnp.float32)],
        compiler_params=_cparams(("arbitrary",)),
        name="router",
    )(h1, wr_hi, wr_lo, rbias)


def _sc_move_rows(x_slab, idx, n_out, scatter):
    m = idx.shape[0]
    rnd = SC_GRP * SC_NBUF
    per = m // SC_WORKERS
    assert m % (SC_WORKERS * rnd) == 0
    row = x_slab.shape[1:]
    mesh = plsc.VectorSubcoreMesh(core_axis_name="c", subcore_axis_name="s")

    @pl.kernel(out_type=jax.ShapeDtypeStruct((n_out,) + row, x_slab.dtype), mesh=mesh,
               scratch_types=[pltpu.VMEM((rnd,), jnp.int32), pltpu.VMEM((SC_NBUF, SC_GRP) + row, x_slab.dtype),
                              pltpu.SemaphoreType.DMA((SC_NBUF,)), pltpu.SemaphoreType.DMA((SC_NBUF,))])
    def kern(x_hbm, i_hbm, o_hbm, ibuf, buf, lsem, ssem):
        worker = lax.axis_index("c") * (SC_WORKERS // 2) + lax.axis_index("s")
        base = worker * per

        @pl.loop(0, per // rnd)
        def _(r):
            off = base + r * rnd
            pltpu.sync_copy(i_hbm.at[pl.ds(off, rnd)], ibuf)
            loads = []
            stores = []
            for b in range(SC_NBUF):
                indexed = ibuf.at[pl.ds(b * SC_GRP, SC_GRP)]
                linear = pl.ds(off + b * SC_GRP, SC_GRP)
                src = x_hbm.at[linear] if scatter else x_hbm.at[indexed]
                dst = o_hbm.at[indexed] if scatter else o_hbm.at[linear]
                loads.append(pltpu.make_async_copy(src, buf.at[b], lsem.at[b]))
                stores.append(pltpu.make_async_copy(buf.at[b], dst, ssem.at[b]))
            for ld in loads:
                ld.start()
            for b in range(SC_NBUF):
                loads[b].wait()
                stores[b].start()
            for st in stores:
                st.wait()

    return kern(x_slab, idx)


def _experts_kernel(te_ref, tv_ref, tu_ref, na_ref, xs_hbm, wg_ref, wu_ref, wd_ref, ys_hbm,
                    xb0, xb1, yb0, yb1, wgu_s, wd_s, xsem, ysem):
    i = pl.program_id(0)
    nt = pl.num_programs(0)
    n_act = na_ref[0]
    xbufs = (xb0, xb1)
    ybufs = (yb0, yb1)
    tile_rows = TM_MOE * SLAB

    def rows_of(ref, tile):
        return ref.at[pl.ds(pl.multiple_of(tu_ref[tile] * SLAB, SLAB), tile_rows), :]

    def x_copy(slot, tile):
        return pltpu.make_async_copy(rows_of(xs_hbm, tile), xbufs[slot], xsem.at[slot])

    def y_copy(slot, tile):
        return pltpu.make_async_copy(ybufs[slot], rows_of(ys_hbm, tile), ysem.at[slot])

    @pl.when(i == 0)
    def _():
        x_copy(0, 0).start()

    iprev = jnp.maximum(i - 1, 0)
    inext = jnp.minimum(i + 1, nt - 1)

    @pl.when(jnp.logical_and(i < n_act, jnp.logical_or(i == 0, te_ref[i] != te_ref[iprev])))
    def _():
        de = wg_ref.shape[2]
        wgu_s[:, 0:de] = wg_ref[0].astype(wgu_s.dtype)
        wgu_s[:, de:] = wu_ref[0].astype(wgu_s.dtype)
        wd_s[...] = wd_ref[0].astype(wd_s.dtype)

    for slot in range(2):
        other = 1 - slot

        @pl.when(jnp.logical_and(i < n_act, (i & 1) == slot))
        def _(slot=slot, other=other):
            x_copy(slot, i).wait()
            x_copy(other, inext).start()
            x = _slab_load(xbufs[slot], (), TM_MOE).astype(jnp.bfloat16)
            gu = _dot(x, wgu_s[...])
            de = gu.shape[1] // 2
            gate = gu[:, :de]
            hid = (gate / (1.0 + jnp.exp(-gate))) * gu[:, de:]
            y = _dot(hid.astype(jnp.bfloat16), wd_s[...])
            _slab_store(ybufs[slot], (), y)

            @pl.when(i >= 1)
            def _():
                y_copy(other, iprev).wait()

            y_copy(slot, i).start()

        @pl.when(jnp.logical_and(i == n_act, (i & 1) == slot))
        def _(slot=slot, other=other):
            x_copy(slot, i).wait()
            y_copy(other, iprev).wait()


def _experts(xs_slab, wg, wu, wd, tile_e, tile_valid, tile_u, n_active):
    nt = tile_e.shape[0]
    d = wg.shape[1]
    de = wg.shape[2]
    buf = pltpu.VMEM((TM_MOE * SLAB, LANES), jnp.float32)
    return pl.pallas_call(
        _experts_kernel,
        out_shape=jax.ShapeDtypeStruct(xs_slab.shape, jnp.float32),
        grid_spec=pltpu.PrefetchScalarGridSpec(
            num_scalar_prefetch=4,
            grid=(nt,),
            in_specs=[
                pl.BlockSpec(memory_space=pl.ANY),
                pl.BlockSpec((1, d, de), lambda i, te, tv, tu, na: (te[i], 0, 0)),
                pl.BlockSpec((1, d, de), lambda i, te, tv, tu, na: (te[i], 0, 0)),
                pl.BlockSpec((1, de, d), lambda i, te, tv, tu, na: (te[i], 0, 0)),
            ],
            out_specs=pl.BlockSpec(memory_space=pl.ANY),
            scratch_shapes=[buf, buf, buf, buf, pltpu.VMEM((d, 2 * de), jnp.bfloat16), pltpu.VMEM((de, d), jnp.bfloat16),
                            pltpu.SemaphoreType.DMA((2,)), pltpu.SemaphoreType.DMA((2,))],
        ),
        compiler_params=_cparams(("arbitrary",)),
        name="experts",
    )(tile_e, tile_valid, tile_u, n_active, xs_slab, wg, wu, wd)


def _combine_kernel(h_ref, w_ref, *rest):
    y_refs = rest[:TOP_K]
    sgu_ref, sd_ref, g_ref, b_ref, o_ref = rest[TOP_K:]
    tn = h_ref.shape[0]
    h = h_ref[...]
    gu = _dot(h.astype(jnp.bfloat16), sgu_ref[...])
    ds = gu.shape[1] // 2
    gate = gu[:, :ds]
    hid = (gate / (1.0 + jnp.exp(-gate))) * gu[:, ds:]
    shared = _dot(hid.astype(jnp.bfloat16), sd_ref[...])

    w = w_ref[...]
    routed = _slab_load(y_refs[0], (), tn) * w[:, 0:1]
    for k in range(1, TOP_K):
        routed = routed + _slab_load(y_refs[k], (), tn) * w[:, k:k + 1]
    o_ref[...] = _layernorm_f32(ALPHA * h + (routed + shared), g_ref[...], b_ref[...])


def _combine(h1, w_tok, y_slab, sgu, sd, g, b):
    t, d = h1.shape
    tn = min(256, t)
    nblk = t // tn
    ds2 = sgu.shape[1]
    vec = pl.BlockSpec((1, d), lambda i: (0, 0))

    def slot_spec(k):
        return pl.BlockSpec((tn * SLAB, LANES), lambda i: (k * nblk + i, 0))

    return pl.pallas_call(
        _combine_kernel,
        out_shape=jax.ShapeDtypeStruct((t, d), jnp.float32),
        grid=(nblk,),
        in_specs=[
            pl.BlockSpec((tn, d), lambda i: (i, 0)),
            pl.BlockSpec((tn, TOP_K), lambda i: (i, 0)),
            *[slot_spec(k) for k in range(TOP_K)],
            pl.BlockSpec((d, ds2), lambda i: (0, 0)),
            pl.BlockSpec((ds2 // 2, d), lambda i: (0, 0)),
            vec, vec,
        ],
        out_specs=pl.BlockSpec((tn, d), lambda i: (i, 0)),
        compiler_params=_cparams(("parallel",)),
        name="combine_ln2",
    )(h1, w_tok, *([y_slab] * TOP_K), sgu, sd, g, b)


def _rel_bucket(rel):
    nb = N_BUCKETS // 2
    max_exact = nb // 2
    ret = jnp.where(rel > 0, nb, 0)
    n = jnp.abs(rel)
    nf = jnp.maximum(n, 1).astype(jnp.float32)
    large = max_exact + (jnp.log(nf / max_exact) / math.log(MAX_DISTANCE / max_exact) * (nb - max_exact)).astype(jnp.int32)
    large = jnp.minimum(large, nb - 1)
    return ret + jnp.where(n < max_exact, n, large)


def _bias_of_rel(rel_bias, rel):
    return rel_bias.astype(jnp.float32)[_rel_bucket(rel)]


def _toeplitz(vec, nrow, ncol, off):
    lo = off - (nrow - 1)
    v = vec[lo:off + ncol]
    p = v.shape[0] + 1
    v = jnp.concatenate([v, v[:1]], axis=0)
    flat = jnp.tile(v, (nrow + 1, 1))
    base = off - lo
    out = flat[base:base + nrow * (p - 1)].reshape(nrow, p - 1, vec.shape[1])
    return out[:, :ncol]


def _bias_tables_a(rel_bias, sink):
    bias_a = rel_bias[:, :A_Q_HEADS]
    m = jnp.arange(3 * TQ_A + TQ_A - 1, dtype=jnp.int32)
    rel = 2 * TQ_A - 1 - m
    vec = jnp.where((jnp.abs(rel) <= WINDOW)[:, None], _bias_of_rel(bias_a, rel), NEG)
    band = _toeplitz(vec, 3 * TQ_A, TQ_A, 3 * TQ_A - 1)
    mm = jnp.arange(TQ_A + N_META - 1, dtype=jnp.int32)
    meta_first = _toeplitz(_bias_of_rel(bias_a, -1 - mm), N_META, TQ_A, N_META - 1)
    meta_far = jnp.broadcast_to(_bias_of_rel(bias_a, jnp.int32(-2 * MAX_DISTANCE)), (N_META, TQ_A, A_Q_HEADS))
    pad = jnp.full((NK_A - 3 * TQ_A - N_META, TQ_A, A_Q_HEADS), NEG, jnp.float32)
    blocked = jnp.full((TQ_A, TQ_A, A_Q_HEADS), NEG, jnp.float32)
    first = jnp.concatenate([blocked, band[TQ_A:], meta_first, pad])
    middle = jnp.concatenate([band, meta_far, pad])
    last = jnp.concatenate([band[:2 * TQ_A], blocked, meta_far, pad])
    tab = jnp.stack([first, middle, last])
    tab = tab.reshape(3, NK_A, TQ_A, A_Q_HEADS // 2, 2)
    tab = jnp.transpose(tab, (0, 3, 1, 4, 2)).reshape(3, A_Q_HEADS // 2, NK_A, 2 * TQ_A) * LOG2E
    sink_rows = jnp.repeat(sink.astype(jnp.float32).reshape(A_Q_HEADS // 2, 1, 2), TQ_A, axis=2) * LOG2E
    return tab, sink_rows


def _bias_tables_b(rel_bias, s):
    bias_b = rel_bias[:, A_Q_HEADS:]
    near = []
    for d in (-1, 0, 1):
        m = jnp.arange(TQ_B + TK_B - 1, dtype=jnp.int32)
        vec = _bias_of_rel(bias_b, TK_B * d + TK_B - 1 - m)
        near.append(_toeplitz(vec, TK_B, TQ_B, TK_B - 1))
    far_l = jnp.broadcast_to(_bias_of_rel(bias_b, jnp.int32(-TK_B - 1)), (TK_B, TQ_B, B_HEADS))
    far_r = jnp.broadcast_to(_bias_of_rel(bias_b, jnp.int32(TK_B + 1)), (TK_B, TQ_B, B_HEADS))
    tabs = jnp.transpose(jnp.stack([far_l] + near + [far_r]), (3, 0, 1, 2)) * LOG2E
    m = jnp.arange(s + N_META - 1, dtype=jnp.int32)
    vec = _bias_of_rel(bias_b, -1 - m)
    meta = jnp.transpose(_toeplitz(vec, N_META, s, N_META - 1), (2, 0, 1)) * LOG2E
    return tabs, meta


def _prep_w_in(w_in):
    a_w = A_Q_HEADS * HEAD_DIM
    kv = A_KV_HEADS * HEAD_DIM
    bqk = B_HEADS * 2 * HEAD_DIM
    scale = HEAD_DIM ** -0.5
    qa = w_in[:, :a_w] * (scale * LOG2E)
    ka = w_in[:, a_w:a_w + kv]
    va = w_in[:, a_w + kv:a_w + 2 * kv]
    o = a_w + 2 * kv
    qb = w_in[:, o:o + bqk] * (scale * LOG2E)
    kb = w_in[:, o + bqk:o + 2 * bqk]
    vb = w_in[:, o + 2 * bqk:]

    def dup(w):
        return jnp.concatenate([w[:, g * HEAD_DIM:(g + 1) * HEAD_DIM] for g in range(A_KV_HEADS) for _ in range(2)], axis=1)

    w = jnp.concatenate([qa, dup(ka), qb, kb], axis=1).astype(jnp.bfloat16)
    return w, jnp.concatenate([vb, va], axis=1).T.astype(jnp.bfloat16)


def _trunk_front(x, prm):
    bsz, s, d = x.shape
    t = bsz * s
    x2d = x.reshape(t, d)
    proj, vt, vta = _ln_inproj(x2d, prm["ln_emb_g"], prm["ln_emb_b"], prm["w_in"], prm["w_vt"])
    proj3 = proj.reshape(bsz, s, PROJ_COLS)
    oa = _mixer_a(proj3, vta, prm["km_a"], prm["vtm_a"], prm["tab_a"], prm["sink_rows"])
    tabs_b, meta_b = _bias_tables_b(prm["rel_bias"], s)
    ob = _mixer_b(proj3, vt, prm["proj_meta"], prm["vt_meta"], tabs_b, meta_b, prm["lam"].reshape(1).astype(jnp.float32),
                  prm["subln_g"])
    h1, h1_slab = _outproj_ln1(x2d, oa.reshape(t, -1), ob.reshape(t, -1), prm["w_out"], prm["ln_emb_g"],
                               prm["ln_emb_b"], prm["ln1_g"], prm["ln1_b"])

    keys, wts, counts = _router(h1, prm["wr_hi"], prm["wr_lo"], prm["router_bias"])
    n_exp = counts.shape[0]
    n_asg = t * TOP_K
    assert n_asg <= (1 << KEY_SHIFT) and n_asg % LANES == 0
    order = jnp.sort(keys.reshape(n_asg)) & ((1 << KEY_SHIFT) - 1)
    tok = order >> 3
    dst = (order & (TOP_K - 1)) * t + tok
    sc_pad = SC_WORKERS * SC_GRP * SC_NBUF
    assert sc_pad >= TM_MOE and n_asg % sc_pad == 0
    xs = _sc_move_rows(h1_slab.reshape(t, SLAB, LANES), jnp.concatenate([tok, jnp.zeros((sc_pad,), jnp.int32)]),
                       n_asg + sc_pad, scatter=False)
    xs_slab = xs.reshape((n_asg + sc_pad) * SLAB, LANES)
    counts = counts[:, 0].astype(jnp.int32)
    tiles_e = (counts + TM_MOE - 1) // TM_MOE
    tend = jnp.cumsum(tiles_e)
    tstart = tend - tiles_e
    ustart = jnp.cumsum(counts) - counts
    nt = n_asg // TM_MOE + n_exp + 1
    tid = jnp.arange(nt, dtype=jnp.int32)
    tile_e = jnp.minimum(jnp.sum((tend[None, :] <= tid[:, None]).astype(jnp.int32), axis=1), n_exp - 1)
    onehot = (tile_e[:, None] == jnp.arange(n_exp, dtype=jnp.int32)[None, :]).astype(jnp.int32)
    in_e = (tid - jnp.sum(onehot * tstart[None, :], axis=1)) * TM_MOE
    active = tid < tend[-1]
    tile_valid = jnp.where(active, jnp.clip(jnp.sum(onehot * counts[None, :], axis=1) - in_e, 0, TM_MOE), 0)
    tile_u = jnp.where(active, jnp.sum(onehot * ustart[None, :], axis=1) + in_e, 0)

    tiles = (tile_e.astype(jnp.int32), tile_valid.astype(jnp.int32), tile_u.astype(jnp.int32),
             tend[-1:].astype(jnp.int32))
    return h1, wts.T, xs_slab, dst, tiles, x.shape


def _trunk_back(front, prm):
    h1, w_tok, xs_slab, dst, tiles, shape = front
    n_asg = dst.shape[0]
    ys_slab = _experts(xs_slab, prm["w_gate"], prm["w_up"], prm["w_down"], *tiles)
    y = _sc_move_rows(ys_slab.reshape(-1, SLAB, LANES), dst, n_asg, scatter=True)
    out = _combine(h1, w_tok, y.reshape(n_asg * SLAB, LANES), prm["ws_gu"], prm["ws_down"], prm["ln2_g"], prm["ln2_b"])
    return out.reshape(shape)


def kernel(x_prompt, x_sample, meta_tokens, ln_emb_g, ln_emb_b, rel_bias, w_in, attn_sink, lambda_q1, lambda_k1, lambda_q2, lambda_k2, subln_g, w_out, ln1_g, ln1_b, w_router, router_bias, w_gate, w_up, w_down, ws_gate, ws_up, ws_down, ln2_g, ln2_b):
    f32 = jnp.float32
    bf16 = jnp.bfloat16
    l = 0
    row = lambda v: v.reshape(1, -1).astype(f32)
    lam = (jnp.exp(jnp.sum(lambda_q1[l].astype(f32) * lambda_k1[l].astype(f32)))
           - jnp.exp(jnp.sum(lambda_q2[l].astype(f32) * lambda_k2[l].astype(f32))) + LAMBDA_INIT)
    wr_t = w_router[l].astype(f32).T
    wr_hi = wr_t.astype(bf16)
    prm = {
        "ln_emb_g": row(ln_emb_g), "ln_emb_b": row(ln_emb_b),
        "rel_bias": rel_bias,
        "lam": lam,
        "subln_g": subln_g[l].astype(f32).reshape(-1, 1),
        "w_out": w_out[l].astype(bf16),
        "ln1_g": row(ln1_g[l]), "ln1_b": row(ln1_b[l]),
        "wr_hi": wr_hi, "wr_lo": (wr_t - wr_hi.astype(f32)).astype(bf16),
        "router_bias": router_bias[l].astype(f32).reshape(-1, 1),
        "w_gate": w_gate[l], "w_up": w_up[l], "w_down": w_down[l],
        "ws_gu": jnp.concatenate([ws_gate[l], ws_up[l]], axis=-1).astype(bf16),
        "ws_down": ws_down[l].astype(bf16),
        "ln2_g": row(ln2_g[l]), "ln2_b": row(ln2_b[l]),
    }
    prm["w_in"], prm["w_vt"] = _prep_w_in(w_in[l])
    prm["proj_meta"], prm["vt_meta"], vta_meta = _ln_inproj(meta_tokens.astype(f32), prm["ln_emb_g"], prm["ln_emb_b"],
                                                           prm["w_in"], prm["w_vt"])
    meta_pad = NK_A - 3 * TQ_A - N_META
    prm["km_a"] = jnp.pad(prm["proj_meta"][:, KA_BLK * LANES:(KA_BLK + 2) * LANES], ((0, meta_pad), (0, 0)))
    prm["vtm_a"] = jnp.pad(vta_meta, ((0, 0), (0, meta_pad)))
    prm["tab_a"], prm["sink_rows"] = _bias_tables_a(rel_bias, attn_sink[l])
    front_p = _trunk_front(x_prompt, prm)
    front_s = _trunk_front(x_sample, prm)
    return (_trunk_back(front_p, prm), _trunk_back(front_s, prm))
```

```python
import functools
import math

import jax
import jax.numpy as jnp
from jax import lax
from jax.experimental import pallas as pl
from jax.experimental.pallas import tpu as pltpu
from jax.experimental.pallas import tpu_sc as plsc

N_META = 16
HEAD_DIM = 64
WINDOW = 128
A_Q_HEADS = 8
A_KV_HEADS = 2
B_HEADS = 4
N_BUCKETS = 32
MAX_DISTANCE = 128
TOP_K = 8
N_GROUPS = 8
TOPK_GROUPS = 4
ROUTED_SCALE = 2.5
LN_EPS = 1e-5
DEPTH = 1
ALPHA = (2 * DEPTH) ** 0.25
NEG = -1e30
LAMBDA_INIT = 0.8 - 0.6 * math.exp(-0.3 * 0)

LANES = 128
VMEM_LIMIT = 48 * 1024 * 1024

QA_BLK = 0
KA_BLK = 4
QB_BLK = 6
KB_BLK = 10
PROJ_COLS = 14 * LANES
VT_ROWS = LANES + 16
VA_ROWS = HEAD_DIM + 16
LOG2E = 1.4426950408889634

TQ_A = 128
NK_A = 4 * TQ_A
TQ_B = 512
TK_B = 512
TM_MOE = 256
SLAB = 8
SC_GRP = 16
SC_NBUF = 4
SC_WORKERS = 32
KEY_SHIFT = 20


def _cparams(sem):
    return pltpu.CompilerParams(dimension_semantics=sem, vmem_limit_bytes=VMEM_LIMIT)


def _layernorm_f32(x, g, b):
    mu = jnp.mean(x, axis=-1, keepdims=True)
    xc = x - mu
    var = jnp.mean(xc * xc, axis=-1, keepdims=True)
    return xc * lax.rsqrt(var + LN_EPS) * g + b


def _dot_nt(a, b):
    return lax.dot_general(a, b, (((1,), (1,)), ((), ())), preferred_element_type=jnp.float32)


def _dot(a, b):
    return jnp.dot(a, b, preferred_element_type=jnp.float32)


def _ln_inproj_kernel(x_ref, g_ref, b_ref, w_ref, wvt_ref, o_ref, vt_ref, vta_ref):
    h = _layernorm_f32(x_ref[...], g_ref[...], b_ref[...]).astype(jnp.bfloat16)
    o_ref[...] = _dot(h, w_ref[...]).astype(o_ref.dtype)
    vt = _dot_nt(wvt_ref[...], h).astype(vt_ref.dtype)
    ones = jnp.ones((16, vt.shape[1]), vt_ref.dtype)
    for hd in range(B_HEADS):
        vt_ref[hd * VT_ROWS:hd * VT_ROWS + LANES, :] = vt[hd * LANES:(hd + 1) * LANES, :]
        vt_ref[hd * VT_ROWS + LANES:(hd + 1) * VT_ROWS, :] = ones
    base = B_HEADS * LANES
    for g in range(A_KV_HEADS):
        vta_ref[g * VA_ROWS:g * VA_ROWS + HEAD_DIM, :] = vt[base + g * HEAD_DIM:base + (g + 1) * HEAD_DIM, :]
        vta_ref[g * VA_ROWS + HEAD_DIM:(g + 1) * VA_ROWS, :] = ones


def _ln_inproj(x2d, g, b, w, wvt):
    t, d = x2d.shape
    n = w.shape[1]
    tm = min(512, t)
    return pl.pallas_call(
        _ln_inproj_kernel,
        out_shape=(jax.ShapeDtypeStruct((t, n), jnp.bfloat16),
                   jax.ShapeDtypeStruct((B_HEADS * VT_ROWS, t), jnp.bfloat16),
                   jax.ShapeDtypeStruct((A_KV_HEADS * VA_ROWS, t), jnp.bfloat16)),
        grid=(t // tm,),
        in_specs=[
            pl.BlockSpec((tm, d), lambda i: (i, 0)),
            pl.BlockSpec((1, d), lambda i: (0, 0)),
            pl.BlockSpec((1, d), lambda i: (0, 0)),
            pl.BlockSpec((d, n), lambda i: (0, 0)),
            pl.BlockSpec((wvt.shape[0], d), lambda i: (0, 0)),
        ],
        out_specs=(pl.BlockSpec((tm, n), lambda i: (i, 0)),
                   pl.BlockSpec((B_HEADS * VT_ROWS, tm), lambda i: (0, i)),
                   pl.BlockSpec((A_KV_HEADS * VA_ROWS, tm), lambda i: (0, i))),
        compiler_params=_cparams(("parallel",)),
        name="ln_inproj",
    )(x2d, g, b, w, wvt)


def _mixer_a_kernel(q_ref, k_ref, vt_ref, km_ref, vtm_ref, tab_ref, sink_ref, o_ref, *, nblk, sub):
    i = pl.program_id(1)
    lane = lax.broadcasted_iota(jnp.int32, (1, LANES), 1)
    lo = lane < HEAD_DIM
    for j in range(sub):
        gi = i * sub + j
        sp = pl.multiple_of(jnp.maximum(gi - 1, 0) * TQ_A, TQ_A)
        sc = pl.multiple_of(gi * TQ_A, TQ_A)
        sn = pl.multiple_of(jnp.minimum(gi + 1, nblk - 1) * TQ_A, TQ_A)
        variant = jnp.where(gi == 0, 0, jnp.where(gi == nblk - 1, 2, 1))
        rows = slice(j * TQ_A, (j + 1) * TQ_A)
        for g in range(A_KV_HEADS):
            gs = slice(g * LANES, (g + 1) * LANES)
            vr = slice(g * VA_ROWS, (g + 1) * VA_ROWS)
            k_all = jnp.concatenate([k_ref[0, pl.ds(sp, TQ_A), gs], k_ref[0, pl.ds(sc, TQ_A), gs],
                                     k_ref[0, pl.ds(sn, TQ_A), gs], km_ref[:, gs]], axis=0)
            vt_all = jnp.concatenate([vt_ref[vr, pl.ds(sp, TQ_A)], vt_ref[vr, pl.ds(sc, TQ_A)],
                                      vt_ref[vr, pl.ds(sn, TQ_A)], vtm_ref[vr, :]], axis=1)
            for pp in range(2):
                hp = 2 * g + pp
                cols = slice(hp * LANES, (hp + 1) * LANES)
                qc = q_ref[0, rows, cols]
                q2 = jnp.concatenate([jnp.where(lo, qc, jnp.zeros_like(qc)), jnp.where(lo, jnp.zeros_like(qc), qc)],
                                     axis=0)
                s = _dot_nt(k_all, q2) + tab_ref[variant, hp]
                sink = sink_ref[hp]
                m = jnp.maximum(jnp.max(s, axis=0, keepdims=True), sink)
                p = jnp.exp2(s - m)
                acc = _dot(vt_all, p.astype(jnp.bfloat16))
                o = acc[0:HEAD_DIM, :] / (acc[HEAD_DIM:HEAD_DIM + 1, :] + jnp.exp2(sink - m))
                o2 = jnp.concatenate([o[:, :TQ_A], o[:, TQ_A:]], axis=0)
                o_ref[0, rows, cols] = o2.T.astype(o_ref.dtype)


def _mixer_a(proj3, vta, km, vtm, tab, sink):
    bsz, s, _ = proj3.shape
    nblk = s // TQ_A
    assert nblk >= 2
    sub = 4 if nblk % 4 == 0 else 1
    nq = nblk // sub
    tq = sub * TQ_A
    kern = functools.partial(_mixer_a_kernel, nblk=nblk, sub=sub)
    return pl.pallas_call(
        kern,
        out_shape=jax.ShapeDtypeStruct((bsz, s, A_Q_HEADS * HEAD_DIM), jnp.bfloat16),
        grid=(bsz, nq),
        in_specs=[
            pl.BlockSpec((1, tq, 4 * LANES), lambda b, i: (b, i, QA_BLK // 4)),
            pl.BlockSpec((1, s, 2 * LANES), lambda b, i: (b, 0, KA_BLK // 2)),
            pl.BlockSpec((A_KV_HEADS * VA_ROWS, s), lambda b, i: (0, b)),
            pl.BlockSpec(km.shape, lambda b, i: (0, 0)),
            pl.BlockSpec(vtm.shape, lambda b, i: (0, 0)),
            pl.BlockSpec(tab.shape, lambda b, i: (0, 0, 0, 0)),
            pl.BlockSpec(sink.shape, lambda b, i: (0, 0, 0)),
        ],
        out_specs=pl.BlockSpec((1, tq, 4 * LANES), lambda b, i: (b, i, 0)),
        compiler_params=_cparams(("parallel", "arbitrary")),
        name="mixer_a",
    )(proj3, proj3, vta, km, vtm, tab, sink)


def _mixer_b_kernel(sc_ref, q_ref, k_ref, vt_ref, km_ref, vtm_ref, tab_ref, tabm_ref, g_ref, o_ref,
                    m_ref, acc_ref, s0_ref, s1_ref, *, nk):
    h = pl.program_id(1)
    i = pl.program_id(2)
    lane = lax.broadcasted_iota(jnp.int32, (1, LANES), 1)
    lo = lane < HEAD_DIM
    q = q_ref[0]
    qs = (jnp.where(lo, q, jnp.zeros_like(q)), jnp.where(lo, jnp.zeros_like(q), q))
    lam = sc_ref[0]
    sbufs = (s0_ref, s1_ref)

    km = km_ref[...]
    vtm = vtm_ref[...]
    for c in range(2):
        s = _dot_nt(km, qs[c]) + tabm_ref[0]
        m = jnp.max(s, axis=0, keepdims=True)
        p = jnp.exp2(s - m)
        m_ref[c] = m
        acc_ref[c] = _dot(vtm, p.astype(jnp.bfloat16))

    def scores(j, slot):
        start = pl.multiple_of(j * TK_B, TK_B)
        kj = k_ref[0, pl.ds(start, TK_B), :]
        bias = tab_ref[0, jnp.clip(j - i, -2, 2) + 2]
        for c in range(2):
            sbufs[slot][c] = _dot_nt(kj, qs[c]) + bias

    def accumulate(j, slot):
        start = pl.multiple_of(j * TK_B, TK_B)
        vtj = vt_ref[:, pl.ds(start, TK_B)]
        for c in range(2):
            s = sbufs[slot][c]
            m_prev = m_ref[c]
            m_new = jnp.maximum(m_prev, jnp.max(s, axis=0, keepdims=True))
            a = jnp.exp2(m_prev - m_new)
            p = jnp.exp2(s - m_new)
            m_ref[c] = m_new
            acc_ref[c] = a * acc_ref[c] + _dot(vtj, p.astype(jnp.bfloat16))

    scores(0, 0)

    def pair(jj, carry):
        j = 2 * jj
        scores(j + 1, 1)
        accumulate(j, 0)
        scores(j + 2, 0)
        accumulate(j + 1, 1)
        return carry

    lax.fori_loop(0, nk // 2 - 1, pair, 0)
    scores(nk - 1, 1)
    accumulate(nk - 2, 0)
    accumulate(nk - 1, 1)

    o0 = acc_ref[0, 0:LANES, :] / acc_ref[0, LANES:LANES + 1, :]
    o1 = acc_ref[1, 0:LANES, :] / acc_ref[1, LANES:LANES + 1, :]
    o = o0 - lam * o1
    ms = jnp.mean(o * o, axis=0, keepdims=True)
    o = o * lax.rsqrt(ms + LN_EPS) * (g_ref[...] * (1.0 - LAMBDA_INIT))
    o_ref[0] = o.T.astype(o_ref.dtype)


def _mixer_b(proj3, vt, proj_meta, vt_meta, tab, tabm, scal, subln_g):
    bsz, s, _ = proj3.shape
    nq = s // TQ_B
    nk = s // TK_B
    assert nk % 2 == 0
    kern = functools.partial(_mixer_b_kernel, nk=nk)
    return pl.pallas_call(
        kern,
        out_shape=jax.ShapeDtypeStruct((bsz, s, B_HEADS * LANES), jnp.bfloat16),
        grid=(bsz, B_HEADS, nq),
        in_specs=[
            pl.BlockSpec(memory_space=pltpu.SMEM),
            pl.BlockSpec((1, TQ_B, LANES), lambda b, h, i: (b, i, QB_BLK + h)),
            pl.BlockSpec((1, s, LANES), lambda b, h, i: (b, 0, KB_BLK + h)),
            pl.BlockSpec((VT_ROWS, s), lambda b, h, i: (h, b)),
            pl.BlockSpec((N_META, LANES), lambda b, h, i: (0, KB_BLK + h)),
            pl.BlockSpec((VT_ROWS, N_META), lambda b, h, i: (h, 0)),
            pl.BlockSpec((1, 5, TK_B, TQ_B), lambda b, h, i: (h, 0, 0, 0)),
            pl.BlockSpec((1, N_META, TQ_B), lambda b, h, i: (h, 0, i)),
            pl.BlockSpec((LANES, 1), lambda b, h, i: (0, 0)),
        ],
        out_specs=pl.BlockSpec((1, TQ_B, LANES), lambda b, h, i: (b, i, h)),
        scratch_shapes=[
            pltpu.VMEM((2, 1, TQ_B), jnp.float32),
            pltpu.VMEM((2, VT_ROWS, TQ_B), jnp.float32),
            pltpu.VMEM((2, TK_B, TQ_B), jnp.float32),
            pltpu.VMEM((2, TK_B, TQ_B), jnp.float32),
        ],
        compiler_params=_cparams(("parallel", "parallel", "arbitrary")),
        name="mixer_b",
    )(scal, proj3, proj3, vt, proj_meta, vt_meta, tab, tabm, subln_g)


def _slab_load(ref, lead, rows):
    return jnp.concatenate([ref[lead + (pl.ds(s, rows, stride=SLAB), slice(None))] for s in range(SLAB)], axis=1)


def _slab_store(ref, lead, val):
    rows = val.shape[0]
    for s in range(SLAB):
        ref[lead + (pl.ds(s, rows, stride=SLAB), slice(None))] = val[:, s * LANES:(s + 1) * LANES]


def _outproj_kernel(x_ref, oa_ref, ob_ref, w_ref, eg_ref, eb_ref, g_ref, b_ref, o_ref, os_ref):
    h0 = _layernorm_f32(x_ref[...], eg_ref[...], eb_ref[...])
    half = oa_ref.shape[1]
    mix = _dot(oa_ref[...], w_ref[0:half, :]) + _dot(ob_ref[...], w_ref[half:, :])
    h1 = _layernorm_f32(ALPHA * h0 + mix, g_ref[...], b_ref[...])
    o_ref[...] = h1
    _slab_store(os_ref, (), h1)


def _outproj_ln1(x2d, oa, ob, w_out, eg, eb, g, b):
    t, d = x2d.shape
    tm = min(512, t)
    wa = oa.shape[1]
    wb = ob.shape[1]
    vec = pl.BlockSpec((1, d), lambda i: (0, 0))
    return pl.pallas_call(
        _outproj_kernel,
        out_shape=(jax.ShapeDtypeStruct((t, d), jnp.float32),
                   jax.ShapeDtypeStruct((t * SLAB, LANES), jnp.float32)),
        grid=(t // tm,),
        in_specs=[
            pl.BlockSpec((tm, d), lambda i: (i, 0)),
            pl.BlockSpec((tm, wa), lambda i: (i, 0)),
            pl.BlockSpec((tm, wb), lambda i: (i, 0)),
            pl.BlockSpec((wa + wb, d), lambda i: (0, 0)),
            vec, vec, vec, vec,
        ],
        out_specs=(pl.BlockSpec((tm, d), lambda i: (i, 0)),
                   pl.BlockSpec((tm * SLAB, LANES), lambda i: (i, 0))),
        compiler_params=_cparams(("parallel",)),
        name="outproj_ln1",
    )(x2d, oa, ob, w_out, eg, eb, g, b)


def _router_kernel(h_ref, wh_ref, wl_ref, rb_ref, e_ref, w_ref, cnt_ref, carry_ref, *, n_exp):
    i = pl.program_id(0)
    tn = h_ref.shape[0]
    gsz = n_exp // N_GROUPS

    @pl.when(i == 0)
    def _():
        carry_ref[...] = jnp.zeros_like(carry_ref)

    x = h_ref[...]
    xh = x.astype(jnp.bfloat16)
    xl = (x - xh.astype(jnp.float32)).astype(jnp.bfloat16)
    logits = _dot_nt(wh_ref[...], xh) + (_dot_nt(wh_ref[...], xl) + _dot_nt(wl_ref[...], xh))
    scores = 1.0 / (1.0 + jnp.exp(-logits))
    biased = scores + rb_ref[...]

    g3 = biased.reshape(N_GROUPS, gsz, tn)
    it3 = lax.broadcasted_iota(jnp.int32, (N_GROUPS, gsz, tn), 1)
    mx1 = jnp.max(g3, axis=1, keepdims=True)
    first = jnp.min(jnp.where(g3 == mx1, it3, gsz), axis=1, keepdims=True)
    mx2 = jnp.max(jnp.where(it3 == first, -jnp.inf, g3), axis=1, keepdims=True)
    gscore = (mx1 + mx2).reshape(N_GROUPS, tn)

    itg = lax.broadcasted_iota(jnp.int32, (N_GROUPS, tn), 0)
    gsel = jnp.zeros((N_GROUPS, tn), jnp.bool_)
    cur = gscore
    for _ in range(TOPK_GROUPS):
        mx = jnp.max(cur, axis=0, keepdims=True)
        fi = jnp.min(jnp.where(cur == mx, itg, N_GROUPS), axis=0, keepdims=True)
        hit = itg == fi
        gsel = jnp.logical_or(gsel, hit)
        cur = jnp.where(hit, -jnp.inf, cur)
    emask = jnp.broadcast_to(gsel.reshape(N_GROUPS, 1, tn), (N_GROUPS, gsz, tn)).reshape(n_exp, tn)
    cur = jnp.where(emask, biased, NEG)

    ite = lax.broadcasted_iota(jnp.int32, (n_exp, tn), 0)
    hits = []
    eidx = []
    wsel = []
    for _ in range(TOP_K):
        mx = jnp.max(cur, axis=0, keepdims=True)
        fi = jnp.min(jnp.where(cur == mx, ite, n_exp), axis=0, keepdims=True)
        hit = ite == fi
        hits.append(hit)
        eidx.append(fi)
        wsel.append(jnp.sum(jnp.where(hit, scores, 0.0), axis=0, keepdims=True))
        cur = jnp.where(hit, -jnp.inf, cur)
    sel = hits[0]
    for hit in hits[1:]:
        sel = jnp.logical_or(sel, hit)
    self32 = jnp.where(sel, 1.0, 0.0)

    carry_ref[...] = carry_ref[...] + jnp.sum(self32, axis=1, keepdims=True)
    cnt_ref[...] = carry_ref[...]

    wcat = jnp.concatenate(wsel, axis=0)
    wcat = wcat / jnp.sum(wcat, axis=0, keepdims=True) * ROUTED_SCALE
    tok = i * tn + lax.broadcasted_iota(jnp.int32, (TOP_K, tn), 1)
    slot = lax.broadcasted_iota(jnp.int32, (TOP_K, tn), 0)
    e_ref[...] = jnp.concatenate(eidx, axis=0) * (1 << KEY_SHIFT) + (tok * TOP_K + slot)
    w_ref[...] = wcat


def _router(h1, wr_hi, wr_lo, rbias):
    t, d = h1.shape
    n_exp = wr_hi.shape[0]
    tn = min(512, t)
    kern = functools.partial(_router_kernel, n_exp=n_exp)
    row = pl.BlockSpec((TOP_K, tn), lambda i: (0, i))
    return pl.pallas_call(
        kern,
        out_shape=(
            jax.ShapeDtypeStruct((TOP_K, t), jnp.int32),
            jax.ShapeDtypeStruct((TOP_K, t), jnp.float32),
            jax.ShapeDtypeStruct((n_exp, 1), jnp.float32),
        ),
        grid=(t // tn,),
        in_specs=[
            pl.BlockSpec((tn, d), lambda i: (i, 0)),
            pl.BlockSpec((n_exp, d), lambda i: (0, 0)),
            pl.BlockSpec((n_exp, d), lambda i: (0, 0)),
            pl.BlockSpec((n_exp, 1), lambda i: (0, 0)),
        ],
        out_specs=(row, row, pl.BlockSpec((n_exp, 1), lambda i: (0, 0))),
        scratch_shapes=[pltpu.VMEM((n_exp, 1), jnp.float32)],
        compiler_params=_cparams(("arbitrary",)),
        name="router",
    )(h1, wr_hi, wr_lo, rbias)


def _sc_move_rows(x_slab, idx, n_out, scatter):
    m = idx.shape[0]
    rnd = SC_GRP * SC_NBUF
    per = m // SC_WORKERS
    assert m % (SC_WORKERS * rnd) == 0
    row = x_slab.shape[1:]
    mesh = plsc.VectorSubcoreMesh(core_axis_name="c", subcore_axis_name="s")

    @pl.kernel(out_type=jax.ShapeDtypeStruct((n_out,) + row, x_slab.dtype), mesh=mesh,
               scratch_types=[pltpu.VMEM((rnd,), jnp.int32), pltpu.VMEM((SC_NBUF, SC_GRP) + row, x_slab.dtype),
                              pltpu.SemaphoreType.DMA((SC_NBUF,)), pltpu.SemaphoreType.DMA((SC_NBUF,))])
    def kern(x_hbm, i_hbm, o_hbm, ibuf, buf, lsem, ssem):
        worker = lax.axis_index("c") * (SC_WORKERS // 2) + lax.axis_index("s")
        base = worker * per

        @pl.loop(0, per // rnd)
        def _(r):
            off = base + r * rnd
            pltpu.sync_copy(i_hbm.at[pl.ds(off, rnd)], ibuf)
            loads = []
            stores = []
            for b in range(SC_NBUF):
                indexed = ibuf.at[pl.ds(b * SC_GRP, SC_GRP)]
                linear = pl.ds(off + b * SC_GRP, SC_GRP)
                src = x_hbm.at[linear] if scatter else x_hbm.at[indexed]
                dst = o_hbm.at[indexed] if scatter else o_hbm.at[linear]
                loads.append(pltpu.make_async_copy(src, buf.at[b], lsem.at[b]))
                stores.append(pltpu.make_async_copy(buf.at[b], dst, ssem.at[b]))
            for ld in loads:
                ld.start()
            for b in range(SC_NBUF):
                loads[b].wait()
                stores[b].start()
            for st in stores:
                st.wait()

    return kern(x_slab, idx)


X_BUFS = 3
Y_BUFS = 2


def _experts_kernel(te_ref, tv_ref, tu_ref, na_ref, xs_hbm, wg_ref, wu_ref, wd_ref, ys_hbm,
                    xb0, xb1, xb2, yb0, yb1, wgu_s, wd_s, xsem, ysem):
    i = pl.program_id(0)
    nt = pl.num_programs(0)
    n_act = na_ref[0]
    xbufs = (xb0, xb1, xb2)
    ybufs = (yb0, yb1)
    tile_rows = TM_MOE * SLAB

    def x_copy(slot, tile):
        tile = jnp.minimum(tile, nt - 1)
        start = pl.multiple_of(tu_ref[tile] * SLAB, SLAB)
        return pltpu.make_async_copy(xs_hbm.at[pl.ds(start, tile_rows), :], xbufs[slot], xsem.at[slot])

    def y_pieces(slot, tile, go):
        nvalid = tv_ref[tile]
        base = tu_ref[tile]
        size = TM_MOE
        while size >= 1:
            @pl.when((nvalid & size) != 0)
            def _(size=size):
                off = nvalid & ~(2 * size - 1)
                src = ybufs[slot].at[pl.ds(pl.multiple_of(off * SLAB, SLAB), size * SLAB), :]
                dst = ys_hbm.at[pl.ds(pl.multiple_of((base + off) * SLAB, SLAB), size * SLAB), :]
                go(pltpu.make_async_copy(src, dst, ysem.at[slot]))
            size //= 2

    @pl.when(i == 0)
    def _():
        x_copy(0, 0).start()
        x_copy(1, 1).start()

    iprev = jnp.maximum(i - 1, 0)

    @pl.when(jnp.logical_and(i < n_act, jnp.logical_or(i == 0, te_ref[i] != te_ref[iprev])))
    def _():
        de = wg_ref.shape[2]
        wgu_s[:, 0:de] = wg_ref[0].astype(wgu_s.dtype)
        wgu_s[:, de:] = wu_ref[0].astype(wgu_s.dtype)
        wd_s[...] = wd_ref[0].astype(wd_s.dtype)

    phase = i % (X_BUFS * Y_BUFS)
    for c in range(X_BUFS * Y_BUFS):
        xs, ys = c % X_BUFS, c % Y_BUFS

        @pl.when(jnp.logical_and(i < n_act, phase == c))
        def _(xs=xs, ys=ys):
            x_copy(xs, i).wait()
            x_copy((xs + 2) % X_BUFS, i + 2).start()
            x = _slab_load(xbufs[xs], (), TM_MOE).astype(jnp.bfloat16)
            gu = _dot(x, wgu_s[...])
            de = gu.shape[1] // 2
            gate = gu[:, :de]
            hid = (gate / (1.0 + jnp.exp(-gate))) * gu[:, de:]
            y = _dot(hid.astype(jnp.bfloat16), wd_s[...])

            @pl.when(i >= Y_BUFS)
            def _():
                y_pieces(ys, i - Y_BUFS, lambda cp: cp.wait())

            _slab_store(ybufs[ys], (), y)
            y_pieces(ys, i, lambda cp: cp.start())

        @pl.when(jnp.logical_and(i == n_act, phase == c))
        def _(xs=xs, ys=ys):
            x_copy(xs, i).wait()
            x_copy((xs + 1) % X_BUFS, i + 1).wait()

            @pl.when(i >= 2)
            def _():
                y_pieces(ys, i - 2, lambda cp: cp.wait())

            @pl.when(i >= 1)
            def _():
                y_pieces(1 - ys, i - 1, lambda cp: cp.wait())


def _experts(xs_slab, wg, wu, wd, tile_e, tile_valid, tile_u, n_active):
    nt = tile_e.shape[0]
    d = wg.shape[1]
    de = wg.shape[2]
    buf = pltpu.VMEM((TM_MOE * SLAB, LANES), jnp.float32)
    return pl.pallas_call(
        _experts_kernel,
        out_shape=jax.ShapeDtypeStruct(xs_slab.shape, jnp.float32),
        grid_spec=pltpu.PrefetchScalarGridSpec(
            num_scalar_prefetch=4,
            grid=(nt,),
            in_specs=[
                pl.BlockSpec(memory_space=pl.ANY),
                pl.BlockSpec((1, d, de), lambda i, te, tv, tu, na: (te[i], 0, 0)),
                pl.BlockSpec((1, d, de), lambda i, te, tv, tu, na: (te[i], 0, 0)),
                pl.BlockSpec((1, de, d), lambda i, te, tv, tu, na: (te[i], 0, 0)),
            ],
            out_specs=pl.BlockSpec(memory_space=pl.ANY),
            scratch_shapes=[buf] * (X_BUFS + Y_BUFS) + [
                pltpu.VMEM((d, 2 * de), jnp.bfloat16), pltpu.VMEM((de, d), jnp.bfloat16),
                pltpu.SemaphoreType.DMA((X_BUFS,)), pltpu.SemaphoreType.DMA((Y_BUFS,))],
        ),
        compiler_params=_cparams(("arbitrary",)),
        name="experts",
    )(tile_e, tile_valid, tile_u, n_active, xs_slab, wg, wu, wd)


def _combine_kernel(h_ref, w_ref, *rest):
    y_refs = rest[:TOP_K]
    sgu_ref, sd_ref, g_ref, b_ref, o_ref = rest[TOP_K:]
    tn = h_ref.shape[0]
    h = h_ref[...]
    gu = _dot(h.astype(jnp.bfloat16), sgu_ref[...])
    ds = gu.shape[1] // 2
    gate = gu[:, :ds]
    hid = (gate / (1.0 + jnp.exp(-gate))) * gu[:, ds:]
    shared = _dot(hid.astype(jnp.bfloat16), sd_ref[...])

    w = w_ref[...]
    routed = _slab_load(y_refs[0], (), tn) * w[:, 0:1]
    for k in range(1, TOP_K):
        routed = routed + _slab_load(y_refs[k], (), tn) * w[:, k:k + 1]
    o_ref[...] = _layernorm_f32(ALPHA * h + (routed + shared), g_ref[...], b_ref[...])


def _combine(h1, w_tok, y_slab, sgu, sd, g, b):
    t, d = h1.shape
    tn = min(256, t)
    nblk = t // tn
    ds2 = sgu.shape[1]
    vec = pl.BlockSpec((1, d), lambda i: (0, 0))

    def slot_spec(k):
        return pl.BlockSpec((tn * SLAB, LANES), lambda i: (k * nblk + i, 0))

    return pl.pallas_call(
        _combine_kernel,
        out_shape=jax.ShapeDtypeStruct((t, d), jnp.float32),
        grid=(nblk,),
        in_specs=[
            pl.BlockSpec((tn, d), lambda i: (i, 0)),
            pl.BlockSpec((tn, TOP_K), lambda i: (i, 0)),
            *[slot_spec(k) for k in range(TOP_K)],
            pl.BlockSpec((d, ds2), lambda i: (0, 0)),
            pl.BlockSpec((ds2 // 2, d), lambda i: (0, 0)),
            vec, vec,
        ],
        out_specs=pl.BlockSpec((tn, d), lambda i: (i, 0)),
        compiler_params=_cparams(("parallel",)),
        name="combine_ln2",
    )(h1, w_tok, *([y_slab] * TOP_K), sgu, sd, g, b)


def _rel_bucket(rel):
    nb = N_BUCKETS // 2
    max_exact = nb // 2
    ret = jnp.where(rel > 0, nb, 0)
    n = jnp.abs(rel)
    nf = jnp.maximum(n, 1).astype(jnp.float32)
    large = max_exact + (jnp.log(nf / max_exact) / math.log(MAX_DISTANCE / max_exact) * (nb - max_exact)).astype(jnp.int32)
    large = jnp.minimum(large, nb - 1)
    return ret + jnp.where(n < max_exact, n, large)


def _bias_of_rel(rel_bias, rel):
    return rel_bias.astype(jnp.float32)[_rel_bucket(rel)]


def _toeplitz(vec, nrow, ncol, off):
    lo = off - (nrow - 1)
    v = vec[lo:off + ncol]
    p = v.shape[0] + 1
    v = jnp.concatenate([v, v[:1]], axis=0)
    flat = jnp.tile(v, (nrow + 1, 1))
    base = off - lo
    out = flat[base:base + nrow * (p - 1)].reshape(nrow, p - 1, vec.shape[1])
    return out[:, :ncol]


def _bias_tables_a(rel_bias, sink):
    bias_a = rel_bias[:, :A_Q_HEADS]
    m = jnp.arange(3 * TQ_A + TQ_A - 1, dtype=jnp.int32)
    rel = 2 * TQ_A - 1 - m
    vec = jnp.where((jnp.abs(rel) <= WINDOW)[:, None], _bias_of_rel(bias_a, rel), NEG)
    band = _toeplitz(vec, 3 * TQ_A, TQ_A, 3 * TQ_A - 1)
    mm = jnp.arange(TQ_A + N_META - 1, dtype=jnp.int32)
    meta_first = _toeplitz(_bias_of_rel(bias_a, -1 - mm), N_META, TQ_A, N_META - 1)
    meta_far = jnp.broadcast_to(_bias_of_rel(bias_a, jnp.int32(-2 * MAX_DISTANCE)), (N_META, TQ_A, A_Q_HEADS))
    pad = jnp.full((NK_A - 3 * TQ_A - N_META, TQ_A, A_Q_HEADS), NEG, jnp.float32)
    blocked = jnp.full((TQ_A, TQ_A, A_Q_HEADS), NEG, jnp.float32)
    first = jnp.concatenate([blocked, band[TQ_A:], meta_first, pad])
    middle = jnp.concatenate([band, meta_far, pad])
    last = jnp.concatenate([band[:2 * TQ_A], blocked, meta_far, pad])
    tab = jnp.stack([first, middle, last])
    tab = tab.reshape(3, NK_A, TQ_A, A_Q_HEADS // 2, 2)
    tab = jnp.transpose(tab, (0, 3, 1, 4, 2)).reshape(3, A_Q_HEADS // 2, NK_A, 2 * TQ_A) * LOG2E
    sink_rows = jnp.repeat(sink.astype(jnp.float32).reshape(A_Q_HEADS // 2, 1, 2), TQ_A, axis=2) * LOG2E
    return tab, sink_rows


def _bias_tables_b(rel_bias, s):
    bias_b = rel_bias[:, A_Q_HEADS:]
    near = []
    for d in (-1, 0, 1):
        m = jnp.arange(TQ_B + TK_B - 1, dtype=jnp.int32)
        vec = _bias_of_rel(bias_b, TK_B * d + TK_B - 1 - m)
        near.append(_toeplitz(vec, TK_B, TQ_B, TK_B - 1))
    far_l = jnp.broadcast_to(_bias_of_rel(bias_b, jnp.int32(-TK_B - 1)), (TK_B, TQ_B, B_HEADS))
    far_r = jnp.broadcast_to(_bias_of_rel(bias_b, jnp.int32(TK_B + 1)), (TK_B, TQ_B, B_HEADS))
    tabs = jnp.transpose(jnp.stack([far_l] + near + [far_r]), (3, 0, 1, 2)) * LOG2E
    m = jnp.arange(s + N_META - 1, dtype=jnp.int32)
    vec = _bias_of_rel(bias_b, -1 - m)
    meta = jnp.transpose(_toeplitz(vec, N_META, s, N_META - 1), (2, 0, 1)) * LOG2E
    return tabs, meta


def _prep_w_in(w_in):
    a_w = A_Q_HEADS * HEAD_DIM
    kv = A_KV_HEADS * HEAD_DIM
    bqk = B_HEADS * 2 * HEAD_DIM
    scale = HEAD_DIM ** -0.5
    qa = w_in[:, :a_w] * (scale * LOG2E)
    ka = w_in[:, a_w:a_w + kv]
    va = w_in[:, a_w + kv:a_w + 2 * kv]
    o = a_w + 2 * kv
    qb = w_in[:, o:o + bqk] * (scale * LOG2E)
    kb = w_in[:, o + bqk:o + 2 * bqk]
    vb = w_in[:, o + 2 * bqk:]

    def dup(w):
        return jnp.concatenate([w[:, g * HEAD_DIM:(g + 1) * HEAD_DIM] for g in range(A_KV_HEADS) for _ in range(2)], axis=1)

    w = jnp.concatenate([qa, dup(ka), qb, kb], axis=1).astype(jnp.bfloat16)
    return w, jnp.concatenate([vb, va], axis=1).T.astype(jnp.bfloat16)


def _trunk_front(x, prm):
    bsz, s, d = x.shape
    t = bsz * s
    x2d = x.reshape(t, d)
    proj, vt, vta = _ln_inproj(x2d, prm["ln_emb_g"], prm["ln_emb_b"], prm["w_in"], prm["w_vt"])
    proj3 = proj.reshape(bsz, s, PROJ_COLS)
    oa = _mixer_a(proj3, vta, prm["km_a"], prm["vtm_a"], prm["tab_a"], prm["sink_rows"])
    tabs_b, meta_b = _bias_tables_b(prm["rel_bias"], s)
    ob = _mixer_b(proj3, vt, prm["proj_meta"], prm["vt_meta"], tabs_b, meta_b, prm["lam"].reshape(1).astype(jnp.float32),
                  prm["subln_g"])
    h1, h1_slab = _outproj_ln1(x2d, oa.reshape(t, -1), ob.reshape(t, -1), prm["w_out"], prm["ln_emb_g"],
                               prm["ln_emb_b"], prm["ln1_g"], prm["ln1_b"])

    keys, wts, counts = _router(h1, prm["wr_hi"], prm["wr_lo"], prm["router_bias"])
    n_exp = counts.shape[0]
    n_asg = t * TOP_K
    assert n_asg <= (1 << KEY_SHIFT) and n_asg % LANES == 0
    order = jnp.sort(keys.reshape(n_asg)) & ((1 << KEY_SHIFT) - 1)
    tok = order >> 3
    dst = (order & (TOP_K - 1)) * t + tok
    sc_pad = SC_WORKERS * SC_GRP * SC_NBUF
    assert sc_pad >= TM_MOE and n_asg % sc_pad == 0
    xs = _sc_move_rows(h1_slab.reshape(t, SLAB, LANES), jnp.concatenate([tok, jnp.zeros((sc_pad,), jnp.int32)]),
                       n_asg + sc_pad, scatter=False)
    xs_slab = xs.reshape((n_asg + sc_pad) * SLAB, LANES)
    counts = counts[:, 0].astype(jnp.int32)
    tiles_e = (counts + TM_MOE - 1) // TM_MOE
    tend = jnp.cumsum(tiles_e)
    tstart = tend - tiles_e
    ustart = jnp.cumsum(counts) - counts
    nt = n_asg // TM_MOE + n_exp + 1
    tid = jnp.arange(nt, dtype=jnp.int32)
    tile_e = jnp.minimum(jnp.sum((tend[None, :] <= tid[:, None]).astype(jnp.int32), axis=1), n_exp - 1)
    onehot = (tile_e[:, None] == jnp.arange(n_exp, dtype=jnp.int32)[None, :]).astype(jnp.int32)
    in_e = (tid - jnp.sum(onehot * tstart[None, :], axis=1)) * TM_MOE
    active = tid < tend[-1]
    tile_valid = jnp.where(active, jnp.clip(jnp.sum(onehot * counts[None, :], axis=1) - in_e, 0, TM_MOE), 0)
    tile_u = jnp.where(active, jnp.sum(onehot * ustart[None, :], axis=1) + in_e, 0)

    tiles = (tile_e.astype(jnp.int32), tile_valid.astype(jnp.int32), tile_u.astype(jnp.int32),
             tend[-1:].astype(jnp.int32))
    return h1, wts.T, xs_slab, dst, tiles, x.shape


def _trunk_back(front, prm):
    h1, w_tok, xs_slab, dst, tiles, shape = front
    n_asg = dst.shape[0]
    ys_slab = _experts(xs_slab, prm["w_gate"], prm["w_up"], prm["w_down"], *tiles)
    y = _sc_move_rows(ys_slab.reshape(-1, SLAB, LANES), dst, n_asg, scatter=True)
    out = _combine(h1, w_tok, y.reshape(n_asg * SLAB, LANES), prm["ws_gu"], prm["ws_down"], prm["ln2_g"], prm["ln2_b"])
    return out.reshape(shape)


def kernel(x_prompt, x_sample, meta_tokens, ln_emb_g, ln_emb_b, rel_bias, w_in, attn_sink, lambda_q1, lambda_k1, lambda_q2, lambda_k2, subln_g, w_out, ln1_g, ln1_b, w_router, router_bias, w_gate, w_up, w_down, ws_gate, ws_up, ws_down, ln2_g, ln2_b):
    f32 = jnp.float32
    bf16 = jnp.bfloat16
    l = 0
    row = lambda v: v.reshape(1, -1).astype(f32)
    lam = (jnp.exp(jnp.sum(lambda_q1[l].astype(f32) * lambda_k1[l].astype(f32)))
           - jnp.exp(jnp.sum(lambda_q2[l].astype(f32) * lambda_k2[l].astype(f32))) + LAMBDA_INIT)
    wr_t = w_router[l].astype(f32).T
    wr_hi = wr_t.astype(bf16)
    prm = {
        "ln_emb_g": row(ln_emb_g), "ln_emb_b": row(ln_emb_b),
        "rel_bias": rel_bias,
        "lam": lam,
        "subln_g": subln_g[l].astype(f32).reshape(-1, 1),
        "w_out": w_out[l].astype(bf16),
        "ln1_g": row(ln1_g[l]), "ln1_b": row(ln1_b[l]),
        "wr_hi": wr_hi, "wr_lo": (wr_t - wr_hi.astype(f32)).astype(bf16),
        "router_bias": router_bias[l].astype(f32).reshape(-1, 1),
        "w_gate": w_gate[l], "w_up": w_up[l], "w_down": w_down[l],
        "ws_gu": jnp.concatenate([ws_gate[l], ws_up[l]], axis=-1).astype(bf16),
        "ws_down": ws_down[l].astype(bf16),
        "ln2_g": row(ln2_g[l]), "ln2_b": row(ln2_b[l]),
    }
    prm["w_in"], prm["w_vt"] = _prep_w_in(w_in[l])
    prm["proj_meta"], prm["vt_meta"], vta_meta = _ln_inproj(meta_tokens.astype(f32), prm["ln_emb_g"], prm["ln_emb_b"],
                                                           prm["w_in"], prm["w_vt"])
    meta_pad = NK_A - 3 * TQ_A - N_META
    prm["km_a"] = jnp.pad(prm["proj_meta"][:, KA_BLK * LANES:(KA_BLK + 2) * LANES], ((0, meta_pad), (0, 0)))
    prm["vtm_a"] = jnp.pad(vta_meta, ((0, 0), (0, meta_pad)))
    prm["tab_a"], prm["sink_rows"] = _bias_tables_a(rel_bias, attn_sink[l])
    front_p = _trunk_front(x_prompt, prm)
    front_s = _trunk_front(x_sample, prm)
    return (_trunk_back(front_p, prm), _trunk_back(front_s, prm))
```

```python
import functools
import math

import jax
import jax.numpy as jnp
from jax import lax
from jax.experimental import pallas as pl
from jax.experimental.pallas import tpu as pltpu
from jax.experimental.pallas import tpu_sc as plsc

N_META = 16
HEAD_DIM = 64
WINDOW = 128
A_Q_HEADS = 8
A_KV_HEADS = 2
B_HEADS = 4
N_BUCKETS = 32
MAX_DISTANCE = 128
TOP_K = 8
N_GROUPS = 8
TOPK_GROUPS = 4
ROUTED_SCALE = 2.5
LN_EPS = 1e-5
DEPTH = 1
ALPHA = (2 * DEPTH) ** 0.25
NEG = -1e30
LAMBDA_INIT = 0.8 - 0.6 * math.exp(-0.3 * 0)

LANES = 128
VMEM_LIMIT = 48 * 1024 * 1024

QA_BLK = 0
KA_BLK = 4
QB_BLK = 6
KB_BLK = 10
PROJ_COLS = 14 * LANES
VT_ROWS = LANES + 16
VA_ROWS = HEAD_DIM + 16
LOG2E = 1.4426950408889634

TQ_A = 128
NK_A = 4 * TQ_A
TQ_B = 512
TK_B = 512
TM_MOE = 256
SLAB = 8
SC_GRP = 16
SC_NBUF = 4
SC_WORKERS = 32
KEY_SHIFT = 20


def _cparams(sem):
    return pltpu.CompilerParams(dimension_semantics=sem, vmem_limit_bytes=VMEM_LIMIT)


def _layernorm_f32(x, g, b):
    mu = jnp.mean(x, axis=-1, keepdims=True)
    xc = x - mu
    var = jnp.mean(xc * xc, axis=-1, keepdims=True)
    return xc * lax.rsqrt(var + LN_EPS) * g + b


def _dot_nt(a, b):
    return lax.dot_general(a, b, (((1,), (1,)), ((), ())), preferred_element_type=jnp.float32)


def _dot(a, b):
    return jnp.dot(a, b, preferred_element_type=jnp.float32)


def _ln_inproj_kernel(x_ref, g_ref, b_ref, w_ref, wvt_ref, o_ref, vt_ref, vta_ref):
    h = _layernorm_f32(x_ref[...], g_ref[...], b_ref[...]).astype(jnp.bfloat16)
    o_ref[...] = _dot(h, w_ref[...]).astype(o_ref.dtype)
    vt = _dot_nt(wvt_ref[...], h).astype(vt_ref.dtype)
    ones = jnp.ones((16, vt.shape[1]), vt_ref.dtype)
    for hd in range(B_HEADS):
        vt_ref[hd * VT_ROWS:hd * VT_ROWS + LANES, :] = vt[hd * LANES:(hd + 1) * LANES, :]
        vt_ref[hd * VT_ROWS + LANES:(hd + 1) * VT_ROWS, :] = ones
    base = B_HEADS * LANES
    for g in range(A_KV_HEADS):
        vta_ref[g * VA_ROWS:g * VA_ROWS + HEAD_DIM, :] = vt[base + g * HEAD_DIM:base + (g + 1) * HEAD_DIM, :]
        vta_ref[g * VA_ROWS + HEAD_DIM:(g + 1) * VA_ROWS, :] = ones


def _ln_inproj(x2d, g, b, w, wvt):
    t, d = x2d.shape
    n = w.shape[1]
    tm = min(512, t)
    return pl.pallas_call(
        _ln_inproj_kernel,
        out_shape=(jax.ShapeDtypeStruct((t, n), jnp.bfloat16),
                   jax.ShapeDtypeStruct((B_HEADS * VT_ROWS, t), jnp.bfloat16),
                   jax.ShapeDtypeStruct((A_KV_HEADS * VA_ROWS, t), jnp.bfloat16)),
        grid=(t // tm,),
        in_specs=[
            pl.BlockSpec((tm, d), lambda i: (i, 0)),
            pl.BlockSpec((1, d), lambda i: (0, 0)),
            pl.BlockSpec((1, d), lambda i: (0, 0)),
            pl.BlockSpec((d, n), lambda i: (0, 0)),
            pl.BlockSpec((wvt.shape[0], d), lambda i: (0, 0)),
        ],
        out_specs=(pl.BlockSpec((tm, n), lambda i: (i, 0)),
                   pl.BlockSpec((B_HEADS * VT_ROWS, tm), lambda i: (0, i)),
                   pl.BlockSpec((A_KV_HEADS * VA_ROWS, tm), lambda i: (0, i))),
        compiler_params=_cparams(("parallel",)),
        name="ln_inproj",
    )(x2d, g, b, w, wvt)


def _mixer_a_kernel(q_ref, k_ref, vt_ref, km_ref, vtm_ref, tab_ref, sink_ref, o_ref, *, nblk, sub):
    i = pl.program_id(1)
    lane = lax.broadcasted_iota(jnp.int32, (1, LANES), 1)
    lo = lane < HEAD_DIM
    for j in range(sub):
        gi = i * sub + j
        sp = pl.multiple_of(jnp.maximum(gi - 1, 0) * TQ_A, TQ_A)
        sc = pl.multiple_of(gi * TQ_A, TQ_A)
        sn = pl.multiple_of(jnp.minimum(gi + 1, nblk - 1) * TQ_A, TQ_A)
        variant = jnp.where(gi == 0, 0, jnp.where(gi == nblk - 1, 2, 1))
        rows = slice(j * TQ_A, (j + 1) * TQ_A)
        for g in range(A_KV_HEADS):
            gs = slice(g * LANES, (g + 1) * LANES)
            vr = slice(g * VA_ROWS, (g + 1) * VA_ROWS)
            k_all = jnp.concatenate([k_ref[0, pl.ds(sp, TQ_A), gs], k_ref[0, pl.ds(sc, TQ_A), gs],
                                     k_ref[0, pl.ds(sn, TQ_A), gs], km_ref[:, gs]], axis=0)
            vt_all = jnp.concatenate([vt_ref[vr, pl.ds(sp, TQ_A)], vt_ref[vr, pl.ds(sc, TQ_A)],
                                      vt_ref[vr, pl.ds(sn, TQ_A)], vtm_ref[vr, :]], axis=1)
            for pp in range(2):
                hp = 2 * g + pp
                cols = slice(hp * LANES, (hp + 1) * LANES)
                qc = q_ref[0, rows, cols]
                q2 = jnp.concatenate([jnp.where(lo, qc, jnp.zeros_like(qc)), jnp.where(lo, jnp.zeros_like(qc), qc)],
                                     axis=0)
                s = _dot_nt(k_all, q2) + tab_ref[variant, hp]
                sink = sink_ref[hp]
                m = jnp.maximum(jnp.max(s, axis=0, keepdims=True), sink)
                p = jnp.exp2(s - m)
                acc = _dot(vt_all, p.astype(jnp.bfloat16))
                o = acc[0:HEAD_DIM, :] / (acc[HEAD_DIM:HEAD_DIM + 1, :] + jnp.exp2(sink - m))
                o2 = jnp.concatenate([o[:, :TQ_A], o[:, TQ_A:]], axis=0)
                o_ref[0, rows, cols] = o2.T.astype(o_ref.dtype)


def _mixer_a(proj3, vta, km, vtm, tab, sink):
    bsz, s, _ = proj3.shape
    nblk = s // TQ_A
    assert nblk >= 2
    sub = 4 if nblk % 4 == 0 else 1
    nq = nblk // sub
    tq = sub * TQ_A
    kern = functools.partial(_mixer_a_kernel, nblk=nblk, sub=sub)
    return pl.pallas_call(
        kern,
        out_shape=jax.ShapeDtypeStruct((bsz, s, A_Q_HEADS * HEAD_DIM), jnp.bfloat16),
        grid=(bsz, nq),
        in_specs=[
            pl.BlockSpec((1, tq, 4 * LANES), lambda b, i: (b, i, QA_BLK // 4)),
            pl.BlockSpec((1, s, 2 * LANES), lambda b, i: (b, 0, KA_BLK // 2)),
            pl.BlockSpec((A_KV_HEADS * VA_ROWS, s), lambda b, i: (0, b)),
            pl.BlockSpec(km.shape, lambda b, i: (0, 0)),
            pl.BlockSpec(vtm.shape, lambda b, i: (0, 0)),
            pl.BlockSpec(tab.shape, lambda b, i: (0, 0, 0, 0)),
            pl.BlockSpec(sink.shape, lambda b, i: (0, 0, 0)),
        ],
        out_specs=pl.BlockSpec((1, tq, 4 * LANES), lambda b, i: (b, i, 0)),
        compiler_params=_cparams(("parallel", "arbitrary")),
        name="mixer_a",
    )(proj3, proj3, vta, km, vtm, tab, sink)


def _mixer_b_kernel(sc_ref, q_ref, k_ref, vt_ref, km_ref, vtm_ref, tab_ref, tabm_ref, g_ref, o_ref,
                    m_ref, acc_ref, s0_ref, s1_ref, *, nk):
    h = pl.program_id(1)
    i = pl.program_id(2)
    lane = lax.broadcasted_iota(jnp.int32, (1, LANES), 1)
    lo = lane < HEAD_DIM
    q = q_ref[0]
    qs = (jnp.where(lo, q, jnp.zeros_like(q)), jnp.where(lo, jnp.zeros_like(q), q))
    lam = sc_ref[0]
    sbufs = (s0_ref, s1_ref)

    km = km_ref[...]
    vtm = vtm_ref[...]
    for c in range(2):
        s = _dot_nt(km, qs[c]) + tabm_ref[0]
        m = jnp.max(s, axis=0, keepdims=True)
        p = jnp.exp2(s - m)
        m_ref[c] = m
        acc_ref[c] = _dot(vtm, p.astype(jnp.bfloat16))

    def scores(j, slot):
        start = pl.multiple_of(j * TK_B, TK_B)
        kj = k_ref[0, pl.ds(start, TK_B), :]
        bias = tab_ref[0, jnp.clip(j - i, -2, 2) + 2]
        for c in range(2):
            sbufs[slot][c] = _dot_nt(kj, qs[c]) + bias

    def accumulate(j, slot):
        start = pl.multiple_of(j * TK_B, TK_B)
        vtj = vt_ref[:, pl.ds(start, TK_B)]
        for c in range(2):
            s = sbufs[slot][c]
            m_prev = m_ref[c]
            m_new = jnp.maximum(m_prev, jnp.max(s, axis=0, keepdims=True))
            a = jnp.exp2(m_prev - m_new)
            p = jnp.exp2(s - m_new)
            m_ref[c] = m_new
            acc_ref[c] = a * acc_ref[c] + _dot(vtj, p.astype(jnp.bfloat16))

    scores(0, 0)

    def pair(jj, carry):
        j = 2 * jj
        scores(j + 1, 1)
        accumulate(j, 0)
        scores(j + 2, 0)
        accumulate(j + 1, 1)
        return carry

    lax.fori_loop(0, nk // 2 - 1, pair, 0)
    scores(nk - 1, 1)
    accumulate(nk - 2, 0)
    accumulate(nk - 1, 1)

    o0 = acc_ref[0, 0:LANES, :] / acc_ref[0, LANES:LANES + 1, :]
    o1 = acc_ref[1, 0:LANES, :] / acc_ref[1, LANES:LANES + 1, :]
    o = o0 - lam * o1
    ms = jnp.mean(o * o, axis=0, keepdims=True)
    o = o * lax.rsqrt(ms + LN_EPS) * (g_ref[...] * (1.0 - LAMBDA_INIT))
    o_ref[0] = o.T.astype(o_ref.dtype)


def _mixer_b(proj3, vt, proj_meta, vt_meta, tab, tabm, scal, subln_g):
    bsz, s, _ = proj3.shape
    nq = s // TQ_B
    nk = s // TK_B
    assert nk % 2 == 0
    kern = functools.partial(_mixer_b_kernel, nk=nk)
    return pl.pallas_call(
        kern,
        out_shape=jax.ShapeDtypeStruct((bsz, s, B_HEADS * LANES), jnp.bfloat16),
        grid=(bsz, B_HEADS, nq),
        in_specs=[
            pl.BlockSpec(memory_space=pltpu.SMEM),
            pl.BlockSpec((1, TQ_B, LANES), lambda b, h, i: (b, i, QB_BLK + h)),
            pl.BlockSpec((1, s, LANES), lambda b, h, i: (b, 0, KB_BLK + h)),
            pl.BlockSpec((VT_ROWS, s), lambda b, h, i: (h, b)),
            pl.BlockSpec((N_META, LANES), lambda b, h, i: (0, KB_BLK + h)),
            pl.BlockSpec((VT_ROWS, N_META), lambda b, h, i: (h, 0)),
            pl.BlockSpec((1, 5, TK_B, TQ_B), lambda b, h, i: (h, 0, 0, 0)),
            pl.BlockSpec((1, N_META, TQ_B), lambda b, h, i: (h, 0, i)),
            pl.BlockSpec((LANES, 1), lambda b, h, i: (0, 0)),
        ],
        out_specs=pl.BlockSpec((1, TQ_B, LANES), lambda b, h, i: (b, i, h)),
        scratch_shapes=[
            pltpu.VMEM((2, 1, TQ_B), jnp.float32),
            pltpu.VMEM((2, VT_ROWS, TQ_B), jnp.float32),
            pltpu.VMEM((2, TK_B, TQ_B), jnp.float32),
            pltpu.VMEM((2, TK_B, TQ_B), jnp.float32),
        ],
        compiler_params=_cparams(("parallel", "parallel", "arbitrary")),
        name="mixer_b",
    )(scal, proj3, proj3, vt, proj_meta, vt_meta, tab, tabm, subln_g)


def _slab_load(ref, lead, rows):
    return jnp.concatenate([ref[lead + (pl.ds(s, rows, stride=SLAB), slice(None))] for s in range(SLAB)], axis=1)


def _slab_store(ref, lead, val):
    rows = val.shape[0]
    for s in range(SLAB):
        ref[lead + (pl.ds(s, rows, stride=SLAB), slice(None))] = val[:, s * LANES:(s + 1) * LANES]


def _outproj_kernel(x_ref, oa_ref, ob_ref, w_ref, eg_ref, eb_ref, g_ref, b_ref, o_ref, os_ref):
    h0 = _layernorm_f32(x_ref[...], eg_ref[...], eb_ref[...])
    half = oa_ref.shape[1]
    mix = _dot(oa_ref[...], w_ref[0:half, :]) + _dot(ob_ref[...], w_ref[half:, :])
    h1 = _layernorm_f32(ALPHA * h0 + mix, g_ref[...], b_ref[...])
    o_ref[...] = h1
    _slab_store(os_ref, (), h1)


def _outproj_ln1(x2d, oa, ob, w_out, eg, eb, g, b):
    t, d = x2d.shape
    tm = min(512, t)
    wa = oa.shape[1]
    wb = ob.shape[1]
    vec = pl.BlockSpec((1, d), lambda i: (0, 0))
    return pl.pallas_call(
        _outproj_kernel,
        out_shape=(jax.ShapeDtypeStruct((t, d), jnp.float32),
                   jax.ShapeDtypeStruct((t * SLAB, LANES), jnp.float32)),
        grid=(t // tm,),
        in_specs=[
            pl.BlockSpec((tm, d), lambda i: (i, 0)),
            pl.BlockSpec((tm, wa), lambda i: (i, 0)),
            pl.BlockSpec((tm, wb), lambda i: (i, 0)),
            pl.BlockSpec((wa + wb, d), lambda i: (0, 0)),
            vec, vec, vec, vec,
        ],
        out_specs=(pl.BlockSpec((tm, d), lambda i: (i, 0)),
                   pl.BlockSpec((tm * SLAB, LANES), lambda i: (i, 0))),
        compiler_params=_cparams(("parallel",)),
        name="outproj_ln1",
    )(x2d, oa, ob, w_out, eg, eb, g, b)


def _router_kernel(h_ref, wh_ref, wl_ref, rb_ref, e_ref, w_ref, cnt_ref, carry_ref, *, n_exp):
    i = pl.program_id(0)
    tn = h_ref.shape[0]
    gsz = n_exp // N_GROUPS

    @pl.when(i == 0)
    def _():
        carry_ref[...] = jnp.zeros_like(carry_ref)

    x = h_ref[...]
    xh = x.astype(jnp.bfloat16)
    xl = (x - xh.astype(jnp.float32)).astype(jnp.bfloat16)
    logits = _dot_nt(wh_ref[...], xh) + (_dot_nt(wh_ref[...], xl) + _dot_nt(wl_ref[...], xh))
    scores = 1.0 / (1.0 + jnp.exp(-logits))
    biased = scores + rb_ref[...]

    g3 = biased.reshape(N_GROUPS, gsz, tn)
    it3 = lax.broadcasted_iota(jnp.int32, (N_GROUPS, gsz, tn), 1)
    mx1 = jnp.max(g3, axis=1, keepdims=True)
    first = jnp.min(jnp.where(g3 == mx1, it3, gsz), axis=1, keepdims=True)
    mx2 = jnp.max(jnp.where(it3 == first, -jnp.inf, g3), axis=1, keepdims=True)
    gscore = (mx1 + mx2).reshape(N_GROUPS, tn)

    itg = lax.broadcasted_iota(jnp.int32, (N_GROUPS, tn), 0)
    gsel = jnp.zeros((N_GROUPS, tn), jnp.bool_)
    cur = gscore
    for _ in range(TOPK_GROUPS):
        mx = jnp.max(cur, axis=0, keepdims=True)
        fi = jnp.min(jnp.where(cur == mx, itg, N_GROUPS), axis=0, keepdims=True)
        hit = itg == fi
        gsel = jnp.logical_or(gsel, hit)
        cur = jnp.where(hit, -jnp.inf, cur)
    emask = jnp.broadcast_to(gsel.reshape(N_GROUPS, 1, tn), (N_GROUPS, gsz, tn)).reshape(n_exp, tn)
    cur = jnp.where(emask, biased, NEG)

    ite = lax.broadcasted_iota(jnp.int32, (n_exp, tn), 0)
    hits = []
    eidx = []
    wsel = []
    for _ in range(TOP_K):
        mx = jnp.max(cur, axis=0, keepdims=True)
        fi = jnp.min(jnp.where(cur == mx, ite, n_exp), axis=0, keepdims=True)
        hit = ite == fi
        hits.append(hit)
        eidx.append(fi)
        wsel.append(jnp.sum(jnp.where(hit, scores, 0.0), axis=0, keepdims=True))
        cur = jnp.where(hit, -jnp.inf, cur)
    sel = hits[0]
    for hit in hits[1:]:
        sel = jnp.logical_or(sel, hit)
    self32 = jnp.where(sel, 1.0, 0.0)

    carry_ref[...] = carry_ref[...] + jnp.sum(self32, axis=1, keepdims=True)
    cnt_ref[...] = carry_ref[...]

    wcat = jnp.concatenate(wsel, axis=0)
    wcat = wcat / jnp.sum(wcat, axis=0, keepdims=True) * ROUTED_SCALE
    tok = i * tn + lax.broadcasted_iota(jnp.int32, (TOP_K, tn), 1)
    slot = lax.broadcasted_iota(jnp.int32, (TOP_K, tn), 0)
    e_ref[...] = jnp.concatenate(eidx, axis=0) * (1 << KEY_SHIFT) + (tok * TOP_K + slot)
    w_ref[...] = wcat


def _router(h1, wr_hi, wr_lo, rbias):
    t, d = h1.shape
    n_exp = wr_hi.shape[0]
    tn = min(512, t)
    kern = functools.partial(_router_kernel, n_exp=n_exp)
    row = pl.BlockSpec((TOP_K, tn), lambda i: (0, i))
    return pl.pallas_call(
        kern,
        out_shape=(
            jax.ShapeDtypeStruct((TOP_K, t), jnp.int32),
            jax.ShapeDtypeStruct((TOP_K, t), jnp.float32),
            jax.ShapeDtypeStruct((n_exp, 1), jnp.float32),
        ),
        grid=(t // tn,),
        in_specs=[
            pl.BlockSpec((tn, d), lambda i: (i, 0)),
            pl.BlockSpec((n_exp, d), lambda i: (0, 0)),
            pl.BlockSpec((n_exp, d), lambda i: (0, 0)),
            pl.BlockSpec((n_exp, 1), lambda i: (0, 0)),
        ],
        out_specs=(row, row, pl.BlockSpec((n_exp, 1), lambda i: (0, 0))),
        scratch_shapes=[pltpu.VMEM((n_exp, 1), jnp.float32)],
        compiler_params=_cparams(("arbitrary",)),
        name="router",
    )(h1, wr_hi, wr_lo, rbias)


def _sc_move_rows(x_slab, idx, n_out, scatter):
    m = idx.shape[0]
    rnd = SC_GRP * SC_NBUF
    per = m // SC_WORKERS
    assert m % (SC_WORKERS * rnd) == 0
    row = x_slab.shape[1:]
    mesh = plsc.VectorSubcoreMesh(core_axis_name="c", subcore_axis_name="s")

    @pl.kernel(out_type=jax.ShapeDtypeStruct((n_out,) + row, x_slab.dtype), mesh=mesh,
               scratch_types=[pltpu.VMEM((rnd,), jnp.int32), pltpu.VMEM((SC_NBUF, SC_GRP) + row, x_slab.dtype),
                              pltpu.SemaphoreType.DMA((SC_NBUF,)), pltpu.SemaphoreType.DMA((SC_NBUF,))])
    def kern(x_hbm, i_hbm, o_hbm, ibuf, buf, lsem, ssem):
        worker = lax.axis_index("c") * (SC_WORKERS // 2) + lax.axis_index("s")
        base = worker * per

        @pl.loop(0, per // rnd)
        def _(r):
            off = base + r * rnd
            pltpu.sync_copy(i_hbm.at[pl.ds(off, rnd)], ibuf)
            loads = []
            stores = []
            for b in range(SC_NBUF):
                indexed = ibuf.at[pl.ds(b * SC_GRP, SC_GRP)]
                linear = pl.ds(off + b * SC_GRP, SC_GRP)
                src = x_hbm.at[linear] if scatter else x_hbm.at[indexed]
                dst = o_hbm.at[indexed] if scatter else o_hbm.at[linear]
                loads.append(pltpu.make_async_copy(src, buf.at[b], lsem.at[b]))
                stores.append(pltpu.make_async_copy(buf.at[b], dst, ssem.at[b]))
            for ld in loads:
                ld.start()
            for b in range(SC_NBUF):
                loads[b].wait()
                stores[b].start()
            for st in stores:
                st.wait()

    return kern(x_slab, idx)


X_BUFS = 3
Y_BUFS = 2


def _experts_kernel(te_ref, tv_ref, tu_ref, na_ref, xs_hbm, wg_ref, wu_ref, wd_ref, ys_hbm,
                    xb0, xb1, xb2, yb0, yb1, wgu_s, wd_s, xsem, ysem):
    i = pl.program_id(0)
    nt = pl.num_programs(0)
    n_act = na_ref[0]
    xbufs = (xb0, xb1, xb2)
    ybufs = (yb0, yb1)
    tile_rows = TM_MOE * SLAB

    def x_copy(slot, tile):
        tile = jnp.minimum(tile, nt - 1)
        start = pl.multiple_of(tu_ref[tile] * SLAB, SLAB)
        return pltpu.make_async_copy(xs_hbm.at[pl.ds(start, tile_rows), :], xbufs[slot], xsem.at[slot])

    def y_pieces(slot, tile, go):
        nvalid = tv_ref[tile]
        base = tu_ref[tile]
        size = TM_MOE
        while size >= 1:
            @pl.when((nvalid & size) != 0)
            def _(size=size):
                off = nvalid & ~(2 * size - 1)
                src = ybufs[slot].at[pl.ds(pl.multiple_of(off * SLAB, SLAB), size * SLAB), :]
                dst = ys_hbm.at[pl.ds(pl.multiple_of((base + off) * SLAB, SLAB), size * SLAB), :]
                go(pltpu.make_async_copy(src, dst, ysem.at[slot]))
            size //= 2

    @pl.when(i == 0)
    def _():
        x_copy(0, 0).start()
        x_copy(1, 1).start()

    iprev = jnp.maximum(i - 1, 0)

    @pl.when(jnp.logical_and(i < n_act, jnp.logical_or(i == 0, te_ref[i] != te_ref[iprev])))
    def _():
        de = wg_ref.shape[2]
        wgu_s[:, 0:de] = wg_ref[0].astype(wgu_s.dtype)
        wgu_s[:, de:] = wu_ref[0].astype(wgu_s.dtype)
        wd_s[...] = wd_ref[0].astype(wd_s.dtype)

    phase = i % (X_BUFS * Y_BUFS)
    for c in range(X_BUFS * Y_BUFS):
        xs, ys = c % X_BUFS, c % Y_BUFS

        @pl.when(jnp.logical_and(i < n_act, phase == c))
        def _(xs=xs, ys=ys):
            x_copy(xs, i).wait()
            x_copy((xs + 2) % X_BUFS, i + 2).start()
            x = _slab_load(xbufs[xs], (), TM_MOE).astype(jnp.bfloat16)
            gu = _dot(x, wgu_s[...])
            de = gu.shape[1] // 2
            gate = gu[:, :de]
            hid = (gate / (1.0 + jnp.exp(-gate))) * gu[:, de:]
            y = _dot(hid.astype(jnp.bfloat16), wd_s[...])

            @pl.when(i >= Y_BUFS)
            def _():
                y_pieces(ys, i - Y_BUFS, lambda cp: cp.wait())

            _slab_store(ybufs[ys], (), y)
            y_pieces(ys, i, lambda cp: cp.start(priority=1))

        @pl.when(jnp.logical_and(i == n_act, phase == c))
        def _(xs=xs, ys=ys):
            x_copy(xs, i).wait()
            x_copy((xs + 1) % X_BUFS, i + 1).wait()

            @pl.when(i >= 2)
            def _():
                y_pieces(ys, i - 2, lambda cp: cp.wait())

            @pl.when(i >= 1)
            def _():
                y_pieces(1 - ys, i - 1, lambda cp: cp.wait())


def _experts(xs_slab, wg, wu, wd, tile_e, tile_valid, tile_u, n_active):
    nt = tile_e.shape[0]
    d = wg.shape[1]
    de = wg.shape[2]
    buf = pltpu.VMEM((TM_MOE * SLAB, LANES), jnp.float32)
    return pl.pallas_call(
        _experts_kernel,
        out_shape=jax.ShapeDtypeStruct(xs_slab.shape, jnp.float32),
        grid_spec=pltpu.PrefetchScalarGridSpec(
            num_scalar_prefetch=4,
            grid=(nt,),
            in_specs=[
                pl.BlockSpec(memory_space=pl.ANY),
                pl.BlockSpec((1, d, de), lambda i, te, tv, tu, na: (te[i], 0, 0)),
                pl.BlockSpec((1, d, de), lambda i, te, tv, tu, na: (te[i], 0, 0)),
                pl.BlockSpec((1, de, d), lambda i, te, tv, tu, na: (te[i], 0, 0)),
            ],
            out_specs=pl.BlockSpec(memory_space=pl.ANY),
            scratch_shapes=[buf] * (X_BUFS + Y_BUFS) + [
                pltpu.VMEM((d, 2 * de), jnp.bfloat16), pltpu.VMEM((de, d), jnp.bfloat16),
                pltpu.SemaphoreType.DMA((X_BUFS,)), pltpu.SemaphoreType.DMA((Y_BUFS,))],
        ),
        compiler_params=_cparams(("arbitrary",)),
        name="experts",
    )(tile_e, tile_valid, tile_u, n_active, xs_slab, wg, wu, wd)


def _combine_kernel(h_ref, w_ref, *rest):
    y_refs = rest[:TOP_K]
    sgu_ref, sd_ref, g_ref, b_ref, o_ref = rest[TOP_K:]
    tn = h_ref.shape[0]
    h = h_ref[...]
    gu = _dot(h.astype(jnp.bfloat16), sgu_ref[...])
    ds = gu.shape[1] // 2
    gate = gu[:, :ds]
    hid = (gate / (1.0 + jnp.exp(-gate))) * gu[:, ds:]
    shared = _dot(hid.astype(jnp.bfloat16), sd_ref[...])

    w = w_ref[...]
    routed = _slab_load(y_refs[0], (), tn) * w[:, 0:1]
    for k in range(1, TOP_K):
        routed = routed + _slab_load(y_refs[k], (), tn) * w[:, k:k + 1]
    o_ref[...] = _layernorm_f32(ALPHA * h + (routed + shared), g_ref[...], b_ref[...])


def _combine(h1, w_tok, y_slab, sgu, sd, g, b):
    t, d = h1.shape
    tn = min(256, t)
    nblk = t // tn
    ds2 = sgu.shape[1]
    vec = pl.BlockSpec((1, d), lambda i: (0, 0))

    def slot_spec(k):
        return pl.BlockSpec((tn * SLAB, LANES), lambda i: (k * nblk + i, 0))

    return pl.pallas_call(
        _combine_kernel,
        out_shape=jax.ShapeDtypeStruct((t, d), jnp.float32),
        grid=(nblk,),
        in_specs=[
            pl.BlockSpec((tn, d), lambda i: (i, 0)),
            pl.BlockSpec((tn, TOP_K), lambda i: (i, 0)),
            *[slot_spec(k) for k in range(TOP_K)],
            pl.BlockSpec((d, ds2), lambda i: (0, 0)),
            pl.BlockSpec((ds2 // 2, d), lambda i: (0, 0)),
            vec, vec,
        ],
        out_specs=pl.BlockSpec((tn, d), lambda i: (i, 0)),
        compiler_params=_cparams(("parallel",)),
        name="combine_ln2",
    )(h1, w_tok, *([y_slab] * TOP_K), sgu, sd, g, b)


def _rel_bucket(rel):
    nb = N_BUCKETS // 2
    max_exact = nb // 2
    ret = jnp.where(rel > 0, nb, 0)
    n = jnp.abs(rel)
    nf = jnp.maximum(n, 1).astype(jnp.float32)
    large = max_exact + (jnp.log(nf / max_exact) / math.log(MAX_DISTANCE / max_exact) * (nb - max_exact)).astype(jnp.int32)
    large = jnp.minimum(large, nb - 1)
    return ret + jnp.where(n < max_exact, n, large)


def _bias_of_rel(rel_bias, rel):
    return rel_bias.astype(jnp.float32)[_rel_bucket(rel)]


def _toeplitz(vec, nrow, ncol, off):
    lo = off - (nrow - 1)
    v = vec[lo:off + ncol]
    p = v.shape[0] + 1
    v = jnp.concatenate([v, v[:1]], axis=0)
    flat = jnp.tile(v, (nrow + 1, 1))
    base = off - lo
    out = flat[base:base + nrow * (p - 1)].reshape(nrow, p - 1, vec.shape[1])
    return out[:, :ncol]


def _bias_tables_a(rel_bias, sink):
    bias_a = rel_bias[:, :A_Q_HEADS]
    m = jnp.arange(3 * TQ_A + TQ_A - 1, dtype=jnp.int32)
    rel = 2 * TQ_A - 1 - m
    vec = jnp.where((jnp.abs(rel) <= WINDOW)[:, None], _bias_of_rel(bias_a, rel), NEG)
    band = _toeplitz(vec, 3 * TQ_A, TQ_A, 3 * TQ_A - 1)
    mm = jnp.arange(TQ_A + N_META - 1, dtype=jnp.int32)
    meta_first = _toeplitz(_bias_of_rel(bias_a, -1 - mm), N_META, TQ_A, N_META - 1)
    meta_far = jnp.broadcast_to(_bias_of_rel(bias_a, jnp.int32(-2 * MAX_DISTANCE)), (N_META, TQ_A, A_Q_HEADS))
    pad = jnp.full((NK_A - 3 * TQ_A - N_META, TQ_A, A_Q_HEADS), NEG, jnp.float32)
    blocked = jnp.full((TQ_A, TQ_A, A_Q_HEADS), NEG, jnp.float32)
    first = jnp.concatenate([blocked, band[TQ_A:], meta_first, pad])
    middle = jnp.concatenate([band, meta_far, pad])
    last = jnp.concatenate([band[:2 * TQ_A], blocked, meta_far, pad])
    tab = jnp.stack([first, middle, last])
    tab = tab.reshape(3, NK_A, TQ_A, A_Q_HEADS // 2, 2)
    tab = jnp.transpose(tab, (0, 3, 1, 4, 2)).reshape(3, A_Q_HEADS // 2, NK_A, 2 * TQ_A) * LOG2E
    sink_rows = jnp.repeat(sink.astype(jnp.float32).reshape(A_Q_HEADS // 2, 1, 2), TQ_A, axis=2) * LOG2E
    return tab, sink_rows


def _bias_tables_b(rel_bias, s):
    bias_b = rel_bias[:, A_Q_HEADS:]
    near = []
    for d in (-1, 0, 1):
        m = jnp.arange(TQ_B + TK_B - 1, dtype=jnp.int32)
        vec = _bias_of_rel(bias_b, TK_B * d + TK_B - 1 - m)
        near.append(_toeplitz(vec, TK_B, TQ_B, TK_B - 1))
    far_l = jnp.broadcast_to(_bias_of_rel(bias_b, jnp.int32(-TK_B - 1)), (TK_B, TQ_B, B_HEADS))
    far_r = jnp.broadcast_to(_bias_of_rel(bias_b, jnp.int32(TK_B + 1)), (TK_B, TQ_B, B_HEADS))
    tabs = jnp.transpose(jnp.stack([far_l] + near + [far_r]), (3, 0, 1, 2)) * LOG2E
    m = jnp.arange(s + N_META - 1, dtype=jnp.int32)
    vec = _bias_of_rel(bias_b, -1 - m)
    meta = jnp.transpose(_toeplitz(vec, N_META, s, N_META - 1), (2, 0, 1)) * LOG2E
    return tabs, meta


def _prep_w_in(w_in):
    a_w = A_Q_HEADS * HEAD_DIM
    kv = A_KV_HEADS * HEAD_DIM
    bqk = B_HEADS * 2 * HEAD_DIM
    scale = HEAD_DIM ** -0.5
    qa = w_in[:, :a_w] * (scale * LOG2E)
    ka = w_in[:, a_w:a_w + kv]
    va = w_in[:, a_w + kv:a_w + 2 * kv]
    o = a_w + 2 * kv
    qb = w_in[:, o:o + bqk] * (scale * LOG2E)
    kb = w_in[:, o + bqk:o + 2 * bqk]
    vb = w_in[:, o + 2 * bqk:]

    def dup(w):
        return jnp.concatenate([w[:, g * HEAD_DIM:(g + 1) * HEAD_DIM] for g in range(A_KV_HEADS) for _ in range(2)], axis=1)

    w = jnp.concatenate([qa, dup(ka), qb, kb], axis=1).astype(jnp.bfloat16)
    return w, jnp.concatenate([vb, va], axis=1).T.astype(jnp.bfloat16)


def _trunk_front(x, prm):
    bsz, s, d = x.shape
    t = bsz * s
    x2d = x.reshape(t, d)
    proj, vt, vta = _ln_inproj(x2d, prm["ln_emb_g"], prm["ln_emb_b"], prm["w_in"], prm["w_vt"])
    proj3 = proj.reshape(bsz, s, PROJ_COLS)
    oa = _mixer_a(proj3, vta, prm["km_a"], prm["vtm_a"], prm["tab_a"], prm["sink_rows"])
    tabs_b, meta_b = _bias_tables_b(prm["rel_bias"], s)
    ob = _mixer_b(proj3, vt, prm["proj_meta"], prm["vt_meta"], tabs_b, meta_b, prm["lam"].reshape(1).astype(jnp.float32),
                  prm["subln_g"])
    h1, h1_slab = _outproj_ln1(x2d, oa.reshape(t, -1), ob.reshape(t, -1), prm["w_out"], prm["ln_emb_g"],
                               prm["ln_emb_b"], prm["ln1_g"], prm["ln1_b"])

    keys, wts, counts = _router(h1, prm["wr_hi"], prm["wr_lo"], prm["router_bias"])
    n_exp = counts.shape[0]
    n_asg = t * TOP_K
    assert n_asg <= (1 << KEY_SHIFT) and n_asg % LANES == 0
    order = jnp.sort(keys.reshape(n_asg)) & ((1 << KEY_SHIFT) - 1)
    tok = order >> 3
    dst = (order & (TOP_K - 1)) * t + tok
    sc_pad = SC_WORKERS * SC_GRP * SC_NBUF
    assert sc_pad >= TM_MOE and n_asg % sc_pad == 0
    xs = _sc_move_rows(h1_slab.reshape(t, SLAB, LANES), jnp.concatenate([tok, jnp.zeros((sc_pad,), jnp.int32)]),
                       n_asg + sc_pad, scatter=False)
    xs_slab = xs.reshape((n_asg + sc_pad) * SLAB, LANES)
    counts = counts[:, 0].astype(jnp.int32)
    tiles_e = (counts + TM_MOE - 1) // TM_MOE
    tend = jnp.cumsum(tiles_e)
    tstart = tend - tiles_e
    ustart = jnp.cumsum(counts) - counts
    nt = n_asg // TM_MOE + n_exp + 1
    tid = jnp.arange(nt, dtype=jnp.int32)
    tile_e = jnp.minimum(jnp.sum((tend[None, :] <= tid[:, None]).astype(jnp.int32), axis=1), n_exp - 1)
    onehot = (tile_e[:, None] == jnp.arange(n_exp, dtype=jnp.int32)[None, :]).astype(jnp.int32)
    in_e = (tid - jnp.sum(onehot * tstart[None, :], axis=1)) * TM_MOE
    active = tid < tend[-1]
    tile_valid = jnp.where(active, jnp.clip(jnp.sum(onehot * counts[None, :], axis=1) - in_e, 0, TM_MOE), 0)
    tile_u = jnp.where(active, jnp.sum(onehot * ustart[None, :], axis=1) + in_e, 0)

    tiles = (tile_e.astype(jnp.int32), tile_valid.astype(jnp.int32), tile_u.astype(jnp.int32),
             tend[-1:].astype(jnp.int32))
    return h1, wts.T, xs_slab, dst, tiles, x.shape


def _trunk_back(front, prm):
    h1, w_tok, xs_slab, dst, tiles, shape = front
    n_asg = dst.shape[0]
    ys_slab = _experts(xs_slab, prm["w_gate"], prm["w_up"], prm["w_down"], *tiles)
    y = _sc_move_rows(ys_slab.reshape(-1, SLAB, LANES), dst, n_asg, scatter=True)
    out = _combine(h1, w_tok, y.reshape(n_asg * SLAB, LANES), prm["ws_gu"], prm["ws_down"], prm["ln2_g"], prm["ln2_b"])
    return out.reshape(shape)


def kernel(x_prompt, x_sample, meta_tokens, ln_emb_g, ln_emb_b, rel_bias, w_in, attn_sink, lambda_q1, lambda_k1, lambda_q2, lambda_k2, subln_g, w_out, ln1_g, ln1_b, w_router, router_bias, w_gate, w_up, w_down, ws_gate, ws_up, ws_down, ln2_g, ln2_b):
    f32 = jnp.float32
    bf16 = jnp.bfloat16
    l = 0
    row = lambda v: v.reshape(1, -1).astype(f32)
    lam = (jnp.exp(jnp.sum(lambda_q1[l].astype(f32) * lambda_k1[l].astype(f32)))
           - jnp.exp(jnp.sum(lambda_q2[l].astype(f32) * lambda_k2[l].astype(f32))) + LAMBDA_INIT)
    wr_t = w_router[l].astype(f32).T
    wr_hi = wr_t.astype(bf16)
    prm = {
        "ln_emb_g": row(ln_emb_g), "ln_emb_b": row(ln_emb_b),
        "rel_bias": rel_bias,
        "lam": lam,
        "subln_g": subln_g[l].astype(f32).reshape(-1, 1),
        "w_out": w_out[l].astype(bf16),
        "ln1_g": row(ln1_g[l]), "ln1_b": row(ln1_b[l]),
        "wr_hi": wr_hi, "wr_lo": (wr_t - wr_hi.astype(f32)).astype(bf16),
        "router_bias": router_bias[l].astype(f32).reshape(-1, 1),
        "w_gate": w_gate[l], "w_up": w_up[l], "w_down": w_down[l],
        "ws_gu": jnp.concatenate([ws_gate[l], ws_up[l]], axis=-1).astype(bf16),
        "ws_down": ws_down[l].astype(bf16),
        "ln2_g": row(ln2_g[l]), "ln2_b": row(ln2_b[l]),
    }
    prm["w_in"], prm["w_vt"] = _prep_w_in(w_in[l])
    prm["proj_meta"], prm["vt_meta"], vta_meta = _ln_inproj(meta_tokens.astype(f32), prm["ln_emb_g"], prm["ln_emb_b"],
                                                           prm["w_in"], prm["w_vt"])
    meta_pad = NK_A - 3 * TQ_A - N_META
    prm["km_a"] = jnp.pad(prm["proj_meta"][:, KA_BLK * LANES:(KA_BLK + 2) * LANES], ((0, meta_pad), (0, 0)))
    prm["vtm_a"] = jnp.pad(vta_meta, ((0, 0), (0, meta_pad)))
    prm["tab_a"], prm["sink_rows"] = _bias_tables_a(rel_bias, attn_sink[l])
    front_p = _trunk_front(x_prompt, prm)
    front_s = _trunk_front(x_sample, prm)
    return (_trunk_back(front_p, prm), _trunk_back(front_s, prm))
```

```python
import functools
import math

import jax
import jax.numpy as jnp
from jax import lax
from jax.experimental import pallas as pl
from jax.experimental.pallas import tpu as pltpu
from jax.experimental.pallas import tpu_sc as plsc

N_META = 16
HEAD_DIM = 64
WINDOW = 128
A_Q_HEADS = 8
A_KV_HEADS = 2
B_HEADS = 4
N_BUCKETS = 32
MAX_DISTANCE = 128
TOP_K = 8
N_GROUPS = 8
TOPK_GROUPS = 4
ROUTED_SCALE = 2.5
LN_EPS = 1e-5
DEPTH = 1
ALPHA = (2 * DEPTH) ** 0.25
NEG = -1e30
LAMBDA_INIT = 0.8 - 0.6 * math.exp(-0.3 * 0)

LANES = 128
VMEM_LIMIT = 48 * 1024 * 1024

QA_BLK = 0
KA_BLK = 4
QB_BLK = 6
KB_BLK = 10
PROJ_COLS = 14 * LANES
VT_ROWS = LANES + 16
VA_ROWS = HEAD_DIM + 16
LOG2E = 1.4426950408889634

TQ_A = 128
NK_A = 4 * TQ_A
TQ_B = 512
TK_B = 512
TM_MOE = 512
SLAB = 8
SC_GRP = 16
SC_NBUF = 4
SC_WORKERS = 32
KEY_SHIFT = 20


def _cparams(sem):
    return pltpu.CompilerParams(dimension_semantics=sem, vmem_limit_bytes=VMEM_LIMIT)


def _layernorm_f32(x, g, b):
    mu = jnp.mean(x, axis=-1, keepdims=True)
    xc = x - mu
    var = jnp.mean(xc * xc, axis=-1, keepdims=True)
    return xc * lax.rsqrt(var + LN_EPS) * g + b


def _dot_nt(a, b):
    return lax.dot_general(a, b, (((1,), (1,)), ((), ())), preferred_element_type=jnp.float32)


def _dot(a, b):
    return jnp.dot(a, b, preferred_element_type=jnp.float32)


def _ln_inproj_kernel(x_ref, g_ref, b_ref, w_ref, wvt_ref, o_ref, vt_ref, vta_ref):
    h = _layernorm_f32(x_ref[...], g_ref[...], b_ref[...]).astype(jnp.bfloat16)
    o_ref[...] = _dot(h, w_ref[...]).astype(o_ref.dtype)
    vt = _dot_nt(wvt_ref[...], h).astype(vt_ref.dtype)
    ones = jnp.ones((16, vt.shape[1]), vt_ref.dtype)
    for hd in range(B_HEADS):
        vt_ref[hd * VT_ROWS:hd * VT_ROWS + LANES, :] = vt[hd * LANES:(hd + 1) * LANES, :]
        vt_ref[hd * VT_ROWS + LANES:(hd + 1) * VT_ROWS, :] = ones
    base = B_HEADS * LANES
    for g in range(A_KV_HEADS):
        vta_ref[g * VA_ROWS:g * VA_ROWS + HEAD_DIM, :] = vt[base + g * HEAD_DIM:base + (g + 1) * HEAD_DIM, :]
        vta_ref[g * VA_ROWS + HEAD_DIM:(g + 1) * VA_ROWS, :] = ones


def _ln_inproj(x2d, g, b, w, wvt):
    t, d = x2d.shape
    n = w.shape[1]
    tm = min(512, t)
    return pl.pallas_call(
        _ln_inproj_kernel,
        out_shape=(jax.ShapeDtypeStruct((t, n), jnp.bfloat16),
                   jax.ShapeDtypeStruct((B_HEADS * VT_ROWS, t), jnp.bfloat16),
                   jax.ShapeDtypeStruct((A_KV_HEADS * VA_ROWS, t), jnp.bfloat16)),
        grid=(t // tm,),
        in_specs=[
            pl.BlockSpec((tm, d), lambda i: (i, 0)),
            pl.BlockSpec((1, d), lambda i: (0, 0)),
            pl.BlockSpec((1, d), lambda i: (0, 0)),
            pl.BlockSpec((d, n), lambda i: (0, 0)),
            pl.BlockSpec((wvt.shape[0], d), lambda i: (0, 0)),
        ],
        out_specs=(pl.BlockSpec((tm, n), lambda i: (i, 0)),
                   pl.BlockSpec((B_HEADS * VT_ROWS, tm), lambda i: (0, i)),
                   pl.BlockSpec((A_KV_HEADS * VA_ROWS, tm), lambda i: (0, i))),
        compiler_params=_cparams(("parallel",)),
        name="ln_inproj",
    )(x2d, g, b, w, wvt)


def _mixer_a_kernel(q_ref, k_ref, vt_ref, km_ref, vtm_ref, tab_ref, sink_ref, o_ref, *, nblk, sub):
    i = pl.program_id(1)
    lane = lax.broadcasted_iota(jnp.int32, (1, LANES), 1)
    lo = lane < HEAD_DIM
    for j in range(sub):
        gi = i * sub + j
        sp = pl.multiple_of(jnp.maximum(gi - 1, 0) * TQ_A, TQ_A)
        sc = pl.multiple_of(gi * TQ_A, TQ_A)
        sn = pl.multiple_of(jnp.minimum(gi + 1, nblk - 1) * TQ_A, TQ_A)
        variant = jnp.where(gi == 0, 0, jnp.where(gi == nblk - 1, 2, 1))
        rows = slice(j * TQ_A, (j + 1) * TQ_A)
        for g in range(A_KV_HEADS):
            gs = slice(g * LANES, (g + 1) * LANES)
            vr = slice(g * VA_ROWS, (g + 1) * VA_ROWS)
            k_all = jnp.concatenate([k_ref[0, pl.ds(sp, TQ_A), gs], k_ref[0, pl.ds(sc, TQ_A), gs],
                                     k_ref[0, pl.ds(sn, TQ_A), gs], km_ref[:, gs]], axis=0)
            vt_all = jnp.concatenate([vt_ref[vr, pl.ds(sp, TQ_A)], vt_ref[vr, pl.ds(sc, TQ_A)],
                                      vt_ref[vr, pl.ds(sn, TQ_A)], vtm_ref[vr, :]], axis=1)
            for pp in range(2):
                hp = 2 * g + pp
                cols = slice(hp * LANES, (hp + 1) * LANES)
                qc = q_ref[0, rows, cols]
                q2 = jnp.concatenate([jnp.where(lo, qc, jnp.zeros_like(qc)), jnp.where(lo, jnp.zeros_like(qc), qc)],
                                     axis=0)
                s = _dot_nt(k_all, q2) + tab_ref[variant, hp]
                sink = sink_ref[hp]
                m = jnp.maximum(jnp.max(s, axis=0, keepdims=True), sink)
                p = jnp.exp2(s - m)
                acc = _dot(vt_all, p.astype(jnp.bfloat16))
                o = acc[0:HEAD_DIM, :] / (acc[HEAD_DIM:HEAD_DIM + 1, :] + jnp.exp2(sink - m))
                o2 = jnp.concatenate([o[:, :TQ_A], o[:, TQ_A:]], axis=0)
                o_ref[0, rows, cols] = o2.T.astype(o_ref.dtype)


def _mixer_a(proj3, vta, km, vtm, tab, sink):
    bsz, s, _ = proj3.shape
    nblk = s // TQ_A
    assert nblk >= 2
    sub = 4 if nblk % 4 == 0 else 1
    nq = nblk // sub
    tq = sub * TQ_A
    kern = functools.partial(_mixer_a_kernel, nblk=nblk, sub=sub)
    return pl.pallas_call(
        kern,
        out_shape=jax.ShapeDtypeStruct((bsz, s, A_Q_HEADS * HEAD_DIM), jnp.bfloat16),
        grid=(bsz, nq),
        in_specs=[
            pl.BlockSpec((1, tq, 4 * LANES), lambda b, i: (b, i, QA_BLK // 4)),
            pl.BlockSpec((1, s, 2 * LANES), lambda b, i: (b, 0, KA_BLK // 2)),
            pl.BlockSpec((A_KV_HEADS * VA_ROWS, s), lambda b, i: (0, b)),
            pl.BlockSpec(km.shape, lambda b, i: (0, 0)),
            pl.BlockSpec(vtm.shape, lambda b, i: (0, 0)),
            pl.BlockSpec(tab.shape, lambda b, i: (0, 0, 0, 0)),
            pl.BlockSpec(sink.shape, lambda b, i: (0, 0, 0)),
        ],
        out_specs=pl.BlockSpec((1, tq, 4 * LANES), lambda b, i: (b, i, 0)),
        compiler_params=_cparams(("parallel", "arbitrary")),
        name="mixer_a",
    )(proj3, proj3, vta, km, vtm, tab, sink)


def _mixer_b_kernel(sc_ref, q_ref, k_ref, vt_ref, km_ref, vtm_ref, tab_ref, tabm_ref, g_ref, o_ref,
                    m_ref, acc_ref, s0_ref, s1_ref, *, nk):
    h = pl.program_id(1)
    i = pl.program_id(2)
    lane = lax.broadcasted_iota(jnp.int32, (1, LANES), 1)
    lo = lane < HEAD_DIM
    q = q_ref[0]
    qs = (jnp.where(lo, q, jnp.zeros_like(q)), jnp.where(lo, jnp.zeros_like(q), q))
    lam = sc_ref[0]
    sbufs = (s0_ref, s1_ref)

    km = km_ref[...]
    vtm = vtm_ref[...]
    for c in range(2):
        s = _dot_nt(km, qs[c]) + tabm_ref[0]
        m = jnp.max(s, axis=0, keepdims=True)
        p = jnp.exp2(s - m)
        m_ref[c] = m
        acc_ref[c] = _dot(vtm, p.astype(jnp.bfloat16))

    def scores(j, slot):
        start = pl.multiple_of(j * TK_B, TK_B)
        kj = k_ref[0, pl.ds(start, TK_B), :]
        bias = tab_ref[0, jnp.clip(j - i, -2, 2) + 2]
        for c in range(2):
            sbufs[slot][c] = _dot_nt(kj, qs[c]) + bias

    def accumulate(j, slot):
        start = pl.multiple_of(j * TK_B, TK_B)
        vtj = vt_ref[:, pl.ds(start, TK_B)]
        for c in range(2):
            s = sbufs[slot][c]
            m_prev = m_ref[c]
            m_new = jnp.maximum(m_prev, jnp.max(s, axis=0, keepdims=True))
            a = jnp.exp2(m_prev - m_new)
            p = jnp.exp2(s - m_new)
            m_ref[c] = m_new
            acc_ref[c] = a * acc_ref[c] + _dot(vtj, p.astype(jnp.bfloat16))

    scores(0, 0)

    def pair(jj, carry):
        j = 2 * jj
        scores(j + 1, 1)
        accumulate(j, 0)
        scores(j + 2, 0)
        accumulate(j + 1, 1)
        return carry

    lax.fori_loop(0, nk // 2 - 1, pair, 0)
    scores(nk - 1, 1)
    accumulate(nk - 2, 0)
    accumulate(nk - 1, 1)

    o0 = acc_ref[0, 0:LANES, :] / acc_ref[0, LANES:LANES + 1, :]
    o1 = acc_ref[1, 0:LANES, :] / acc_ref[1, LANES:LANES + 1, :]
    o = o0 - lam * o1
    ms = jnp.mean(o * o, axis=0, keepdims=True)
    o = o * lax.rsqrt(ms + LN_EPS) * (g_ref[...] * (1.0 - LAMBDA_INIT))
    o_ref[0] = o.T.astype(o_ref.dtype)


def _mixer_b(proj3, vt, proj_meta, vt_meta, tab, tabm, scal, subln_g):
    bsz, s, _ = proj3.shape
    nq = s // TQ_B
    nk = s // TK_B
    assert nk % 2 == 0
    kern = functools.partial(_mixer_b_kernel, nk=nk)
    return pl.pallas_call(
        kern,
        out_shape=jax.ShapeDtypeStruct((bsz, s, B_HEADS * LANES), jnp.bfloat16),
        grid=(bsz, B_HEADS, nq),
        in_specs=[
            pl.BlockSpec(memory_space=pltpu.SMEM),
            pl.BlockSpec((1, TQ_B, LANES), lambda b, h, i: (b, i, QB_BLK + h)),
            pl.BlockSpec((1, s, LANES), lambda b, h, i: (b, 0, KB_BLK + h)),
            pl.BlockSpec((VT_ROWS, s), lambda b, h, i: (h, b)),
            pl.BlockSpec((N_META, LANES), lambda b, h, i: (0, KB_BLK + h)),
            pl.BlockSpec((VT_ROWS, N_META), lambda b, h, i: (h, 0)),
            pl.BlockSpec((1, 5, TK_B, TQ_B), lambda b, h, i: (h, 0, 0, 0)),
            pl.BlockSpec((1, N_META, TQ_B), lambda b, h, i: (h, 0, i)),
            pl.BlockSpec((LANES, 1), lambda b, h, i: (0, 0)),
        ],
        out_specs=pl.BlockSpec((1, TQ_B, LANES), lambda b, h, i: (b, i, h)),
        scratch_shapes=[
            pltpu.VMEM((2, 1, TQ_B), jnp.float32),
            pltpu.VMEM((2, VT_ROWS, TQ_B), jnp.float32),
            pltpu.VMEM((2, TK_B, TQ_B), jnp.float32),
            pltpu.VMEM((2, TK_B, TQ_B), jnp.float32),
        ],
        compiler_params=_cparams(("parallel", "parallel", "arbitrary")),
        name="mixer_b",
    )(scal, proj3, proj3, vt, proj_meta, vt_meta, tab, tabm, subln_g)


def _slab_load(ref, lead, rows):
    return jnp.concatenate([ref[lead + (pl.ds(s, rows, stride=SLAB), slice(None))] for s in range(SLAB)], axis=1)


def _slab_store(ref, lead, val):
    rows = val.shape[0]
    for s in range(SLAB):
        ref[lead + (pl.ds(s, rows, stride=SLAB), slice(None))] = val[:, s * LANES:(s + 1) * LANES]


def _outproj_kernel(x_ref, oa_ref, ob_ref, w_ref, eg_ref, eb_ref, g_ref, b_ref, o_ref, os_ref):
    h0 = _layernorm_f32(x_ref[...], eg_ref[...], eb_ref[...])
    half = oa_ref.shape[1]
    mix = _dot(oa_ref[...], w_ref[0:half, :]) + _dot(ob_ref[...], w_ref[half:, :])
    h1 = _layernorm_f32(ALPHA * h0 + mix, g_ref[...], b_ref[...])
    o_ref[...] = h1
    _slab_store(os_ref, (), h1)


def _outproj_ln1(x2d, oa, ob, w_out, eg, eb, g, b):
    t, d = x2d.shape
    tm = min(512, t)
    wa = oa.shape[1]
    wb = ob.shape[1]
    vec = pl.BlockSpec((1, d), lambda i: (0, 0))
    return pl.pallas_call(
        _outproj_kernel,
        out_shape=(jax.ShapeDtypeStruct((t, d), jnp.float32),
                   jax.ShapeDtypeStruct((t * SLAB, LANES), jnp.float32)),
        grid=(t // tm,),
        in_specs=[
            pl.BlockSpec((tm, d), lambda i: (i, 0)),
            pl.BlockSpec((tm, wa), lambda i: (i, 0)),
            pl.BlockSpec((tm, wb), lambda i: (i, 0)),
            pl.BlockSpec((wa + wb, d), lambda i: (0, 0)),
            vec, vec, vec, vec,
        ],
        out_specs=(pl.BlockSpec((tm, d), lambda i: (i, 0)),
                   pl.BlockSpec((tm * SLAB, LANES), lambda i: (i, 0))),
        compiler_params=_cparams(("parallel",)),
        name="outproj_ln1",
    )(x2d, oa, ob, w_out, eg, eb, g, b)


def _router_kernel(h_ref, wh_ref, wl_ref, rb_ref, e_ref, w_ref, cnt_ref, carry_ref, *, n_exp):
    i = pl.program_id(0)
    tn = h_ref.shape[0]
    gsz = n_exp // N_GROUPS

    @pl.when(i == 0)
    def _():
        carry_ref[...] = jnp.zeros_like(carry_ref)

    x = h_ref[...]
    xh = x.astype(jnp.bfloat16)
    xl = (x - xh.astype(jnp.float32)).astype(jnp.bfloat16)
    logits = _dot_nt(wh_ref[...], xh) + (_dot_nt(wh_ref[...], xl) + _dot_nt(wl_ref[...], xh))
    scores = 1.0 / (1.0 + jnp.exp(-logits))
    biased = scores + rb_ref[...]

    g3 = biased.reshape(N_GROUPS, gsz, tn)
    it3 = lax.broadcasted_iota(jnp.int32, (N_GROUPS, gsz, tn), 1)
    mx1 = jnp.max(g3, axis=1, keepdims=True)
    first = jnp.min(jnp.where(g3 == mx1, it3, gsz), axis=1, keepdims=True)
    mx2 = jnp.max(jnp.where(it3 == first, -jnp.inf, g3), axis=1, keepdims=True)
    gscore = (mx1 + mx2).reshape(N_GROUPS, tn)

    itg = lax.broadcasted_iota(jnp.int32, (N_GROUPS, tn), 0)
    gsel = jnp.zeros((N_GROUPS, tn), jnp.bool_)
    cur = gscore
    for _ in range(TOPK_GROUPS):
        mx = jnp.max(cur, axis=0, keepdims=True)
        fi = jnp.min(jnp.where(cur == mx, itg, N_GROUPS), axis=0, keepdims=True)
        hit = itg == fi
        gsel = jnp.logical_or(gsel, hit)
        cur = jnp.where(hit, -jnp.inf, cur)
    emask = jnp.broadcast_to(gsel.reshape(N_GROUPS, 1, tn), (N_GROUPS, gsz, tn)).reshape(n_exp, tn)
    cur = jnp.where(emask, biased, NEG)

    ite = lax.broadcasted_iota(jnp.int32, (n_exp, tn), 0)
    hits = []
    eidx = []
    wsel = []
    for _ in range(TOP_K):
        mx = jnp.max(cur, axis=0, keepdims=True)
        fi = jnp.min(jnp.where(cur == mx, ite, n_exp), axis=0, keepdims=True)
        hit = ite == fi
        hits.append(hit)
        eidx.append(fi)
        wsel.append(jnp.sum(jnp.where(hit, scores, 0.0), axis=0, keepdims=True))
        cur = jnp.where(hit, -jnp.inf, cur)
    sel = hits[0]
    for hit in hits[1:]:
        sel = jnp.logical_or(sel, hit)
    self32 = jnp.where(sel, 1.0, 0.0)

    carry_ref[...] = carry_ref[...] + jnp.sum(self32, axis=1, keepdims=True)
    cnt_ref[...] = carry_ref[...]

    wcat = jnp.concatenate(wsel, axis=0)
    wcat = wcat / jnp.sum(wcat, axis=0, keepdims=True) * ROUTED_SCALE
    tok = i * tn + lax.broadcasted_iota(jnp.int32, (TOP_K, tn), 1)
    slot = lax.broadcasted_iota(jnp.int32, (TOP_K, tn), 0)
    e_ref[...] = jnp.concatenate(eidx, axis=0) * (1 << KEY_SHIFT) + (tok * TOP_K + slot)
    w_ref[...] = wcat


def _router(h1, wr_hi, wr_lo, rbias):
    t, d = h1.shape
    n_exp = wr_hi.shape[0]
    tn = min(512, t)
    kern = functools.partial(_router_kernel, n_exp=n_exp)
    row = pl.BlockSpec((TOP_K, tn), lambda i: (0, i))
    return pl.pallas_call(
        kern,
        out_shape=(
            jax.ShapeDtypeStruct((TOP_K, t), jnp.int32),
            jax.ShapeDtypeStruct((TOP_K, t), jnp.float32),
            jax.ShapeDtypeStruct((n_exp, 1), jnp.float32),
        ),
        grid=(t // tn,),
        in_specs=[
            pl.BlockSpec((tn, d), lambda i: (i, 0)),
            pl.BlockSpec((n_exp, d), lambda i: (0, 0)),
            pl.BlockSpec((n_exp, d), lambda i: (0, 0)),
            pl.BlockSpec((n_exp, 1), lambda i: (0, 0)),
        ],
        out_specs=(row, row, pl.BlockSpec((n_exp, 1), lambda i: (0, 0))),
        scratch_shapes=[pltpu.VMEM((n_exp, 1), jnp.float32)],
        compiler_params=_cparams(("arbitrary",)),
        name="router",
    )(h1, wr_hi, wr_lo, rbias)


def _sc_move_rows(x_slab, idx, n_out, scatter):
    m = idx.shape[0]
    rnd = SC_GRP * SC_NBUF
    per = m // SC_WORKERS
    assert m % (SC_WORKERS * rnd) == 0
    row = x_slab.shape[1:]
    mesh = plsc.VectorSubcoreMesh(core_axis_name="c", subcore_axis_name="s")

    @pl.kernel(out_type=jax.ShapeDtypeStruct((n_out,) + row, x_slab.dtype), mesh=mesh,
               scratch_types=[pltpu.VMEM((rnd,), jnp.int32), pltpu.VMEM((SC_NBUF, SC_GRP) + row, x_slab.dtype),
                              pltpu.SemaphoreType.DMA((SC_NBUF,)), pltpu.SemaphoreType.DMA((SC_NBUF,))])
    def kern(x_hbm, i_hbm, o_hbm, ibuf, buf, lsem, ssem):
        worker = lax.axis_index("c") * (SC_WORKERS // 2) + lax.axis_index("s")
        base = worker * per

        @pl.loop(0, per // rnd)
        def _(r):
            off = base + r * rnd
            pltpu.sync_copy(i_hbm.at[pl.ds(off, rnd)], ibuf)
            loads = []
            stores = []
            for b in range(SC_NBUF):
                indexed = ibuf.at[pl.ds(b * SC_GRP, SC_GRP)]
                linear = pl.ds(off + b * SC_GRP, SC_GRP)
                src = x_hbm.at[linear] if scatter else x_hbm.at[indexed]
                dst = o_hbm.at[indexed] if scatter else o_hbm.at[linear]
                loads.append(pltpu.make_async_copy(src, buf.at[b], lsem.at[b]))
                stores.append(pltpu.make_async_copy(buf.at[b], dst, ssem.at[b]))
            for ld in loads:
                ld.start()
            for b in range(SC_NBUF):
                loads[b].wait()
                stores[b].start()
            for st in stores:
                st.wait()

    return kern(x_slab, idx)


X_BUFS = 3
Y_BUFS = 2


def _experts_kernel(te_ref, tv_ref, tu_ref, na_ref, xs_hbm, wg_ref, wu_ref, wd_ref, ys_hbm,
                    xb0, xb1, xb2, yb0, yb1, wgu_s, wd_s, xsem, ysem):
    i = pl.program_id(0)
    nt = pl.num_programs(0)
    n_act = na_ref[0]
    xbufs = (xb0, xb1, xb2)
    ybufs = (yb0, yb1)
    tile_rows = TM_MOE * SLAB

    def x_copy(slot, tile):
        tile = jnp.minimum(tile, nt - 1)
        start = pl.multiple_of(tu_ref[tile] * SLAB, SLAB)
        return pltpu.make_async_copy(xs_hbm.at[pl.ds(start, tile_rows), :], xbufs[slot], xsem.at[slot])

    def y_pieces(slot, tile, go):
        nvalid = tv_ref[tile]
        base = tu_ref[tile]
        size = TM_MOE
        while size >= 1:
            @pl.when((nvalid & size) != 0)
            def _(size=size):
                off = nvalid & ~(2 * size - 1)
                src = ybufs[slot].at[pl.ds(pl.multiple_of(off * SLAB, SLAB), size * SLAB), :]
                dst = ys_hbm.at[pl.ds(pl.multiple_of((base + off) * SLAB, SLAB), size * SLAB), :]
                go(pltpu.make_async_copy(src, dst, ysem.at[slot]))
            size //= 2

    @pl.when(i == 0)
    def _():
        x_copy(0, 0).start()
        x_copy(1, 1).start()

    iprev = jnp.maximum(i - 1, 0)

    @pl.when(jnp.logical_and(i < n_act, jnp.logical_or(i == 0, te_ref[i] != te_ref[iprev])))
    def _():
        de = wg_ref.shape[2]
        wgu_s[:, 0:de] = wg_ref[0].astype(wgu_s.dtype)
        wgu_s[:, de:] = wu_ref[0].astype(wgu_s.dtype)
        wd_s[...] = wd_ref[0].astype(wd_s.dtype)

    phase = i % (X_BUFS * Y_BUFS)
    for c in range(X_BUFS * Y_BUFS):
        xs, ys = c % X_BUFS, c % Y_BUFS

        @pl.when(jnp.logical_and(i < n_act, phase == c))
        def _(xs=xs, ys=ys):
            x_copy(xs, i).wait()
            x_copy((xs + 2) % X_BUFS, i + 2).start()
            x = _slab_load(xbufs[xs], (), TM_MOE).astype(jnp.bfloat16)
            gu = _dot(x, wgu_s[...])
            de = gu.shape[1] // 2
            gate = gu[:, :de]
            hid = (gate / (1.0 + jnp.exp(-gate))) * gu[:, de:]
            y = _dot(hid.astype(jnp.bfloat16), wd_s[...])

            @pl.when(i >= Y_BUFS)
            def _():
                y_pieces(ys, i - Y_BUFS, lambda cp: cp.wait())

            _slab_store(ybufs[ys], (), y)
            y_pieces(ys, i, lambda cp: cp.start(priority=1))

        @pl.when(jnp.logical_and(i == n_act, phase == c))
        def _(xs=xs, ys=ys):
            x_copy(xs, i).wait()
            x_copy((xs + 1) % X_BUFS, i + 1).wait()

            @pl.when(i >= 2)
            def _():
                y_pieces(ys, i - 2, lambda cp: cp.wait())

            @pl.when(i >= 1)
            def _():
                y_pieces(1 - ys, i - 1, lambda cp: cp.wait())


def _experts(xs_slab, wg, wu, wd, tile_e, tile_valid, tile_u, n_active):
    nt = tile_e.shape[0]
    d = wg.shape[1]
    de = wg.shape[2]
    buf = pltpu.VMEM((TM_MOE * SLAB, LANES), jnp.float32)
    return pl.pallas_call(
        _experts_kernel,
        out_shape=jax.ShapeDtypeStruct(xs_slab.shape, jnp.float32),
        grid_spec=pltpu.PrefetchScalarGridSpec(
            num_scalar_prefetch=4,
            grid=(nt,),
            in_specs=[
                pl.BlockSpec(memory_space=pl.ANY),
                pl.BlockSpec((1, d, de), lambda i, te, tv, tu, na: (te[i], 0, 0)),
                pl.BlockSpec((1, d, de), lambda i, te, tv, tu, na: (te[i], 0, 0)),
                pl.BlockSpec((1, de, d), lambda i, te, tv, tu, na: (te[i], 0, 0)),
            ],
            out_specs=pl.BlockSpec(memory_space=pl.ANY),
            scratch_shapes=[buf] * (X_BUFS + Y_BUFS) + [
                pltpu.VMEM((d, 2 * de), jnp.bfloat16), pltpu.VMEM((de, d), jnp.bfloat16),
                pltpu.SemaphoreType.DMA((X_BUFS,)), pltpu.SemaphoreType.DMA((Y_BUFS,))],
        ),
        compiler_params=_cparams(("arbitrary",)),
        name="experts",
    )(tile_e, tile_valid, tile_u, n_active, xs_slab, wg, wu, wd)


def _combine_kernel(h_ref, w_ref, *rest):
    y_refs = rest[:TOP_K]
    sgu_ref, sd_ref, g_ref, b_ref, o_ref = rest[TOP_K:]
    tn = h_ref.shape[0]
    h = h_ref[...]
    gu = _dot(h.astype(jnp.bfloat16), sgu_ref[...])
    ds = gu.shape[1] // 2
    gate = gu[:, :ds]
    hid = (gate / (1.0 + jnp.exp(-gate))) * gu[:, ds:]
    shared = _dot(hid.astype(jnp.bfloat16), sd_ref[...])

    w = w_ref[...]
    routed = _slab_load(y_refs[0], (), tn) * w[:, 0:1]
    for k in range(1, TOP_K):
        routed = routed + _slab_load(y_refs[k], (), tn) * w[:, k:k + 1]
    o_ref[...] = _layernorm_f32(ALPHA * h + (routed + shared), g_ref[...], b_ref[...])


def _combine(h1, w_tok, y_slab, sgu, sd, g, b):
    t, d = h1.shape
    tn = min(256, t)
    nblk = t // tn
    ds2 = sgu.shape[1]
    vec = pl.BlockSpec((1, d), lambda i: (0, 0))

    def slot_spec(k):
        return pl.BlockSpec((tn * SLAB, LANES), lambda i: (k * nblk + i, 0))

    return pl.pallas_call(
        _combine_kernel,
        out_shape=jax.ShapeDtypeStruct((t, d), jnp.float32),
        grid=(nblk,),
        in_specs=[
            pl.BlockSpec((tn, d), lambda i: (i, 0)),
            pl.BlockSpec((tn, TOP_K), lambda i: (i, 0)),
            *[slot_spec(k) for k in range(TOP_K)],
            pl.BlockSpec((d, ds2), lambda i: (0, 0)),
            pl.BlockSpec((ds2 // 2, d), lambda i: (0, 0)),
            vec, vec,
        ],
        out_specs=pl.BlockSpec((tn, d), lambda i: (i, 0)),
        compiler_params=_cparams(("parallel",)),
        name="combine_ln2",
    )(h1, w_tok, *([y_slab] * TOP_K), sgu, sd, g, b)


def _rel_bucket(rel):
    nb = N_BUCKETS // 2
    max_exact = nb // 2
    ret = jnp.where(rel > 0, nb, 0)
    n = jnp.abs(rel)
    nf = jnp.maximum(n, 1).astype(jnp.float32)
    large = max_exact + (jnp.log(nf / max_exact) / math.log(MAX_DISTANCE / max_exact) * (nb - max_exact)).astype(jnp.int32)
    large = jnp.minimum(large, nb - 1)
    return ret + jnp.where(n < max_exact, n, large)


def _bias_of_rel(rel_bias, rel):
    return rel_bias.astype(jnp.float32)[_rel_bucket(rel)]


def _toeplitz(vec, nrow, ncol, off):
    lo = off - (nrow - 1)
    v = vec[lo:off + ncol]
    p = v.shape[0] + 1
    v = jnp.concatenate([v, v[:1]], axis=0)
    flat = jnp.tile(v, (nrow + 1, 1))
    base = off - lo
    out = flat[base:base + nrow * (p - 1)].reshape(nrow, p - 1, vec.shape[1])
    return out[:, :ncol]


def _bias_tables_a(rel_bias, sink):
    bias_a = rel_bias[:, :A_Q_HEADS]
    m = jnp.arange(3 * TQ_A + TQ_A - 1, dtype=jnp.int32)
    rel = 2 * TQ_A - 1 - m
    vec = jnp.where((jnp.abs(rel) <= WINDOW)[:, None], _bias_of_rel(bias_a, rel), NEG)
    band = _toeplitz(vec, 3 * TQ_A, TQ_A, 3 * TQ_A - 1)
    mm = jnp.arange(TQ_A + N_META - 1, dtype=jnp.int32)
    meta_first = _toeplitz(_bias_of_rel(bias_a, -1 - mm), N_META, TQ_A, N_META - 1)
    meta_far = jnp.broadcast_to(_bias_of_rel(bias_a, jnp.int32(-2 * MAX_DISTANCE)), (N_META, TQ_A, A_Q_HEADS))
    pad = jnp.full((NK_A - 3 * TQ_A - N_META, TQ_A, A_Q_HEADS), NEG, jnp.float32)
    blocked = jnp.full((TQ_A, TQ_A, A_Q_HEADS), NEG, jnp.float32)
    first = jnp.concatenate([blocked, band[TQ_A:], meta_first, pad])
    middle = jnp.concatenate([band, meta_far, pad])
    last = jnp.concatenate([band[:2 * TQ_A], blocked, meta_far, pad])
    tab = jnp.stack([first, middle, last])
    tab = tab.reshape(3, NK_A, TQ_A, A_Q_HEADS // 2, 2)
    tab = jnp.transpose(tab, (0, 3, 1, 4, 2)).reshape(3, A_Q_HEADS // 2, NK_A, 2 * TQ_A) * LOG2E
    sink_rows = jnp.repeat(sink.astype(jnp.float32).reshape(A_Q_HEADS // 2, 1, 2), TQ_A, axis=2) * LOG2E
    return tab, sink_rows


def _bias_tables_b(rel_bias, s):
    bias_b = rel_bias[:, A_Q_HEADS:]
    near = []
    for d in (-1, 0, 1):
        m = jnp.arange(TQ_B + TK_B - 1, dtype=jnp.int32)
        vec = _bias_of_rel(bias_b, TK_B * d + TK_B - 1 - m)
        near.append(_toeplitz(vec, TK_B, TQ_B, TK_B - 1))
    far_l = jnp.broadcast_to(_bias_of_rel(bias_b, jnp.int32(-TK_B - 1)), (TK_B, TQ_B, B_HEADS))
    far_r = jnp.broadcast_to(_bias_of_rel(bias_b, jnp.int32(TK_B + 1)), (TK_B, TQ_B, B_HEADS))
    tabs = jnp.transpose(jnp.stack([far_l] + near + [far_r]), (3, 0, 1, 2)) * LOG2E
    m = jnp.arange(s + N_META - 1, dtype=jnp.int32)
    vec = _bias_of_rel(bias_b, -1 - m)
    meta = jnp.transpose(_toeplitz(vec, N_META, s, N_META - 1), (2, 0, 1)) * LOG2E
    return tabs, meta


def _prep_w_in(w_in):
    a_w = A_Q_HEADS * HEAD_DIM
    kv = A_KV_HEADS * HEAD_DIM
    bqk = B_HEADS * 2 * HEAD_DIM
    scale = HEAD_DIM ** -0.5
    qa = w_in[:, :a_w] * (scale * LOG2E)
    ka = w_in[:, a_w:a_w + kv]
    va = w_in[:, a_w + kv:a_w + 2 * kv]
    o = a_w + 2 * kv
    qb = w_in[:, o:o + bqk] * (scale * LOG2E)
    kb = w_in[:, o + bqk:o + 2 * bqk]
    vb = w_in[:, o + 2 * bqk:]

    def dup(w):
        return jnp.concatenate([w[:, g * HEAD_DIM:(g + 1) * HEAD_DIM] for g in range(A_KV_HEADS) for _ in range(2)], axis=1)

    w = jnp.concatenate([qa, dup(ka), qb, kb], axis=1).astype(jnp.bfloat16)
    return w, jnp.concatenate([vb, va], axis=1).T.astype(jnp.bfloat16)


def _trunk_front(x, prm):
    bsz, s, d = x.shape
    t = bsz * s
    x2d = x.reshape(t, d)
    proj, vt, vta = _ln_inproj(x2d, prm["ln_emb_g"], prm["ln_emb_b"], prm["w_in"], prm["w_vt"])
    proj3 = proj.reshape(bsz, s, PROJ_COLS)
    oa = _mixer_a(proj3, vta, prm["km_a"], prm["vtm_a"], prm["tab_a"], prm["sink_rows"])
    tabs_b, meta_b = _bias_tables_b(prm["rel_bias"], s)
    ob = _mixer_b(proj3, vt, prm["proj_meta"], prm["vt_meta"], tabs_b, meta_b, prm["lam"].reshape(1).astype(jnp.float32),
                  prm["subln_g"])
    h1, h1_slab = _outproj_ln1(x2d, oa.reshape(t, -1), ob.reshape(t, -1), prm["w_out"], prm["ln_emb_g"],
                               prm["ln_emb_b"], prm["ln1_g"], prm["ln1_b"])

    keys, wts, counts = _router(h1, prm["wr_hi"], prm["wr_lo"], prm["router_bias"])
    n_exp = counts.shape[0]
    n_asg = t * TOP_K
    assert n_asg <= (1 << KEY_SHIFT) and n_asg % LANES == 0
    order = jnp.sort(keys.reshape(n_asg)) & ((1 << KEY_SHIFT) - 1)
    tok = order >> 3
    dst = (order & (TOP_K - 1)) * t + tok
    sc_pad = SC_WORKERS * SC_GRP * SC_NBUF
    assert sc_pad >= TM_MOE and n_asg % sc_pad == 0
    xs = _sc_move_rows(h1_slab.reshape(t, SLAB, LANES), jnp.concatenate([tok, jnp.zeros((sc_pad,), jnp.int32)]),
                       n_asg + sc_pad, scatter=False)
    xs_slab = xs.reshape((n_asg + sc_pad) * SLAB, LANES)
    counts = counts[:, 0].astype(jnp.int32)
    tiles_e = (counts + TM_MOE - 1) // TM_MOE
    tend = jnp.cumsum(tiles_e)
    tstart = tend - tiles_e
    ustart = jnp.cumsum(counts) - counts
    nt = n_asg // TM_MOE + n_exp + 1
    tid = jnp.arange(nt, dtype=jnp.int32)
    tile_e = jnp.minimum(jnp.sum((tend[None, :] <= tid[:, None]).astype(jnp.int32), axis=1), n_exp - 1)
    onehot = (tile_e[:, None] == jnp.arange(n_exp, dtype=jnp.int32)[None, :]).astype(jnp.int32)
    in_e = (tid - jnp.sum(onehot * tstart[None, :], axis=1)) * TM_MOE
    active = tid < tend[-1]
    tile_valid = jnp.where(active, jnp.clip(jnp.sum(onehot * counts[None, :], axis=1) - in_e, 0, TM_MOE), 0)
    tile_u = jnp.where(active, jnp.sum(onehot * ustart[None, :], axis=1) + in_e, 0)

    tiles = (tile_e.astype(jnp.int32), tile_valid.astype(jnp.int32), tile_u.astype(jnp.int32),
             tend[-1:].astype(jnp.int32))
    return h1, wts.T, xs_slab, dst, tiles, x.shape


def _trunk_back(front, prm):
    h1, w_tok, xs_slab, dst, tiles, shape = front
    n_asg = dst.shape[0]
    ys_slab = _experts(xs_slab, prm["w_gate"], prm["w_up"], prm["w_down"], *tiles)
    y = _sc_move_rows(ys_slab.reshape(-1, SLAB, LANES), dst, n_asg, scatter=True)
    out = _combine(h1, w_tok, y.reshape(n_asg * SLAB, LANES), prm["ws_gu"], prm["ws_down"], prm["ln2_g"], prm["ln2_b"])
    return out.reshape(shape)


def kernel(x_prompt, x_sample, meta_tokens, ln_emb_g, ln_emb_b, rel_bias, w_in, attn_sink, lambda_q1, lambda_k1, lambda_q2, lambda_k2, subln_g, w_out, ln1_g, ln1_b, w_router, router_bias, w_gate, w_up, w_down, ws_gate, ws_up, ws_down, ln2_g, ln2_b):
    f32 = jnp.float32
    bf16 = jnp.bfloat16
    l = 0
    row = lambda v: v.reshape(1, -1).astype(f32)
    lam = (jnp.exp(jnp.sum(lambda_q1[l].astype(f32) * lambda_k1[l].astype(f32)))
           - jnp.exp(jnp.sum(lambda_q2[l].astype(f32) * lambda_k2[l].astype(f32))) + LAMBDA_INIT)
    wr_t = w_router[l].astype(f32).T
    wr_hi = wr_t.astype(bf16)
    prm = {
        "ln_emb_g": row(ln_emb_g), "ln_emb_b": row(ln_emb_b),
        "rel_bias": rel_bias,
        "lam": lam,
        "subln_g": subln_g[l].astype(f32).reshape(-1, 1),
        "w_out": w_out[l].astype(bf16),
        "ln1_g": row(ln1_g[l]), "ln1_b": row(ln1_b[l]),
        "wr_hi": wr_hi, "wr_lo": (wr_t - wr_hi.astype(f32)).astype(bf16),
        "router_bias": router_bias[l].astype(f32).reshape(-1, 1),
        "w_gate": w_gate[l], "w_up": w_up[l], "w_down": w_down[l],
        "ws_gu": jnp.concatenate([ws_gate[l], ws_up[l]], axis=-1).astype(bf16),
        "ws_down": ws_down[l].astype(bf16),
        "ln2_g": row(ln2_g[l]), "ln2_b": row(ln2_b[l]),
    }
    prm["w_in"], prm["w_vt"] = _prep_w_in(w_in[l])
    prm["proj_meta"], prm["vt_meta"], vta_meta = _ln_inproj(meta_tokens.astype(f32), prm["ln_emb_g"], prm["ln_emb_b"],
                                                           prm["w_in"], prm["w_vt"])
    meta_pad = NK_A - 3 * TQ_A - N_META
    prm["km_a"] = jnp.pad(prm["proj_meta"][:, KA_BLK * LANES:(KA_BLK + 2) * LANES], ((0, meta_pad), (0, 0)))
    prm["vtm_a"] = jnp.pad(vta_meta, ((0, 0), (0, meta_pad)))
    prm["tab_a"], prm["sink_rows"] = _bias_tables_a(rel_bias, attn_sink[l])
    front_p = _trunk_front(x_prompt, prm)
    front_s = _trunk_front(x_sample, prm)
    return (_trunk_back(front_p, prm), _trunk_back(front_s, prm))
```

```python
import functools
import math

import jax
import jax.numpy as jnp
from jax import lax
from jax.experimental import pallas as pl
from jax.experimental.pallas import tpu as pltpu
from jax.experimental.pallas import tpu_sc as plsc

N_META = 16
HEAD_DIM = 64
WINDOW = 128
A_Q_HEADS = 8
A_KV_HEADS = 2
B_HEADS = 4
N_BUCKETS = 32
MAX_DISTANCE = 128
TOP_K = 8
N_GROUPS = 8
TOPK_GROUPS = 4
ROUTED_SCALE = 2.5
LN_EPS = 1e-5
DEPTH = 1
ALPHA = (2 * DEPTH) ** 0.25
NEG = -1e30
LAMBDA_INIT = 0.8 - 0.6 * math.exp(-0.3 * 0)

LANES = 128
VMEM_LIMIT = 48 * 1024 * 1024

QA_BLK = 0
KA_BLK = 4
QB_BLK = 6
KB_BLK = 10
PROJ_COLS = 14 * LANES
VT_ROWS = LANES + 16
VA_ROWS = HEAD_DIM + 16
LOG2E = 1.4426950408889634

TQ_A = 128
NK_A = 4 * TQ_A
TQ_B = 512
TK_B = 512
TM_MOE = 512
SLAB = 8
SC_GRP = 16
SC_NBUF = 2
SC_WORKERS = 32
KEY_SHIFT = 20


def _cparams(sem):
    return pltpu.CompilerParams(dimension_semantics=sem, vmem_limit_bytes=VMEM_LIMIT)


def _layernorm_f32(x, g, b):
    mu = jnp.mean(x, axis=-1, keepdims=True)
    xc = x - mu
    var = jnp.mean(xc * xc, axis=-1, keepdims=True)
    return xc * lax.rsqrt(var + LN_EPS) * g + b


def _dot_nt(a, b):
    return lax.dot_general(a, b, (((1,), (1,)), ((), ())), preferred_element_type=jnp.float32)


def _dot(a, b):
    return jnp.dot(a, b, preferred_element_type=jnp.float32)


def _ln_inproj_kernel(x_ref, g_ref, b_ref, w_ref, wvt_ref, o_ref, vt_ref, vta_ref):
    h = _layernorm_f32(x_ref[...], g_ref[...], b_ref[...]).astype(jnp.bfloat16)
    o_ref[...] = _dot(h, w_ref[...]).astype(o_ref.dtype)
    vt = _dot_nt(wvt_ref[...], h).astype(vt_ref.dtype)
    ones = jnp.ones((16, vt.shape[1]), vt_ref.dtype)
    for hd in range(B_HEADS):
        vt_ref[hd * VT_ROWS:hd * VT_ROWS + LANES, :] = vt[hd * LANES:(hd + 1) * LANES, :]
        vt_ref[hd * VT_ROWS + LANES:(hd + 1) * VT_ROWS, :] = ones
    base = B_HEADS * LANES
    for g in range(A_KV_HEADS):
        vta_ref[g * VA_ROWS:g * VA_ROWS + HEAD_DIM, :] = vt[base + g * HEAD_DIM:base + (g + 1) * HEAD_DIM, :]
        vta_ref[g * VA_ROWS + HEAD_DIM:(g + 1) * VA_ROWS, :] = ones


def _ln_inproj(x2d, g, b, w, wvt):
    t, d = x2d.shape
    n = w.shape[1]
    tm = min(512, t)
    return pl.pallas_call(
        _ln_inproj_kernel,
        out_shape=(jax.ShapeDtypeStruct((t, n), jnp.bfloat16),
                   jax.ShapeDtypeStruct((B_HEADS * VT_ROWS, t), jnp.bfloat16),
                   jax.ShapeDtypeStruct((A_KV_HEADS * VA_ROWS, t), jnp.bfloat16)),
        grid=(t // tm,),
        in_specs=[
            pl.BlockSpec((tm, d), lambda i: (i, 0)),
            pl.BlockSpec((1, d), lambda i: (0, 0)),
            pl.BlockSpec((1, d), lambda i: (0, 0)),
            pl.BlockSpec((d, n), lambda i: (0, 0)),
            pl.BlockSpec((wvt.shape[0], d), lambda i: (0, 0)),
        ],
        out_specs=(pl.BlockSpec((tm, n), lambda i: (i, 0)),
                   pl.BlockSpec((B_HEADS * VT_ROWS, tm), lambda i: (0, i)),
                   pl.BlockSpec((A_KV_HEADS * VA_ROWS, tm), lambda i: (0, i))),
        compiler_params=_cparams(("parallel",)),
        name="ln_inproj",
    )(x2d, g, b, w, wvt)


def _mixer_a_kernel(q_ref, k_ref, vt_ref, km_ref, vtm_ref, tab_ref, sink_ref, o_ref, *, nblk, sub):
    i = pl.program_id(1)
    lane = lax.broadcasted_iota(jnp.int32, (1, LANES), 1)
    lo = lane < HEAD_DIM
    for j in range(sub):
        gi = i * sub + j
        sp = pl.multiple_of(jnp.maximum(gi - 1, 0) * TQ_A, TQ_A)
        sc = pl.multiple_of(gi * TQ_A, TQ_A)
        sn = pl.multiple_of(jnp.minimum(gi + 1, nblk - 1) * TQ_A, TQ_A)
        variant = jnp.where(gi == 0, 0, jnp.where(gi == nblk - 1, 2, 1))
        rows = slice(j * TQ_A, (j + 1) * TQ_A)
        for g in range(A_KV_HEADS):
            gs = slice(g * LANES, (g + 1) * LANES)
            vr = slice(g * VA_ROWS, (g + 1) * VA_ROWS)
            k_all = jnp.concatenate([k_ref[0, pl.ds(sp, TQ_A), gs], k_ref[0, pl.ds(sc, TQ_A), gs],
                                     k_ref[0, pl.ds(sn, TQ_A), gs], km_ref[:, gs]], axis=0)
            vt_all = jnp.concatenate([vt_ref[vr, pl.ds(sp, TQ_A)], vt_ref[vr, pl.ds(sc, TQ_A)],
                                      vt_ref[vr, pl.ds(sn, TQ_A)], vtm_ref[vr, :]], axis=1)
            for pp in range(2):
                hp = 2 * g + pp
                cols = slice(hp * LANES, (hp + 1) * LANES)
                qc = q_ref[0, rows, cols]
                q2 = jnp.concatenate([jnp.where(lo, qc, jnp.zeros_like(qc)), jnp.where(lo, jnp.zeros_like(qc), qc)],
                                     axis=0)
                s = _dot_nt(k_all, q2) + tab_ref[variant, hp]
                sink = sink_ref[hp]
                m = jnp.maximum(jnp.max(s, axis=0, keepdims=True), sink)
                p = jnp.exp2(s - m)
                acc = _dot(vt_all, p.astype(jnp.bfloat16))
                o = acc[0:HEAD_DIM, :] / (acc[HEAD_DIM:HEAD_DIM + 1, :] + jnp.exp2(sink - m))
                o2 = jnp.concatenate([o[:, :TQ_A], o[:, TQ_A:]], axis=0)
                o_ref[0, rows, cols] = o2.T.astype(o_ref.dtype)


def _mixer_a(proj3, vta, km, vtm, tab, sink):
    bsz, s, _ = proj3.shape
    nblk = s // TQ_A
    assert nblk >= 2
    sub = 4 if nblk % 4 == 0 else 1
    nq = nblk // sub
    tq = sub * TQ_A
    kern = functools.partial(_mixer_a_kernel, nblk=nblk, sub=sub)
    return pl.pallas_call(
        kern,
        out_shape=jax.ShapeDtypeStruct((bsz, s, A_Q_HEADS * HEAD_DIM), jnp.bfloat16),
        grid=(bsz, nq),
        in_specs=[
            pl.BlockSpec((1, tq, 4 * LANES), lambda b, i: (b, i, QA_BLK // 4)),
            pl.BlockSpec((1, s, 2 * LANES), lambda b, i: (b, 0, KA_BLK // 2)),
            pl.BlockSpec((A_KV_HEADS * VA_ROWS, s), lambda b, i: (0, b)),
            pl.BlockSpec(km.shape, lambda b, i: (0, 0)),
            pl.BlockSpec(vtm.shape, lambda b, i: (0, 0)),
            pl.BlockSpec(tab.shape, lambda b, i: (0, 0, 0, 0)),
            pl.BlockSpec(sink.shape, lambda b, i: (0, 0, 0)),
        ],
        out_specs=pl.BlockSpec((1, tq, 4 * LANES), lambda b, i: (b, i, 0)),
        compiler_params=_cparams(("parallel", "arbitrary")),
        name="mixer_a",
    )(proj3, proj3, vta, km, vtm, tab, sink)


def _mixer_b_kernel(sc_ref, q_ref, k_ref, vt_ref, km_ref, vtm_ref, tab_ref, tabm_ref, g_ref, o_ref,
                    m_ref, acc_ref, s0_ref, s1_ref, *, nk):
    h = pl.program_id(1)
    i = pl.program_id(2)
    lane = lax.broadcasted_iota(jnp.int32, (1, LANES), 1)
    lo = lane < HEAD_DIM
    q = q_ref[0]
    qs = (jnp.where(lo, q, jnp.zeros_like(q)), jnp.where(lo, jnp.zeros_like(q), q))
    lam = sc_ref[0]
    sbufs = (s0_ref, s1_ref)

    km = km_ref[...]
    vtm = vtm_ref[...]
    for c in range(2):
        s = _dot_nt(km, qs[c]) + tabm_ref[0]
        m = jnp.max(s, axis=0, keepdims=True)
        p = jnp.exp2(s - m)
        m_ref[c] = m
        acc_ref[c] = _dot(vtm, p.astype(jnp.bfloat16))

    def scores(j, slot):
        start = pl.multiple_of(j * TK_B, TK_B)
        kj = k_ref[0, pl.ds(start, TK_B), :]
        bias = tab_ref[0, jnp.clip(j - i, -2, 2) + 2]
        for c in range(2):
            sbufs[slot][c] = _dot_nt(kj, qs[c]) + bias

    def accumulate(j, slot):
        start = pl.multiple_of(j * TK_B, TK_B)
        vtj = vt_ref[:, pl.ds(start, TK_B)]
        for c in range(2):
            s = sbufs[slot][c]
            m_prev = m_ref[c]
            m_new = jnp.maximum(m_prev, jnp.max(s, axis=0, keepdims=True))
            a = jnp.exp2(m_prev - m_new)
            p = jnp.exp2(s - m_new)
            m_ref[c] = m_new
            acc_ref[c] = a * acc_ref[c] + _dot(vtj, p.astype(jnp.bfloat16))

    scores(0, 0)

    def pair(jj, carry):
        j = 2 * jj
        scores(j + 1, 1)
        accumulate(j, 0)
        scores(j + 2, 0)
        accumulate(j + 1, 1)
        return carry

    lax.fori_loop(0, nk // 2 - 1, pair, 0)
    scores(nk - 1, 1)
    accumulate(nk - 2, 0)
    accumulate(nk - 1, 1)

    o0 = acc_ref[0, 0:LANES, :] / acc_ref[0, LANES:LANES + 1, :]
    o1 = acc_ref[1, 0:LANES, :] / acc_ref[1, LANES:LANES + 1, :]
    o = o0 - lam * o1
    ms = jnp.mean(o * o, axis=0, keepdims=True)
    o = o * lax.rsqrt(ms + LN_EPS) * (g_ref[...] * (1.0 - LAMBDA_INIT))
    o_ref[0] = o.T.astype(o_ref.dtype)


def _mixer_b(proj3, vt, proj_meta, vt_meta, tab, tabm, scal, subln_g):
    bsz, s, _ = proj3.shape
    nq = s // TQ_B
    nk = s // TK_B
    assert nk % 2 == 0
    kern = functools.partial(_mixer_b_kernel, nk=nk)
    return pl.pallas_call(
        kern,
        out_shape=jax.ShapeDtypeStruct((bsz, s, B_HEADS * LANES), jnp.bfloat16),
        grid=(bsz, B_HEADS, nq),
        in_specs=[
            pl.BlockSpec(memory_space=pltpu.SMEM),
            pl.BlockSpec((1, TQ_B, LANES), lambda b, h, i: (b, i, QB_BLK + h)),
            pl.BlockSpec((1, s, LANES), lambda b, h, i: (b, 0, KB_BLK + h)),
            pl.BlockSpec((VT_ROWS, s), lambda b, h, i: (h, b)),
            pl.BlockSpec((N_META, LANES), lambda b, h, i: (0, KB_BLK + h)),
            pl.BlockSpec((VT_ROWS, N_META), lambda b, h, i: (h, 0)),
            pl.BlockSpec((1, 5, TK_B, TQ_B), lambda b, h, i: (h, 0, 0, 0)),
            pl.BlockSpec((1, N_META, TQ_B), lambda b, h, i: (h, 0, i)),
            pl.BlockSpec((LANES, 1), lambda b, h, i: (0, 0)),
        ],
        out_specs=pl.BlockSpec((1, TQ_B, LANES), lambda b, h, i: (b, i, h)),
        scratch_shapes=[
            pltpu.VMEM((2, 1, TQ_B), jnp.float32),
            pltpu.VMEM((2, VT_ROWS, TQ_B), jnp.float32),
            pltpu.VMEM((2, TK_B, TQ_B), jnp.float32),
            pltpu.VMEM((2, TK_B, TQ_B), jnp.float32),
        ],
        compiler_params=_cparams(("parallel", "parallel", "arbitrary")),
        name="mixer_b",
    )(scal, proj3, proj3, vt, proj_meta, vt_meta, tab, tabm, subln_g)


def _slab_load(ref, lead, rows):
    return jnp.concatenate([ref[lead + (pl.ds(s, rows, stride=SLAB), slice(None))] for s in range(SLAB)], axis=1)


def _slab_store(ref, lead, val):
    rows = val.shape[0]
    for s in range(SLAB):
        ref[lead + (pl.ds(s, rows, stride=SLAB), slice(None))] = val[:, s * LANES:(s + 1) * LANES]


def _outproj_kernel(x_ref, oa_ref, ob_ref, w_ref, eg_ref, eb_ref, g_ref, b_ref, o_ref, os_ref):
    h0 = _layernorm_f32(x_ref[...], eg_ref[...], eb_ref[...])
    half = oa_ref.shape[1]
    mix = _dot(oa_ref[...], w_ref[0:half, :]) + _dot(ob_ref[...], w_ref[half:, :])
    h1 = _layernorm_f32(ALPHA * h0 + mix, g_ref[...], b_ref[...])
    o_ref[...] = h1
    _slab_store(os_ref, (), h1)


def _outproj_ln1(x2d, oa, ob, w_out, eg, eb, g, b):
    t, d = x2d.shape
    tm = min(512, t)
    wa = oa.shape[1]
    wb = ob.shape[1]
    vec = pl.BlockSpec((1, d), lambda i: (0, 0))
    return pl.pallas_call(
        _outproj_kernel,
        out_shape=(jax.ShapeDtypeStruct((t, d), jnp.float32),
                   jax.ShapeDtypeStruct((t * SLAB, LANES), jnp.float32)),
        grid=(t // tm,),
        in_specs=[
            pl.BlockSpec((tm, d), lambda i: (i, 0)),
            pl.BlockSpec((tm, wa), lambda i: (i, 0)),
            pl.BlockSpec((tm, wb), lambda i: (i, 0)),
            pl.BlockSpec((wa + wb, d), lambda i: (0, 0)),
            vec, vec, vec, vec,
        ],
        out_specs=(pl.BlockSpec((tm, d), lambda i: (i, 0)),
                   pl.BlockSpec((tm * SLAB, LANES), lambda i: (i, 0))),
        compiler_params=_cparams(("parallel",)),
        name="outproj_ln1",
    )(x2d, oa, ob, w_out, eg, eb, g, b)


def _router_kernel(h_ref, wh_ref, wl_ref, rb_ref, e_ref, w_ref, cnt_ref, carry_ref, *, n_exp):
    i = pl.program_id(0)
    tn = h_ref.shape[0]
    gsz = n_exp // N_GROUPS

    @pl.when(i == 0)
    def _():
        carry_ref[...] = jnp.zeros_like(carry_ref)

    x = h_ref[...]
    xh = x.astype(jnp.bfloat16)
    xl = (x - xh.astype(jnp.float32)).astype(jnp.bfloat16)
    logits = _dot_nt(wh_ref[...], xh) + (_dot_nt(wh_ref[...], xl) + _dot_nt(wl_ref[...], xh))
    scores = 1.0 / (1.0 + jnp.exp(-logits))
    biased = scores + rb_ref[...]

    g3 = biased.reshape(N_GROUPS, gsz, tn)
    it3 = lax.broadcasted_iota(jnp.int32, (N_GROUPS, gsz, tn), 1)
    mx1 = jnp.max(g3, axis=1, keepdims=True)
    first = jnp.min(jnp.where(g3 == mx1, it3, gsz), axis=1, keepdims=True)
    mx2 = jnp.max(jnp.where(it3 == first, -jnp.inf, g3), axis=1, keepdims=True)
    gscore = (mx1 + mx2).reshape(N_GROUPS, tn)

    itg = lax.broadcasted_iota(jnp.int32, (N_GROUPS, tn), 0)
    gsel = jnp.zeros((N_GROUPS, tn), jnp.bool_)
    cur = gscore
    for _ in range(TOPK_GROUPS):
        mx = jnp.max(cur, axis=0, keepdims=True)
        fi = jnp.min(jnp.where(cur == mx, itg, N_GROUPS), axis=0, keepdims=True)
        hit = itg == fi
        gsel = jnp.logical_or(gsel, hit)
        cur = jnp.where(hit, -jnp.inf, cur)
    emask = jnp.broadcast_to(gsel.reshape(N_GROUPS, 1, tn), (N_GROUPS, gsz, tn)).reshape(n_exp, tn)
    cur = jnp.where(emask, biased, NEG)

    ite = lax.broadcasted_iota(jnp.int32, (n_exp, tn), 0)
    hits = []
    eidx = []
    wsel = []
    for _ in range(TOP_K):
        mx = jnp.max(cur, axis=0, keepdims=True)
        fi = jnp.min(jnp.where(cur == mx, ite, n_exp), axis=0, keepdims=True)
        hit = ite == fi
        hits.append(hit)
        eidx.append(fi)
        wsel.append(jnp.sum(jnp.where(hit, scores, 0.0), axis=0, keepdims=True))
        cur = jnp.where(hit, -jnp.inf, cur)
    sel = hits[0]
    for hit in hits[1:]:
        sel = jnp.logical_or(sel, hit)
    self32 = jnp.where(sel, 1.0, 0.0)

    carry_ref[...] = carry_ref[...] + jnp.sum(self32, axis=1, keepdims=True)
    cnt_ref[...] = carry_ref[...]

    wcat = jnp.concatenate(wsel, axis=0)
    wcat = wcat / jnp.sum(wcat, axis=0, keepdims=True) * ROUTED_SCALE
    tok = i * tn + lax.broadcasted_iota(jnp.int32, (TOP_K, tn), 1)
    slot = lax.broadcasted_iota(jnp.int32, (TOP_K, tn), 0)
    e_ref[...] = jnp.concatenate(eidx, axis=0) * (1 << KEY_SHIFT) + (tok * TOP_K + slot)
    w_ref[...] = wcat


def _router(h1, wr_hi, wr_lo, rbias):
    t, d = h1.shape
    n_exp = wr_hi.shape[0]
    tn = min(512, t)
    kern = functools.partial(_router_kernel, n_exp=n_exp)
    row = pl.BlockSpec((TOP_K, tn), lambda i: (0, i))
    return pl.pallas_call(
        kern,
        out_shape=(
            jax.ShapeDtypeStruct((TOP_K, t), jnp.int32),
            jax.ShapeDtypeStruct((TOP_K, t), jnp.float32),
            jax.ShapeDtypeStruct((n_exp, 1), jnp.float32),
        ),
        grid=(t // tn,),
        in_specs=[
            pl.BlockSpec((tn, d), lambda i: (i, 0)),
            pl.BlockSpec((n_exp, d), lambda i: (0, 0)),
            pl.BlockSpec((n_exp, d), lambda i: (0, 0)),
            pl.BlockSpec((n_exp, 1), lambda i: (0, 0)),
        ],
        out_specs=(row, row, pl.BlockSpec((n_exp, 1), lambda i: (0, 0))),
        scratch_shapes=[pltpu.VMEM((n_exp, 1), jnp.float32)],
        compiler_params=_cparams(("arbitrary",)),
        name="router",
    )(h1, wr_hi, wr_lo, rbias)


def _sc_move_rows(x_slab, idx, n_out, scatter):
    m = idx.shape[0]
    rnd = SC_GRP * SC_NBUF
    per = m // SC_WORKERS
    nr = per // rnd
    assert m % (SC_WORKERS * rnd * 2) == 0
    row = x_slab.shape[1:]
    mesh = plsc.VectorSubcoreMesh(core_axis_name="c", subcore_axis_name="s")

    @pl.kernel(out_type=jax.ShapeDtypeStruct((n_out,) + row, x_slab.dtype), mesh=mesh,
               scratch_types=[pltpu.VMEM((per,), jnp.int32), pltpu.VMEM((2, SC_NBUF, SC_GRP) + row, x_slab.dtype),
                              pltpu.SemaphoreType.DMA((2, SC_NBUF)), pltpu.SemaphoreType.DMA((2, SC_NBUF))])
    def kern(x_hbm, i_hbm, o_hbm, ibuf, buf, lsem, ssem):
        worker = lax.axis_index("c") * (SC_WORKERS // 2) + lax.axis_index("s")
        base = worker * per
        pltpu.sync_copy(i_hbm.at[pl.ds(base, per)], ibuf)

        def copies(r, s):
            lds, sts = [], []
            for b in range(SC_NBUF):
                indexed = ibuf.at[pl.ds(r * rnd + b * SC_GRP, SC_GRP)]
                linear = pl.ds(base + r * rnd + b * SC_GRP, SC_GRP)
                src = x_hbm.at[linear] if scatter else x_hbm.at[indexed]
                dst = o_hbm.at[indexed] if scatter else o_hbm.at[linear]
                lds.append(pltpu.make_async_copy(src, buf.at[s, b], lsem.at[s, b]))
                sts.append(pltpu.make_async_copy(buf.at[s, b], dst, ssem.at[s, b]))
            return lds, sts

        def start(cs):
            for c in cs:
                c.start()

        def wait(cs):
            for c in cs:
                c.wait()

        start(copies(0, 0)[0])

        @pl.loop(0, nr // 2)
        def _(q):
            ra = 2 * q
            rb = ra + 1

            @pl.when(q > 0)
            def _():
                wait(copies(ra - 1, 1)[1])

            start(copies(rb, 1)[0])
            lds, sts = copies(ra, 0)
            wait(lds)
            start(sts)
            wait(sts)

            @pl.when(q < nr // 2 - 1)
            def _():
                start(copies(rb + 1, 0)[0])

            lds, sts = copies(rb, 1)
            wait(lds)
            start(sts)

        wait(copies(nr - 1, 1)[1])

    return kern(x_slab, idx)


X_BUFS = 3
Y_BUFS = 2


def _experts_kernel(te_ref, tv_ref, tu_ref, na_ref, xs_hbm, wg_ref, wu_ref, wd_ref, ys_hbm,
                    xb0, xb1, xb2, yb0, yb1, wgu_s, wd_s, xsem, ysem):
    i = pl.program_id(0)
    nt = pl.num_programs(0)
    n_act = na_ref[0]
    xbufs = (xb0, xb1, xb2)
    ybufs = (yb0, yb1)
    tile_rows = TM_MOE * SLAB

    def x_copy(slot, tile):
        tile = jnp.minimum(tile, nt - 1)
        start = pl.multiple_of(tu_ref[tile] * SLAB, SLAB)
        return pltpu.make_async_copy(xs_hbm.at[pl.ds(start, tile_rows), :], xbufs[slot], xsem.at[slot])

    def y_pieces(slot, tile, go):
        nvalid = tv_ref[tile]
        base = tu_ref[tile]
        size = TM_MOE
        while size >= 1:
            @pl.when((nvalid & size) != 0)
            def _(size=size):
                off = nvalid & ~(2 * size - 1)
                src = ybufs[slot].at[pl.ds(pl.multiple_of(off * SLAB, SLAB), size * SLAB), :]
                dst = ys_hbm.at[pl.ds(pl.multiple_of((base + off) * SLAB, SLAB), size * SLAB), :]
                go(pltpu.make_async_copy(src, dst, ysem.at[slot]))
            size //= 2

    @pl.when(i == 0)
    def _():
        x_copy(0, 0).start()
        x_copy(1, 1).start()

    iprev = jnp.maximum(i - 1, 0)

    @pl.when(jnp.logical_and(i < n_act, jnp.logical_or(i == 0, te_ref[i] != te_ref[iprev])))
    def _():
        de = wg_ref.shape[2]
        wgu_s[:, 0:de] = wg_ref[0].astype(wgu_s.dtype)
        wgu_s[:, de:] = wu_ref[0].astype(wgu_s.dtype)
        wd_s[...] = wd_ref[0].astype(wd_s.dtype)

    phase = i % (X_BUFS * Y_BUFS)
    for c in range(X_BUFS * Y_BUFS):
        xs, ys = c % X_BUFS, c % Y_BUFS

        @pl.when(jnp.logical_and(i < n_act, phase == c))
        def _(xs=xs, ys=ys):
            x_copy(xs, i).wait()
            x_copy((xs + 2) % X_BUFS, i + 2).start()
            x = _slab_load(xbufs[xs], (), TM_MOE).astype(jnp.bfloat16)
            gu = _dot(x, wgu_s[...])
            de = gu.shape[1] // 2
            gate = gu[:, :de]
            hid = (gate / (1.0 + jnp.exp(-gate))) * gu[:, de:]
            y = _dot(hid.astype(jnp.bfloat16), wd_s[...])

            @pl.when(i >= Y_BUFS)
            def _():
                y_pieces(ys, i - Y_BUFS, lambda cp: cp.wait())

            _slab_store(ybufs[ys], (), y)
            y_pieces(ys, i, lambda cp: cp.start(priority=1))

        @pl.when(jnp.logical_and(i == n_act, phase == c))
        def _(xs=xs, ys=ys):
            x_copy(xs, i).wait()
            x_copy((xs + 1) % X_BUFS, i + 1).wait()

            @pl.when(i >= 2)
            def _():
                y_pieces(ys, i - 2, lambda cp: cp.wait())

            @pl.when(i >= 1)
            def _():
                y_pieces(1 - ys, i - 1, lambda cp: cp.wait())


def _experts(xs_slab, wg, wu, wd, tile_e, tile_valid, tile_u, n_active):
    nt = tile_e.shape[0]
    d = wg.shape[1]
    de = wg.shape[2]
    buf = pltpu.VMEM((TM_MOE * SLAB, LANES), jnp.float32)
    return pl.pallas_call(
        _experts_kernel,
        out_shape=jax.ShapeDtypeStruct(xs_slab.shape, jnp.float32),
        grid_spec=pltpu.PrefetchScalarGridSpec(
            num_scalar_prefetch=4,
            grid=(nt,),
            in_specs=[
                pl.BlockSpec(memory_space=pl.ANY),
                pl.BlockSpec((1, d, de), lambda i, te, tv, tu, na: (te[i], 0, 0)),
                pl.BlockSpec((1, d, de), lambda i, te, tv, tu, na: (te[i], 0, 0)),
                pl.BlockSpec((1, de, d), lambda i, te, tv, tu, na: (te[i], 0, 0)),
            ],
            out_specs=pl.BlockSpec(memory_space=pl.ANY),
            scratch_shapes=[buf] * (X_BUFS + Y_BUFS) + [
                pltpu.VMEM((d, 2 * de), jnp.bfloat16), pltpu.VMEM((de, d), jnp.bfloat16),
                pltpu.SemaphoreType.DMA((X_BUFS,)), pltpu.SemaphoreType.DMA((Y_BUFS,))],
        ),
        compiler_params=_cparams(("arbitrary",)),
        name="experts",
    )(tile_e, tile_valid, tile_u, n_active, xs_slab, wg, wu, wd)


def _combine_kernel(h_ref, w_ref, *rest):
    y_refs = rest[:TOP_K]
    sgu_ref, sd_ref, g_ref, b_ref, o_ref = rest[TOP_K:]
    tn = h_ref.shape[0]
    h = h_ref[...]
    gu = _dot(h.astype(jnp.bfloat16), sgu_ref[...])
    ds = gu.shape[1] // 2
    gate = gu[:, :ds]
    hid = (gate / (1.0 + jnp.exp(-gate))) * gu[:, ds:]
    shared = _dot(hid.astype(jnp.bfloat16), sd_ref[...])

    w = w_ref[...]
    routed = _slab_load(y_refs[0], (), tn) * w[:, 0:1]
    for k in range(1, TOP_K):
        routed = routed + _slab_load(y_refs[k], (), tn) * w[:, k:k + 1]
    o_ref[...] = _layernorm_f32(ALPHA * h + (routed + shared), g_ref[...], b_ref[...])


def _combine(h1, w_tok, y_slab, sgu, sd, g, b):
    t, d = h1.shape
    tn = min(256, t)
    nblk = t // tn
    ds2 = sgu.shape[1]
    vec = pl.BlockSpec((1, d), lambda i: (0, 0))

    def slot_spec(k):
        return pl.BlockSpec((tn * SLAB, LANES), lambda i: (k * nblk + i, 0))

    return pl.pallas_call(
        _combine_kernel,
        out_shape=jax.ShapeDtypeStruct((t, d), jnp.float32),
        grid=(nblk,),
        in_specs=[
            pl.BlockSpec((tn, d), lambda i: (i, 0)),
            pl.BlockSpec((tn, TOP_K), lambda i: (i, 0)),
            *[slot_spec(k) for k in range(TOP_K)],
            pl.BlockSpec((d, ds2), lambda i: (0, 0)),
            pl.BlockSpec((ds2 // 2, d), lambda i: (0, 0)),
            vec, vec,
        ],
        out_specs=pl.BlockSpec((tn, d), lambda i: (i, 0)),
        compiler_params=_cparams(("parallel",)),
        name="combine_ln2",
    )(h1, w_tok, *([y_slab] * TOP_K), sgu, sd, g, b)


def _rel_bucket(rel):
    nb = N_BUCKETS // 2
    max_exact = nb // 2
    ret = jnp.where(rel > 0, nb, 0)
    n = jnp.abs(rel)
    nf = jnp.maximum(n, 1).astype(jnp.float32)
    large = max_exact + (jnp.log(nf / max_exact) / math.log(MAX_DISTANCE / max_exact) * (nb - max_exact)).astype(jnp.int32)
    large = jnp.minimum(large, nb - 1)
    return ret + jnp.where(n < max_exact, n, large)


def _bias_of_rel(rel_bias, rel):
    return rel_bias.astype(jnp.float32)[_rel_bucket(rel)]


def _toeplitz(vec, nrow, ncol, off):
    lo = off - (nrow - 1)
    v = vec[lo:off + ncol]
    p = v.shape[0] + 1
    v = jnp.concatenate([v, v[:1]], axis=0)
    flat = jnp.tile(v, (nrow + 1, 1))
    base = off - lo
    out = flat[base:base + nrow * (p - 1)].reshape(nrow, p - 1, vec.shape[1])
    return out[:, :ncol]


def _bias_tables_a(rel_bias, sink):
    bias_a = rel_bias[:, :A_Q_HEADS]
    m = jnp.arange(3 * TQ_A + TQ_A - 1, dtype=jnp.int32)
    rel = 2 * TQ_A - 1 - m
    vec = jnp.where((jnp.abs(rel) <= WINDOW)[:, None], _bias_of_rel(bias_a, rel), NEG)
    band = _toeplitz(vec, 3 * TQ_A, TQ_A, 3 * TQ_A - 1)
    mm = jnp.arange(TQ_A + N_META - 1, dtype=jnp.int32)
    meta_first = _toeplitz(_bias_of_rel(bias_a, -1 - mm), N_META, TQ_A, N_META - 1)
    meta_far = jnp.broadcast_to(_bias_of_rel(bias_a, jnp.int32(-2 * MAX_DISTANCE)), (N_META, TQ_A, A_Q_HEADS))
    pad = jnp.full((NK_A - 3 * TQ_A - N_META, TQ_A, A_Q_HEADS), NEG, jnp.float32)
    blocked = jnp.full((TQ_A, TQ_A, A_Q_HEADS), NEG, jnp.float32)
    first = jnp.concatenate([blocked, band[TQ_A:], meta_first, pad])
    middle = jnp.concatenate([band, meta_far, pad])
    last = jnp.concatenate([band[:2 * TQ_A], blocked, meta_far, pad])
    tab = jnp.stack([first, middle, last])
    tab = tab.reshape(3, NK_A, TQ_A, A_Q_HEADS // 2, 2)
    tab = jnp.transpose(tab, (0, 3, 1, 4, 2)).reshape(3, A_Q_HEADS // 2, NK_A, 2 * TQ_A) * LOG2E
    sink_rows = jnp.repeat(sink.astype(jnp.float32).reshape(A_Q_HEADS // 2, 1, 2), TQ_A, axis=2) * LOG2E
    return tab, sink_rows


def _bias_tables_b(rel_bias, s):
    bias_b = rel_bias[:, A_Q_HEADS:]
    near = []
    for d in (-1, 0, 1):
        m = jnp.arange(TQ_B + TK_B - 1, dtype=jnp.int32)
        vec = _bias_of_rel(bias_b, TK_B * d + TK_B - 1 - m)
        near.append(_toeplitz(vec, TK_B, TQ_B, TK_B - 1))
    far_l = jnp.broadcast_to(_bias_of_rel(bias_b, jnp.int32(-TK_B - 1)), (TK_B, TQ_B, B_HEADS))
    far_r = jnp.broadcast_to(_bias_of_rel(bias_b, jnp.int32(TK_B + 1)), (TK_B, TQ_B, B_HEADS))
    tabs = jnp.transpose(jnp.stack([far_l] + near + [far_r]), (3, 0, 1, 2)) * LOG2E
    m = jnp.arange(s + N_META - 1, dtype=jnp.int32)
    vec = _bias_of_rel(bias_b, -1 - m)
    meta = jnp.transpose(_toeplitz(vec, N_META, s, N_META - 1), (2, 0, 1)) * LOG2E
    return tabs, meta


def _prep_w_in(w_in):
    a_w = A_Q_HEADS * HEAD_DIM
    kv = A_KV_HEADS * HEAD_DIM
    bqk = B_HEADS * 2 * HEAD_DIM
    scale = HEAD_DIM ** -0.5
    qa = w_in[:, :a_w] * (scale * LOG2E)
    ka = w_in[:, a_w:a_w + kv]
    va = w_in[:, a_w + kv:a_w + 2 * kv]
    o = a_w + 2 * kv
    qb = w_in[:, o:o + bqk] * (scale * LOG2E)
    kb = w_in[:, o + bqk:o + 2 * bqk]
    vb = w_in[:, o + 2 * bqk:]

    def dup(w):
        return jnp.concatenate([w[:, g * HEAD_DIM:(g + 1) * HEAD_DIM] for g in range(A_KV_HEADS) for _ in range(2)], axis=1)

    w = jnp.concatenate([qa, dup(ka), qb, kb], axis=1).astype(jnp.bfloat16)
    return w, jnp.concatenate([vb, va], axis=1).T.astype(jnp.bfloat16)


def _trunk_front(x, prm):
    bsz, s, d = x.shape
    t = bsz * s
    x2d = x.reshape(t, d)
    proj, vt, vta = _ln_inproj(x2d, prm["ln_emb_g"], prm["ln_emb_b"], prm["w_in"], prm["w_vt"])
    proj3 = proj.reshape(bsz, s, PROJ_COLS)
    oa = _mixer_a(proj3, vta, prm["km_a"], prm["vtm_a"], prm["tab_a"], prm["sink_rows"])
    tabs_b, meta_b = _bias_tables_b(prm["rel_bias"], s)
    ob = _mixer_b(proj3, vt, prm["proj_meta"], prm["vt_meta"], tabs_b, meta_b, prm["lam"].reshape(1).astype(jnp.float32),
                  prm["subln_g"])
    h1, h1_slab = _outproj_ln1(x2d, oa.reshape(t, -1), ob.reshape(t, -1), prm["w_out"], prm["ln_emb_g"],
                               prm["ln_emb_b"], prm["ln1_g"], prm["ln1_b"])

    keys, wts, counts = _router(h1, prm["wr_hi"], prm["wr_lo"], prm["router_bias"])
    n_exp = counts.shape[0]
    n_asg = t * TOP_K
    assert n_asg <= (1 << KEY_SHIFT) and n_asg % LANES == 0
    order = jnp.sort(keys.reshape(n_asg)) & ((1 << KEY_SHIFT) - 1)
    tok = order >> 3
    dst = (order & (TOP_K - 1)) * t + tok
    sc_pad = SC_WORKERS * SC_GRP * SC_NBUF * 2
    assert sc_pad >= TM_MOE and n_asg % sc_pad == 0
    xs = _sc_move_rows(h1_slab.reshape(t, SLAB, LANES), jnp.concatenate([tok, jnp.zeros((sc_pad,), jnp.int32)]),
                       n_asg + sc_pad, scatter=False)
    xs_slab = xs.reshape((n_asg + sc_pad) * SLAB, LANES)
    counts = counts[:, 0].astype(jnp.int32)
    tiles_e = (counts + TM_MOE - 1) // TM_MOE
    tend = jnp.cumsum(tiles_e)
    tstart = tend - tiles_e
    ustart = jnp.cumsum(counts) - counts
    nt = n_asg // TM_MOE + n_exp + 1
    tid = jnp.arange(nt, dtype=jnp.int32)
    tile_e = jnp.minimum(jnp.sum((tend[None, :] <= tid[:, None]).astype(jnp.int32), axis=1), n_exp - 1)
    onehot = (tile_e[:, None] == jnp.arange(n_exp, dtype=jnp.int32)[None, :]).astype(jnp.int32)
    in_e = (tid - jnp.sum(onehot * tstart[None, :], axis=1)) * TM_MOE
    active = tid < tend[-1]
    tile_valid = jnp.where(active, jnp.clip(jnp.sum(onehot * counts[None, :], axis=1) - in_e, 0, TM_MOE), 0)
    tile_u = jnp.where(active, jnp.sum(onehot * ustart[None, :], axis=1) + in_e, 0)

    tiles = (tile_e.astype(jnp.int32), tile_valid.astype(jnp.int32), tile_u.astype(jnp.int32),
             tend[-1:].astype(jnp.int32))
    return h1, wts.T, xs_slab, dst, tiles, x.shape


def _trunk_back(front, prm):
    h1, w_tok, xs_slab, dst, tiles, shape = front
    n_asg = dst.shape[0]
    ys_slab = _experts(xs_slab, prm["w_gate"], prm["w_up"], prm["w_down"], *tiles)
    y = _sc_move_rows(ys_slab.reshape(-1, SLAB, LANES), dst, n_asg, scatter=True)
    out = _combine(h1, w_tok, y.reshape(n_asg * SLAB, LANES), prm["ws_gu"], prm["ws_down"], prm["ln2_g"], prm["ln2_b"])
    return out.reshape(shape)


def kernel(x_prompt, x_sample, meta_tokens, ln_emb_g, ln_emb_b, rel_bias, w_in, attn_sink, lambda_q1, lambda_k1, lambda_q2, lambda_k2, subln_g, w_out, ln1_g, ln1_b, w_router, router_bias, w_gate, w_up, w_down, ws_gate, ws_up, ws_down, ln2_g, ln2_b):
    f32 = jnp.float32
    bf16 = jnp.bfloat16
    l = 0
    row = lambda v: v.reshape(1, -1).astype(f32)
    lam = (jnp.exp(jnp.sum(lambda_q1[l].astype(f32) * lambda_k1[l].astype(f32)))
           - jnp.exp(jnp.sum(lambda_q2[l].astype(f32) * lambda_k2[l].astype(f32))) + LAMBDA_INIT)
    wr_t = w_router[l].astype(f32).T
    wr_hi = wr_t.astype(bf16)
    prm = {
        "ln_emb_g": row(ln_emb_g), "ln_emb_b": row(ln_emb_b),
        "rel_bias": rel_bias,
        "lam": lam,
        "subln_g": subln_g[l].astype(f32).reshape(-1, 1),
        "w_out": w_out[l].astype(bf16),
        "ln1_g": row(ln1_g[l]), "ln1_b": row(ln1_b[l]),
        "wr_hi": wr_hi, "wr_lo": (wr_t - wr_hi.astype(f32)).astype(bf16),
        "router_bias": router_bias[l].astype(f32).reshape(-1, 1),
        "w_gate": w_gate[l], "w_up": w_up[l], "w_down": w_down[l],
        "ws_gu": jnp.concatenate([ws_gate[l], ws_up[l]], axis=-1).astype(bf16),
        "ws_down": ws_down[l].astype(bf16),
        "ln2_g": row(ln2_g[l]), "ln2_b": row(ln2_b[l]),
    }
    prm["w_in"], prm["w_vt"] = _prep_w_in(w_in[l])
    prm["proj_meta"], prm["vt_meta"], vta_meta = _ln_inproj(meta_tokens.astype(f32), prm["ln_emb_g"], prm["ln_emb_b"],
                                                           prm["w_in"], prm["w_vt"])
    meta_pad = NK_A - 3 * TQ_A - N_META
    prm["km_a"] = jnp.pad(prm["proj_meta"][:, KA_BLK * LANES:(KA_BLK + 2) * LANES], ((0, meta_pad), (0, 0)))
    prm["vtm_a"] = jnp.pad(vta_meta, ((0, 0), (0, meta_pad)))
    prm["tab_a"], prm["sink_rows"] = _bias_tables_a(rel_bias, attn_sink[l])
    front_p = _trunk_front(x_prompt, prm)
    front_s = _trunk_front(x_sample, prm)
    return (_trunk_back(front_p, prm), _trunk_back(front_s, prm))
```

```python
import functools
import math

import jax
import jax.numpy as jnp
from jax import lax
from jax.experimental import pallas as pl
from jax.experimental.pallas import tpu as pltpu
from jax.experimental.pallas import tpu_sc as plsc

N_META = 16
HEAD_DIM = 64
WINDOW = 128
A_Q_HEADS = 8
A_KV_HEADS = 2
B_HEADS = 4
N_BUCKETS = 32
MAX_DISTANCE = 128
TOP_K = 8
N_GROUPS = 8
TOPK_GROUPS = 4
ROUTED_SCALE = 2.5
LN_EPS = 1e-5
DEPTH = 1
ALPHA = (2 * DEPTH) ** 0.25
NEG = -1e30
LAMBDA_INIT = 0.8 - 0.6 * math.exp(-0.3 * 0)

LANES = 128
VMEM_LIMIT = 48 * 1024 * 1024

QA_BLK = 0
KA_BLK = 4
QB_BLK = 6
KB_BLK = 10
PROJ_COLS = 14 * LANES
VT_ROWS = LANES + 16
VA_ROWS = HEAD_DIM + 16
LOG2E = 1.4426950408889634

TQ_A = 128
NK_A = 4 * TQ_A
TQ_B = 512
TK_B = 512
TM_MOE = 1024
SLAB = 8
SC_GRP = 16
SC_NBUF = 2
SC_WORKERS = 32
KEY_SHIFT = 20


def _cparams(sem):
    return pltpu.CompilerParams(dimension_semantics=sem, vmem_limit_bytes=VMEM_LIMIT)


def _layernorm_f32(x, g, b):
    mu = jnp.mean(x, axis=-1, keepdims=True)
    xc = x - mu
    var = jnp.mean(xc * xc, axis=-1, keepdims=True)
    return xc * lax.rsqrt(var + LN_EPS) * g + b


def _dot_nt(a, b):
    return lax.dot_general(a, b, (((1,), (1,)), ((), ())), preferred_element_type=jnp.float32)


def _dot(a, b):
    return jnp.dot(a, b, preferred_element_type=jnp.float32)


def _ln_inproj_kernel(x_ref, g_ref, b_ref, w_ref, wvt_ref, o_ref, vt_ref, vta_ref):
    h = _layernorm_f32(x_ref[...], g_ref[...], b_ref[...]).astype(jnp.bfloat16)
    o_ref[...] = _dot(h, w_ref[...]).astype(o_ref.dtype)
    vt = _dot_nt(wvt_ref[...], h).astype(vt_ref.dtype)
    ones = jnp.ones((16, vt.shape[1]), vt_ref.dtype)
    for hd in range(B_HEADS):
        vt_ref[hd * VT_ROWS:hd * VT_ROWS + LANES, :] = vt[hd * LANES:(hd + 1) * LANES, :]
        vt_ref[hd * VT_ROWS + LANES:(hd + 1) * VT_ROWS, :] = ones
    base = B_HEADS * LANES
    for g in range(A_KV_HEADS):
        vta_ref[g * VA_ROWS:g * VA_ROWS + HEAD_DIM, :] = vt[base + g * HEAD_DIM:base + (g + 1) * HEAD_DIM, :]
        vta_ref[g * VA_ROWS + HEAD_DIM:(g + 1) * VA_ROWS, :] = ones


def _ln_inproj(x2d, g, b, w, wvt):
    t, d = x2d.shape
    n = w.shape[1]
    tm = min(512, t)
    return pl.pallas_call(
        _ln_inproj_kernel,
        out_shape=(jax.ShapeDtypeStruct((t, n), jnp.bfloat16),
                   jax.ShapeDtypeStruct((B_HEADS * VT_ROWS, t), jnp.bfloat16),
                   jax.ShapeDtypeStruct((A_KV_HEADS * VA_ROWS, t), jnp.bfloat16)),
        grid=(t // tm,),
        in_specs=[
            pl.BlockSpec((tm, d), lambda i: (i, 0)),
            pl.BlockSpec((1, d), lambda i: (0, 0)),
            pl.BlockSpec((1, d), lambda i: (0, 0)),
            pl.BlockSpec((d, n), lambda i: (0, 0)),
            pl.BlockSpec((wvt.shape[0], d), lambda i: (0, 0)),
        ],
        out_specs=(pl.BlockSpec((tm, n), lambda i: (i, 0)),
                   pl.BlockSpec((B_HEADS * VT_ROWS, tm), lambda i: (0, i)),
                   pl.BlockSpec((A_KV_HEADS * VA_ROWS, tm), lambda i: (0, i))),
        compiler_params=_cparams(("parallel",)),
        name="ln_inproj",
    )(x2d, g, b, w, wvt)


def _mixer_a_kernel(q_ref, k_ref, vt_ref, km_ref, vtm_ref, tab_ref, sink_ref, o_ref, *, nblk, sub):
    i = pl.program_id(1)
    lane = lax.broadcasted_iota(jnp.int32, (1, LANES), 1)
    lo = lane < HEAD_DIM
    for j in range(sub):
        gi = i * sub + j
        sp = pl.multiple_of(jnp.maximum(gi - 1, 0) * TQ_A, TQ_A)
        sc = pl.multiple_of(gi * TQ_A, TQ_A)
        sn = pl.multiple_of(jnp.minimum(gi + 1, nblk - 1) * TQ_A, TQ_A)
        variant = jnp.where(gi == 0, 0, jnp.where(gi == nblk - 1, 2, 1))
        rows = slice(j * TQ_A, (j + 1) * TQ_A)
        for g in range(A_KV_HEADS):
            gs = slice(g * LANES, (g + 1) * LANES)
            vr = slice(g * VA_ROWS, (g + 1) * VA_ROWS)
            k_all = jnp.concatenate([k_ref[0, pl.ds(sp, TQ_A), gs], k_ref[0, pl.ds(sc, TQ_A), gs],
                                     k_ref[0, pl.ds(sn, TQ_A), gs], km_ref[:, gs]], axis=0)
            vt_all = jnp.concatenate([vt_ref[vr, pl.ds(sp, TQ_A)], vt_ref[vr, pl.ds(sc, TQ_A)],
                                      vt_ref[vr, pl.ds(sn, TQ_A)], vtm_ref[vr, :]], axis=1)
            for pp in range(2):
                hp = 2 * g + pp
                cols = slice(hp * LANES, (hp + 1) * LANES)
                qc = q_ref[0, rows, cols]
                q2 = jnp.concatenate([jnp.where(lo, qc, jnp.zeros_like(qc)), jnp.where(lo, jnp.zeros_like(qc), qc)],
                                     axis=0)
                s = _dot_nt(k_all, q2) + tab_ref[variant, hp]
                sink = sink_ref[hp]
                m = jnp.maximum(jnp.max(s, axis=0, keepdims=True), sink)
                p = jnp.exp2(s - m)
                acc = _dot(vt_all, p.astype(jnp.bfloat16))
                o = acc[0:HEAD_DIM, :] / (acc[HEAD_DIM:HEAD_DIM + 1, :] + jnp.exp2(sink - m))
                o2 = jnp.concatenate([o[:, :TQ_A], o[:, TQ_A:]], axis=0)
                o_ref[0, rows, cols] = o2.T.astype(o_ref.dtype)


def _mixer_a(proj3, vta, km, vtm, tab, sink):
    bsz, s, _ = proj3.shape
    nblk = s // TQ_A
    assert nblk >= 2
    sub = 4 if nblk % 4 == 0 else 1
    nq = nblk // sub
    tq = sub * TQ_A
    kern = functools.partial(_mixer_a_kernel, nblk=nblk, sub=sub)
    return pl.pallas_call(
        kern,
        out_shape=jax.ShapeDtypeStruct((bsz, s, A_Q_HEADS * HEAD_DIM), jnp.bfloat16),
        grid=(bsz, nq),
        in_specs=[
            pl.BlockSpec((1, tq, 4 * LANES), lambda b, i: (b, i, QA_BLK // 4)),
            pl.BlockSpec((1, s, 2 * LANES), lambda b, i: (b, 0, KA_BLK // 2)),
            pl.BlockSpec((A_KV_HEADS * VA_ROWS, s), lambda b, i: (0, b)),
            pl.BlockSpec(km.shape, lambda b, i: (0, 0)),
            pl.BlockSpec(vtm.shape, lambda b, i: (0, 0)),
            pl.BlockSpec(tab.shape, lambda b, i: (0, 0, 0, 0)),
            pl.BlockSpec(sink.shape, lambda b, i: (0, 0, 0)),
        ],
        out_specs=pl.BlockSpec((1, tq, 4 * LANES), lambda b, i: (b, i, 0)),
        compiler_params=_cparams(("parallel", "arbitrary")),
        name="mixer_a",
    )(proj3, proj3, vta, km, vtm, tab, sink)


def _mixer_b_kernel(sc_ref, q_ref, k_ref, vt_ref, km_ref, vtm_ref, tab_ref, tabm_ref, g_ref, o_ref,
                    m_ref, acc_ref, s0_ref, s1_ref, *, nk):
    h = pl.program_id(1)
    i = pl.program_id(2)
    lane = lax.broadcasted_iota(jnp.int32, (1, LANES), 1)
    lo = lane < HEAD_DIM
    q = q_ref[0]
    qs = (jnp.where(lo, q, jnp.zeros_like(q)), jnp.where(lo, jnp.zeros_like(q), q))
    lam = sc_ref[0]
    sbufs = (s0_ref, s1_ref)

    km = km_ref[...]
    vtm = vtm_ref[...]
    for c in range(2):
        s = _dot_nt(km, qs[c]) + tabm_ref[0]
        m = jnp.max(s, axis=0, keepdims=True)
        p = jnp.exp2(s - m)
        m_ref[c] = m
        acc_ref[c] = _dot(vtm, p.astype(jnp.bfloat16))

    def scores(j, slot):
        start = pl.multiple_of(j * TK_B, TK_B)
        kj = k_ref[0, pl.ds(start, TK_B), :]
        bias = tab_ref[0, jnp.clip(j - i, -2, 2) + 2]
        for c in range(2):
            sbufs[slot][c] = _dot_nt(kj, qs[c]) + bias

    def accumulate(j, slot):
        start = pl.multiple_of(j * TK_B, TK_B)
        vtj = vt_ref[:, pl.ds(start, TK_B)]
        for c in range(2):
            s = sbufs[slot][c]
            m_prev = m_ref[c]
            m_new = jnp.maximum(m_prev, jnp.max(s, axis=0, keepdims=True))
            a = jnp.exp2(m_prev - m_new)
            p = jnp.exp2(s - m_new)
            m_ref[c] = m_new
            acc_ref[c] = a * acc_ref[c] + _dot(vtj, p.astype(jnp.bfloat16))

    scores(0, 0)

    def pair(jj, carry):
        j = 2 * jj
        scores(j + 1, 1)
        accumulate(j, 0)
        scores(j + 2, 0)
        accumulate(j + 1, 1)
        return carry

    lax.fori_loop(0, nk // 2 - 1, pair, 0)
    scores(nk - 1, 1)
    accumulate(nk - 2, 0)
    accumulate(nk - 1, 1)

    o0 = acc_ref[0, 0:LANES, :] / acc_ref[0, LANES:LANES + 1, :]
    o1 = acc_ref[1, 0:LANES, :] / acc_ref[1, LANES:LANES + 1, :]
    o = o0 - lam * o1
    ms = jnp.mean(o * o, axis=0, keepdims=True)
    o = o * lax.rsqrt(ms + LN_EPS) * (g_ref[...] * (1.0 - LAMBDA_INIT))
    o_ref[0] = o.T.astype(o_ref.dtype)


def _mixer_b(proj3, vt, proj_meta, vt_meta, tab, tabm, scal, subln_g):
    bsz, s, _ = proj3.shape
    nq = s // TQ_B
    nk = s // TK_B
    assert nk % 2 == 0
    kern = functools.partial(_mixer_b_kernel, nk=nk)
    return pl.pallas_call(
        kern,
        out_shape=jax.ShapeDtypeStruct((bsz, s, B_HEADS * LANES), jnp.bfloat16),
        grid=(bsz, B_HEADS, nq),
        in_specs=[
            pl.BlockSpec(memory_space=pltpu.SMEM),
            pl.BlockSpec((1, TQ_B, LANES), lambda b, h, i: (b, i, QB_BLK + h)),
            pl.BlockSpec((1, s, LANES), lambda b, h, i: (b, 0, KB_BLK + h)),
            pl.BlockSpec((VT_ROWS, s), lambda b, h, i: (h, b)),
            pl.BlockSpec((N_META, LANES), lambda b, h, i: (0, KB_BLK + h)),
            pl.BlockSpec((VT_ROWS, N_META), lambda b, h, i: (h, 0)),
            pl.BlockSpec((1, 5, TK_B, TQ_B), lambda b, h, i: (h, 0, 0, 0)),
            pl.BlockSpec((1, N_META, TQ_B), lambda b, h, i: (h, 0, i)),
            pl.BlockSpec((LANES, 1), lambda b, h, i: (0, 0)),
        ],
        out_specs=pl.BlockSpec((1, TQ_B, LANES), lambda b, h, i: (b, i, h)),
        scratch_shapes=[
            pltpu.VMEM((2, 1, TQ_B), jnp.float32),
            pltpu.VMEM((2, VT_ROWS, TQ_B), jnp.float32),
            pltpu.VMEM((2, TK_B, TQ_B), jnp.float32),
            pltpu.VMEM((2, TK_B, TQ_B), jnp.float32),
        ],
        compiler_params=_cparams(("parallel", "parallel", "arbitrary")),
        name="mixer_b",
    )(scal, proj3, proj3, vt, proj_meta, vt_meta, tab, tabm, subln_g)


def _slab_load(ref, lead, rows):
    return jnp.concatenate([ref[lead + (pl.ds(s, rows, stride=SLAB), slice(None))] for s in range(SLAB)], axis=1)


def _slab_store(ref, lead, val):
    rows = val.shape[0]
    for s in range(SLAB):
        ref[lead + (pl.ds(s, rows, stride=SLAB), slice(None))] = val[:, s * LANES:(s + 1) * LANES]


def _outproj_kernel(x_ref, oa_ref, ob_ref, w_ref, eg_ref, eb_ref, g_ref, b_ref, o_ref, os_ref):
    h0 = _layernorm_f32(x_ref[...], eg_ref[...], eb_ref[...])
    half = oa_ref.shape[1]
    mix = _dot(oa_ref[...], w_ref[0:half, :]) + _dot(ob_ref[...], w_ref[half:, :])
    h1 = _layernorm_f32(ALPHA * h0 + mix, g_ref[...], b_ref[...])
    o_ref[...] = h1
    _slab_store(os_ref, (), h1)


def _outproj_ln1(x2d, oa, ob, w_out, eg, eb, g, b):
    t, d = x2d.shape
    tm = min(512, t)
    wa = oa.shape[1]
    wb = ob.shape[1]
    vec = pl.BlockSpec((1, d), lambda i: (0, 0))
    return pl.pallas_call(
        _outproj_kernel,
        out_shape=(jax.ShapeDtypeStruct((t, d), jnp.float32),
                   jax.ShapeDtypeStruct((t * SLAB, LANES), jnp.float32)),
        grid=(t // tm,),
        in_specs=[
            pl.BlockSpec((tm, d), lambda i: (i, 0)),
            pl.BlockSpec((tm, wa), lambda i: (i, 0)),
            pl.BlockSpec((tm, wb), lambda i: (i, 0)),
            pl.BlockSpec((wa + wb, d), lambda i: (0, 0)),
            vec, vec, vec, vec,
        ],
        out_specs=(pl.BlockSpec((tm, d), lambda i: (i, 0)),
                   pl.BlockSpec((tm * SLAB, LANES), lambda i: (i, 0))),
        compiler_params=_cparams(("parallel",)),
        name="outproj_ln1",
    )(x2d, oa, ob, w_out, eg, eb, g, b)


def _router_kernel(h_ref, wh_ref, wl_ref, rb_ref, e_ref, w_ref, cnt_ref, carry_ref, *, n_exp):
    i = pl.program_id(0)
    tn = h_ref.shape[0]
    gsz = n_exp // N_GROUPS

    @pl.when(i == 0)
    def _():
        carry_ref[...] = jnp.zeros_like(carry_ref)

    x = h_ref[...]
    xh = x.astype(jnp.bfloat16)
    xl = (x - xh.astype(jnp.float32)).astype(jnp.bfloat16)
    logits = _dot_nt(wh_ref[...], xh) + (_dot_nt(wh_ref[...], xl) + _dot_nt(wl_ref[...], xh))
    scores = 1.0 / (1.0 + jnp.exp(-logits))
    biased = scores + rb_ref[...]

    g3 = biased.reshape(N_GROUPS, gsz, tn)
    it3 = lax.broadcasted_iota(jnp.int32, (N_GROUPS, gsz, tn), 1)
    mx1 = jnp.max(g3, axis=1, keepdims=True)
    first = jnp.min(jnp.where(g3 == mx1, it3, gsz), axis=1, keepdims=True)
    mx2 = jnp.max(jnp.where(it3 == first, -jnp.inf, g3), axis=1, keepdims=True)
    gscore = (mx1 + mx2).reshape(N_GROUPS, tn)

    itg = lax.broadcasted_iota(jnp.int32, (N_GROUPS, tn), 0)
    gsel = jnp.zeros((N_GROUPS, tn), jnp.bool_)
    cur = gscore
    for _ in range(TOPK_GROUPS):
        mx = jnp.max(cur, axis=0, keepdims=True)
        fi = jnp.min(jnp.where(cur == mx, itg, N_GROUPS), axis=0, keepdims=True)
        hit = itg == fi
        gsel = jnp.logical_or(gsel, hit)
        cur = jnp.where(hit, -jnp.inf, cur)
    emask = jnp.broadcast_to(gsel.reshape(N_GROUPS, 1, tn), (N_GROUPS, gsz, tn)).reshape(n_exp, tn)
    cur = jnp.where(emask, biased, NEG)

    ite = lax.broadcasted_iota(jnp.int32, (n_exp, tn), 0)
    hits = []
    eidx = []
    wsel = []
    for _ in range(TOP_K):
        mx = jnp.max(cur, axis=0, keepdims=True)
        fi = jnp.min(jnp.where(cur == mx, ite, n_exp), axis=0, keepdims=True)
        hit = ite == fi
        hits.append(hit)
        eidx.append(fi)
        wsel.append(jnp.sum(jnp.where(hit, scores, 0.0), axis=0, keepdims=True))
        cur = jnp.where(hit, -jnp.inf, cur)
    sel = hits[0]
    for hit in hits[1:]:
        sel = jnp.logical_or(sel, hit)
    self32 = jnp.where(sel, 1.0, 0.0)

    carry_ref[...] = carry_ref[...] + jnp.sum(self32, axis=1, keepdims=True)
    cnt_ref[...] = carry_ref[...]

    wcat = jnp.concatenate(wsel, axis=0)
    wcat = wcat / jnp.sum(wcat, axis=0, keepdims=True) * ROUTED_SCALE
    tok = i * tn + lax.broadcasted_iota(jnp.int32, (TOP_K, tn), 1)
    slot = lax.broadcasted_iota(jnp.int32, (TOP_K, tn), 0)
    e_ref[...] = jnp.concatenate(eidx, axis=0) * (1 << KEY_SHIFT) + (tok * TOP_K + slot)
    w_ref[...] = wcat


def _router(h1, wr_hi, wr_lo, rbias):
    t, d = h1.shape
    n_exp = wr_hi.shape[0]
    tn = min(512, t)
    kern = functools.partial(_router_kernel, n_exp=n_exp)
    row = pl.BlockSpec((TOP_K, tn), lambda i: (0, i))
    return pl.pallas_call(
        kern,
        out_shape=(
            jax.ShapeDtypeStruct((TOP_K, t), jnp.int32),
            jax.ShapeDtypeStruct((TOP_K, t), jnp.float32),
            jax.ShapeDtypeStruct((n_exp, 1), jnp.float32),
        ),
        grid=(t // tn,),
        in_specs=[
            pl.BlockSpec((tn, d), lambda i: (i, 0)),
            pl.BlockSpec((n_exp, d), lambda i: (0, 0)),
            pl.BlockSpec((n_exp, d), lambda i: (0, 0)),
            pl.BlockSpec((n_exp, 1), lambda i: (0, 0)),
        ],
        out_specs=(row, row, pl.BlockSpec((n_exp, 1), lambda i: (0, 0))),
        scratch_shapes=[pltpu.VMEM((n_exp, 1), jnp.float32)],
        compiler_params=_cparams(("arbitrary",)),
        name="router",
    )(h1, wr_hi, wr_lo, rbias)


def _sc_move_rows(x_slab, idx, n_out, scatter):
    m = idx.shape[0]
    rnd = SC_GRP * SC_NBUF
    per = m // SC_WORKERS
    nr = per // rnd
    assert m % (SC_WORKERS * rnd * 2) == 0
    row = x_slab.shape[1:]
    mesh = plsc.VectorSubcoreMesh(core_axis_name="c", subcore_axis_name="s")

    @pl.kernel(out_type=jax.ShapeDtypeStruct((n_out,) + row, x_slab.dtype), mesh=mesh,
               scratch_types=[pltpu.VMEM((per,), jnp.int32), pltpu.VMEM((2, SC_NBUF, SC_GRP) + row, x_slab.dtype),
                              pltpu.SemaphoreType.DMA((2, SC_NBUF)), pltpu.SemaphoreType.DMA((2, SC_NBUF))])
    def kern(x_hbm, i_hbm, o_hbm, ibuf, buf, lsem, ssem):
        worker = lax.axis_index("c") * (SC_WORKERS // 2) + lax.axis_index("s")
        base = worker * per
        pltpu.sync_copy(i_hbm.at[pl.ds(base, per)], ibuf)

        def copies(r, s):
            lds, sts = [], []
            for b in range(SC_NBUF):
                indexed = ibuf.at[pl.ds(r * rnd + b * SC_GRP, SC_GRP)]
                linear = pl.ds(base + r * rnd + b * SC_GRP, SC_GRP)
                src = x_hbm.at[linear] if scatter else x_hbm.at[indexed]
                dst = o_hbm.at[indexed] if scatter else o_hbm.at[linear]
                lds.append(pltpu.make_async_copy(src, buf.at[s, b], lsem.at[s, b]))
                sts.append(pltpu.make_async_copy(buf.at[s, b], dst, ssem.at[s, b]))
            return lds, sts

        def start(cs):
            for c in cs:
                c.start()

        def wait(cs):
            for c in cs:
                c.wait()

        start(copies(0, 0)[0])

        @pl.loop(0, nr // 2)
        def _(q):
            ra = 2 * q
            rb = ra + 1

            @pl.when(q > 0)
            def _():
                wait(copies(ra - 1, 1)[1])

            start(copies(rb, 1)[0])
            lds, sts = copies(ra, 0)
            wait(lds)
            start(sts)
            wait(sts)

            @pl.when(q < nr // 2 - 1)
            def _():
                start(copies(rb + 1, 0)[0])

            lds, sts = copies(rb, 1)
            wait(lds)
            start(sts)

        wait(copies(nr - 1, 1)[1])

    return kern(x_slab, idx)


X_BUFS = 3
Y_BUFS = 2


def _experts_kernel(te_ref, tv_ref, tu_ref, na_ref, xs_hbm, wg_ref, wu_ref, wd_ref, ys_hbm,
                    xb0, xb1, xb2, yb0, yb1, wgu_s, wd_s, xsem, ysem):
    i = pl.program_id(0)
    nt = pl.num_programs(0)
    n_act = na_ref[0]
    xbufs = (xb0, xb1, xb2)
    ybufs = (yb0, yb1)
    tile_rows = TM_MOE * SLAB

    def x_copy(slot, tile):
        tile = jnp.minimum(tile, nt - 1)
        start = pl.multiple_of(tu_ref[tile] * SLAB, SLAB)
        return pltpu.make_async_copy(xs_hbm.at[pl.ds(start, tile_rows), :], xbufs[slot], xsem.at[slot])

    def y_pieces(slot, tile, go):
        nvalid = tv_ref[tile]
        base = tu_ref[tile]
        size = TM_MOE
        while size >= 1:
            @pl.when((nvalid & size) != 0)
            def _(size=size):
                off = nvalid & ~(2 * size - 1)
                src = ybufs[slot].at[pl.ds(pl.multiple_of(off * SLAB, SLAB), size * SLAB), :]
                dst = ys_hbm.at[pl.ds(pl.multiple_of((base + off) * SLAB, SLAB), size * SLAB), :]
                go(pltpu.make_async_copy(src, dst, ysem.at[slot]))
            size //= 2

    @pl.when(i == 0)
    def _():
        x_copy(0, 0).start()
        x_copy(1, 1).start()

    iprev = jnp.maximum(i - 1, 0)

    @pl.when(jnp.logical_and(i < n_act, jnp.logical_or(i == 0, te_ref[i] != te_ref[iprev])))
    def _():
        de = wg_ref.shape[2]
        wgu_s[:, 0:de] = wg_ref[0].astype(wgu_s.dtype)
        wgu_s[:, de:] = wu_ref[0].astype(wgu_s.dtype)
        wd_s[...] = wd_ref[0].astype(wd_s.dtype)

    phase = i % (X_BUFS * Y_BUFS)
    for c in range(X_BUFS * Y_BUFS):
        xs, ys = c % X_BUFS, c % Y_BUFS

        @pl.when(jnp.logical_and(i < n_act, phase == c))
        def _(xs=xs, ys=ys):
            x_copy(xs, i).wait()
            x_copy((xs + 2) % X_BUFS, i + 2).start()
            x = _slab_load(xbufs[xs], (), TM_MOE).astype(jnp.bfloat16)
            gu = _dot(x, wgu_s[...])
            de = gu.shape[1] // 2
            gate = gu[:, :de]
            hid = (gate / (1.0 + jnp.exp(-gate))) * gu[:, de:]
            y = _dot(hid.astype(jnp.bfloat16), wd_s[...])

            @pl.when(i >= Y_BUFS)
            def _():
                y_pieces(ys, i - Y_BUFS, lambda cp: cp.wait())

            _slab_store(ybufs[ys], (), y)
            y_pieces(ys, i, lambda cp: cp.start(priority=1))

        @pl.when(jnp.logical_and(i == n_act, phase == c))
        def _(xs=xs, ys=ys):
            x_copy(xs, i).wait()
            x_copy((xs + 1) % X_BUFS, i + 1).wait()

            @pl.when(i >= 2)
            def _():
                y_pieces(ys, i - 2, lambda cp: cp.wait())

            @pl.when(i >= 1)
            def _():
                y_pieces(1 - ys, i - 1, lambda cp: cp.wait())


def _experts(xs_slab, wg, wu, wd, tile_e, tile_valid, tile_u, n_active):
    nt = tile_e.shape[0]
    d = wg.shape[1]
    de = wg.shape[2]
    buf = pltpu.VMEM((TM_MOE * SLAB, LANES), jnp.float32)
    return pl.pallas_call(
        _experts_kernel,
        out_shape=jax.ShapeDtypeStruct(xs_slab.shape, jnp.float32),
        grid_spec=pltpu.PrefetchScalarGridSpec(
            num_scalar_prefetch=4,
            grid=(nt,),
            in_specs=[
                pl.BlockSpec(memory_space=pl.ANY),
                pl.BlockSpec((1, d, de), lambda i, te, tv, tu, na: (te[i], 0, 0)),
                pl.BlockSpec((1, d, de), lambda i, te, tv, tu, na: (te[i], 0, 0)),
                pl.BlockSpec((1, de, d), lambda i, te, tv, tu, na: (te[i], 0, 0)),
            ],
            out_specs=pl.BlockSpec(memory_space=pl.ANY),
            scratch_shapes=[buf] * (X_BUFS + Y_BUFS) + [
                pltpu.VMEM((d, 2 * de), jnp.bfloat16), pltpu.VMEM((de, d), jnp.bfloat16),
                pltpu.SemaphoreType.DMA((X_BUFS,)), pltpu.SemaphoreType.DMA((Y_BUFS,))],
        ),
        compiler_params=_cparams(("arbitrary",)),
        name="experts",
    )(tile_e, tile_valid, tile_u, n_active, xs_slab, wg, wu, wd)


def _combine_kernel(h_ref, w_ref, *rest):
    y_refs = rest[:TOP_K]
    sgu_ref, sd_ref, g_ref, b_ref, o_ref = rest[TOP_K:]
    tn = h_ref.shape[0]
    h = h_ref[...]
    gu = _dot(h.astype(jnp.bfloat16), sgu_ref[...])
    ds = gu.shape[1] // 2
    gate = gu[:, :ds]
    hid = (gate / (1.0 + jnp.exp(-gate))) * gu[:, ds:]
    shared = _dot(hid.astype(jnp.bfloat16), sd_ref[...])

    w = w_ref[...]
    routed = _slab_load(y_refs[0], (), tn) * w[:, 0:1]
    for k in range(1, TOP_K):
        routed = routed + _slab_load(y_refs[k], (), tn) * w[:, k:k + 1]
    o_ref[...] = _layernorm_f32(ALPHA * h + (routed + shared), g_ref[...], b_ref[...])


def _combine(h1, w_tok, y_slab, sgu, sd, g, b):
    t, d = h1.shape
    tn = min(256, t)
    nblk = t // tn
    ds2 = sgu.shape[1]
    vec = pl.BlockSpec((1, d), lambda i: (0, 0))

    def slot_spec(k):
        return pl.BlockSpec((tn * SLAB, LANES), lambda i: (k * nblk + i, 0))

    return pl.pallas_call(
        _combine_kernel,
        out_shape=jax.ShapeDtypeStruct((t, d), jnp.float32),
        grid=(nblk,),
        in_specs=[
            pl.BlockSpec((tn, d), lambda i: (i, 0)),
            pl.BlockSpec((tn, TOP_K), lambda i: (i, 0)),
            *[slot_spec(k) for k in range(TOP_K)],
            pl.BlockSpec((d, ds2), lambda i: (0, 0)),
            pl.BlockSpec((ds2 // 2, d), lambda i: (0, 0)),
            vec, vec,
        ],
        out_specs=pl.BlockSpec((tn, d), lambda i: (i, 0)),
        compiler_params=_cparams(("parallel",)),
        name="combine_ln2",
    )(h1, w_tok, *([y_slab] * TOP_K), sgu, sd, g, b)


def _rel_bucket(rel):
    nb = N_BUCKETS // 2
    max_exact = nb // 2
    ret = jnp.where(rel > 0, nb, 0)
    n = jnp.abs(rel)
    nf = jnp.maximum(n, 1).astype(jnp.float32)
    large = max_exact + (jnp.log(nf / max_exact) / math.log(MAX_DISTANCE / max_exact) * (nb - max_exact)).astype(jnp.int32)
    large = jnp.minimum(large, nb - 1)
    return ret + jnp.where(n < max_exact, n, large)


def _bias_of_rel(rel_bias, rel):
    return rel_bias.astype(jnp.float32)[_rel_bucket(rel)]


def _toeplitz(vec, nrow, ncol, off):
    lo = off - (nrow - 1)
    v = vec[lo:off + ncol]
    p = v.shape[0] + 1
    v = jnp.concatenate([v, v[:1]], axis=0)
    flat = jnp.tile(v, (nrow + 1, 1))
    base = off - lo
    out = flat[base:base + nrow * (p - 1)].reshape(nrow, p - 1, vec.shape[1])
    return out[:, :ncol]


def _bias_tables_a(rel_bias, sink):
    bias_a = rel_bias[:, :A_Q_HEADS]
    m = jnp.arange(3 * TQ_A + TQ_A - 1, dtype=jnp.int32)
    rel = 2 * TQ_A - 1 - m
    vec = jnp.where((jnp.abs(rel) <= WINDOW)[:, None], _bias_of_rel(bias_a, rel), NEG)
    band = _toeplitz(vec, 3 * TQ_A, TQ_A, 3 * TQ_A - 1)
    mm = jnp.arange(TQ_A + N_META - 1, dtype=jnp.int32)
    meta_first = _toeplitz(_bias_of_rel(bias_a, -1 - mm), N_META, TQ_A, N_META - 1)
    meta_far = jnp.broadcast_to(_bias_of_rel(bias_a, jnp.int32(-2 * MAX_DISTANCE)), (N_META, TQ_A, A_Q_HEADS))
    pad = jnp.full((NK_A - 3 * TQ_A - N_META, TQ_A, A_Q_HEADS), NEG, jnp.float32)
    blocked = jnp.full((TQ_A, TQ_A, A_Q_HEADS), NEG, jnp.float32)
    first = jnp.concatenate([blocked, band[TQ_A:], meta_first, pad])
    middle = jnp.concatenate([band, meta_far, pad])
    last = jnp.concatenate([band[:2 * TQ_A], blocked, meta_far, pad])
    tab = jnp.stack([first, middle, last])
    tab = tab.reshape(3, NK_A, TQ_A, A_Q_HEADS // 2, 2)
    tab = jnp.transpose(tab, (0, 3, 1, 4, 2)).reshape(3, A_Q_HEADS // 2, NK_A, 2 * TQ_A) * LOG2E
    sink_rows = jnp.repeat(sink.astype(jnp.float32).reshape(A_Q_HEADS // 2, 1, 2), TQ_A, axis=2) * LOG2E
    return tab, sink_rows


def _bias_tables_b(rel_bias, s):
    bias_b = rel_bias[:, A_Q_HEADS:]
    near = []
    for d in (-1, 0, 1):
        m = jnp.arange(TQ_B + TK_B - 1, dtype=jnp.int32)
        vec = _bias_of_rel(bias_b, TK_B * d + TK_B - 1 - m)
        near.append(_toeplitz(vec, TK_B, TQ_B, TK_B - 1))
    far_l = jnp.broadcast_to(_bias_of_rel(bias_b, jnp.int32(-TK_B - 1)), (TK_B, TQ_B, B_HEADS))
    far_r = jnp.broadcast_to(_bias_of_rel(bias_b, jnp.int32(TK_B + 1)), (TK_B, TQ_B, B_HEADS))
    tabs = jnp.transpose(jnp.stack([far_l] + near + [far_r]), (3, 0, 1, 2)) * LOG2E
    m = jnp.arange(s + N_META - 1, dtype=jnp.int32)
    vec = _bias_of_rel(bias_b, -1 - m)
    meta = jnp.transpose(_toeplitz(vec, N_META, s, N_META - 1), (2, 0, 1)) * LOG2E
    return tabs, meta


def _prep_w_in(w_in):
    a_w = A_Q_HEADS * HEAD_DIM
    kv = A_KV_HEADS * HEAD_DIM
    bqk = B_HEADS * 2 * HEAD_DIM
    scale = HEAD_DIM ** -0.5
    qa = w_in[:, :a_w] * (scale * LOG2E)
    ka = w_in[:, a_w:a_w + kv]
    va = w_in[:, a_w + kv:a_w + 2 * kv]
    o = a_w + 2 * kv
    qb = w_in[:, o:o + bqk] * (scale * LOG2E)
    kb = w_in[:, o + bqk:o + 2 * bqk]
    vb = w_in[:, o + 2 * bqk:]

    def dup(w):
        return jnp.concatenate([w[:, g * HEAD_DIM:(g + 1) * HEAD_DIM] for g in range(A_KV_HEADS) for _ in range(2)], axis=1)

    w = jnp.concatenate([qa, dup(ka), qb, kb], axis=1).astype(jnp.bfloat16)
    return w, jnp.concatenate([vb, va], axis=1).T.astype(jnp.bfloat16)


def _trunk_front(x, prm):
    bsz, s, d = x.shape
    t = bsz * s
    x2d = x.reshape(t, d)
    proj, vt, vta = _ln_inproj(x2d, prm["ln_emb_g"], prm["ln_emb_b"], prm["w_in"], prm["w_vt"])
    proj3 = proj.reshape(bsz, s, PROJ_COLS)
    oa = _mixer_a(proj3, vta, prm["km_a"], prm["vtm_a"], prm["tab_a"], prm["sink_rows"])
    tabs_b, meta_b = _bias_tables_b(prm["rel_bias"], s)
    ob = _mixer_b(proj3, vt, prm["proj_meta"], prm["vt_meta"], tabs_b, meta_b, prm["lam"].reshape(1).astype(jnp.float32),
                  prm["subln_g"])
    h1, h1_slab = _outproj_ln1(x2d, oa.reshape(t, -1), ob.reshape(t, -1), prm["w_out"], prm["ln_emb_g"],
                               prm["ln_emb_b"], prm["ln1_g"], prm["ln1_b"])

    keys, wts, counts = _router(h1, prm["wr_hi"], prm["wr_lo"], prm["router_bias"])
    n_exp = counts.shape[0]
    n_asg = t * TOP_K
    assert n_asg <= (1 << KEY_SHIFT) and n_asg % LANES == 0
    order = jnp.sort(keys.reshape(n_asg)) & ((1 << KEY_SHIFT) - 1)
    tok = order >> 3
    dst = (order & (TOP_K - 1)) * t + tok
    sc_pad = SC_WORKERS * SC_GRP * SC_NBUF * 2
    assert sc_pad >= TM_MOE and n_asg % sc_pad == 0
    xs = _sc_move_rows(h1_slab.reshape(t, SLAB, LANES), jnp.concatenate([tok, jnp.zeros((sc_pad,), jnp.int32)]),
                       n_asg + sc_pad, scatter=False)
    xs_slab = xs.reshape((n_asg + sc_pad) * SLAB, LANES)
    counts = counts[:, 0].astype(jnp.int32)
    tiles_e = (counts + TM_MOE - 1) // TM_MOE
    tend = jnp.cumsum(tiles_e)
    tstart = tend - tiles_e
    ustart = jnp.cumsum(counts) - counts
    nt = n_asg // TM_MOE + n_exp + 1
    tid = jnp.arange(nt, dtype=jnp.int32)
    tile_e = jnp.minimum(jnp.sum((tend[None, :] <= tid[:, None]).astype(jnp.int32), axis=1), n_exp - 1)
    onehot = (tile_e[:, None] == jnp.arange(n_exp, dtype=jnp.int32)[None, :]).astype(jnp.int32)
    in_e = (tid - jnp.sum(onehot * tstart[None, :], axis=1)) * TM_MOE
    active = tid < tend[-1]
    tile_valid = jnp.where(active, jnp.clip(jnp.sum(onehot * counts[None, :], axis=1) - in_e, 0, TM_MOE), 0)
    tile_u = jnp.where(active, jnp.sum(onehot * ustart[None, :], axis=1) + in_e, 0)

    tiles = (tile_e.astype(jnp.int32), tile_valid.astype(jnp.int32), tile_u.astype(jnp.int32),
             tend[-1:].astype(jnp.int32))
    return h1, wts.T, xs_slab, dst, tiles, x.shape


def _trunk_back(front, prm):
    h1, w_tok, xs_slab, dst, tiles, shape = front
    n_asg = dst.shape[0]
    ys_slab = _experts(xs_slab, prm["w_gate"], prm["w_up"], prm["w_down"], *tiles)
    y = _sc_move_rows(ys_slab.reshape(-1, SLAB, LANES), dst, n_asg, scatter=True)
    out = _combine(h1, w_tok, y.reshape(n_asg * SLAB, LANES), prm["ws_gu"], prm["ws_down"], prm["ln2_g"], prm["ln2_b"])
    return out.reshape(shape)


def kernel(x_prompt, x_sample, meta_tokens, ln_emb_g, ln_emb_b, rel_bias, w_in, attn_sink, lambda_q1, lambda_k1, lambda_q2, lambda_k2, subln_g, w_out, ln1_g, ln1_b, w_router, router_bias, w_gate, w_up, w_down, ws_gate, ws_up, ws_down, ln2_g, ln2_b):
    f32 = jnp.float32
    bf16 = jnp.bfloat16
    l = 0
    row = lambda v: v.reshape(1, -1).astype(f32)
    lam = (jnp.exp(jnp.sum(lambda_q1[l].astype(f32) * lambda_k1[l].astype(f32)))
           - jnp.exp(jnp.sum(lambda_q2[l].astype(f32) * lambda_k2[l].astype(f32))) + LAMBDA_INIT)
    wr_t = w_router[l].astype(f32).T
    wr_hi = wr_t.astype(bf16)
    prm = {
        "ln_emb_g": row(ln_emb_g), "ln_emb_b": row(ln_emb_b),
        "rel_bias": rel_bias,
        "lam": lam,
        "subln_g": subln_g[l].astype(f32).reshape(-1, 1),
        "w_out": w_out[l].astype(bf16),
        "ln1_g": row(ln1_g[l]), "ln1_b": row(ln1_b[l]),
        "wr_hi": wr_hi, "wr_lo": (wr_t - wr_hi.astype(f32)).astype(bf16),
        "router_bias": router_bias[l].astype(f32).reshape(-1, 1),
        "w_gate": w_gate[l], "w_up": w_up[l], "w_down": w_down[l],
        "ws_gu": jnp.concatenate([ws_gate[l], ws_up[l]], axis=-1).astype(bf16),
        "ws_down": ws_down[l].astype(bf16),
        "ln2_g": row(ln2_g[l]), "ln2_b": row(ln2_b[l]),
    }
    prm["w_in"], prm["w_vt"] = _prep_w_in(w_in[l])
    prm["proj_meta"], prm["vt_meta"], vta_meta = _ln_inproj(meta_tokens.astype(f32), prm["ln_emb_g"], prm["ln_emb_b"],
                                                           prm["w_in"], prm["w_vt"])
    meta_pad = NK_A - 3 * TQ_A - N_META
    prm["km_a"] = jnp.pad(prm["proj_meta"][:, KA_BLK * LANES:(KA_BLK + 2) * LANES], ((0, meta_pad), (0, 0)))
    prm["vtm_a"] = jnp.pad(vta_meta, ((0, 0), (0, meta_pad)))
    prm["tab_a"], prm["sink_rows"] = _bias_tables_a(rel_bias, attn_sink[l])
    front_p = _trunk_front(x_prompt, prm)
    front_s = _trunk_front(x_sample, prm)
    return (_trunk_back(front_p, prm), _trunk_back(front_s, prm))
```

```python
import functools
import math

import jax
import jax.numpy as jnp
from jax import lax
from jax.experimental import pallas as pl
from jax.experimental.pallas import tpu as pltpu
from jax.experimental.pallas import tpu_sc as plsc

N_META = 16
HEAD_DIM = 64
WINDOW = 128
A_Q_HEADS = 8
A_KV_HEADS = 2
B_HEADS = 4
N_BUCKETS = 32
MAX_DISTANCE = 128
TOP_K = 8
N_GROUPS = 8
TOPK_GROUPS = 4
ROUTED_SCALE = 2.5
LN_EPS = 1e-5
DEPTH = 1
ALPHA = (2 * DEPTH) ** 0.25
NEG = -1e30
LAMBDA_INIT = 0.8 - 0.6 * math.exp(-0.3 * 0)

LANES = 128
VMEM_LIMIT = 48 * 1024 * 1024

QA_BLK = 0
KA_BLK = 4
QB_BLK = 6
KB_BLK = 10
PROJ_COLS = 14 * LANES
VT_ROWS = LANES + 16
VA_ROWS = HEAD_DIM + 16
LOG2E = 1.4426950408889634

TQ_A = 128
NK_A = 4 * TQ_A
TQ_B = 512
TK_B = 512
TM_MOE = 512
SLAB = 8
SC_GRP = 16
SC_NBUF = 4
SC_WORKERS = 32
KEY_SHIFT = 20


def _cparams(sem):
    return pltpu.CompilerParams(dimension_semantics=sem, vmem_limit_bytes=VMEM_LIMIT)


def _layernorm_f32(x, g, b):
    mu = jnp.mean(x, axis=-1, keepdims=True)
    xc = x - mu
    var = jnp.mean(xc * xc, axis=-1, keepdims=True)
    return xc * lax.rsqrt(var + LN_EPS) * g + b


def _dot_nt(a, b):
    return lax.dot_general(a, b, (((1,), (1,)), ((), ())), preferred_element_type=jnp.float32)


def _dot(a, b):
    return jnp.dot(a, b, preferred_element_type=jnp.float32)


def _ln_inproj_kernel(x_ref, g_ref, b_ref, w_ref, wvt_ref, o_ref, vt_ref, vta_ref):
    h = _layernorm_f32(x_ref[...], g_ref[...], b_ref[...]).astype(jnp.bfloat16)
    o_ref[...] = _dot(h, w_ref[...]).astype(o_ref.dtype)
    vt = _dot_nt(wvt_ref[...], h).astype(vt_ref.dtype)
    ones = jnp.ones((16, vt.shape[1]), vt_ref.dtype)
    for hd in range(B_HEADS):
        vt_ref[hd * VT_ROWS:hd * VT_ROWS + LANES, :] = vt[hd * LANES:(hd + 1) * LANES, :]
        vt_ref[hd * VT_ROWS + LANES:(hd + 1) * VT_ROWS, :] = ones
    base = B_HEADS * LANES
    for g in range(A_KV_HEADS):
        vta_ref[g * VA_ROWS:g * VA_ROWS + HEAD_DIM, :] = vt[base + g * HEAD_DIM:base + (g + 1) * HEAD_DIM, :]
        vta_ref[g * VA_ROWS + HEAD_DIM:(g + 1) * VA_ROWS, :] = ones


def _ln_inproj(x2d, g, b, w, wvt):
    t, d = x2d.shape
    n = w.shape[1]
    tm = min(512, t)
    return pl.pallas_call(
        _ln_inproj_kernel,
        out_shape=(jax.ShapeDtypeStruct((t, n), jnp.bfloat16),
                   jax.ShapeDtypeStruct((B_HEADS * VT_ROWS, t), jnp.bfloat16),
                   jax.ShapeDtypeStruct((A_KV_HEADS * VA_ROWS, t), jnp.bfloat16)),
        grid=(t // tm,),
        in_specs=[
            pl.BlockSpec((tm, d), lambda i: (i, 0)),
            pl.BlockSpec((1, d), lambda i: (0, 0)),
            pl.BlockSpec((1, d), lambda i: (0, 0)),
            pl.BlockSpec((d, n), lambda i: (0, 0)),
            pl.BlockSpec((wvt.shape[0], d), lambda i: (0, 0)),
        ],
        out_specs=(pl.BlockSpec((tm, n), lambda i: (i, 0)),
                   pl.BlockSpec((B_HEADS * VT_ROWS, tm), lambda i: (0, i)),
                   pl.BlockSpec((A_KV_HEADS * VA_ROWS, tm), lambda i: (0, i))),
        compiler_params=_cparams(("parallel",)),
        name="ln_inproj",
    )(x2d, g, b, w, wvt)


def _mixer_a_kernel(q_ref, k_ref, vt_ref, km_ref, vtm_ref, tab_ref, sink_ref, o_ref, *, nblk, sub):
    i = pl.program_id(1)
    lane = lax.broadcasted_iota(jnp.int32, (1, LANES), 1)
    lo = lane < HEAD_DIM
    for j in range(sub):
        gi = i * sub + j
        sp = pl.multiple_of(jnp.maximum(gi - 1, 0) * TQ_A, TQ_A)
        sc = pl.multiple_of(gi * TQ_A, TQ_A)
        sn = pl.multiple_of(jnp.minimum(gi + 1, nblk - 1) * TQ_A, TQ_A)
        variant = jnp.where(gi == 0, 0, jnp.where(gi == nblk - 1, 2, 1))
        rows = slice(j * TQ_A, (j + 1) * TQ_A)
        for g in range(A_KV_HEADS):
            gs = slice(g * LANES, (g + 1) * LANES)
            vr = slice(g * VA_ROWS, (g + 1) * VA_ROWS)
            k_all = jnp.concatenate([k_ref[0, pl.ds(sp, TQ_A), gs], k_ref[0, pl.ds(sc, TQ_A), gs],
                                     k_ref[0, pl.ds(sn, TQ_A), gs], km_ref[:, gs]], axis=0)
            vt_all = jnp.concatenate([vt_ref[vr, pl.ds(sp, TQ_A)], vt_ref[vr, pl.ds(sc, TQ_A)],
                                      vt_ref[vr, pl.ds(sn, TQ_A)], vtm_ref[vr, :]], axis=1)
            for pp in range(2):
                hp = 2 * g + pp
                cols = slice(hp * LANES, (hp + 1) * LANES)
                qc = q_ref[0, rows, cols]
                q2 = jnp.concatenate([jnp.where(lo, qc, jnp.zeros_like(qc)), jnp.where(lo, jnp.zeros_like(qc), qc)],
                                     axis=0)
                s = _dot_nt(k_all, q2) + tab_ref[variant, hp]
                sink = sink_ref[hp]
                m = jnp.maximum(jnp.max(s, axis=0, keepdims=True), sink)
                p = jnp.exp2(s - m)
                acc = _dot(vt_all, p.astype(jnp.bfloat16))
                o = acc[0:HEAD_DIM, :] / (acc[HEAD_DIM:HEAD_DIM + 1, :] + jnp.exp2(sink - m))
                o2 = jnp.concatenate([o[:, :TQ_A], o[:, TQ_A:]], axis=0)
                o_ref[0, rows, cols] = o2.T.astype(o_ref.dtype)


def _mixer_a(proj3, vta, km, vtm, tab, sink):
    bsz, s, _ = proj3.shape
    nblk = s // TQ_A
    assert nblk >= 2
    sub = 4 if nblk % 4 == 0 else 1
    nq = nblk // sub
    tq = sub * TQ_A
    kern = functools.partial(_mixer_a_kernel, nblk=nblk, sub=sub)
    return pl.pallas_call(
        kern,
        out_shape=jax.ShapeDtypeStruct((bsz, s, A_Q_HEADS * HEAD_DIM), jnp.bfloat16),
        grid=(bsz, nq),
        in_specs=[
            pl.BlockSpec((1, tq, 4 * LANES), lambda b, i: (b, i, QA_BLK // 4)),
            pl.BlockSpec((1, s, 2 * LANES), lambda b, i: (b, 0, KA_BLK // 2)),
            pl.BlockSpec((A_KV_HEADS * VA_ROWS, s), lambda b, i: (0, b)),
            pl.BlockSpec(km.shape, lambda b, i: (0, 0)),
            pl.BlockSpec(vtm.shape, lambda b, i: (0, 0)),
            pl.BlockSpec(tab.shape, lambda b, i: (0, 0, 0, 0)),
            pl.BlockSpec(sink.shape, lambda b, i: (0, 0, 0)),
        ],
        out_specs=pl.BlockSpec((1, tq, 4 * LANES), lambda b, i: (b, i, 0)),
        compiler_params=_cparams(("parallel", "arbitrary")),
        name="mixer_a",
    )(proj3, proj3, vta, km, vtm, tab, sink)


def _mixer_b_kernel(sc_ref, q_ref, k_ref, vt_ref, km_ref, vtm_ref, tab_ref, tabm_ref, g_ref, o_ref,
                    m_ref, acc_ref, s0_ref, s1_ref, cmax_ref, *, nk):
    h = pl.program_id(1)
    i = pl.program_id(2)
    lane = lax.broadcasted_iota(jnp.int32, (1, LANES), 1)
    lo = lane < HEAD_DIM
    q = q_ref[0]
    qs = (jnp.where(lo, q, jnp.zeros_like(q)), jnp.where(lo, jnp.zeros_like(q), q))
    lam = sc_ref[0]
    sbufs = (s0_ref, s1_ref)

    km = km_ref[...]
    vtm = vtm_ref[...]
    for c in range(2):
        s = _dot_nt(km, qs[c]) + tabm_ref[0]
        m = jnp.max(s, axis=0, keepdims=True)
        p = jnp.exp2(s - m)
        m_ref[c] = m
        acc_ref[c] = _dot(vtm, p.astype(jnp.bfloat16))

    def scores(j, slot):
        start = pl.multiple_of(j * TK_B, TK_B)
        kj = k_ref[0, pl.ds(start, TK_B), :]
        bias = tab_ref[0, jnp.clip(j - i, -2, 2) + 2]
        for c in range(2):
            s = _dot_nt(kj, qs[c]) + bias
            sbufs[slot][c] = s
            cmax_ref[slot, c] = jnp.max(s, axis=0, keepdims=True)

    def accumulate(j, slot):
        start = pl.multiple_of(j * TK_B, TK_B)
        vtj = vt_ref[:, pl.ds(start, TK_B)]
        for c in range(2):
            s = sbufs[slot][c]
            m_prev = m_ref[c]
            m_new = jnp.maximum(m_prev, cmax_ref[slot, c])
            a = jnp.exp2(m_prev - m_new)
            p = jnp.exp2(s - m_new)
            m_ref[c] = m_new
            acc_ref[c] = a * acc_ref[c] + _dot(vtj, p.astype(jnp.bfloat16))

    scores(0, 0)

    def pair(jj, carry):
        j = 2 * jj
        scores(j + 1, 1)
        accumulate(j, 0)
        scores(j + 2, 0)
        accumulate(j + 1, 1)
        return carry

    lax.fori_loop(0, nk // 2 - 1, pair, 0)
    scores(nk - 1, 1)
    accumulate(nk - 2, 0)
    accumulate(nk - 1, 1)

    o0 = acc_ref[0, 0:LANES, :] / acc_ref[0, LANES:LANES + 1, :]
    o1 = acc_ref[1, 0:LANES, :] / acc_ref[1, LANES:LANES + 1, :]
    o = o0 - lam * o1
    ms = jnp.mean(o * o, axis=0, keepdims=True)
    o = o * lax.rsqrt(ms + LN_EPS) * (g_ref[...] * (1.0 - LAMBDA_INIT))
    o_ref[0] = o.T.astype(o_ref.dtype)


def _mixer_b(proj3, vt, proj_meta, vt_meta, tab, tabm, scal, subln_g):
    bsz, s, _ = proj3.shape
    nq = s // TQ_B
    nk = s // TK_B
    assert nk % 2 == 0
    kern = functools.partial(_mixer_b_kernel, nk=nk)
    return pl.pallas_call(
        kern,
        out_shape=jax.ShapeDtypeStruct((bsz, s, B_HEADS * LANES), jnp.bfloat16),
        grid=(bsz, B_HEADS, nq),
        in_specs=[
            pl.BlockSpec(memory_space=pltpu.SMEM),
            pl.BlockSpec((1, TQ_B, LANES), lambda b, h, i: (b, i, QB_BLK + h)),
            pl.BlockSpec((1, s, LANES), lambda b, h, i: (b, 0, KB_BLK + h)),
            pl.BlockSpec((VT_ROWS, s), lambda b, h, i: (h, b)),
            pl.BlockSpec((N_META, LANES), lambda b, h, i: (0, KB_BLK + h)),
            pl.BlockSpec((VT_ROWS, N_META), lambda b, h, i: (h, 0)),
            pl.BlockSpec((1, 5, TK_B, TQ_B), lambda b, h, i: (h, 0, 0, 0)),
            pl.BlockSpec((1, N_META, TQ_B), lambda b, h, i: (h, 0, i)),
            pl.BlockSpec((LANES, 1), lambda b, h, i: (0, 0)),
        ],
        out_specs=pl.BlockSpec((1, TQ_B, LANES), lambda b, h, i: (b, i, h)),
        scratch_shapes=[
            pltpu.VMEM((2, 1, TQ_B), jnp.float32),
            pltpu.VMEM((2, VT_ROWS, TQ_B), jnp.float32),
            pltpu.VMEM((2, TK_B, TQ_B), jnp.float32),
            pltpu.VMEM((2, TK_B, TQ_B), jnp.float32),
            pltpu.VMEM((2, 2, 1, TQ_B), jnp.float32),
        ],
        compiler_params=_cparams(("parallel", "parallel", "arbitrary")),
        name="mixer_b",
    )(scal, proj3, proj3, vt, proj_meta, vt_meta, tab, tabm, subln_g)


def _slab_load(ref, lead, rows):
    return jnp.concatenate([ref[lead + (pl.ds(s, rows, stride=SLAB), slice(None))] for s in range(SLAB)], axis=1)


def _slab_store(ref, lead, val):
    rows = val.shape[0]
    for s in range(SLAB):
        ref[lead + (pl.ds(s, rows, stride=SLAB), slice(None))] = val[:, s * LANES:(s + 1) * LANES]


def _outproj_kernel(x_ref, oa_ref, ob_ref, w_ref, eg_ref, eb_ref, g_ref, b_ref, o_ref, os_ref):
    h0 = _layernorm_f32(x_ref[...], eg_ref[...], eb_ref[...])
    half = oa_ref.shape[1]
    mix = _dot(oa_ref[...], w_ref[0:half, :]) + _dot(ob_ref[...], w_ref[half:, :])
    h1 = _layernorm_f32(ALPHA * h0 + mix, g_ref[...], b_ref[...])
    o_ref[...] = h1
    _slab_store(os_ref, (), h1)


def _outproj_ln1(x2d, oa, ob, w_out, eg, eb, g, b):
    t, d = x2d.shape
    tm = min(512, t)
    wa = oa.shape[1]
    wb = ob.shape[1]
    vec = pl.BlockSpec((1, d), lambda i: (0, 0))
    return pl.pallas_call(
        _outproj_kernel,
        out_shape=(jax.ShapeDtypeStruct((t, d), jnp.float32),
                   jax.ShapeDtypeStruct((t * SLAB, LANES), jnp.float32)),
        grid=(t // tm,),
        in_specs=[
            pl.BlockSpec((tm, d), lambda i: (i, 0)),
            pl.BlockSpec((tm, wa), lambda i: (i, 0)),
            pl.BlockSpec((tm, wb), lambda i: (i, 0)),
            pl.BlockSpec((wa + wb, d), lambda i: (0, 0)),
            vec, vec, vec, vec,
        ],
        out_specs=(pl.BlockSpec((tm, d), lambda i: (i, 0)),
                   pl.BlockSpec((tm * SLAB, LANES), lambda i: (i, 0))),
        compiler_params=_cparams(("parallel",)),
        name="outproj_ln1",
    )(x2d, oa, ob, w_out, eg, eb, g, b)


def _router_kernel(h_ref, wh_ref, wl_ref, rb_ref, e_ref, w_ref, cnt_ref, carry_ref, *, n_exp):
    i = pl.program_id(0)
    tn = h_ref.shape[0]
    gsz = n_exp // N_GROUPS

    @pl.when(i == 0)
    def _():
        carry_ref[...] = jnp.zeros_like(carry_ref)

    x = h_ref[...]
    xh = x.astype(jnp.bfloat16)
    xl = (x - xh.astype(jnp.float32)).astype(jnp.bfloat16)
    logits = _dot_nt(wh_ref[...], xh) + (_dot_nt(wh_ref[...], xl) + _dot_nt(wl_ref[...], xh))
    scores = 1.0 / (1.0 + jnp.exp(-logits))
    biased = scores + rb_ref[...]

    g3 = biased.reshape(N_GROUPS, gsz, tn)
    it3 = lax.broadcasted_iota(jnp.int32, (N_GROUPS, gsz, tn), 1)
    mx1 = jnp.max(g3, axis=1, keepdims=True)
    first = jnp.min(jnp.where(g3 == mx1, it3, gsz), axis=1, keepdims=True)
    mx2 = jnp.max(jnp.where(it3 == first, -jnp.inf, g3), axis=1, keepdims=True)
    gscore = (mx1 + mx2).reshape(N_GROUPS, tn)

    itg = lax.broadcasted_iota(jnp.int32, (N_GROUPS, tn), 0)
    gsel = jnp.zeros((N_GROUPS, tn), jnp.bool_)
    cur = gscore
    for _ in range(TOPK_GROUPS):
        mx = jnp.max(cur, axis=0, keepdims=True)
        fi = jnp.min(jnp.where(cur == mx, itg, N_GROUPS), axis=0, keepdims=True)
        hit = itg == fi
        gsel = jnp.logical_or(gsel, hit)
        cur = jnp.where(hit, -jnp.inf, cur)
    emask = jnp.broadcast_to(gsel.reshape(N_GROUPS, 1, tn), (N_GROUPS, gsz, tn)).reshape(n_exp, tn)
    cur = jnp.where(emask, biased, NEG)

    ite = lax.broadcasted_iota(jnp.int32, (n_exp, tn), 0)
    hits = []
    eidx = []
    wsel = []
    for _ in range(TOP_K):
        mx = jnp.max(cur, axis=0, keepdims=True)
        fi = jnp.min(jnp.where(cur == mx, ite, n_exp), axis=0, keepdims=True)
        hit = ite == fi
        hits.append(hit)
        eidx.append(fi)
        wsel.append(jnp.sum(jnp.where(hit, scores, 0.0), axis=0, keepdims=True))
        cur = jnp.where(hit, -jnp.inf, cur)
    sel = hits[0]
    for hit in hits[1:]:
        sel = jnp.logical_or(sel, hit)
    self32 = jnp.where(sel, 1.0, 0.0)

    carry_ref[...] = carry_ref[...] + jnp.sum(self32, axis=1, keepdims=True)
    cnt_ref[...] = carry_ref[...]

    wcat = jnp.concatenate(wsel, axis=0)
    wcat = wcat / jnp.sum(wcat, axis=0, keepdims=True) * ROUTED_SCALE
    tok = i * tn + lax.broadcasted_iota(jnp.int32, (TOP_K, tn), 1)
    slot = lax.broadcasted_iota(jnp.int32, (TOP_K, tn), 0)
    e_ref[...] = jnp.concatenate(eidx, axis=0) * (1 << KEY_SHIFT) + (tok * TOP_K + slot)
    w_ref[...] = wcat


def _router(h1, wr_hi, wr_lo, rbias):
    t, d = h1.shape
    n_exp = wr_hi.shape[0]
    tn = min(512, t)
    kern = functools.partial(_router_kernel, n_exp=n_exp)
    row = pl.BlockSpec((TOP_K, tn), lambda i: (0, i))
    return pl.pallas_call(
        kern,
        out_shape=(
            jax.ShapeDtypeStruct((TOP_K, t), jnp.int32),
            jax.ShapeDtypeStruct((TOP_K, t), jnp.float32),
            jax.ShapeDtypeStruct((n_exp, 1), jnp.float32),
        ),
        grid=(t // tn,),
        in_specs=[
            pl.BlockSpec((tn, d), lambda i: (i, 0)),
            pl.BlockSpec((n_exp, d), lambda i: (0, 0)),
            pl.BlockSpec((n_exp, d), lambda i: (0, 0)),
            pl.BlockSpec((n_exp, 1), lambda i: (0, 0)),
        ],
        out_specs=(row, row, pl.BlockSpec((n_exp, 1), lambda i: (0, 0))),
        scratch_shapes=[pltpu.VMEM((n_exp, 1), jnp.float32)],
        compiler_params=_cparams(("arbitrary",)),
        name="router",
    )(h1, wr_hi, wr_lo, rbias)


def _sc_move_rows(x_slab, idx, n_out, scatter):
    m = idx.shape[0]
    rnd = SC_GRP * SC_NBUF
    per = m // SC_WORKERS
    assert m % (SC_WORKERS * rnd) == 0
    row = x_slab.shape[1:]
    mesh = plsc.VectorSubcoreMesh(core_axis_name="c", subcore_axis_name="s")

    @pl.kernel(out_type=jax.ShapeDtypeStruct((n_out,) + row, x_slab.dtype), mesh=mesh,
               scratch_types=[pltpu.VMEM((rnd,), jnp.int32), pltpu.VMEM((SC_NBUF, SC_GRP) + row, x_slab.dtype),
                              pltpu.SemaphoreType.DMA((SC_NBUF,)), pltpu.SemaphoreType.DMA((SC_NBUF,))])
    def kern(x_hbm, i_hbm, o_hbm, ibuf, buf, lsem, ssem):
        worker = lax.axis_index("c") * (SC_WORKERS // 2) + lax.axis_index("s")
        base = worker * per

        @pl.loop(0, per // rnd)
        def _(r):
            off = base + r * rnd
            pltpu.sync_copy(i_hbm.at[pl.ds(off, rnd)], ibuf)
            loads = []
            stores = []
            for b in range(SC_NBUF):
                indexed = ibuf.at[pl.ds(b * SC_GRP, SC_GRP)]
                linear = pl.ds(off + b * SC_GRP, SC_GRP)
                src = x_hbm.at[linear] if scatter else x_hbm.at[indexed]
                dst = o_hbm.at[indexed] if scatter else o_hbm.at[linear]
                loads.append(pltpu.make_async_copy(src, buf.at[b], lsem.at[b]))
                stores.append(pltpu.make_async_copy(buf.at[b], dst, ssem.at[b]))
            for ld in loads:
                ld.start()
            for b in range(SC_NBUF):
                loads[b].wait()
                stores[b].start()
            for st in stores:
                st.wait()

    return kern(x_slab, idx)


X_BUFS = 3
Y_BUFS = 2


def _experts_kernel(te_ref, tv_ref, tu_ref, na_ref, xs_hbm, wg_ref, wu_ref, wd_ref, ys_hbm,
                    xb0, xb1, xb2, yb0, yb1, wgu_s, wd_s, xsem, ysem):
    i = pl.program_id(0)
    nt = pl.num_programs(0)
    n_act = na_ref[0]
    xbufs = (xb0, xb1, xb2)
    ybufs = (yb0, yb1)
    tile_rows = TM_MOE * SLAB

    def x_copy(slot, tile):
        tile = jnp.minimum(tile, nt - 1)
        start = pl.multiple_of(tu_ref[tile] * SLAB, SLAB)
        return pltpu.make_async_copy(xs_hbm.at[pl.ds(start, tile_rows), :], xbufs[slot], xsem.at[slot])

    def y_pieces(slot, tile, go):
        nvalid = tv_ref[tile]
        base = tu_ref[tile]
        size = TM_MOE
        while size >= 1:
            @pl.when((nvalid & size) != 0)
            def _(size=size):
                off = nvalid & ~(2 * size - 1)
                src = ybufs[slot].at[pl.ds(pl.multiple_of(off * SLAB, SLAB), size * SLAB), :]
                dst = ys_hbm.at[pl.ds(pl.multiple_of((base + off) * SLAB, SLAB), size * SLAB), :]
                go(pltpu.make_async_copy(src, dst, ysem.at[slot]))
            size //= 2

    @pl.when(i == 0)
    def _():
        x_copy(0, 0).start()
        x_copy(1, 1).start()

    iprev = jnp.maximum(i - 1, 0)

    @pl.when(jnp.logical_and(i < n_act, jnp.logical_or(i == 0, te_ref[i] != te_ref[iprev])))
    def _():
        de = wg_ref.shape[2]
        wgu_s[:, 0:de] = wg_ref[0].astype(wgu_s.dtype)
        wgu_s[:, de:] = wu_ref[0].astype(wgu_s.dtype)
        wd_s[...] = wd_ref[0].astype(wd_s.dtype)

    phase = i % (X_BUFS * Y_BUFS)
    for c in range(X_BUFS * Y_BUFS):
        xs, ys = c % X_BUFS, c % Y_BUFS

        @pl.when(jnp.logical_and(i < n_act, phase == c))
        def _(xs=xs, ys=ys):
            x_copy(xs, i).wait()
            x_copy((xs + 2) % X_BUFS, i + 2).start()
            x = _slab_load(xbufs[xs], (), TM_MOE).astype(jnp.bfloat16)
            gu = _dot(x, wgu_s[...])
            de = gu.shape[1] // 2
            gate = gu[:, :de]
            hid = (gate / (1.0 + jnp.exp(-gate))) * gu[:, de:]
            y = _dot(hid.astype(jnp.bfloat16), wd_s[...])

            @pl.when(i >= Y_BUFS)
            def _():
                y_pieces(ys, i - Y_BUFS, lambda cp: cp.wait())

            _slab_store(ybufs[ys], (), y)
            y_pieces(ys, i, lambda cp: cp.start(priority=1))

        @pl.when(jnp.logical_and(i == n_act, phase == c))
        def _(xs=xs, ys=ys):
            x_copy(xs, i).wait()
            x_copy((xs + 1) % X_BUFS, i + 1).wait()

            @pl.when(i >= 2)
            def _():
                y_pieces(ys, i - 2, lambda cp: cp.wait())

            @pl.when(i >= 1)
            def _():
                y_pieces(1 - ys, i - 1, lambda cp: cp.wait())


def _experts(xs_slab, wg, wu, wd, tile_e, tile_valid, tile_u, n_active):
    nt = tile_e.shape[0]
    d = wg.shape[1]
    de = wg.shape[2]
    buf = pltpu.VMEM((TM_MOE * SLAB, LANES), jnp.float32)
    return pl.pallas_call(
        _experts_kernel,
        out_shape=jax.ShapeDtypeStruct(xs_slab.shape, jnp.float32),
        grid_spec=pltpu.PrefetchScalarGridSpec(
            num_scalar_prefetch=4,
            grid=(nt,),
            in_specs=[
                pl.BlockSpec(memory_space=pl.ANY),
                pl.BlockSpec((1, d, de), lambda i, te, tv, tu, na: (te[i], 0, 0)),
                pl.BlockSpec((1, d, de), lambda i, te, tv, tu, na: (te[i], 0, 0)),
                pl.BlockSpec((1, de, d), lambda i, te, tv, tu, na: (te[i], 0, 0)),
            ],
            out_specs=pl.BlockSpec(memory_space=pl.ANY),
            scratch_shapes=[buf] * (X_BUFS + Y_BUFS) + [
                pltpu.VMEM((d, 2 * de), jnp.bfloat16), pltpu.VMEM((de, d), jnp.bfloat16),
                pltpu.SemaphoreType.DMA((X_BUFS,)), pltpu.SemaphoreType.DMA((Y_BUFS,))],
        ),
        compiler_params=_cparams(("arbitrary",)),
        name="experts",
    )(tile_e, tile_valid, tile_u, n_active, xs_slab, wg, wu, wd)


def _combine_kernel(h_ref, w_ref, *rest):
    y_refs = rest[:TOP_K]
    sgu_ref, sd_ref, g_ref, b_ref, o_ref = rest[TOP_K:]
    tn = h_ref.shape[0]
    h = h_ref[...]
    gu = _dot(h.astype(jnp.bfloat16), sgu_ref[...])
    ds = gu.shape[1] // 2
    gate = gu[:, :ds]
    hid = (gate / (1.0 + jnp.exp(-gate))) * gu[:, ds:]
    shared = _dot(hid.astype(jnp.bfloat16), sd_ref[...])

    w = w_ref[...]
    routed = _slab_load(y_refs[0], (), tn) * w[:, 0:1]
    for k in range(1, TOP_K):
        routed = routed + _slab_load(y_refs[k], (), tn) * w[:, k:k + 1]
    o_ref[...] = _layernorm_f32(ALPHA * h + (routed + shared), g_ref[...], b_ref[...])


def _combine(h1, w_tok, y_slab, sgu, sd, g, b):
    t, d = h1.shape
    tn = min(256, t)
    nblk = t // tn
    ds2 = sgu.shape[1]
    vec = pl.BlockSpec((1, d), lambda i: (0, 0))

    def slot_spec(k):
        return pl.BlockSpec((tn * SLAB, LANES), lambda i: (k * nblk + i, 0))

    return pl.pallas_call(
        _combine_kernel,
        out_shape=jax.ShapeDtypeStruct((t, d), jnp.float32),
        grid=(nblk,),
        in_specs=[
            pl.BlockSpec((tn, d), lambda i: (i, 0)),
            pl.BlockSpec((tn, TOP_K), lambda i: (i, 0)),
            *[slot_spec(k) for k in range(TOP_K)],
            pl.BlockSpec((d, ds2), lambda i: (0, 0)),
            pl.BlockSpec((ds2 // 2, d), lambda i: (0, 0)),
            vec, vec,
        ],
        out_specs=pl.BlockSpec((tn, d), lambda i: (i, 0)),
        compiler_params=_cparams(("parallel",)),
        name="combine_ln2",
    )(h1, w_tok, *([y_slab] * TOP_K), sgu, sd, g, b)


def _rel_bucket(rel):
    nb = N_BUCKETS // 2
    max_exact = nb // 2
    ret = jnp.where(rel > 0, nb, 0)
    n = jnp.abs(rel)
    nf = jnp.maximum(n, 1).astype(jnp.float32)
    large = max_exact + (jnp.log(nf / max_exact) / math.log(MAX_DISTANCE / max_exact) * (nb - max_exact)).astype(jnp.int32)
    large = jnp.minimum(large, nb - 1)
    return ret + jnp.where(n < max_exact, n, large)


def _bias_of_rel(rel_bias, rel):
    return rel_bias.astype(jnp.float32)[_rel_bucket(rel)]


def _toeplitz(vec, nrow, ncol, off):
    lo = off - (nrow - 1)
    v = vec[lo:off + ncol]
    p = v.shape[0] + 1
    v = jnp.concatenate([v, v[:1]], axis=0)
    flat = jnp.tile(v, (nrow + 1, 1))
    base = off - lo
    out = flat[base:base + nrow * (p - 1)].reshape(nrow, p - 1, vec.shape[1])
    return out[:, :ncol]


def _bias_tables_a(rel_bias, sink):
    bias_a = rel_bias[:, :A_Q_HEADS]
    m = jnp.arange(3 * TQ_A + TQ_A - 1, dtype=jnp.int32)
    rel = 2 * TQ_A - 1 - m
    vec = jnp.where((jnp.abs(rel) <= WINDOW)[:, None], _bias_of_rel(bias_a, rel), NEG)
    band = _toeplitz(vec, 3 * TQ_A, TQ_A, 3 * TQ_A - 1)
    mm = jnp.arange(TQ_A + N_META - 1, dtype=jnp.int32)
    meta_first = _toeplitz(_bias_of_rel(bias_a, -1 - mm), N_META, TQ_A, N_META - 1)
    meta_far = jnp.broadcast_to(_bias_of_rel(bias_a, jnp.int32(-2 * MAX_DISTANCE)), (N_META, TQ_A, A_Q_HEADS))
    pad = jnp.full((NK_A - 3 * TQ_A - N_META, TQ_A, A_Q_HEADS), NEG, jnp.float32)
    blocked = jnp.full((TQ_A, TQ_A, A_Q_HEADS), NEG, jnp.float32)
    first = jnp.concatenate([blocked, band[TQ_A:], meta_first, pad])
    middle = jnp.concatenate([band, meta_far, pad])
    last = jnp.concatenate([band[:2 * TQ_A], blocked, meta_far, pad])
    tab = jnp.stack([first, middle, last])
    tab = tab.reshape(3, NK_A, TQ_A, A_Q_HEADS // 2, 2)
    tab = jnp.transpose(tab, (0, 3, 1, 4, 2)).reshape(3, A_Q_HEADS // 2, NK_A, 2 * TQ_A) * LOG2E
    sink_rows = jnp.repeat(sink.astype(jnp.float32).reshape(A_Q_HEADS // 2, 1, 2), TQ_A, axis=2) * LOG2E
    return tab, sink_rows


def _bias_tables_b(rel_bias, s):
    bias_b = rel_bias[:, A_Q_HEADS:]
    near = []
    for d in (-1, 0, 1):
        m = jnp.arange(TQ_B + TK_B - 1, dtype=jnp.int32)
        vec = _bias_of_rel(bias_b, TK_B * d + TK_B - 1 - m)
        near.append(_toeplitz(vec, TK_B, TQ_B, TK_B - 1))
    far_l = jnp.broadcast_to(_bias_of_rel(bias_b, jnp.int32(-TK_B - 1)), (TK_B, TQ_B, B_HEADS))
    far_r = jnp.broadcast_to(_bias_of_rel(bias_b, jnp.int32(TK_B + 1)), (TK_B, TQ_B, B_HEADS))
    tabs = jnp.transpose(jnp.stack([far_l] + near + [far_r]), (3, 0, 1, 2)) * LOG2E
    m = jnp.arange(s + N_META - 1, dtype=jnp.int32)
    vec = _bias_of_rel(bias_b, -1 - m)
    meta = jnp.transpose(_toeplitz(vec, N_META, s, N_META - 1), (2, 0, 1)) * LOG2E
    return tabs, meta


def _prep_w_in(w_in):
    a_w = A_Q_HEADS * HEAD_DIM
    kv = A_KV_HEADS * HEAD_DIM
    bqk = B_HEADS * 2 * HEAD_DIM
    scale = HEAD_DIM ** -0.5
    qa = w_in[:, :a_w] * (scale * LOG2E)
    ka = w_in[:, a_w:a_w + kv]
    va = w_in[:, a_w + kv:a_w + 2 * kv]
    o = a_w + 2 * kv
    qb = w_in[:, o:o + bqk] * (scale * LOG2E)
    kb = w_in[:, o + bqk:o + 2 * bqk]
    vb = w_in[:, o + 2 * bqk:]

    def dup(w):
        return jnp.concatenate([w[:, g * HEAD_DIM:(g + 1) * HEAD_DIM] for g in range(A_KV_HEADS) for _ in range(2)], axis=1)

    w = jnp.concatenate([qa, dup(ka), qb, kb], axis=1).astype(jnp.bfloat16)
    return w, jnp.concatenate([vb, va], axis=1).T.astype(jnp.bfloat16)


def _trunk_front(x, prm):
    bsz, s, d = x.shape
    t = bsz * s
    x2d = x.reshape(t, d)
    proj, vt, vta = _ln_inproj(x2d, prm["ln_emb_g"], prm["ln_emb_b"], prm["w_in"], prm["w_vt"])
    proj3 = proj.reshape(bsz, s, PROJ_COLS)
    oa = _mixer_a(proj3, vta, prm["km_a"], prm["vtm_a"], prm["tab_a"], prm["sink_rows"])
    tabs_b, meta_b = _bias_tables_b(prm["rel_bias"], s)
    ob = _mixer_b(proj3, vt, prm["proj_meta"], prm["vt_meta"], tabs_b, meta_b, prm["lam"].reshape(1).astype(jnp.float32),
                  prm["subln_g"])
    h1, h1_slab = _outproj_ln1(x2d, oa.reshape(t, -1), ob.reshape(t, -1), prm["w_out"], prm["ln_emb_g"],
                               prm["ln_emb_b"], prm["ln1_g"], prm["ln1_b"])

    keys, wts, counts = _router(h1, prm["wr_hi"], prm["wr_lo"], prm["router_bias"])
    n_exp = counts.shape[0]
    n_asg = t * TOP_K
    assert n_asg <= (1 << KEY_SHIFT) and n_asg % LANES == 0
    order = jnp.sort(keys.reshape(n_asg)) & ((1 << KEY_SHIFT) - 1)
    tok = order >> 3
    dst = (order & (TOP_K - 1)) * t + tok
    sc_pad = SC_WORKERS * SC_GRP * SC_NBUF
    assert sc_pad >= TM_MOE and n_asg % sc_pad == 0
    xs = _sc_move_rows(h1_slab.reshape(t, SLAB, LANES), jnp.concatenate([tok, jnp.zeros((sc_pad,), jnp.int32)]),
                       n_asg + sc_pad, scatter=False)
    xs_slab = xs.reshape((n_asg + sc_pad) * SLAB, LANES)
    counts = counts[:, 0].astype(jnp.int32)
    tiles_e = (counts + TM_MOE - 1) // TM_MOE
    tend = jnp.cumsum(tiles_e)
    tstart = tend - tiles_e
    ustart = jnp.cumsum(counts) - counts
    nt = n_asg // TM_MOE + n_exp + 1
    tid = jnp.arange(nt, dtype=jnp.int32)
    tile_e = jnp.minimum(jnp.sum((tend[None, :] <= tid[:, None]).astype(jnp.int32), axis=1), n_exp - 1)
    onehot = (tile_e[:, None] == jnp.arange(n_exp, dtype=jnp.int32)[None, :]).astype(jnp.int32)
    in_e = (tid - jnp.sum(onehot * tstart[None, :], axis=1)) * TM_MOE
    active = tid < tend[-1]
    tile_valid = jnp.where(active, jnp.clip(jnp.sum(onehot * counts[None, :], axis=1) - in_e, 0, TM_MOE), 0)
    tile_u = jnp.where(active, jnp.sum(onehot * ustart[None, :], axis=1) + in_e, 0)

    tiles = (tile_e.astype(jnp.int32), tile_valid.astype(jnp.int32), tile_u.astype(jnp.int32),
             tend[-1:].astype(jnp.int32))
    return h1, wts.T, xs_slab, dst, tiles, x.shape


def _trunk_back(front, prm):
    h1, w_tok, xs_slab, dst, tiles, shape = front
    n_asg = dst.shape[0]
    ys_slab = _experts(xs_slab, prm["w_gate"], prm["w_up"], prm["w_down"], *tiles)
    y = _sc_move_rows(ys_slab.reshape(-1, SLAB, LANES), dst, n_asg, scatter=True)
    out = _combine(h1, w_tok, y.reshape(n_asg * SLAB, LANES), prm["ws_gu"], prm["ws_down"], prm["ln2_g"], prm["ln2_b"])
    return out.reshape(shape)


def kernel(x_prompt, x_sample, meta_tokens, ln_emb_g, ln_emb_b, rel_bias, w_in, attn_sink, lambda_q1, lambda_k1, lambda_q2, lambda_k2, subln_g, w_out, ln1_g, ln1_b, w_router, router_bias, w_gate, w_up, w_down, ws_gate, ws_up, ws_down, ln2_g, ln2_b):
    f32 = jnp.float32
    bf16 = jnp.bfloat16
    l = 0
    row = lambda v: v.reshape(1, -1).astype(f32)
    lam = (jnp.exp(jnp.sum(lambda_q1[l].astype(f32) * lambda_k1[l].astype(f32)))
           - jnp.exp(jnp.sum(lambda_q2[l].astype(f32) * lambda_k2[l].astype(f32))) + LAMBDA_INIT)
    wr_t = w_router[l].astype(f32).T
    wr_hi = wr_t.astype(bf16)
    prm = {
        "ln_emb_g": row(ln_emb_g), "ln_emb_b": row(ln_emb_b),
        "rel_bias": rel_bias,
        "lam": lam,
        "subln_g": subln_g[l].astype(f32).reshape(-1, 1),
        "w_out": w_out[l].astype(bf16),
        "ln1_g": row(ln1_g[l]), "ln1_b": row(ln1_b[l]),
        "wr_hi": wr_hi, "wr_lo": (wr_t - wr_hi.astype(f32)).astype(bf16),
        "router_bias": router_bias[l].astype(f32).reshape(-1, 1),
        "w_gate": w_gate[l], "w_up": w_up[l], "w_down": w_down[l],
        "ws_gu": jnp.concatenate([ws_gate[l], ws_up[l]], axis=-1).astype(bf16),
        "ws_down": ws_down[l].astype(bf16),
        "ln2_g": row(ln2_g[l]), "ln2_b": row(ln2_b[l]),
    }
    prm["w_in"], prm["w_vt"] = _prep_w_in(w_in[l])
    prm["proj_meta"], prm["vt_meta"], vta_meta = _ln_inproj(meta_tokens.astype(f32), prm["ln_emb_g"], prm["ln_emb_b"],
                                                           prm["w_in"], prm["w_vt"])
    meta_pad = NK_A - 3 * TQ_A - N_META
    prm["km_a"] = jnp.pad(prm["proj_meta"][:, KA_BLK * LANES:(KA_BLK + 2) * LANES], ((0, meta_pad), (0, 0)))
    prm["vtm_a"] = jnp.pad(vta_meta, ((0, 0), (0, meta_pad)))
    prm["tab_a"], prm["sink_rows"] = _bias_tables_a(rel_bias, attn_sink[l])
    front_p = _trunk_front(x_prompt, prm)
    front_s = _trunk_front(x_sample, prm)
    return (_trunk_back(front_p, prm), _trunk_back(front_s, prm))
```

```python
import functools
import math

import jax
import jax.numpy as jnp
from jax import lax
from jax.experimental import pallas as pl
from jax.experimental.pallas import tpu as pltpu
from jax.experimental.pallas import tpu_sc as plsc

N_META = 16
HEAD_DIM = 64
WINDOW = 128
A_Q_HEADS = 8
A_KV_HEADS = 2
B_HEADS = 4
N_BUCKETS = 32
MAX_DISTANCE = 128
TOP_K = 8
N_GROUPS = 8
TOPK_GROUPS = 4
ROUTED_SCALE = 2.5
LN_EPS = 1e-5
DEPTH = 1
ALPHA = (2 * DEPTH) ** 0.25
NEG = -1e30
LAMBDA_INIT = 0.8 - 0.6 * math.exp(-0.3 * 0)

LANES = 128
VMEM_LIMIT = 48 * 1024 * 1024

QA_BLK = 0
KA_BLK = 4
QB_BLK = 6
KB_BLK = 10
PROJ_COLS = 14 * LANES
VT_ROWS = LANES + 16
VA_ROWS = HEAD_DIM + 16
LOG2E = 1.4426950408889634

TQ_A = 128
NK_A = 4 * TQ_A
TQ_B = 512
TK_B = 512
TM_MOE = 512
SLAB = 8
SC_GRP = 16
SC_NBUF = 4
SC_WORKERS = 32
KEY_SHIFT = 20


def _cparams(sem):
    return pltpu.CompilerParams(dimension_semantics=sem, vmem_limit_bytes=VMEM_LIMIT)


def _layernorm_f32(x, g, b):
    mu = jnp.mean(x, axis=-1, keepdims=True)
    xc = x - mu
    var = jnp.mean(xc * xc, axis=-1, keepdims=True)
    return xc * lax.rsqrt(var + LN_EPS) * g + b


def _dot_nt(a, b):
    return lax.dot_general(a, b, (((1,), (1,)), ((), ())), preferred_element_type=jnp.float32)


def _dot(a, b):
    return jnp.dot(a, b, preferred_element_type=jnp.float32)


def _ln_inproj_kernel(x_ref, g_ref, b_ref, w_ref, wvt_ref, o_ref, vt_ref, vta_ref):
    h = _layernorm_f32(x_ref[...], g_ref[...], b_ref[...]).astype(jnp.bfloat16)
    o_ref[...] = _dot(h, w_ref[...]).astype(o_ref.dtype)
    vt = _dot_nt(wvt_ref[...], h).astype(vt_ref.dtype)
    ones = jnp.ones((16, vt.shape[1]), vt_ref.dtype)
    for hd in range(B_HEADS):
        vt_ref[hd * VT_ROWS:hd * VT_ROWS + LANES, :] = vt[hd * LANES:(hd + 1) * LANES, :]
        vt_ref[hd * VT_ROWS + LANES:(hd + 1) * VT_ROWS, :] = ones
    base = B_HEADS * LANES
    for g in range(A_KV_HEADS):
        vta_ref[g * VA_ROWS:g * VA_ROWS + HEAD_DIM, :] = vt[base + g * HEAD_DIM:base + (g + 1) * HEAD_DIM, :]
        vta_ref[g * VA_ROWS + HEAD_DIM:(g + 1) * VA_ROWS, :] = ones


def _ln_inproj(x2d, g, b, w, wvt):
    t, d = x2d.shape
    n = w.shape[1]
    tm = min(512, t)
    return pl.pallas_call(
        _ln_inproj_kernel,
        out_shape=(jax.ShapeDtypeStruct((t, n), jnp.bfloat16),
                   jax.ShapeDtypeStruct((B_HEADS * VT_ROWS, t), jnp.bfloat16),
                   jax.ShapeDtypeStruct((A_KV_HEADS * VA_ROWS, t), jnp.bfloat16)),
        grid=(t // tm,),
        in_specs=[
            pl.BlockSpec((tm, d), lambda i: (i, 0)),
            pl.BlockSpec((1, d), lambda i: (0, 0)),
            pl.BlockSpec((1, d), lambda i: (0, 0)),
            pl.BlockSpec((d, n), lambda i: (0, 0)),
            pl.BlockSpec((wvt.shape[0], d), lambda i: (0, 0)),
        ],
        out_specs=(pl.BlockSpec((tm, n), lambda i: (i, 0)),
                   pl.BlockSpec((B_HEADS * VT_ROWS, tm), lambda i: (0, i)),
                   pl.BlockSpec((A_KV_HEADS * VA_ROWS, tm), lambda i: (0, i))),
        compiler_params=_cparams(("parallel",)),
        name="ln_inproj",
    )(x2d, g, b, w, wvt)


def _mixer_a_kernel(q_ref, k_ref, vt_ref, km_ref, vtm_ref, tab_ref, sink_ref, o_ref, *, nblk, sub):
    i = pl.program_id(1)
    lane = lax.broadcasted_iota(jnp.int32, (1, LANES), 1)
    lo = lane < HEAD_DIM
    for j in range(sub):
        gi = i * sub + j
        sp = pl.multiple_of(jnp.maximum(gi - 1, 0) * TQ_A, TQ_A)
        sc = pl.multiple_of(gi * TQ_A, TQ_A)
        sn = pl.multiple_of(jnp.minimum(gi + 1, nblk - 1) * TQ_A, TQ_A)
        variant = jnp.where(gi == 0, 0, jnp.where(gi == nblk - 1, 2, 1))
        rows = slice(j * TQ_A, (j + 1) * TQ_A)
        for g in range(A_KV_HEADS):
            gs = slice(g * LANES, (g + 1) * LANES)
            vr = slice(g * VA_ROWS, (g + 1) * VA_ROWS)
            k_all = jnp.concatenate([k_ref[0, pl.ds(sp, TQ_A), gs], k_ref[0, pl.ds(sc, TQ_A), gs],
                                     k_ref[0, pl.ds(sn, TQ_A), gs], km_ref[:, gs]], axis=0)
            vt_all = jnp.concatenate([vt_ref[vr, pl.ds(sp, TQ_A)], vt_ref[vr, pl.ds(sc, TQ_A)],
                                      vt_ref[vr, pl.ds(sn, TQ_A)], vtm_ref[vr, :]], axis=1)
            for pp in range(2):
                hp = 2 * g + pp
                cols = slice(hp * LANES, (hp + 1) * LANES)
                qc = q_ref[0, rows, cols]
                q2 = jnp.concatenate([jnp.where(lo, qc, jnp.zeros_like(qc)), jnp.where(lo, jnp.zeros_like(qc), qc)],
                                     axis=0)
                s = _dot_nt(k_all, q2) + tab_ref[variant, hp]
                sink = sink_ref[hp]
                m = jnp.maximum(jnp.max(s, axis=0, keepdims=True), sink)
                p = jnp.exp2(s - m)
                acc = _dot(vt_all, p.astype(jnp.bfloat16))
                o = acc[0:HEAD_DIM, :] / (acc[HEAD_DIM:HEAD_DIM + 1, :] + jnp.exp2(sink - m))
                o2 = jnp.concatenate([o[:, :TQ_A], o[:, TQ_A:]], axis=0)
                o_ref[0, rows, cols] = o2.T.astype(o_ref.dtype)


def _mixer_a(proj3, vta, km, vtm, tab, sink):
    bsz, s, _ = proj3.shape
    nblk = s // TQ_A
    assert nblk >= 2
    sub = 4 if nblk % 4 == 0 else 1
    nq = nblk // sub
    tq = sub * TQ_A
    kern = functools.partial(_mixer_a_kernel, nblk=nblk, sub=sub)
    return pl.pallas_call(
        kern,
        out_shape=jax.ShapeDtypeStruct((bsz, s, A_Q_HEADS * HEAD_DIM), jnp.bfloat16),
        grid=(bsz, nq),
        in_specs=[
            pl.BlockSpec((1, tq, 4 * LANES), lambda b, i: (b, i, QA_BLK // 4)),
            pl.BlockSpec((1, s, 2 * LANES), lambda b, i: (b, 0, KA_BLK // 2)),
            pl.BlockSpec((A_KV_HEADS * VA_ROWS, s), lambda b, i: (0, b)),
            pl.BlockSpec(km.shape, lambda b, i: (0, 0)),
            pl.BlockSpec(vtm.shape, lambda b, i: (0, 0)),
            pl.BlockSpec(tab.shape, lambda b, i: (0, 0, 0, 0)),
            pl.BlockSpec(sink.shape, lambda b, i: (0, 0, 0)),
        ],
        out_specs=pl.BlockSpec((1, tq, 4 * LANES), lambda b, i: (b, i, 0)),
        compiler_params=_cparams(("parallel", "arbitrary")),
        name="mixer_a",
    )(proj3, proj3, vta, km, vtm, tab, sink)


def _mixer_b_kernel(sc_ref, q_ref, k_ref, vt_ref, km_ref, vtm_ref, tab_ref, tabm_ref, g_ref, o_ref,
                    m_ref, acc_ref, s0_ref, s1_ref, cmax_ref, *, nk):
    i = pl.program_id(2)
    lane = lax.broadcasted_iota(jnp.int32, (1, LANES), 1)
    lo = lane < HEAD_DIM
    q = q_ref[0]
    qs = (jnp.where(lo, q, jnp.zeros_like(q)), jnp.where(lo, jnp.zeros_like(q), q))
    lam = sc_ref[0]
    sbufs = (s0_ref, s1_ref)

    km = km_ref[...]
    vtm = vtm_ref[...]
    for c in range(2):
        s = _dot_nt(km, qs[c]) + tabm_ref[0]
        m = jnp.max(s, axis=0, keepdims=True)
        p = jnp.exp2(s - m)
        m_ref[c] = m
        acc_ref[c] = _dot(vtm, p.astype(jnp.bfloat16))

    def scores(j, slot):
        start = pl.multiple_of(j * TK_B, TK_B)
        kj = k_ref[0, pl.ds(start, TK_B), :]
        bias = tab_ref[0, jnp.clip(j - i, -2, 2) + 2]
        for c in range(2):
            s = _dot_nt(kj, qs[c]) + bias
            sbufs[slot][c] = s
            cmax_ref[slot, c] = jnp.max(s, axis=0, keepdims=True)

    def accumulate(j, slot):
        start = pl.multiple_of(j * TK_B, TK_B)
        vtj = vt_ref[:, pl.ds(start, TK_B)]
        for c in range(2):
            s = sbufs[slot][c]
            m_prev = m_ref[c]
            m_new = jnp.maximum(m_prev, cmax_ref[slot, c])
            a = jnp.exp2(m_prev - m_new)
            p = jnp.exp2(s - m_new)
            m_ref[c] = m_new
            acc_ref[c] = a * acc_ref[c] + _dot(vtj, p.astype(jnp.bfloat16))

    scores(0, 0)

    def pair(jj, carry):
        j = 2 * jj
        scores(j + 1, 1)
        accumulate(j, 0)
        scores(j + 2, 0)
        accumulate(j + 1, 1)
        return carry

    lax.fori_loop(0, nk // 2 - 1, pair, 0)
    scores(nk - 1, 1)
    accumulate(nk - 2, 0)
    accumulate(nk - 1, 1)

    o0 = acc_ref[0, 0:LANES, :] / acc_ref[0, LANES:LANES + 1, :]
    o1 = acc_ref[1, 0:LANES, :] / acc_ref[1, LANES:LANES + 1, :]
    o = o0 - lam * o1
    ms = jnp.mean(o * o, axis=0, keepdims=True)
    o = o * lax.rsqrt(ms + LN_EPS) * (g_ref[...] * (1.0 - LAMBDA_INIT))
    o_ref[0] = o.T.astype(o_ref.dtype)


def _mixer_b(proj3, vt, proj_meta, vt_meta, tab, tabm, scal, subln_g):
    bsz, s, _ = proj3.shape
    nq = s // TQ_B
    nk = s // TK_B
    assert nk % 2 == 0
    kern = functools.partial(_mixer_b_kernel, nk=nk)
    return pl.pallas_call(
        kern,
        out_shape=jax.ShapeDtypeStruct((bsz, s, B_HEADS * LANES), jnp.bfloat16),
        grid=(bsz, B_HEADS, nq),
        in_specs=[
            pl.BlockSpec(memory_space=pltpu.SMEM),
            pl.BlockSpec((1, TQ_B, LANES), lambda b, h, i: (b, i, QB_BLK + h)),
            pl.BlockSpec((1, s, LANES), lambda b, h, i: (b, 0, KB_BLK + h)),
            pl.BlockSpec((VT_ROWS, s), lambda b, h, i: (h, b)),
            pl.BlockSpec((N_META, LANES), lambda b, h, i: (0, KB_BLK + h)),
            pl.BlockSpec((VT_ROWS, N_META), lambda b, h, i: (h, 0)),
            pl.BlockSpec((1, 5, TK_B, TQ_B), lambda b, h, i: (h, 0, 0, 0)),
            pl.BlockSpec((1, N_META, TQ_B), lambda b, h, i: (h, 0, i)),
            pl.BlockSpec((LANES, 1), lambda b, h, i: (0, 0)),
        ],
        out_specs=pl.BlockSpec((1, TQ_B, LANES), lambda b, h, i: (b, i, h)),
        scratch_shapes=[
            pltpu.VMEM((2, 1, TQ_B), jnp.float32),
            pltpu.VMEM((2, VT_ROWS, TQ_B), jnp.float32),
            pltpu.VMEM((2, TK_B, TQ_B), jnp.float32),
            pltpu.VMEM((2, TK_B, TQ_B), jnp.float32),
            pltpu.VMEM((2, 2, 1, TQ_B), jnp.float32),
        ],
        compiler_params=_cparams(("parallel", "parallel", "arbitrary")),
        name="mixer_b",
    )(scal, proj3, proj3, vt, proj_meta, vt_meta, tab, tabm, subln_g)


def _slab_load(ref, lead, rows):
    return jnp.concatenate([ref[lead + (pl.ds(s, rows, stride=SLAB), slice(None))] for s in range(SLAB)], axis=1)


def _slab_store(ref, lead, val):
    rows = val.shape[0]
    for s in range(SLAB):
        ref[lead + (pl.ds(s, rows, stride=SLAB), slice(None))] = val[:, s * LANES:(s + 1) * LANES]


def _outproj_kernel(x_ref, oa_ref, ob_ref, w_ref, eg_ref, eb_ref, g_ref, b_ref, o_ref, os_ref):
    h0 = _layernorm_f32(x_ref[...], eg_ref[...], eb_ref[...])
    half = oa_ref.shape[1]
    mix = _dot(oa_ref[...], w_ref[0:half, :]) + _dot(ob_ref[...], w_ref[half:, :])
    h1 = _layernorm_f32(ALPHA * h0 + mix, g_ref[...], b_ref[...])
    o_ref[...] = h1
    _slab_store(os_ref, (), h1)


def _outproj_ln1(x2d, oa, ob, w_out, eg, eb, g, b):
    t, d = x2d.shape
    tm = min(512, t)
    wa = oa.shape[1]
    wb = ob.shape[1]
    vec = pl.BlockSpec((1, d), lambda i: (0, 0))
    return pl.pallas_call(
        _outproj_kernel,
        out_shape=(jax.ShapeDtypeStruct((t, d), jnp.float32),
                   jax.ShapeDtypeStruct((t * SLAB, LANES), jnp.float32)),
        grid=(t // tm,),
        in_specs=[
            pl.BlockSpec((tm, d), lambda i: (i, 0)),
            pl.BlockSpec((tm, wa), lambda i: (i, 0)),
            pl.BlockSpec((tm, wb), lambda i: (i, 0)),
            pl.BlockSpec((wa + wb, d), lambda i: (0, 0)),
            vec, vec, vec, vec,
        ],
        out_specs=(pl.BlockSpec((tm, d), lambda i: (i, 0)),
                   pl.BlockSpec((tm * SLAB, LANES), lambda i: (i, 0))),
        compiler_params=_cparams(("parallel",)),
        name="outproj_ln1",
    )(x2d, oa, ob, w_out, eg, eb, g, b)


def _router_kernel(h_ref, wh_ref, wl_ref, rb_ref, e_ref, w_ref, cnt_ref, carry_ref, *, n_exp):
    i = pl.program_id(0)
    tn = h_ref.shape[0]
    gsz = n_exp // N_GROUPS

    @pl.when(i == 0)
    def _():
        carry_ref[...] = jnp.zeros_like(carry_ref)

    x = h_ref[...]
    xh = x.astype(jnp.bfloat16)
    xl = (x - xh.astype(jnp.float32)).astype(jnp.bfloat16)
    logits = _dot_nt(wh_ref[...], xh) + (_dot_nt(wh_ref[...], xl) + _dot_nt(wl_ref[...], xh))
    scores = 1.0 / (1.0 + jnp.exp(-logits))
    biased = scores + rb_ref[...]

    g3 = biased.reshape(N_GROUPS, gsz, tn)
    it3 = lax.broadcasted_iota(jnp.int32, (N_GROUPS, gsz, tn), 1)
    mx1 = jnp.max(g3, axis=1, keepdims=True)
    first = jnp.min(jnp.where(g3 == mx1, it3, gsz), axis=1, keepdims=True)
    mx2 = jnp.max(jnp.where(it3 == first, -jnp.inf, g3), axis=1, keepdims=True)
    gscore = (mx1 + mx2).reshape(N_GROUPS, tn)

    itg = lax.broadcasted_iota(jnp.int32, (N_GROUPS, tn), 0)
    gsel = jnp.zeros((N_GROUPS, tn), jnp.bool_)
    cur = gscore
    for _ in range(TOPK_GROUPS):
        mx = jnp.max(cur, axis=0, keepdims=True)
        fi = jnp.min(jnp.where(cur == mx, itg, N_GROUPS), axis=0, keepdims=True)
        hit = itg == fi
        gsel = jnp.logical_or(gsel, hit)
        cur = jnp.where(hit, -jnp.inf, cur)
    emask = jnp.broadcast_to(gsel.reshape(N_GROUPS, 1, tn), (N_GROUPS, gsz, tn)).reshape(n_exp, tn)
    cur = jnp.where(emask, biased, NEG)

    ite = lax.broadcasted_iota(jnp.int32, (n_exp, tn), 0)
    hits = []
    eidx = []
    wsel = []
    for _ in range(TOP_K):
        mx = jnp.max(cur, axis=0, keepdims=True)
        fi = jnp.min(jnp.where(cur == mx, ite, n_exp), axis=0, keepdims=True)
        hit = ite == fi
        hits.append(hit)
        eidx.append(fi)
        wsel.append(jnp.sum(jnp.where(hit, scores, 0.0), axis=0, keepdims=True))
        cur = jnp.where(hit, -jnp.inf, cur)
    sel = hits[0]
    for hit in hits[1:]:
        sel = jnp.logical_or(sel, hit)
    self32 = jnp.where(sel, 1.0, 0.0)

    carry_ref[...] = carry_ref[...] + jnp.sum(self32, axis=1, keepdims=True)
    cnt_ref[...] = carry_ref[...]

    wcat = jnp.concatenate(wsel, axis=0)
    wcat = wcat / jnp.sum(wcat, axis=0, keepdims=True) * ROUTED_SCALE
    tok = i * tn + lax.broadcasted_iota(jnp.int32, (TOP_K, tn), 1)
    slot = lax.broadcasted_iota(jnp.int32, (TOP_K, tn), 0)
    e_ref[...] = jnp.concatenate(eidx, axis=0) * (1 << KEY_SHIFT) + (tok * TOP_K + slot)
    w_ref[...] = wcat


def _router(h1, wr_hi, wr_lo, rbias):
    t, d = h1.shape
    n_exp = wr_hi.shape[0]
    tn = min(512, t)
    kern = functools.partial(_router_kernel, n_exp=n_exp)
    row = pl.BlockSpec((TOP_K, tn), lambda i: (0, i))
    return pl.pallas_call(
        kern,
        out_shape=(
            jax.ShapeDtypeStruct((TOP_K, t), jnp.int32),
            jax.ShapeDtypeStruct((TOP_K, t), jnp.float32),
            jax.ShapeDtypeStruct((n_exp, 1), jnp.float32),
        ),
        grid=(t // tn,),
        in_specs=[
            pl.BlockSpec((tn, d), lambda i: (i, 0)),
            pl.BlockSpec((n_exp, d), lambda i: (0, 0)),
            pl.BlockSpec((n_exp, d), lambda i: (0, 0)),
            pl.BlockSpec((n_exp, 1), lambda i: (0, 0)),
        ],
        out_specs=(row, row, pl.BlockSpec((n_exp, 1), lambda i: (0, 0))),
        scratch_shapes=[pltpu.VMEM((n_exp, 1), jnp.float32)],
        compiler_params=_cparams(("arbitrary",)),
        name="router",
    )(h1, wr_hi, wr_lo, rbias)


def _sc_move_rows(x_slab, idx, n_out, scatter):
    m = idx.shape[0]
    rnd = SC_GRP * SC_NBUF
    per = m // SC_WORKERS
    assert m % (SC_WORKERS * rnd) == 0
    row = x_slab.shape[1:]
    mesh = plsc.VectorSubcoreMesh(core_axis_name="c", subcore_axis_name="s")

    @pl.kernel(out_type=jax.ShapeDtypeStruct((n_out,) + row, x_slab.dtype), mesh=mesh,
               scratch_types=[pltpu.VMEM((rnd,), jnp.int32), pltpu.VMEM((SC_NBUF, SC_GRP) + row, x_slab.dtype),
                              pltpu.SemaphoreType.DMA((SC_NBUF,)), pltpu.SemaphoreType.DMA((SC_NBUF,))])
    def kern(x_hbm, i_hbm, o_hbm, ibuf, buf, lsem, ssem):
        worker = lax.axis_index("c") * (SC_WORKERS // 2) + lax.axis_index("s")
        base = worker * per

        @pl.loop(0, per // rnd)
        def _(r):
            off = base + r * rnd
            pltpu.sync_copy(i_hbm.at[pl.ds(off, rnd)], ibuf)
            loads = []
            stores = []
            for b in range(SC_NBUF):
                indexed = ibuf.at[pl.ds(b * SC_GRP, SC_GRP)]
                linear = pl.ds(off + b * SC_GRP, SC_GRP)
                src = x_hbm.at[linear] if scatter else x_hbm.at[indexed]
                dst = o_hbm.at[indexed] if scatter else o_hbm.at[linear]
                loads.append(pltpu.make_async_copy(src, buf.at[b], lsem.at[b]))
                stores.append(pltpu.make_async_copy(buf.at[b], dst, ssem.at[b]))
            for ld in loads:
                ld.start()
            for b in range(SC_NBUF):
                loads[b].wait()
                stores[b].start()
            for st in stores:
                st.wait()

    return kern(x_slab, idx)


X_BUFS = 3
Y_BUFS = 2


def _experts_kernel(te_ref, tv_ref, tu_ref, na_ref, xs_hbm, wg_ref, wu_ref, wd_ref, ys_hbm,
                    xb0, xb1, xb2, yb0, yb1, wgu_s, wd_s, xsem, ysem):
    i = pl.program_id(0)
    nt = pl.num_programs(0)
    n_act = na_ref[0]
    xbufs = (xb0, xb1, xb2)
    ybufs = (yb0, yb1)
    tile_rows = TM_MOE * SLAB

    def x_copy(slot, tile):
        tile = jnp.minimum(tile, nt - 1)
        start = pl.multiple_of(tu_ref[tile] * SLAB, SLAB)
        return pltpu.make_async_copy(xs_hbm.at[pl.ds(start, tile_rows), :], xbufs[slot], xsem.at[slot])

    def y_pieces(slot, tile, go):
        nvalid = tv_ref[tile]
        base = tu_ref[tile]
        size = TM_MOE
        while size >= 1:
            @pl.when((nvalid & size) != 0)
            def _(size=size):
                off = nvalid & ~(2 * size - 1)
                src = ybufs[slot].at[pl.ds(pl.multiple_of(off * SLAB, SLAB), size * SLAB), :]
                dst = ys_hbm.at[pl.ds(pl.multiple_of((base + off) * SLAB, SLAB), size * SLAB), :]
                go(pltpu.make_async_copy(src, dst, ysem.at[slot]))
            size //= 2

    @pl.when(i == 0)
    def _():
        x_copy(0, 0).start()
        x_copy(1, 1).start()

    iprev = jnp.maximum(i - 1, 0)

    @pl.when(jnp.logical_and(i < n_act, jnp.logical_or(i == 0, te_ref[i] != te_ref[iprev])))
    def _():
        de = wg_ref.shape[2]
        wgu_s[:, 0:de] = wg_ref[0].astype(wgu_s.dtype)
        wgu_s[:, de:] = wu_ref[0].astype(wgu_s.dtype)
        wd_s[...] = wd_ref[0].astype(wd_s.dtype)

    phase = i % (X_BUFS * Y_BUFS)
    for c in range(X_BUFS * Y_BUFS):
        xs, ys = c % X_BUFS, c % Y_BUFS

        @pl.when(jnp.logical_and(i < n_act, phase == c))
        def _(xs=xs, ys=ys):
            x_copy(xs, i).wait()
            x_copy((xs + 2) % X_BUFS, i + 2).start()
            x = _slab_load(xbufs[xs], (), TM_MOE).astype(jnp.bfloat16)
            gu = _dot(x, wgu_s[...])
            de = gu.shape[1] // 2
            gate = gu[:, :de]
            hid = (gate / (1.0 + jnp.exp(-gate))) * gu[:, de:]
            y = _dot(hid.astype(jnp.bfloat16), wd_s[...])

            @pl.when(i >= Y_BUFS)
            def _():
                y_pieces(ys, i - Y_BUFS, lambda cp: cp.wait())

            _slab_store(ybufs[ys], (), y)
            y_pieces(ys, i, lambda cp: cp.start())

        @pl.when(jnp.logical_and(i == n_act, phase == c))
        def _(xs=xs, ys=ys):
            x_copy(xs, i).wait()
            x_copy((xs + 1) % X_BUFS, i + 1).wait()

            @pl.when(i >= 2)
            def _():
                y_pieces(ys, i - 2, lambda cp: cp.wait())

            @pl.when(i >= 1)
            def _():
                y_pieces(1 - ys, i - 1, lambda cp: cp.wait())


def _experts(xs_slab, wg, wu, wd, tile_e, tile_valid, tile_u, n_active):
    nt = tile_e.shape[0]
    d = wg.shape[1]
    de = wg.shape[2]
    buf = pltpu.VMEM((TM_MOE * SLAB, LANES), jnp.float32)
    return pl.pallas_call(
        _experts_kernel,
        out_shape=jax.ShapeDtypeStruct(xs_slab.shape, jnp.float32),
        grid_spec=pltpu.PrefetchScalarGridSpec(
            num_scalar_prefetch=4,
            grid=(nt,),
            in_specs=[
                pl.BlockSpec(memory_space=pl.ANY),
                pl.BlockSpec((1, d, de), lambda i, te, tv, tu, na: (te[i], 0, 0)),
                pl.BlockSpec((1, d, de), lambda i, te, tv, tu, na: (te[i], 0, 0)),
                pl.BlockSpec((1, de, d), lambda i, te, tv, tu, na: (te[i], 0, 0)),
            ],
            out_specs=pl.BlockSpec(memory_space=pl.ANY),
            scratch_shapes=[buf] * (X_BUFS + Y_BUFS) + [
                pltpu.VMEM((d, 2 * de), jnp.bfloat16), pltpu.VMEM((de, d), jnp.bfloat16),
                pltpu.SemaphoreType.DMA((X_BUFS,)), pltpu.SemaphoreType.DMA((Y_BUFS,))],
        ),
        compiler_params=_cparams(("arbitrary",)),
        name="experts",
    )(tile_e, tile_valid, tile_u, n_active, xs_slab, wg, wu, wd)


def _combine_kernel(h_ref, w_ref, *rest):
    y_refs = rest[:TOP_K]
    sgu_ref, sd_ref, g_ref, b_ref, o_ref = rest[TOP_K:]
    tn = h_ref.shape[0]
    h = h_ref[...]
    gu = _dot(h.astype(jnp.bfloat16), sgu_ref[...])
    ds = gu.shape[1] // 2
    gate = gu[:, :ds]
    hid = (gate / (1.0 + jnp.exp(-gate))) * gu[:, ds:]
    shared = _dot(hid.astype(jnp.bfloat16), sd_ref[...])

    w = w_ref[...]
    routed = _slab_load(y_refs[0], (), tn) * w[:, 0:1]
    for k in range(1, TOP_K):
        routed = routed + _slab_load(y_refs[k], (), tn) * w[:, k:k + 1]
    o_ref[...] = _layernorm_f32(ALPHA * h + (routed + shared), g_ref[...], b_ref[...])


def _combine(h1, w_tok, y_slab, sgu, sd, g, b):
    t, d = h1.shape
    tn = min(256, t)
    nblk = t // tn
    ds2 = sgu.shape[1]
    vec = pl.BlockSpec((1, d), lambda i: (0, 0))

    def slot_spec(k):
        return pl.BlockSpec((tn * SLAB, LANES), lambda i: (k * nblk + i, 0))

    return pl.pallas_call(
        _combine_kernel,
        out_shape=jax.ShapeDtypeStruct((t, d), jnp.float32),
        grid=(nblk,),
        in_specs=[
            pl.BlockSpec((tn, d), lambda i: (i, 0)),
            pl.BlockSpec((tn, TOP_K), lambda i: (i, 0)),
            *[slot_spec(k) for k in range(TOP_K)],
            pl.BlockSpec((d, ds2), lambda i: (0, 0)),
            pl.BlockSpec((ds2 // 2, d), lambda i: (0, 0)),
            vec, vec,
        ],
        out_specs=pl.BlockSpec((tn, d), lambda i: (i, 0)),
        compiler_params=_cparams(("parallel",)),
        name="combine_ln2",
    )(h1, w_tok, *([y_slab] * TOP_K), sgu, sd, g, b)


def _rel_bucket(rel):
    nb = N_BUCKETS // 2
    max_exact = nb // 2
    ret = jnp.where(rel > 0, nb, 0)
    n = jnp.abs(rel)
    nf = jnp.maximum(n, 1).astype(jnp.float32)
    large = max_exact + (jnp.log(nf / max_exact) / math.log(MAX_DISTANCE / max_exact) * (nb - max_exact)).astype(jnp.int32)
    large = jnp.minimum(large, nb - 1)
    return ret + jnp.where(n < max_exact, n, large)


def _bias_of_rel(rel_bias, rel):
    return rel_bias.astype(jnp.float32)[_rel_bucket(rel)]


def _toeplitz(vec, nrow, ncol, off):
    lo = off - (nrow - 1)
    v = vec[lo:off + ncol]
    p = v.shape[0] + 1
    v = jnp.concatenate([v, v[:1]], axis=0)
    flat = jnp.tile(v, (nrow + 1, 1))
    base = off - lo
    out = flat[base:base + nrow * (p - 1)].reshape(nrow, p - 1, vec.shape[1])
    return out[:, :ncol]


def _bias_tables_a(rel_bias, sink):
    bias_a = rel_bias[:, :A_Q_HEADS]
    m = jnp.arange(3 * TQ_A + TQ_A - 1, dtype=jnp.int32)
    rel = 2 * TQ_A - 1 - m
    vec = jnp.where((jnp.abs(rel) <= WINDOW)[:, None], _bias_of_rel(bias_a, rel), NEG)
    band = _toeplitz(vec, 3 * TQ_A, TQ_A, 3 * TQ_A - 1)
    mm = jnp.arange(TQ_A + N_META - 1, dtype=jnp.int32)
    meta_first = _toeplitz(_bias_of_rel(bias_a, -1 - mm), N_META, TQ_A, N_META - 1)
    meta_far = jnp.broadcast_to(_bias_of_rel(bias_a, jnp.int32(-2 * MAX_DISTANCE)), (N_META, TQ_A, A_Q_HEADS))
    pad = jnp.full((NK_A - 3 * TQ_A - N_META, TQ_A, A_Q_HEADS), NEG, jnp.float32)
    blocked = jnp.full((TQ_A, TQ_A, A_Q_HEADS), NEG, jnp.float32)
    first = jnp.concatenate([blocked, band[TQ_A:], meta_first, pad])
    middle = jnp.concatenate([band, meta_far, pad])
    last = jnp.concatenate([band[:2 * TQ_A], blocked, meta_far, pad])
    tab = jnp.stack([first, middle, last])
    tab = tab.reshape(3, NK_A, TQ_A, A_Q_HEADS // 2, 2)
    tab = jnp.transpose(tab, (0, 3, 1, 4, 2)).reshape(3, A_Q_HEADS // 2, NK_A, 2 * TQ_A) * LOG2E
    sink_rows = jnp.repeat(sink.astype(jnp.float32).reshape(A_Q_HEADS // 2, 1, 2), TQ_A, axis=2) * LOG2E
    return tab, sink_rows


def _bias_tables_b(rel_bias, s):
    bias_b = rel_bias[:, A_Q_HEADS:]
    near = []
    for d in (-1, 0, 1):
        m = jnp.arange(TQ_B + TK_B - 1, dtype=jnp.int32)
        vec = _bias_of_rel(bias_b, TK_B * d + TK_B - 1 - m)
        near.append(_toeplitz(vec, TK_B, TQ_B, TK_B - 1))
    far_l = jnp.broadcast_to(_bias_of_rel(bias_b, jnp.int32(-TK_B - 1)), (TK_B, TQ_B, B_HEADS))
    far_r = jnp.broadcast_to(_bias_of_rel(bias_b, jnp.int32(TK_B + 1)), (TK_B, TQ_B, B_HEADS))
    tabs = jnp.transpose(jnp.stack([far_l] + near + [far_r]), (3, 0, 1, 2)) * LOG2E
    m = jnp.arange(s + N_META - 1, dtype=jnp.int32)
    vec = _bias_of_rel(bias_b, -1 - m)
    meta = jnp.transpose(_toeplitz(vec, N_META, s, N_META - 1), (2, 0, 1)) * LOG2E
    return tabs, meta


def _prep_w_in(w_in):
    a_w = A_Q_HEADS * HEAD_DIM
    kv = A_KV_HEADS * HEAD_DIM
    bqk = B_HEADS * 2 * HEAD_DIM
    scale = HEAD_DIM ** -0.5
    qa = w_in[:, :a_w] * (scale * LOG2E)
    ka = w_in[:, a_w:a_w + kv]
    va = w_in[:, a_w + kv:a_w + 2 * kv]
    o = a_w + 2 * kv
    qb = w_in[:, o:o + bqk] * (scale * LOG2E)
    kb = w_in[:, o + bqk:o + 2 * bqk]
    vb = w_in[:, o + 2 * bqk:]

    def dup(w):
        return jnp.concatenate([w[:, g * HEAD_DIM:(g + 1) * HEAD_DIM] for g in range(A_KV_HEADS) for _ in range(2)], axis=1)

    w = jnp.concatenate([qa, dup(ka), qb, kb], axis=1).astype(jnp.bfloat16)
    return w, jnp.concatenate([vb, va], axis=1).T.astype(jnp.bfloat16)


def _trunk_front(x, prm):
    bsz, s, d = x.shape
    t = bsz * s
    x2d = x.reshape(t, d)
    proj, vt, vta = _ln_inproj(x2d, prm["ln_emb_g"], prm["ln_emb_b"], prm["w_in"], prm["w_vt"])
    proj3 = proj.reshape(bsz, s, PROJ_COLS)
    oa = _mixer_a(proj3, vta, prm["km_a"], prm["vtm_a"], prm["tab_a"], prm["sink_rows"])
    tabs_b, meta_b = _bias_tables_b(prm["rel_bias"], s)
    ob = _mixer_b(proj3, vt, prm["proj_meta"], prm["vt_meta"], tabs_b, meta_b, prm["lam"].reshape(1).astype(jnp.float32),
                  prm["subln_g"])
    h1, h1_slab = _outproj_ln1(x2d, oa.reshape(t, -1), ob.reshape(t, -1), prm["w_out"], prm["ln_emb_g"],
                               prm["ln_emb_b"], prm["ln1_g"], prm["ln1_b"])

    keys, wts, counts = _router(h1, prm["wr_hi"], prm["wr_lo"], prm["router_bias"])
    n_exp = counts.shape[0]
    n_asg = t * TOP_K
    assert n_asg <= (1 << KEY_SHIFT) and n_asg % LANES == 0
    order = jnp.sort(keys.reshape(n_asg)) & ((1 << KEY_SHIFT) - 1)
    tok = order >> 3
    dst = (order & (TOP_K - 1)) * t + tok
    sc_pad = SC_WORKERS * SC_GRP * SC_NBUF
    assert sc_pad >= TM_MOE and n_asg % sc_pad == 0
    xs = _sc_move_rows(h1_slab.reshape(t, SLAB, LANES), jnp.concatenate([tok, jnp.zeros((sc_pad,), jnp.int32)]),
                       n_asg + sc_pad, scatter=False)
    xs_slab = xs.reshape((n_asg + sc_pad) * SLAB, LANES)
    counts = counts[:, 0].astype(jnp.int32)
    tiles_e = (counts + TM_MOE - 1) // TM_MOE
    tend = jnp.cumsum(tiles_e)
    tstart = tend - tiles_e
    ustart = jnp.cumsum(counts) - counts
    nt = n_asg // TM_MOE + n_exp + 1
    tid = jnp.arange(nt, dtype=jnp.int32)
    tile_e = jnp.minimum(jnp.sum((tend[None, :] <= tid[:, None]).astype(jnp.int32), axis=1), n_exp - 1)
    onehot = (tile_e[:, None] == jnp.arange(n_exp, dtype=jnp.int32)[None, :]).astype(jnp.int32)
    in_e = (tid - jnp.sum(onehot * tstart[None, :], axis=1)) * TM_MOE
    active = tid < tend[-1]
    tile_valid = jnp.where(active, jnp.clip(jnp.sum(onehot * counts[None, :], axis=1) - in_e, 0, TM_MOE), 0)
    tile_u = jnp.where(active, jnp.sum(onehot * ustart[None, :], axis=1) + in_e, 0)

    tiles = (tile_e.astype(jnp.int32), tile_valid.astype(jnp.int32), tile_u.astype(jnp.int32),
             tend[-1:].astype(jnp.int32))
    return h1, wts.T, xs_slab, dst, tiles, x.shape


def _trunk_back(front, prm):
    h1, w_tok, xs_slab, dst, tiles, shape = front
    n_asg = dst.shape[0]
    ys_slab = _experts(xs_slab, prm["w_gate"], prm["w_up"], prm["w_down"], *tiles)
    y = _sc_move_rows(ys_slab.reshape(-1, SLAB, LANES), dst, n_asg, scatter=True)
    out = _combine(h1, w_tok, y.reshape(n_asg * SLAB, LANES), prm["ws_gu"], prm["ws_down"], prm["ln2_g"], prm["ln2_b"])
    return out.reshape(shape)


def kernel(x_prompt, x_sample, meta_tokens, ln_emb_g, ln_emb_b, rel_bias, w_in, attn_sink, lambda_q1, lambda_k1, lambda_q2, lambda_k2, subln_g, w_out, ln1_g, ln1_b, w_router, router_bias, w_gate, w_up, w_down, ws_gate, ws_up, ws_down, ln2_g, ln2_b):
    f32 = jnp.float32
    bf16 = jnp.bfloat16
    l = 0
    row = lambda v: v.reshape(1, -1).astype(f32)
    lam = (jnp.exp(jnp.sum(lambda_q1[l].astype(f32) * lambda_k1[l].astype(f32)))
           - jnp.exp(jnp.sum(lambda_q2[l].astype(f32) * lambda_k2[l].astype(f32))) + LAMBDA_INIT)
    wr_t = w_router[l].astype(f32).T
    wr_hi = wr_t.astype(bf16)
    prm = {
        "ln_emb_g": row(ln_emb_g), "ln_emb_b": row(ln_emb_b),
        "rel_bias": rel_bias,
        "lam": lam,
        "subln_g": subln_g[l].astype(f32).reshape(-1, 1),
        "w_out": w_out[l].astype(bf16),
        "ln1_g": row(ln1_g[l]), "ln1_b": row(ln1_b[l]),
        "wr_hi": wr_hi, "wr_lo": (wr_t - wr_hi.astype(f32)).astype(bf16),
        "router_bias": router_bias[l].astype(f32).reshape(-1, 1),
        "w_gate": w_gate[l], "w_up": w_up[l], "w_down": w_down[l],
        "ws_gu": jnp.concatenate([ws_gate[l], ws_up[l]], axis=-1).astype(bf16),
        "ws_down": ws_down[l].astype(bf16),
        "ln2_g": row(ln2_g[l]), "ln2_b": row(ln2_b[l]),
    }
    prm["w_in"], prm["w_vt"] = _prep_w_in(w_in[l])
    prm["proj_meta"], prm["vt_meta"], vta_meta = _ln_inproj(meta_tokens.astype(f32), prm["ln_emb_g"], prm["ln_emb_b"],
                                                           prm["w_in"], prm["w_vt"])
    meta_pad = NK_A - 3 * TQ_A - N_META
    prm["km_a"] = jnp.pad(prm["proj_meta"][:, KA_BLK * LANES:(KA_BLK + 2) * LANES], ((0, meta_pad), (0, 0)))
    prm["vtm_a"] = jnp.pad(vta_meta, ((0, 0), (0, meta_pad)))
    prm["tab_a"], prm["sink_rows"] = _bias_tables_a(rel_bias, attn_sink[l])
    front_p = _trunk_front(x_prompt, prm)
    front_s = _trunk_front(x_sample, prm)
    return (_trunk_back(front_p, prm), _trunk_back(front_s, prm))
```

```python
import functools
import math

import jax
import jax.numpy as jnp
from jax import lax
from jax.experimental import pallas as pl
from jax.experimental.pallas import tpu as pltpu
from jax.experimental.pallas import tpu_sc as plsc

N_META = 16
HEAD_DIM = 64
WINDOW = 128
A_Q_HEADS = 8
A_KV_HEADS = 2
B_HEADS = 4
N_BUCKETS = 32
MAX_DISTANCE = 128
TOP_K = 8
N_GROUPS = 8
TOPK_GROUPS = 4
ROUTED_SCALE = 2.5
LN_EPS = 1e-5
DEPTH = 1
ALPHA = (2 * DEPTH) ** 0.25
NEG = -1e30
LAMBDA_INIT = 0.8 - 0.6 * math.exp(-0.3 * 0)

LANES = 128
VMEM_LIMIT = 48 * 1024 * 1024

QA_BLK = 0
KA_BLK = 4
QB_BLK = 6
KB_BLK = 10
PROJ_COLS = 14 * LANES
VT_ROWS = LANES + 16
VA_ROWS = HEAD_DIM + 16
LOG2E = 1.4426950408889634

TQ_A = 128
NK_A = 4 * TQ_A
TQ_B = 512
TK_B = 512
TM_MOE = 512
SLAB = 8
SC_GRP = 32
SC_NBUF = 2
SC_WORKERS = 32
KEY_SHIFT = 20


def _cparams(sem):
    return pltpu.CompilerParams(dimension_semantics=sem, vmem_limit_bytes=VMEM_LIMIT)


def _layernorm_f32(x, g, b):
    mu = jnp.mean(x, axis=-1, keepdims=True)
    xc = x - mu
    var = jnp.mean(xc * xc, axis=-1, keepdims=True)
    return xc * lax.rsqrt(var + LN_EPS) * g + b


def _dot_nt(a, b):
    return lax.dot_general(a, b, (((1,), (1,)), ((), ())), preferred_element_type=jnp.float32)


def _dot(a, b):
    return jnp.dot(a, b, preferred_element_type=jnp.float32)


def _ln_inproj_kernel(x_ref, g_ref, b_ref, w_ref, wvt_ref, o_ref, vt_ref, vta_ref):
    h = _layernorm_f32(x_ref[...], g_ref[...], b_ref[...]).astype(jnp.bfloat16)
    o_ref[...] = _dot(h, w_ref[...]).astype(o_ref.dtype)
    vt = _dot_nt(wvt_ref[...], h).astype(vt_ref.dtype)
    ones = jnp.ones((16, vt.shape[1]), vt_ref.dtype)
    for hd in range(B_HEADS):
        vt_ref[hd * VT_ROWS:hd * VT_ROWS + LANES, :] = vt[hd * LANES:(hd + 1) * LANES, :]
        vt_ref[hd * VT_ROWS + LANES:(hd + 1) * VT_ROWS, :] = ones
    base = B_HEADS * LANES
    for g in range(A_KV_HEADS):
        vta_ref[g * VA_ROWS:g * VA_ROWS + HEAD_DIM, :] = vt[base + g * HEAD_DIM:base + (g + 1) * HEAD_DIM, :]
        vta_ref[g * VA_ROWS + HEAD_DIM:(g + 1) * VA_ROWS, :] = ones


def _ln_inproj(x2d, g, b, w, wvt):
    t, d = x2d.shape
    n = w.shape[1]
    tm = min(512, t)
    return pl.pallas_call(
        _ln_inproj_kernel,
        out_shape=(jax.ShapeDtypeStruct((t, n), jnp.bfloat16),
                   jax.ShapeDtypeStruct((B_HEADS * VT_ROWS, t), jnp.bfloat16),
                   jax.ShapeDtypeStruct((A_KV_HEADS * VA_ROWS, t), jnp.bfloat16)),
        grid=(t // tm,),
        in_specs=[
            pl.BlockSpec((tm, d), lambda i: (i, 0)),
            pl.BlockSpec((1, d), lambda i: (0, 0)),
            pl.BlockSpec((1, d), lambda i: (0, 0)),
            pl.BlockSpec((d, n), lambda i: (0, 0)),
            pl.BlockSpec((wvt.shape[0], d), lambda i: (0, 0)),
        ],
        out_specs=(pl.BlockSpec((tm, n), lambda i: (i, 0)),
                   pl.BlockSpec((B_HEADS * VT_ROWS, tm), lambda i: (0, i)),
                   pl.BlockSpec((A_KV_HEADS * VA_ROWS, tm), lambda i: (0, i))),
        compiler_params=_cparams(("parallel",)),
        name="ln_inproj",
    )(x2d, g, b, w, wvt)


def _mixer_a_kernel(q_ref, k_ref, vt_ref, km_ref, vtm_ref, tab_ref, sink_ref, o_ref, *, nblk, sub):
    i = pl.program_id(1)
    lane = lax.broadcasted_iota(jnp.int32, (1, LANES), 1)
    lo = lane < HEAD_DIM
    for j in range(sub):
        gi = i * sub + j
        sp = pl.multiple_of(jnp.maximum(gi - 1, 0) * TQ_A, TQ_A)
        sc = pl.multiple_of(gi * TQ_A, TQ_A)
        sn = pl.multiple_of(jnp.minimum(gi + 1, nblk - 1) * TQ_A, TQ_A)
        variant = jnp.where(gi == 0, 0, jnp.where(gi == nblk - 1, 2, 1))
        rows = slice(j * TQ_A, (j + 1) * TQ_A)
        for g in range(A_KV_HEADS):
            gs = slice(g * LANES, (g + 1) * LANES)
            vr = slice(g * VA_ROWS, (g + 1) * VA_ROWS)
            k_all = jnp.concatenate([k_ref[0, pl.ds(sp, TQ_A), gs], k_ref[0, pl.ds(sc, TQ_A), gs],
                                     k_ref[0, pl.ds(sn, TQ_A), gs], km_ref[:, gs]], axis=0)
            vt_all = jnp.concatenate([vt_ref[vr, pl.ds(sp, TQ_A)], vt_ref[vr, pl.ds(sc, TQ_A)],
                                      vt_ref[vr, pl.ds(sn, TQ_A)], vtm_ref[vr, :]], axis=1)
            for pp in range(2):
                hp = 2 * g + pp
                cols = slice(hp * LANES, (hp + 1) * LANES)
                qc = q_ref[0, rows, cols]
                q2 = jnp.concatenate([jnp.where(lo, qc, jnp.zeros_like(qc)), jnp.where(lo, jnp.zeros_like(qc), qc)],
                                     axis=0)
                s = _dot_nt(k_all, q2) + tab_ref[variant, hp]
                sink = sink_ref[hp]
                m = jnp.maximum(jnp.max(s, axis=0, keepdims=True), sink)
                p = jnp.exp2(s - m)
                acc = _dot(vt_all, p.astype(jnp.bfloat16))
                o = acc[0:HEAD_DIM, :] / (acc[HEAD_DIM:HEAD_DIM + 1, :] + jnp.exp2(sink - m))
                o2 = jnp.concatenate([o[:, :TQ_A], o[:, TQ_A:]], axis=0)
                o_ref[0, rows, cols] = o2.T.astype(o_ref.dtype)


def _mixer_a(proj3, vta, km, vtm, tab, sink):
    bsz, s, _ = proj3.shape
    nblk = s // TQ_A
    assert nblk >= 2
    sub = 4 if nblk % 4 == 0 else 1
    nq = nblk // sub
    tq = sub * TQ_A
    kern = functools.partial(_mixer_a_kernel, nblk=nblk, sub=sub)
    return pl.pallas_call(
        kern,
        out_shape=jax.ShapeDtypeStruct((bsz, s, A_Q_HEADS * HEAD_DIM), jnp.bfloat16),
        grid=(bsz, nq),
        in_specs=[
            pl.BlockSpec((1, tq, 4 * LANES), lambda b, i: (b, i, QA_BLK // 4)),
            pl.BlockSpec((1, s, 2 * LANES), lambda b, i: (b, 0, KA_BLK // 2)),
            pl.BlockSpec((A_KV_HEADS * VA_ROWS, s), lambda b, i: (0, b)),
            pl.BlockSpec(km.shape, lambda b, i: (0, 0)),
            pl.BlockSpec(vtm.shape, lambda b, i: (0, 0)),
            pl.BlockSpec(tab.shape, lambda b, i: (0, 0, 0, 0)),
            pl.BlockSpec(sink.shape, lambda b, i: (0, 0, 0)),
        ],
        out_specs=pl.BlockSpec((1, tq, 4 * LANES), lambda b, i: (b, i, 0)),
        compiler_params=_cparams(("parallel", "arbitrary")),
        name="mixer_a",
    )(proj3, proj3, vta, km, vtm, tab, sink)


def _mixer_b_kernel(sc_ref, q_ref, k_ref, vt_ref, km_ref, vtm_ref, tab_ref, tabm_ref, g_ref, o_ref,
                    m_ref, acc_ref, s0_ref, s1_ref, cmax_ref, *, nk):
    i = pl.program_id(2)
    lane = lax.broadcasted_iota(jnp.int32, (1, LANES), 1)
    lo = lane < HEAD_DIM
    q = q_ref[0]
    qs = (jnp.where(lo, q, jnp.zeros_like(q)), jnp.where(lo, jnp.zeros_like(q), q))
    lam = sc_ref[0]
    sbufs = (s0_ref, s1_ref)

    km = km_ref[...]
    vtm = vtm_ref[...]
    for c in range(2):
        s = _dot_nt(km, qs[c]) + tabm_ref[0]
        m = jnp.max(s, axis=0, keepdims=True)
        p = jnp.exp2(s - m)
        m_ref[c] = m
        acc_ref[c] = _dot(vtm, p.astype(jnp.bfloat16))

    def scores(j, slot):
        start = pl.multiple_of(j * TK_B, TK_B)
        kj = k_ref[0, pl.ds(start, TK_B), :]
        bias = tab_ref[0, jnp.clip(j - i, -2, 2) + 2]
        for c in range(2):
            s = _dot_nt(kj, qs[c]) + bias
            sbufs[slot][c] = s
            cmax_ref[slot, c] = jnp.max(s, axis=0, keepdims=True)

    def accumulate(j, slot):
        start = pl.multiple_of(j * TK_B, TK_B)
        vtj = vt_ref[:, pl.ds(start, TK_B)]
        for c in range(2):
            s = sbufs[slot][c]
            m_prev = m_ref[c]
            m_new = jnp.maximum(m_prev, cmax_ref[slot, c])
            a = jnp.exp2(m_prev - m_new)
            p = jnp.exp2(s - m_new)
            m_ref[c] = m_new
            acc_ref[c] = a * acc_ref[c] + _dot(vtj, p.astype(jnp.bfloat16))

    scores(0, 0)

    def pair(jj, carry):
        j = 2 * jj
        scores(j + 1, 1)
        accumulate(j, 0)
        scores(j + 2, 0)
        accumulate(j + 1, 1)
        return carry

    lax.fori_loop(0, nk // 2 - 1, pair, 0)
    scores(nk - 1, 1)
    accumulate(nk - 2, 0)
    accumulate(nk - 1, 1)

    o0 = acc_ref[0, 0:LANES, :] / acc_ref[0, LANES:LANES + 1, :]
    o1 = acc_ref[1, 0:LANES, :] / acc_ref[1, LANES:LANES + 1, :]
    o = o0 - lam * o1
    ms = jnp.mean(o * o, axis=0, keepdims=True)
    o = o * lax.rsqrt(ms + LN_EPS) * (g_ref[...] * (1.0 - LAMBDA_INIT))
    o_ref[0] = o.T.astype(o_ref.dtype)


def _mixer_b(proj3, vt, proj_meta, vt_meta, tab, tabm, scal, subln_g):
    bsz, s, _ = proj3.shape
    nq = s // TQ_B
    nk = s // TK_B
    assert nk % 2 == 0
    kern = functools.partial(_mixer_b_kernel, nk=nk)
    return pl.pallas_call(
        kern,
        out_shape=jax.ShapeDtypeStruct((bsz, s, B_HEADS * LANES), jnp.bfloat16),
        grid=(bsz, B_HEADS, nq),
        in_specs=[
            pl.BlockSpec(memory_space=pltpu.SMEM),
            pl.BlockSpec((1, TQ_B, LANES), lambda b, h, i: (b, i, QB_BLK + h)),
            pl.BlockSpec((1, s, LANES), lambda b, h, i: (b, 0, KB_BLK + h)),
            pl.BlockSpec((VT_ROWS, s), lambda b, h, i: (h, b)),
            pl.BlockSpec((N_META, LANES), lambda b, h, i: (0, KB_BLK + h)),
            pl.BlockSpec((VT_ROWS, N_META), lambda b, h, i: (h, 0)),
            pl.BlockSpec((1, 5, TK_B, TQ_B), lambda b, h, i: (h, 0, 0, 0)),
            pl.BlockSpec((1, N_META, TQ_B), lambda b, h, i: (h, 0, i)),
            pl.BlockSpec((LANES, 1), lambda b, h, i: (0, 0)),
        ],
        out_specs=pl.BlockSpec((1, TQ_B, LANES), lambda b, h, i: (b, i, h)),
        scratch_shapes=[
            pltpu.VMEM((2, 1, TQ_B), jnp.float32),
            pltpu.VMEM((2, VT_ROWS, TQ_B), jnp.float32),
            pltpu.VMEM((2, TK_B, TQ_B), jnp.float32),
            pltpu.VMEM((2, TK_B, TQ_B), jnp.float32),
            pltpu.VMEM((2, 2, 1, TQ_B), jnp.float32),
        ],
        compiler_params=_cparams(("parallel", "parallel", "arbitrary")),
        name="mixer_b",
    )(scal, proj3, proj3, vt, proj_meta, vt_meta, tab, tabm, subln_g)


def _slab_load(ref, lead, rows):
    return jnp.concatenate([ref[lead + (pl.ds(s, rows, stride=SLAB), slice(None))] for s in range(SLAB)], axis=1)


def _slab_store(ref, lead, val):
    rows = val.shape[0]
    for s in range(SLAB):
        ref[lead + (pl.ds(s, rows, stride=SLAB), slice(None))] = val[:, s * LANES:(s + 1) * LANES]


def _outproj_kernel(x_ref, oa_ref, ob_ref, w_ref, eg_ref, eb_ref, g_ref, b_ref, o_ref, os_ref):
    h0 = _layernorm_f32(x_ref[...], eg_ref[...], eb_ref[...])
    half = oa_ref.shape[1]
    mix = _dot(oa_ref[...], w_ref[0:half, :]) + _dot(ob_ref[...], w_ref[half:, :])
    h1 = _layernorm_f32(ALPHA * h0 + mix, g_ref[...], b_ref[...])
    o_ref[...] = h1
    _slab_store(os_ref, (), h1)


def _outproj_ln1(x2d, oa, ob, w_out, eg, eb, g, b):
    t, d = x2d.shape
    tm = min(512, t)
    wa = oa.shape[1]
    wb = ob.shape[1]
    vec = pl.BlockSpec((1, d), lambda i: (0, 0))
    return pl.pallas_call(
        _outproj_kernel,
        out_shape=(jax.ShapeDtypeStruct((t, d), jnp.float32),
                   jax.ShapeDtypeStruct((t * SLAB, LANES), jnp.float32)),
        grid=(t // tm,),
        in_specs=[
            pl.BlockSpec((tm, d), lambda i: (i, 0)),
            pl.BlockSpec((tm, wa), lambda i: (i, 0)),
            pl.BlockSpec((tm, wb), lambda i: (i, 0)),
            pl.BlockSpec((wa + wb, d), lambda i: (0, 0)),
            vec, vec, vec, vec,
        ],
        out_specs=(pl.BlockSpec((tm, d), lambda i: (i, 0)),
                   pl.BlockSpec((tm * SLAB, LANES), lambda i: (i, 0))),
        compiler_params=_cparams(("parallel",)),
        name="outproj_ln1",
    )(x2d, oa, ob, w_out, eg, eb, g, b)


def _router_kernel(h_ref, wh_ref, wl_ref, rb_ref, e_ref, w_ref, cnt_ref, carry_ref, *, n_exp):
    i = pl.program_id(0)
    tn = h_ref.shape[0]
    gsz = n_exp // N_GROUPS

    @pl.when(i == 0)
    def _():
        carry_ref[...] = jnp.zeros_like(carry_ref)

    x = h_ref[...]
    xh = x.astype(jnp.bfloat16)
    xl = (x - xh.astype(jnp.float32)).astype(jnp.bfloat16)
    logits = _dot_nt(wh_ref[...], xh) + (_dot_nt(wh_ref[...], xl) + _dot_nt(wl_ref[...], xh))
    scores = 1.0 / (1.0 + jnp.exp(-logits))
    biased = scores + rb_ref[...]

    g3 = biased.reshape(N_GROUPS, gsz, tn)
    it3 = lax.broadcasted_iota(jnp.int32, (N_GROUPS, gsz, tn), 1)
    mx1 = jnp.max(g3, axis=1, keepdims=True)
    first = jnp.min(jnp.where(g3 == mx1, it3, gsz), axis=1, keepdims=True)
    mx2 = jnp.max(jnp.where(it3 == first, -jnp.inf, g3), axis=1, keepdims=True)
    gscore = (mx1 + mx2).reshape(N_GROUPS, tn)

    itg = lax.broadcasted_iota(jnp.int32, (N_GROUPS, tn), 0)
    gsel = jnp.zeros((N_GROUPS, tn), jnp.bool_)
    cur = gscore
    for _ in range(TOPK_GROUPS):
        mx = jnp.max(cur, axis=0, keepdims=True)
        fi = jnp.min(jnp.where(cur == mx, itg, N_GROUPS), axis=0, keepdims=True)
        hit = itg == fi
        gsel = jnp.logical_or(gsel, hit)
        cur = jnp.where(hit, -jnp.inf, cur)
    emask = jnp.broadcast_to(gsel.reshape(N_GROUPS, 1, tn), (N_GROUPS, gsz, tn)).reshape(n_exp, tn)
    cur = jnp.where(emask, biased, NEG)

    ite = lax.broadcasted_iota(jnp.int32, (n_exp, tn), 0)
    hits = []
    eidx = []
    wsel = []
    for _ in range(TOP_K):
        mx = jnp.max(cur, axis=0, keepdims=True)
        fi = jnp.min(jnp.where(cur == mx, ite, n_exp), axis=0, keepdims=True)
        hit = ite == fi
        hits.append(hit)
        eidx.append(fi)
        wsel.append(jnp.sum(jnp.where(hit, scores, 0.0), axis=0, keepdims=True))
        cur = jnp.where(hit, -jnp.inf, cur)
    sel = hits[0]
    for hit in hits[1:]:
        sel = jnp.logical_or(sel, hit)
    self32 = jnp.where(sel, 1.0, 0.0)

    carry_ref[...] = carry_ref[...] + jnp.sum(self32, axis=1, keepdims=True)
    cnt_ref[...] = carry_ref[...]

    wcat = jnp.concatenate(wsel, axis=0)
    wcat = wcat / jnp.sum(wcat, axis=0, keepdims=True) * ROUTED_SCALE
    tok = i * tn + lax.broadcasted_iota(jnp.int32, (TOP_K, tn), 1)
    slot = lax.broadcasted_iota(jnp.int32, (TOP_K, tn), 0)
    e_ref[...] = jnp.concatenate(eidx, axis=0) * (1 << KEY_SHIFT) + (tok * TOP_K + slot)
    w_ref[...] = wcat


def _router(h1, wr_hi, wr_lo, rbias):
    t, d = h1.shape
    n_exp = wr_hi.shape[0]
    tn = min(512, t)
    kern = functools.partial(_router_kernel, n_exp=n_exp)
    row = pl.BlockSpec((TOP_K, tn), lambda i: (0, i))
    return pl.pallas_call(
        kern,
        out_shape=(
            jax.ShapeDtypeStruct((TOP_K, t), jnp.int32),
            jax.ShapeDtypeStruct((TOP_K, t), jnp.float32),
            jax.ShapeDtypeStruct((n_exp, 1), jnp.float32),
        ),
        grid=(t // tn,),
        in_specs=[
            pl.BlockSpec((tn, d), lambda i: (i, 0)),
            pl.BlockSpec((n_exp, d), lambda i: (0, 0)),
            pl.BlockSpec((n_exp, d), lambda i: (0, 0)),
            pl.BlockSpec((n_exp, 1), lambda i: (0, 0)),
        ],
        out_specs=(row, row, pl.BlockSpec((n_exp, 1), lambda i: (0, 0))),
        scratch_shapes=[pltpu.VMEM((n_exp, 1), jnp.float32)],
        compiler_params=_cparams(("arbitrary",)),
        name="router",
    )(h1, wr_hi, wr_lo, rbias)


def _sc_move_rows(x_slab, idx, n_out, scatter):
    m = idx.shape[0]
    rnd = SC_GRP * SC_NBUF
    per = m // SC_WORKERS
    assert m % (SC_WORKERS * rnd) == 0
    row = x_slab.shape[1:]
    mesh = plsc.VectorSubcoreMesh(core_axis_name="c", subcore_axis_name="s")

    @pl.kernel(out_type=jax.ShapeDtypeStruct((n_out,) + row, x_slab.dtype), mesh=mesh,
               scratch_types=[pltpu.VMEM((rnd,), jnp.int32), pltpu.VMEM((SC_NBUF, SC_GRP) + row, x_slab.dtype),
                              pltpu.SemaphoreType.DMA((SC_NBUF,)), pltpu.SemaphoreType.DMA((SC_NBUF,))])
    def kern(x_hbm, i_hbm, o_hbm, ibuf, buf, lsem, ssem):
        worker = lax.axis_index("c") * (SC_WORKERS // 2) + lax.axis_index("s")
        base = worker * per

        @pl.loop(0, per // rnd)
        def _(r):
            off = base + r * rnd
            pltpu.sync_copy(i_hbm.at[pl.ds(off, rnd)], ibuf)
            loads = []
            stores = []
            for b in range(SC_NBUF):
                indexed = ibuf.at[pl.ds(b * SC_GRP, SC_GRP)]
                linear = pl.ds(off + b * SC_GRP, SC_GRP)
                src = x_hbm.at[linear] if scatter else x_hbm.at[indexed]
                dst = o_hbm.at[indexed] if scatter else o_hbm.at[linear]
                loads.append(pltpu.make_async_copy(src, buf.at[b], lsem.at[b]))
                stores.append(pltpu.make_async_copy(buf.at[b], dst, ssem.at[b]))
            for ld in loads:
                ld.start()
            for b in range(SC_NBUF):
                loads[b].wait()
                stores[b].start()
            for st in stores:
                st.wait()

    return kern(x_slab, idx)


X_BUFS = 3
Y_BUFS = 2


def _experts_kernel(te_ref, tv_ref, tu_ref, na_ref, xs_hbm, wg_ref, wu_ref, wd_ref, ys_hbm,
                    xb0, xb1, xb2, yb0, yb1, wgu_s, wd_s, xsem, ysem):
    i = pl.program_id(0)
    nt = pl.num_programs(0)
    n_act = na_ref[0]
    xbufs = (xb0, xb1, xb2)
    ybufs = (yb0, yb1)
    tile_rows = TM_MOE * SLAB

    def x_copy(slot, tile):
        tile = jnp.minimum(tile, nt - 1)
        start = pl.multiple_of(tu_ref[tile] * SLAB, SLAB)
        return pltpu.make_async_copy(xs_hbm.at[pl.ds(start, tile_rows), :], xbufs[slot], xsem.at[slot])

    def y_pieces(slot, tile, go):
        nvalid = tv_ref[tile]
        base = tu_ref[tile]
        size = TM_MOE
        while size >= 1:
            @pl.when((nvalid & size) != 0)
            def _(size=size):
                off = nvalid & ~(2 * size - 1)
                src = ybufs[slot].at[pl.ds(pl.multiple_of(off * SLAB, SLAB), size * SLAB), :]
                dst = ys_hbm.at[pl.ds(pl.multiple_of((base + off) * SLAB, SLAB), size * SLAB), :]
                go(pltpu.make_async_copy(src, dst, ysem.at[slot]))
            size //= 2

    @pl.when(i == 0)
    def _():
        x_copy(0, 0).start()
        x_copy(1, 1).start()

    iprev = jnp.maximum(i - 1, 0)

    @pl.when(jnp.logical_and(i < n_act, jnp.logical_or(i == 0, te_ref[i] != te_ref[iprev])))
    def _():
        de = wg_ref.shape[2]
        wgu_s[:, 0:de] = wg_ref[0].astype(wgu_s.dtype)
        wgu_s[:, de:] = wu_ref[0].astype(wgu_s.dtype)
        wd_s[...] = wd_ref[0].astype(wd_s.dtype)

    phase = i % (X_BUFS * Y_BUFS)
    for c in range(X_BUFS * Y_BUFS):
        xs, ys = c % X_BUFS, c % Y_BUFS

        @pl.when(jnp.logical_and(i < n_act, phase == c))
        def _(xs=xs, ys=ys):
            x_copy(xs, i).wait()
            x_copy((xs + 2) % X_BUFS, i + 2).start()
            x = _slab_load(xbufs[xs], (), TM_MOE).astype(jnp.bfloat16)
            gu = _dot(x, wgu_s[...])
            de = gu.shape[1] // 2
            gate = gu[:, :de]
            hid = (gate / (1.0 + jnp.exp(-gate))) * gu[:, de:]
            y = _dot(hid.astype(jnp.bfloat16), wd_s[...])

            @pl.when(i >= Y_BUFS)
            def _():
                y_pieces(ys, i - Y_BUFS, lambda cp: cp.wait())

            _slab_store(ybufs[ys], (), y)
            y_pieces(ys, i, lambda cp: cp.start())

        @pl.when(jnp.logical_and(i == n_act, phase == c))
        def _(xs=xs, ys=ys):
            x_copy(xs, i).wait()
            x_copy((xs + 1) % X_BUFS, i + 1).wait()

            @pl.when(i >= 2)
            def _():
                y_pieces(ys, i - 2, lambda cp: cp.wait())

            @pl.when(i >= 1)
            def _():
                y_pieces(1 - ys, i - 1, lambda cp: cp.wait())


def _experts(xs_slab, wg, wu, wd, tile_e, tile_valid, tile_u, n_active):
    nt = tile_e.shape[0]
    d = wg.shape[1]
    de = wg.shape[2]
    buf = pltpu.VMEM((TM_MOE * SLAB, LANES), jnp.float32)
    return pl.pallas_call(
        _experts_kernel,
        out_shape=jax.ShapeDtypeStruct(xs_slab.shape, jnp.float32),
        grid_spec=pltpu.PrefetchScalarGridSpec(
            num_scalar_prefetch=4,
            grid=(nt,),
            in_specs=[
                pl.BlockSpec(memory_space=pl.ANY),
                pl.BlockSpec((1, d, de), lambda i, te, tv, tu, na: (te[i], 0, 0)),
                pl.BlockSpec((1, d, de), lambda i, te, tv, tu, na: (te[i], 0, 0)),
                pl.BlockSpec((1, de, d), lambda i, te, tv, tu, na: (te[i], 0, 0)),
            ],
            out_specs=pl.BlockSpec(memory_space=pl.ANY),
            scratch_shapes=[buf] * (X_BUFS + Y_BUFS) + [
                pltpu.VMEM((d, 2 * de), jnp.bfloat16), pltpu.VMEM((de, d), jnp.bfloat16),
                pltpu.SemaphoreType.DMA((X_BUFS,)), pltpu.SemaphoreType.DMA((Y_BUFS,))],
        ),
        compiler_params=_cparams(("arbitrary",)),
        name="experts",
    )(tile_e, tile_valid, tile_u, n_active, xs_slab, wg, wu, wd)


def _combine_kernel(h_ref, w_ref, *rest):
    y_refs = rest[:TOP_K]
    sgu_ref, sd_ref, g_ref, b_ref, o_ref = rest[TOP_K:]
    tn = h_ref.shape[0]
    h = h_ref[...]
    gu = _dot(h.astype(jnp.bfloat16), sgu_ref[...])
    ds = gu.shape[1] // 2
    gate = gu[:, :ds]
    hid = (gate / (1.0 + jnp.exp(-gate))) * gu[:, ds:]
    shared = _dot(hid.astype(jnp.bfloat16), sd_ref[...])

    w = w_ref[...]
    routed = _slab_load(y_refs[0], (), tn) * w[:, 0:1]
    for k in range(1, TOP_K):
        routed = routed + _slab_load(y_refs[k], (), tn) * w[:, k:k + 1]
    o_ref[...] = _layernorm_f32(ALPHA * h + (routed + shared), g_ref[...], b_ref[...])


def _combine(h1, w_tok, y_slab, sgu, sd, g, b):
    t, d = h1.shape
    tn = min(256, t)
    nblk = t // tn
    ds2 = sgu.shape[1]
    vec = pl.BlockSpec((1, d), lambda i: (0, 0))

    def slot_spec(k):
        return pl.BlockSpec((tn * SLAB, LANES), lambda i: (k * nblk + i, 0))

    return pl.pallas_call(
        _combine_kernel,
        out_shape=jax.ShapeDtypeStruct((t, d), jnp.float32),
        grid=(nblk,),
        in_specs=[
            pl.BlockSpec((tn, d), lambda i: (i, 0)),
            pl.BlockSpec((tn, TOP_K), lambda i: (i, 0)),
            *[slot_spec(k) for k in range(TOP_K)],
            pl.BlockSpec((d, ds2), lambda i: (0, 0)),
            pl.BlockSpec((ds2 // 2, d), lambda i: (0, 0)),
            vec, vec,
        ],
        out_specs=pl.BlockSpec((tn, d), lambda i: (i, 0)),
        compiler_params=_cparams(("parallel",)),
        name="combine_ln2",
    )(h1, w_tok, *([y_slab] * TOP_K), sgu, sd, g, b)


def _rel_bucket(rel):
    nb = N_BUCKETS // 2
    max_exact = nb // 2
    ret = jnp.where(rel > 0, nb, 0)
    n = jnp.abs(rel)
    nf = jnp.maximum(n, 1).astype(jnp.float32)
    large = max_exact + (jnp.log(nf / max_exact) / math.log(MAX_DISTANCE / max_exact) * (nb - max_exact)).astype(jnp.int32)
    large = jnp.minimum(large, nb - 1)
    return ret + jnp.where(n < max_exact, n, large)


def _bias_of_rel(rel_bias, rel):
    return rel_bias.astype(jnp.float32)[_rel_bucket(rel)]


def _toeplitz(vec, nrow, ncol, off):
    lo = off - (nrow - 1)
    v = vec[lo:off + ncol]
    p = v.shape[0] + 1
    v = jnp.concatenate([v, v[:1]], axis=0)
    flat = jnp.tile(v, (nrow + 1, 1))
    base = off - lo
    out = flat[base:base + nrow * (p - 1)].reshape(nrow, p - 1, vec.shape[1])
    return out[:, :ncol]


def _bias_tables_a(rel_bias, sink):
    bias_a = rel_bias[:, :A_Q_HEADS]
    m = jnp.arange(3 * TQ_A + TQ_A - 1, dtype=jnp.int32)
    rel = 2 * TQ_A - 1 - m
    vec = jnp.where((jnp.abs(rel) <= WINDOW)[:, None], _bias_of_rel(bias_a, rel), NEG)
    band = _toeplitz(vec, 3 * TQ_A, TQ_A, 3 * TQ_A - 1)
    mm = jnp.arange(TQ_A + N_META - 1, dtype=jnp.int32)
    meta_first = _toeplitz(_bias_of_rel(bias_a, -1 - mm), N_META, TQ_A, N_META - 1)
    meta_far = jnp.broadcast_to(_bias_of_rel(bias_a, jnp.int32(-2 * MAX_DISTANCE)), (N_META, TQ_A, A_Q_HEADS))
    pad = jnp.full((NK_A - 3 * TQ_A - N_META, TQ_A, A_Q_HEADS), NEG, jnp.float32)
    blocked = jnp.full((TQ_A, TQ_A, A_Q_HEADS), NEG, jnp.float32)
    first = jnp.concatenate([blocked, band[TQ_A:], meta_first, pad])
    middle = jnp.concatenate([band, meta_far, pad])
    last = jnp.concatenate([band[:2 * TQ_A], blocked, meta_far, pad])
    tab = jnp.stack([first, middle, last])
    tab = tab.reshape(3, NK_A, TQ_A, A_Q_HEADS // 2, 2)
    tab = jnp.transpose(tab, (0, 3, 1, 4, 2)).reshape(3, A_Q_HEADS // 2, NK_A, 2 * TQ_A) * LOG2E
    sink_rows = jnp.repeat(sink.astype(jnp.float32).reshape(A_Q_HEADS // 2, 1, 2), TQ_A, axis=2) * LOG2E
    return tab, sink_rows


def _bias_tables_b(rel_bias, s):
    bias_b = rel_bias[:, A_Q_HEADS:]
    near = []
    for d in (-1, 0, 1):
        m = jnp.arange(TQ_B + TK_B - 1, dtype=jnp.int32)
        vec = _bias_of_rel(bias_b, TK_B * d + TK_B - 1 - m)
        near.append(_toeplitz(vec, TK_B, TQ_B, TK_B - 1))
    far_l = jnp.broadcast_to(_bias_of_rel(bias_b, jnp.int32(-TK_B - 1)), (TK_B, TQ_B, B_HEADS))
    far_r = jnp.broadcast_to(_bias_of_rel(bias_b, jnp.int32(TK_B + 1)), (TK_B, TQ_B, B_HEADS))
    tabs = jnp.transpose(jnp.stack([far_l] + near + [far_r]), (3, 0, 1, 2)) * LOG2E
    m = jnp.arange(s + N_META - 1, dtype=jnp.int32)
    vec = _bias_of_rel(bias_b, -1 - m)
    meta = jnp.transpose(_toeplitz(vec, N_META, s, N_META - 1), (2, 0, 1)) * LOG2E
    return tabs, meta


def _prep_w_in(w_in):
    a_w = A_Q_HEADS * HEAD_DIM
    kv = A_KV_HEADS * HEAD_DIM
    bqk = B_HEADS * 2 * HEAD_DIM
    scale = HEAD_DIM ** -0.5
    qa = w_in[:, :a_w] * (scale * LOG2E)
    ka = w_in[:, a_w:a_w + kv]
    va = w_in[:, a_w + kv:a_w + 2 * kv]
    o = a_w + 2 * kv
    qb = w_in[:, o:o + bqk] * (scale * LOG2E)
    kb = w_in[:, o + bqk:o + 2 * bqk]
    vb = w_in[:, o + 2 * bqk:]

    def dup(w):
        return jnp.concatenate([w[:, g * HEAD_DIM:(g + 1) * HEAD_DIM] for g in range(A_KV_HEADS) for _ in range(2)], axis=1)

    w = jnp.concatenate([qa, dup(ka), qb, kb], axis=1).astype(jnp.bfloat16)
    return w, jnp.concatenate([vb, va], axis=1).T.astype(jnp.bfloat16)


def _trunk_front(x, prm):
    bsz, s, d = x.shape
    t = bsz * s
    x2d = x.reshape(t, d)
    proj, vt, vta = _ln_inproj(x2d, prm["ln_emb_g"], prm["ln_emb_b"], prm["w_in"], prm["w_vt"])
    proj3 = proj.reshape(bsz, s, PROJ_COLS)
    oa = _mixer_a(proj3, vta, prm["km_a"], prm["vtm_a"], prm["tab_a"], prm["sink_rows"])
    tabs_b, meta_b = _bias_tables_b(prm["rel_bias"], s)
    ob = _mixer_b(proj3, vt, prm["proj_meta"], prm["vt_meta"], tabs_b, meta_b, prm["lam"].reshape(1).astype(jnp.float32),
                  prm["subln_g"])
    h1, h1_slab = _outproj_ln1(x2d, oa.reshape(t, -1), ob.reshape(t, -1), prm["w_out"], prm["ln_emb_g"],
                               prm["ln_emb_b"], prm["ln1_g"], prm["ln1_b"])

    keys, wts, counts = _router(h1, prm["wr_hi"], prm["wr_lo"], prm["router_bias"])
    n_exp = counts.shape[0]
    n_asg = t * TOP_K
    assert n_asg <= (1 << KEY_SHIFT) and n_asg % LANES == 0
    order = jnp.sort(keys.reshape(n_asg)) & ((1 << KEY_SHIFT) - 1)
    tok = order >> 3
    dst = (order & (TOP_K - 1)) * t + tok
    sc_pad = SC_WORKERS * SC_GRP * SC_NBUF
    assert sc_pad >= TM_MOE and n_asg % sc_pad == 0
    xs = _sc_move_rows(h1_slab.reshape(t, SLAB, LANES), jnp.concatenate([tok, jnp.zeros((sc_pad,), jnp.int32)]),
                       n_asg + sc_pad, scatter=False)
    xs_slab = xs.reshape((n_asg + sc_pad) * SLAB, LANES)
    counts = counts[:, 0].astype(jnp.int32)
    tiles_e = (counts + TM_MOE - 1) // TM_MOE
    tend = jnp.cumsum(tiles_e)
    tstart = tend - tiles_e
    ustart = jnp.cumsum(counts) - counts
    nt = n_asg // TM_MOE + n_exp + 1
    tid = jnp.arange(nt, dtype=jnp.int32)
    tile_e = jnp.minimum(jnp.sum((tend[None, :] <= tid[:, None]).astype(jnp.int32), axis=1), n_exp - 1)
    onehot = (tile_e[:, None] == jnp.arange(n_exp, dtype=jnp.int32)[None, :]).astype(jnp.int32)
    in_e = (tid - jnp.sum(onehot * tstart[None, :], axis=1)) * TM_MOE
    active = tid < tend[-1]
    tile_valid = jnp.where(active, jnp.clip(jnp.sum(onehot * counts[None, :], axis=1) - in_e, 0, TM_MOE), 0)
    tile_u = jnp.where(active, jnp.sum(onehot * ustart[None, :], axis=1) + in_e, 0)

    tiles = (tile_e.astype(jnp.int32), tile_valid.astype(jnp.int32), tile_u.astype(jnp.int32),
             tend[-1:].astype(jnp.int32))
    return h1, wts.T, xs_slab, dst, tiles, x.shape


def _trunk_back(front, prm):
    h1, w_tok, xs_slab, dst, tiles, shape = front
    n_asg = dst.shape[0]
    ys_slab = _experts(xs_slab, prm["w_gate"], prm["w_up"], prm["w_down"], *tiles)
    y = _sc_move_rows(ys_slab.reshape(-1, SLAB, LANES), dst, n_asg, scatter=True)
    out = _combine(h1, w_tok, y.reshape(n_asg * SLAB, LANES), prm["ws_gu"], prm["ws_down"], prm["ln2_g"], prm["ln2_b"])
    return out.reshape(shape)


def kernel(x_prompt, x_sample, meta_tokens, ln_emb_g, ln_emb_b, rel_bias, w_in, attn_sink, lambda_q1, lambda_k1, lambda_q2, lambda_k2, subln_g, w_out, ln1_g, ln1_b, w_router, router_bias, w_gate, w_up, w_down, ws_gate, ws_up, ws_down, ln2_g, ln2_b):
    f32 = jnp.float32
    bf16 = jnp.bfloat16
    l = 0
    row = lambda v: v.reshape(1, -1).astype(f32)
    lam = (jnp.exp(jnp.sum(lambda_q1[l].astype(f32) * lambda_k1[l].astype(f32)))
           - jnp.exp(jnp.sum(lambda_q2[l].astype(f32) * lambda_k2[l].astype(f32))) + LAMBDA_INIT)
    wr_t = w_router[l].astype(f32).T
    wr_hi = wr_t.astype(bf16)
    prm = {
        "ln_emb_g": row(ln_emb_g), "ln_emb_b": row(ln_emb_b),
        "rel_bias": rel_bias,
        "lam": lam,
        "subln_g": subln_g[l].astype(f32).reshape(-1, 1),
        "w_out": w_out[l].astype(bf16),
        "ln1_g": row(ln1_g[l]), "ln1_b": row(ln1_b[l]),
        "wr_hi": wr_hi, "wr_lo": (wr_t - wr_hi.astype(f32)).astype(bf16),
        "router_bias": router_bias[l].astype(f32).reshape(-1, 1),
        "w_gate": w_gate[l], "w_up": w_up[l], "w_down": w_down[l],
        "ws_gu": jnp.concatenate([ws_gate[l], ws_up[l]], axis=-1).astype(bf16),
        "ws_down": ws_down[l].astype(bf16),
        "ln2_g": row(ln2_g[l]), "ln2_b": row(ln2_b[l]),
    }
    prm["w_in"], prm["w_vt"] = _prep_w_in(w_in[l])
    prm["proj_meta"], prm["vt_meta"], vta_meta = _ln_inproj(meta_tokens.astype(f32), prm["ln_emb_g"], prm["ln_emb_b"],
                                                           prm["w_in"], prm["w_vt"])
    meta_pad = NK_A - 3 * TQ_A - N_META
    prm["km_a"] = jnp.pad(prm["proj_meta"][:, KA_BLK * LANES:(KA_BLK + 2) * LANES], ((0, meta_pad), (0, 0)))
    prm["vtm_a"] = jnp.pad(vta_meta, ((0, 0), (0, meta_pad)))
    prm["tab_a"], prm["sink_rows"] = _bias_tables_a(rel_bias, attn_sink[l])
    front_p = _trunk_front(x_prompt, prm)
    front_s = _trunk_front(x_sample, prm)
    return (_trunk_back(front_p, prm), _trunk_back(front_s, prm))
```

```python
import functools
import math

import jax
import jax.numpy as jnp
from jax import lax
from jax.experimental import pallas as pl
from jax.experimental.pallas import tpu as pltpu
from jax.experimental.pallas import tpu_sc as plsc

N_META = 16
HEAD_DIM = 64
WINDOW = 128
A_Q_HEADS = 8
A_KV_HEADS = 2
B_HEADS = 4
N_BUCKETS = 32
MAX_DISTANCE = 128
TOP_K = 8
N_GROUPS = 8
TOPK_GROUPS = 4
ROUTED_SCALE = 2.5
LN_EPS = 1e-5
DEPTH = 1
ALPHA = (2 * DEPTH) ** 0.25
NEG = -1e30
LAMBDA_INIT = 0.8 - 0.6 * math.exp(-0.3 * 0)

LANES = 128
VMEM_LIMIT = 48 * 1024 * 1024

QA_BLK = 0
KA_BLK = 4
QB_BLK = 6
KB_BLK = 10
PROJ_COLS = 14 * LANES
VT_ROWS = LANES + 16
VA_ROWS = HEAD_DIM + 16
LOG2E = 1.4426950408889634

TQ_A = 128
NK_A = 4 * TQ_A
TQ_B = 512
TK_B = 512
TM_MOE = 512
SLAB = 8
SC_GRP = 32
SC_NBUF = 2
SC_WORKERS = 32
KEY_SHIFT = 20


def _cparams(sem):
    return pltpu.CompilerParams(dimension_semantics=sem, vmem_limit_bytes=VMEM_LIMIT)


def _layernorm_f32(x, g, b):
    mu = jnp.mean(x, axis=-1, keepdims=True)
    xc = x - mu
    var = jnp.mean(xc * xc, axis=-1, keepdims=True)
    return xc * lax.rsqrt(var + LN_EPS) * g + b


def _dot_nt(a, b):
    return lax.dot_general(a, b, (((1,), (1,)), ((), ())), preferred_element_type=jnp.float32)


def _dot(a, b):
    return jnp.dot(a, b, preferred_element_type=jnp.float32)


def _ln_inproj_kernel(x_ref, g_ref, b_ref, w_ref, wvt_ref, o_ref, vt_ref, vta_ref):
    h = _layernorm_f32(x_ref[...], g_ref[...], b_ref[...]).astype(jnp.bfloat16)
    o_ref[...] = _dot(h, w_ref[...]).astype(o_ref.dtype)
    vt = _dot_nt(wvt_ref[...], h).astype(vt_ref.dtype)
    ones = jnp.ones((16, vt.shape[1]), vt_ref.dtype)
    for hd in range(B_HEADS):
        vt_ref[hd * VT_ROWS:hd * VT_ROWS + LANES, :] = vt[hd * LANES:(hd + 1) * LANES, :]
        vt_ref[hd * VT_ROWS + LANES:(hd + 1) * VT_ROWS, :] = ones
    base = B_HEADS * LANES
    for g in range(A_KV_HEADS):
        vta_ref[g * VA_ROWS:g * VA_ROWS + HEAD_DIM, :] = vt[base + g * HEAD_DIM:base + (g + 1) * HEAD_DIM, :]
        vta_ref[g * VA_ROWS + HEAD_DIM:(g + 1) * VA_ROWS, :] = ones


def _ln_inproj(x2d, g, b, w, wvt):
    t, d = x2d.shape
    n = w.shape[1]
    tm = min(1024, t)
    return pl.pallas_call(
        _ln_inproj_kernel,
        out_shape=(jax.ShapeDtypeStruct((t, n), jnp.bfloat16),
                   jax.ShapeDtypeStruct((B_HEADS * VT_ROWS, t), jnp.bfloat16),
                   jax.ShapeDtypeStruct((A_KV_HEADS * VA_ROWS, t), jnp.bfloat16)),
        grid=(t // tm,),
        in_specs=[
            pl.BlockSpec((tm, d), lambda i: (i, 0)),
            pl.BlockSpec((1, d), lambda i: (0, 0)),
            pl.BlockSpec((1, d), lambda i: (0, 0)),
            pl.BlockSpec((d, n), lambda i: (0, 0)),
            pl.BlockSpec((wvt.shape[0], d), lambda i: (0, 0)),
        ],
        out_specs=(pl.BlockSpec((tm, n), lambda i: (i, 0)),
                   pl.BlockSpec((B_HEADS * VT_ROWS, tm), lambda i: (0, i)),
                   pl.BlockSpec((A_KV_HEADS * VA_ROWS, tm), lambda i: (0, i))),
        compiler_params=_cparams(("parallel",)),
        name="ln_inproj",
    )(x2d, g, b, w, wvt)


def _mixer_a_kernel(q_ref, k_ref, vt_ref, km_ref, vtm_ref, tab_ref, sink_ref, o_ref, *, nblk, sub):
    i = pl.program_id(1)
    lane = lax.broadcasted_iota(jnp.int32, (1, LANES), 1)
    lo = lane < HEAD_DIM
    for j in range(sub):
        gi = i * sub + j
        sp = pl.multiple_of(jnp.maximum(gi - 1, 0) * TQ_A, TQ_A)
        sc = pl.multiple_of(gi * TQ_A, TQ_A)
        sn = pl.multiple_of(jnp.minimum(gi + 1, nblk - 1) * TQ_A, TQ_A)
        variant = jnp.where(gi == 0, 0, jnp.where(gi == nblk - 1, 2, 1))
        rows = slice(j * TQ_A, (j + 1) * TQ_A)
        for g in range(A_KV_HEADS):
            gs = slice(g * LANES, (g + 1) * LANES)
            vr = slice(g * VA_ROWS, (g + 1) * VA_ROWS)
            k_all = jnp.concatenate([k_ref[0, pl.ds(sp, TQ_A), gs], k_ref[0, pl.ds(sc, TQ_A), gs],
                                     k_ref[0, pl.ds(sn, TQ_A), gs], km_ref[:, gs]], axis=0)
            vt_all = jnp.concatenate([vt_ref[vr, pl.ds(sp, TQ_A)], vt_ref[vr, pl.ds(sc, TQ_A)],
                                      vt_ref[vr, pl.ds(sn, TQ_A)], vtm_ref[vr, :]], axis=1)
            for pp in range(2):
                hp = 2 * g + pp
                cols = slice(hp * LANES, (hp + 1) * LANES)
                qc = q_ref[0, rows, cols]
                q2 = jnp.concatenate([jnp.where(lo, qc, jnp.zeros_like(qc)), jnp.where(lo, jnp.zeros_like(qc), qc)],
                                     axis=0)
                s = _dot_nt(k_all, q2) + tab_ref[variant, hp]
                sink = sink_ref[hp]
                m = jnp.maximum(jnp.max(s, axis=0, keepdims=True), sink)
                p = jnp.exp2(s - m)
                acc = _dot(vt_all, p.astype(jnp.bfloat16))
                o = acc[0:HEAD_DIM, :] / (acc[HEAD_DIM:HEAD_DIM + 1, :] + jnp.exp2(sink - m))
                o2 = jnp.concatenate([o[:, :TQ_A], o[:, TQ_A:]], axis=0)
                o_ref[0, rows, cols] = o2.T.astype(o_ref.dtype)


def _mixer_a(proj3, vta, km, vtm, tab, sink):
    bsz, s, _ = proj3.shape
    nblk = s // TQ_A
    assert nblk >= 2
    sub = 4 if nblk % 4 == 0 else 1
    nq = nblk // sub
    tq = sub * TQ_A
    kern = functools.partial(_mixer_a_kernel, nblk=nblk, sub=sub)
    return pl.pallas_call(
        kern,
        out_shape=jax.ShapeDtypeStruct((bsz, s, A_Q_HEADS * HEAD_DIM), jnp.bfloat16),
        grid=(bsz, nq),
        in_specs=[
            pl.BlockSpec((1, tq, 4 * LANES), lambda b, i: (b, i, QA_BLK // 4)),
            pl.BlockSpec((1, s, 2 * LANES), lambda b, i: (b, 0, KA_BLK // 2)),
            pl.BlockSpec((A_KV_HEADS * VA_ROWS, s), lambda b, i: (0, b)),
            pl.BlockSpec(km.shape, lambda b, i: (0, 0)),
            pl.BlockSpec(vtm.shape, lambda b, i: (0, 0)),
            pl.BlockSpec(tab.shape, lambda b, i: (0, 0, 0, 0)),
            pl.BlockSpec(sink.shape, lambda b, i: (0, 0, 0)),
        ],
        out_specs=pl.BlockSpec((1, tq, 4 * LANES), lambda b, i: (b, i, 0)),
        compiler_params=_cparams(("parallel", "arbitrary")),
        name="mixer_a",
    )(proj3, proj3, vta, km, vtm, tab, sink)


def _mixer_b_kernel(sc_ref, q_ref, k_ref, vt_ref, km_ref, vtm_ref, tab_ref, tabm_ref, g_ref, o_ref,
                    m_ref, acc_ref, s0_ref, s1_ref, cmax_ref, *, nk):
    i = pl.program_id(2)
    lane = lax.broadcasted_iota(jnp.int32, (1, LANES), 1)
    lo = lane < HEAD_DIM
    q = q_ref[0]
    qs = (jnp.where(lo, q, jnp.zeros_like(q)), jnp.where(lo, jnp.zeros_like(q), q))
    lam = sc_ref[0]
    sbufs = (s0_ref, s1_ref)

    km = km_ref[...]
    vtm = vtm_ref[...]
    for c in range(2):
        s = _dot_nt(km, qs[c]) + tabm_ref[0]
        m = jnp.max(s, axis=0, keepdims=True)
        p = jnp.exp2(s - m)
        m_ref[c] = m
        acc_ref[c] = _dot(vtm, p.astype(jnp.bfloat16))

    def scores(j, slot):
        start = pl.multiple_of(j * TK_B, TK_B)
        kj = k_ref[0, pl.ds(start, TK_B), :]
        bias = tab_ref[0, jnp.clip(j - i, -2, 2) + 2]
        for c in range(2):
            s = _dot_nt(kj, qs[c]) + bias
            sbufs[slot][c] = s
            cmax_ref[slot, c] = jnp.max(s, axis=0, keepdims=True)

    def accumulate(j, slot):
        start = pl.multiple_of(j * TK_B, TK_B)
        vtj = vt_ref[:, pl.ds(start, TK_B)]
        for c in range(2):
            s = sbufs[slot][c]
            m_prev = m_ref[c]
            m_new = jnp.maximum(m_prev, cmax_ref[slot, c])
            a = jnp.exp2(m_prev - m_new)
            p = jnp.exp2(s - m_new)
            m_ref[c] = m_new
            acc_ref[c] = a * acc_ref[c] + _dot(vtj, p.astype(jnp.bfloat16))

    scores(0, 0)

    def pair(jj, carry):
        j = 2 * jj
        scores(j + 1, 1)
        accumulate(j, 0)
        scores(j + 2, 0)
        accumulate(j + 1, 1)
        return carry

    lax.fori_loop(0, nk // 2 - 1, pair, 0)
    scores(nk - 1, 1)
    accumulate(nk - 2, 0)
    accumulate(nk - 1, 1)

    o0 = acc_ref[0, 0:LANES, :] / acc_ref[0, LANES:LANES + 1, :]
    o1 = acc_ref[1, 0:LANES, :] / acc_ref[1, LANES:LANES + 1, :]
    o = o0 - lam * o1
    ms = jnp.mean(o * o, axis=0, keepdims=True)
    o = o * lax.rsqrt(ms + LN_EPS) * (g_ref[...] * (1.0 - LAMBDA_INIT))
    o_ref[0] = o.T.astype(o_ref.dtype)


def _mixer_b(proj3, vt, proj_meta, vt_meta, tab, tabm, scal, subln_g):
    bsz, s, _ = proj3.shape
    nq = s // TQ_B
    nk = s // TK_B
    assert nk % 2 == 0
    kern = functools.partial(_mixer_b_kernel, nk=nk)
    return pl.pallas_call(
        kern,
        out_shape=jax.ShapeDtypeStruct((bsz, s, B_HEADS * LANES), jnp.bfloat16),
        grid=(bsz, B_HEADS, nq),
        in_specs=[
            pl.BlockSpec(memory_space=pltpu.SMEM),
            pl.BlockSpec((1, TQ_B, LANES), lambda b, h, i: (b, i, QB_BLK + h)),
            pl.BlockSpec((1, s, LANES), lambda b, h, i: (b, 0, KB_BLK + h)),
            pl.BlockSpec((VT_ROWS, s), lambda b, h, i: (h, b)),
            pl.BlockSpec((N_META, LANES), lambda b, h, i: (0, KB_BLK + h)),
            pl.BlockSpec((VT_ROWS, N_META), lambda b, h, i: (h, 0)),
            pl.BlockSpec((1, 5, TK_B, TQ_B), lambda b, h, i: (h, 0, 0, 0)),
            pl.BlockSpec((1, N_META, TQ_B), lambda b, h, i: (h, 0, i)),
            pl.BlockSpec((LANES, 1), lambda b, h, i: (0, 0)),
        ],
        out_specs=pl.BlockSpec((1, TQ_B, LANES), lambda b, h, i: (b, i, h)),
        scratch_shapes=[
            pltpu.VMEM((2, 1, TQ_B), jnp.float32),
            pltpu.VMEM((2, VT_ROWS, TQ_B), jnp.float32),
            pltpu.VMEM((2, TK_B, TQ_B), jnp.float32),
            pltpu.VMEM((2, TK_B, TQ_B), jnp.float32),
            pltpu.VMEM((2, 2, 1, TQ_B), jnp.float32),
        ],
        compiler_params=_cparams(("parallel", "parallel", "arbitrary")),
        name="mixer_b",
    )(scal, proj3, proj3, vt, proj_meta, vt_meta, tab, tabm, subln_g)


def _slab_load(ref, lead, rows):
    return jnp.concatenate([ref[lead + (pl.ds(s, rows, stride=SLAB), slice(None))] for s in range(SLAB)], axis=1)


def _slab_store(ref, lead, val):
    rows = val.shape[0]
    for s in range(SLAB):
        ref[lead + (pl.ds(s, rows, stride=SLAB), slice(None))] = val[:, s * LANES:(s + 1) * LANES]


def _outproj_kernel(x_ref, oa_ref, ob_ref, w_ref, eg_ref, eb_ref, g_ref, b_ref, o_ref, os_ref):
    h0 = _layernorm_f32(x_ref[...], eg_ref[...], eb_ref[...])
    half = oa_ref.shape[1]
    mix = _dot(oa_ref[...], w_ref[0:half, :]) + _dot(ob_ref[...], w_ref[half:, :])
    h1 = _layernorm_f32(ALPHA * h0 + mix, g_ref[...], b_ref[...])
    o_ref[...] = h1
    _slab_store(os_ref, (), h1)


def _outproj_ln1(x2d, oa, ob, w_out, eg, eb, g, b):
    t, d = x2d.shape
    tm = min(512, t)
    wa = oa.shape[1]
    wb = ob.shape[1]
    vec = pl.BlockSpec((1, d), lambda i: (0, 0))
    return pl.pallas_call(
        _outproj_kernel,
        out_shape=(jax.ShapeDtypeStruct((t, d), jnp.float32),
                   jax.ShapeDtypeStruct((t * SLAB, LANES), jnp.float32)),
        grid=(t // tm,),
        in_specs=[
            pl.BlockSpec((tm, d), lambda i: (i, 0)),
            pl.BlockSpec((tm, wa), lambda i: (i, 0)),
            pl.BlockSpec((tm, wb), lambda i: (i, 0)),
            pl.BlockSpec((wa + wb, d), lambda i: (0, 0)),
            vec, vec, vec, vec,
        ],
        out_specs=(pl.BlockSpec((tm, d), lambda i: (i, 0)),
                   pl.BlockSpec((tm * SLAB, LANES), lambda i: (i, 0))),
        compiler_params=_cparams(("parallel",)),
        name="outproj_ln1",
    )(x2d, oa, ob, w_out, eg, eb, g, b)


def _router_kernel(h_ref, wh_ref, wl_ref, rb_ref, e_ref, w_ref, cnt_ref, carry_ref, *, n_exp):
    i = pl.program_id(0)
    tn = h_ref.shape[0]
    gsz = n_exp // N_GROUPS

    @pl.when(i == 0)
    def _():
        carry_ref[...] = jnp.zeros_like(carry_ref)

    x = h_ref[...]
    xh = x.astype(jnp.bfloat16)
    xl = (x - xh.astype(jnp.float32)).astype(jnp.bfloat16)
    logits = _dot_nt(wh_ref[...], xh) + (_dot_nt(wh_ref[...], xl) + _dot_nt(wl_ref[...], xh))
    scores = 1.0 / (1.0 + jnp.exp(-logits))
    biased = scores + rb_ref[...]

    g3 = biased.reshape(N_GROUPS, gsz, tn)
    it3 = lax.broadcasted_iota(jnp.int32, (N_GROUPS, gsz, tn), 1)
    mx1 = jnp.max(g3, axis=1, keepdims=True)
    first = jnp.min(jnp.where(g3 == mx1, it3, gsz), axis=1, keepdims=True)
    mx2 = jnp.max(jnp.where(it3 == first, -jnp.inf, g3), axis=1, keepdims=True)
    gscore = (mx1 + mx2).reshape(N_GROUPS, tn)

    itg = lax.broadcasted_iota(jnp.int32, (N_GROUPS, tn), 0)
    gsel = jnp.zeros((N_GROUPS, tn), jnp.bool_)
    cur = gscore
    for _ in range(TOPK_GROUPS):
        mx = jnp.max(cur, axis=0, keepdims=True)
        fi = jnp.min(jnp.where(cur == mx, itg, N_GROUPS), axis=0, keepdims=True)
        hit = itg == fi
        gsel = jnp.logical_or(gsel, hit)
        cur = jnp.where(hit, -jnp.inf, cur)
    emask = jnp.broadcast_to(gsel.reshape(N_GROUPS, 1, tn), (N_GROUPS, gsz, tn)).reshape(n_exp, tn)
    cur = jnp.where(emask, biased, NEG)

    ite = lax.broadcasted_iota(jnp.int32, (n_exp, tn), 0)
    hits = []
    eidx = []
    wsel = []
    for _ in range(TOP_K):
        mx = jnp.max(cur, axis=0, keepdims=True)
        fi = jnp.min(jnp.where(cur == mx, ite, n_exp), axis=0, keepdims=True)
        hit = ite == fi
        hits.append(hit)
        eidx.append(fi)
        wsel.append(jnp.sum(jnp.where(hit, scores, 0.0), axis=0, keepdims=True))
        cur = jnp.where(hit, -jnp.inf, cur)
    sel = hits[0]
    for hit in hits[1:]:
        sel = jnp.logical_or(sel, hit)
    self32 = jnp.where(sel, 1.0, 0.0)

    carry_ref[...] = carry_ref[...] + jnp.sum(self32, axis=1, keepdims=True)
    cnt_ref[...] = carry_ref[...]

    wcat = jnp.concatenate(wsel, axis=0)
    wcat = wcat / jnp.sum(wcat, axis=0, keepdims=True) * ROUTED_SCALE
    tok = i * tn + lax.broadcasted_iota(jnp.int32, (TOP_K, tn), 1)
    slot = lax.broadcasted_iota(jnp.int32, (TOP_K, tn), 0)
    e_ref[...] = jnp.concatenate(eidx, axis=0) * (1 << KEY_SHIFT) + (tok * TOP_K + slot)
    w_ref[...] = wcat


def _router(h1, wr_hi, wr_lo, rbias):
    t, d = h1.shape
    n_exp = wr_hi.shape[0]
    tn = min(512, t)
    kern = functools.partial(_router_kernel, n_exp=n_exp)
    row = pl.BlockSpec((TOP_K, tn), lambda i: (0, i))
    return pl.pallas_call(
        kern,
        out_shape=(
            jax.ShapeDtypeStruct((TOP_K, t), jnp.int32),
            jax.ShapeDtypeStruct((TOP_K, t), jnp.float32),
            jax.ShapeDtypeStruct((n_exp, 1), jnp.float32),
        ),
        grid=(t // tn,),
        in_specs=[
            pl.BlockSpec((tn, d), lambda i: (i, 0)),
            pl.BlockSpec((n_exp, d), lambda i: (0, 0)),
            pl.BlockSpec((n_exp, d), lambda i: (0, 0)),
            pl.BlockSpec((n_exp, 1), lambda i: (0, 0)),
        ],
        out_specs=(row, row, pl.BlockSpec((n_exp, 1), lambda i: (0, 0))),
        scratch_shapes=[pltpu.VMEM((n_exp, 1), jnp.float32)],
        compiler_params=_cparams(("arbitrary",)),
        name="router",
    )(h1, wr_hi, wr_lo, rbias)


def _sc_move_rows(x_slab, idx, n_out, scatter):
    m = idx.shape[0]
    rnd = SC_GRP * SC_NBUF
    per = m // SC_WORKERS
    assert m % (SC_WORKERS * rnd) == 0
    row = x_slab.shape[1:]
    mesh = plsc.VectorSubcoreMesh(core_axis_name="c", subcore_axis_name="s")

    @pl.kernel(out_type=jax.ShapeDtypeStruct((n_out,) + row, x_slab.dtype), mesh=mesh,
               scratch_types=[pltpu.VMEM((rnd,), jnp.int32), pltpu.VMEM((SC_NBUF, SC_GRP) + row, x_slab.dtype),
                              pltpu.SemaphoreType.DMA((SC_NBUF,)), pltpu.SemaphoreType.DMA((SC_NBUF,))])
    def kern(x_hbm, i_hbm, o_hbm, ibuf, buf, lsem, ssem):
        worker = lax.axis_index("c") * (SC_WORKERS // 2) + lax.axis_index("s")
        base = worker * per

        @pl.loop(0, per // rnd)
        def _(r):
            off = base + r * rnd
            pltpu.sync_copy(i_hbm.at[pl.ds(off, rnd)], ibuf)
            loads = []
            stores = []
            for b in range(SC_NBUF):
                indexed = ibuf.at[pl.ds(b * SC_GRP, SC_GRP)]
                linear = pl.ds(off + b * SC_GRP, SC_GRP)
                src = x_hbm.at[linear] if scatter else x_hbm.at[indexed]
                dst = o_hbm.at[indexed] if scatter else o_hbm.at[linear]
                loads.append(pltpu.make_async_copy(src, buf.at[b], lsem.at[b]))
                stores.append(pltpu.make_async_copy(buf.at[b], dst, ssem.at[b]))
            for ld in loads:
                ld.start()
            for b in range(SC_NBUF):
                loads[b].wait()
                stores[b].start()
            for st in stores:
                st.wait()

    return kern(x_slab, idx)


X_BUFS = 3
Y_BUFS = 2


def _experts_kernel(te_ref, tv_ref, tu_ref, na_ref, xs_hbm, wg_ref, wu_ref, wd_ref, ys_hbm,
                    xb0, xb1, xb2, yb0, yb1, wgu_s, wd_s, xsem, ysem):
    i = pl.program_id(0)
    nt = pl.num_programs(0)
    n_act = na_ref[0]
    xbufs = (xb0, xb1, xb2)
    ybufs = (yb0, yb1)
    tile_rows = TM_MOE * SLAB

    def x_copy(slot, tile):
        tile = jnp.minimum(tile, nt - 1)
        start = pl.multiple_of(tu_ref[tile] * SLAB, SLAB)
        return pltpu.make_async_copy(xs_hbm.at[pl.ds(start, tile_rows), :], xbufs[slot], xsem.at[slot])

    def y_pieces(slot, tile, go):
        nvalid = tv_ref[tile]
        base = tu_ref[tile]
        size = TM_MOE
        while size >= 1:
            @pl.when((nvalid & size) != 0)
            def _(size=size):
                off = nvalid & ~(2 * size - 1)
                src = ybufs[slot].at[pl.ds(pl.multiple_of(off * SLAB, SLAB), size * SLAB), :]
                dst = ys_hbm.at[pl.ds(pl.multiple_of((base + off) * SLAB, SLAB), size * SLAB), :]
                go(pltpu.make_async_copy(src, dst, ysem.at[slot]))
            size //= 2

    @pl.when(i == 0)
    def _():
        x_copy(0, 0).start()
        x_copy(1, 1).start()

    iprev = jnp.maximum(i - 1, 0)

    @pl.when(jnp.logical_and(i < n_act, jnp.logical_or(i == 0, te_ref[i] != te_ref[iprev])))
    def _():
        de = wg_ref.shape[2]
        wgu_s[:, 0:de] = wg_ref[0].astype(wgu_s.dtype)
        wgu_s[:, de:] = wu_ref[0].astype(wgu_s.dtype)
        wd_s[...] = wd_ref[0].astype(wd_s.dtype)

    phase = i % (X_BUFS * Y_BUFS)
    for c in range(X_BUFS * Y_BUFS):
        xs, ys = c % X_BUFS, c % Y_BUFS

        @pl.when(jnp.logical_and(i < n_act, phase == c))
        def _(xs=xs, ys=ys):
            x_copy(xs, i).wait()
            x_copy((xs + 2) % X_BUFS, i + 2).start()
            x = _slab_load(xbufs[xs], (), TM_MOE).astype(jnp.bfloat16)
            gu = _dot(x, wgu_s[...])
            de = gu.shape[1] // 2
            gate = gu[:, :de]
            hid = (gate / (1.0 + jnp.exp(-gate))) * gu[:, de:]
            y = _dot(hid.astype(jnp.bfloat16), wd_s[...])

            @pl.when(i >= Y_BUFS)
            def _():
                y_pieces(ys, i - Y_BUFS, lambda cp: cp.wait())

            _slab_store(ybufs[ys], (), y)
            y_pieces(ys, i, lambda cp: cp.start())

        @pl.when(jnp.logical_and(i == n_act, phase == c))
        def _(xs=xs, ys=ys):
            x_copy(xs, i).wait()
            x_copy((xs + 1) % X_BUFS, i + 1).wait()

            @pl.when(i >= 2)
            def _():
                y_pieces(ys, i - 2, lambda cp: cp.wait())

            @pl.when(i >= 1)
            def _():
                y_pieces(1 - ys, i - 1, lambda cp: cp.wait())


def _experts(xs_slab, wg, wu, wd, tile_e, tile_valid, tile_u, n_active):
    nt = tile_e.shape[0]
    d = wg.shape[1]
    de = wg.shape[2]
    buf = pltpu.VMEM((TM_MOE * SLAB, LANES), jnp.float32)
    return pl.pallas_call(
        _experts_kernel,
        out_shape=jax.ShapeDtypeStruct(xs_slab.shape, jnp.float32),
        grid_spec=pltpu.PrefetchScalarGridSpec(
            num_scalar_prefetch=4,
            grid=(nt,),
            in_specs=[
                pl.BlockSpec(memory_space=pl.ANY),
                pl.BlockSpec((1, d, de), lambda i, te, tv, tu, na: (te[i], 0, 0)),
                pl.BlockSpec((1, d, de), lambda i, te, tv, tu, na: (te[i], 0, 0)),
                pl.BlockSpec((1, de, d), lambda i, te, tv, tu, na: (te[i], 0, 0)),
            ],
            out_specs=pl.BlockSpec(memory_space=pl.ANY),
            scratch_shapes=[buf] * (X_BUFS + Y_BUFS) + [
                pltpu.VMEM((d, 2 * de), jnp.bfloat16), pltpu.VMEM((de, d), jnp.bfloat16),
                pltpu.SemaphoreType.DMA((X_BUFS,)), pltpu.SemaphoreType.DMA((Y_BUFS,))],
        ),
        compiler_params=_cparams(("arbitrary",)),
        name="experts",
    )(tile_e, tile_valid, tile_u, n_active, xs_slab, wg, wu, wd)


def _combine_kernel(h_ref, w_ref, *rest):
    y_refs = rest[:TOP_K]
    sgu_ref, sd_ref, g_ref, b_ref, o_ref = rest[TOP_K:]
    tn = h_ref.shape[0]
    h = h_ref[...]
    gu = _dot(h.astype(jnp.bfloat16), sgu_ref[...])
    ds = gu.shape[1] // 2
    gate = gu[:, :ds]
    hid = (gate / (1.0 + jnp.exp(-gate))) * gu[:, ds:]
    shared = _dot(hid.astype(jnp.bfloat16), sd_ref[...])

    w = w_ref[...]
    routed = _slab_load(y_refs[0], (), tn) * w[:, 0:1]
    for k in range(1, TOP_K):
        routed = routed + _slab_load(y_refs[k], (), tn) * w[:, k:k + 1]
    o_ref[...] = _layernorm_f32(ALPHA * h + (routed + shared), g_ref[...], b_ref[...])


def _combine(h1, w_tok, y_slab, sgu, sd, g, b):
    t, d = h1.shape
    tn = min(256, t)
    nblk = t // tn
    ds2 = sgu.shape[1]
    vec = pl.BlockSpec((1, d), lambda i: (0, 0))

    def slot_spec(k):
        return pl.BlockSpec((tn * SLAB, LANES), lambda i: (k * nblk + i, 0))

    return pl.pallas_call(
        _combine_kernel,
        out_shape=jax.ShapeDtypeStruct((t, d), jnp.float32),
        grid=(nblk,),
        in_specs=[
            pl.BlockSpec((tn, d), lambda i: (i, 0)),
            pl.BlockSpec((tn, TOP_K), lambda i: (i, 0)),
            *[slot_spec(k) for k in range(TOP_K)],
            pl.BlockSpec((d, ds2), lambda i: (0, 0)),
            pl.BlockSpec((ds2 // 2, d), lambda i: (0, 0)),
            vec, vec,
        ],
        out_specs=pl.BlockSpec((tn, d), lambda i: (i, 0)),
        compiler_params=_cparams(("parallel",)),
        name="combine_ln2",
    )(h1, w_tok, *([y_slab] * TOP_K), sgu, sd, g, b)


def _rel_bucket(rel):
    nb = N_BUCKETS // 2
    max_exact = nb // 2
    ret = jnp.where(rel > 0, nb, 0)
    n = jnp.abs(rel)
    nf = jnp.maximum(n, 1).astype(jnp.float32)
    large = max_exact + (jnp.log(nf / max_exact) / math.log(MAX_DISTANCE / max_exact) * (nb - max_exact)).astype(jnp.int32)
    large = jnp.minimum(large, nb - 1)
    return ret + jnp.where(n < max_exact, n, large)


def _bias_of_rel(rel_bias, rel):
    return rel_bias.astype(jnp.float32)[_rel_bucket(rel)]


def _toeplitz(vec, nrow, ncol, off):
    lo = off - (nrow - 1)
    v = vec[lo:off + ncol]
    p = v.shape[0] + 1
    v = jnp.concatenate([v, v[:1]], axis=0)
    flat = jnp.tile(v, (nrow + 1, 1))
    base = off - lo
    out = flat[base:base + nrow * (p - 1)].reshape(nrow, p - 1, vec.shape[1])
    return out[:, :ncol]


def _bias_tables_a(rel_bias, sink):
    bias_a = rel_bias[:, :A_Q_HEADS]
    m = jnp.arange(3 * TQ_A + TQ_A - 1, dtype=jnp.int32)
    rel = 2 * TQ_A - 1 - m
    vec = jnp.where((jnp.abs(rel) <= WINDOW)[:, None], _bias_of_rel(bias_a, rel), NEG)
    band = _toeplitz(vec, 3 * TQ_A, TQ_A, 3 * TQ_A - 1)
    mm = jnp.arange(TQ_A + N_META - 1, dtype=jnp.int32)
    meta_first = _toeplitz(_bias_of_rel(bias_a, -1 - mm), N_META, TQ_A, N_META - 1)
    meta_far = jnp.broadcast_to(_bias_of_rel(bias_a, jnp.int32(-2 * MAX_DISTANCE)), (N_META, TQ_A, A_Q_HEADS))
    pad = jnp.full((NK_A - 3 * TQ_A - N_META, TQ_A, A_Q_HEADS), NEG, jnp.float32)
    blocked = jnp.full((TQ_A, TQ_A, A_Q_HEADS), NEG, jnp.float32)
    first = jnp.concatenate([blocked, band[TQ_A:], meta_first, pad])
    middle = jnp.concatenate([band, meta_far, pad])
    last = jnp.concatenate([band[:2 * TQ_A], blocked, meta_far, pad])
    tab = jnp.stack([first, middle, last])
    tab = tab.reshape(3, NK_A, TQ_A, A_Q_HEADS // 2, 2)
    tab = jnp.transpose(tab, (0, 3, 1, 4, 2)).reshape(3, A_Q_HEADS // 2, NK_A, 2 * TQ_A) * LOG2E
    sink_rows = jnp.repeat(sink.astype(jnp.float32).reshape(A_Q_HEADS // 2, 1, 2), TQ_A, axis=2) * LOG2E
    return tab, sink_rows


def _bias_tables_b(rel_bias, s):
    bias_b = rel_bias[:, A_Q_HEADS:]
    near = []
    for d in (-1, 0, 1):
        m = jnp.arange(TQ_B + TK_B - 1, dtype=jnp.int32)
        vec = _bias_of_rel(bias_b, TK_B * d + TK_B - 1 - m)
        near.append(_toeplitz(vec, TK_B, TQ_B, TK_B - 1))
    far_l = jnp.broadcast_to(_bias_of_rel(bias_b, jnp.int32(-TK_B - 1)), (TK_B, TQ_B, B_HEADS))
    far_r = jnp.broadcast_to(_bias_of_rel(bias_b, jnp.int32(TK_B + 1)), (TK_B, TQ_B, B_HEADS))
    tabs = jnp.transpose(jnp.stack([far_l] + near + [far_r]), (3, 0, 1, 2)) * LOG2E
    m = jnp.arange(s + N_META - 1, dtype=jnp.int32)
    vec = _bias_of_rel(bias_b, -1 - m)
    meta = jnp.transpose(_toeplitz(vec, N_META, s, N_META - 1), (2, 0, 1)) * LOG2E
    return tabs, meta


def _prep_w_in(w_in):
    a_w = A_Q_HEADS * HEAD_DIM
    kv = A_KV_HEADS * HEAD_DIM
    bqk = B_HEADS * 2 * HEAD_DIM
    scale = HEAD_DIM ** -0.5
    qa = w_in[:, :a_w] * (scale * LOG2E)
    ka = w_in[:, a_w:a_w + kv]
    va = w_in[:, a_w + kv:a_w + 2 * kv]
    o = a_w + 2 * kv
    qb = w_in[:, o:o + bqk] * (scale * LOG2E)
    kb = w_in[:, o + bqk:o + 2 * bqk]
    vb = w_in[:, o + 2 * bqk:]

    def dup(w):
        return jnp.concatenate([w[:, g * HEAD_DIM:(g + 1) * HEAD_DIM] for g in range(A_KV_HEADS) for _ in range(2)], axis=1)

    w = jnp.concatenate([qa, dup(ka), qb, kb], axis=1).astype(jnp.bfloat16)
    return w, jnp.concatenate([vb, va], axis=1).T.astype(jnp.bfloat16)


def _trunk_front(x, prm):
    bsz, s, d = x.shape
    t = bsz * s
    x2d = x.reshape(t, d)
    proj, vt, vta = _ln_inproj(x2d, prm["ln_emb_g"], prm["ln_emb_b"], prm["w_in"], prm["w_vt"])
    proj3 = proj.reshape(bsz, s, PROJ_COLS)
    oa = _mixer_a(proj3, vta, prm["km_a"], prm["vtm_a"], prm["tab_a"], prm["sink_rows"])
    tabs_b, meta_b = _bias_tables_b(prm["rel_bias"], s)
    ob = _mixer_b(proj3, vt, prm["proj_meta"], prm["vt_meta"], tabs_b, meta_b, prm["lam"].reshape(1).astype(jnp.float32),
                  prm["subln_g"])
    h1, h1_slab = _outproj_ln1(x2d, oa.reshape(t, -1), ob.reshape(t, -1), prm["w_out"], prm["ln_emb_g"],
                               prm["ln_emb_b"], prm["ln1_g"], prm["ln1_b"])

    keys, wts, counts = _router(h1, prm["wr_hi"], prm["wr_lo"], prm["router_bias"])
    n_exp = counts.shape[0]
    n_asg = t * TOP_K
    assert n_asg <= (1 << KEY_SHIFT) and n_asg % LANES == 0
    order = jnp.sort(keys.reshape(n_asg)) & ((1 << KEY_SHIFT) - 1)
    tok = order >> 3
    dst = (order & (TOP_K - 1)) * t + tok
    sc_pad = SC_WORKERS * SC_GRP * SC_NBUF
    assert sc_pad >= TM_MOE and n_asg % sc_pad == 0
    xs = _sc_move_rows(h1_slab.reshape(t, SLAB, LANES), jnp.concatenate([tok, jnp.zeros((sc_pad,), jnp.int32)]),
                       n_asg + sc_pad, scatter=False)
    xs_slab = xs.reshape((n_asg + sc_pad) * SLAB, LANES)
    counts = counts[:, 0].astype(jnp.int32)
    tiles_e = (counts + TM_MOE - 1) // TM_MOE
    tend = jnp.cumsum(tiles_e)
    tstart = tend - tiles_e
    ustart = jnp.cumsum(counts) - counts
    nt = n_asg // TM_MOE + n_exp + 1
    tid = jnp.arange(nt, dtype=jnp.int32)
    tile_e = jnp.minimum(jnp.sum((tend[None, :] <= tid[:, None]).astype(jnp.int32), axis=1), n_exp - 1)
    onehot = (tile_e[:, None] == jnp.arange(n_exp, dtype=jnp.int32)[None, :]).astype(jnp.int32)
    in_e = (tid - jnp.sum(onehot * tstart[None, :], axis=1)) * TM_MOE
    active = tid < tend[-1]
    tile_valid = jnp.where(active, jnp.clip(jnp.sum(onehot * counts[None, :], axis=1) - in_e, 0, TM_MOE), 0)
    tile_u = jnp.where(active, jnp.sum(onehot * ustart[None, :], axis=1) + in_e, 0)

    tiles = (tile_e.astype(jnp.int32), tile_valid.astype(jnp.int32), tile_u.astype(jnp.int32),
             tend[-1:].astype(jnp.int32))
    return h1, wts.T, xs_slab, dst, tiles, x.shape


def _trunk_back(front, prm):
    h1, w_tok, xs_slab, dst, tiles, shape = front
    n_asg = dst.shape[0]
    ys_slab = _experts(xs_slab, prm["w_gate"], prm["w_up"], prm["w_down"], *tiles)
    y = _sc_move_rows(ys_slab.reshape(-1, SLAB, LANES), dst, n_asg, scatter=True)
    out = _combine(h1, w_tok, y.reshape(n_asg * SLAB, LANES), prm["ws_gu"], prm["ws_down"], prm["ln2_g"], prm["ln2_b"])
    return out.reshape(shape)


def kernel(x_prompt, x_sample, meta_tokens, ln_emb_g, ln_emb_b, rel_bias, w_in, attn_sink, lambda_q1, lambda_k1, lambda_q2, lambda_k2, subln_g, w_out, ln1_g, ln1_b, w_router, router_bias, w_gate, w_up, w_down, ws_gate, ws_up, ws_down, ln2_g, ln2_b):
    f32 = jnp.float32
    bf16 = jnp.bfloat16
    l = 0
    row = lambda v: v.reshape(1, -1).astype(f32)
    lam = (jnp.exp(jnp.sum(lambda_q1[l].astype(f32) * lambda_k1[l].astype(f32)))
           - jnp.exp(jnp.sum(lambda_q2[l].astype(f32) * lambda_k2[l].astype(f32))) + LAMBDA_INIT)
    wr_t = w_router[l].astype(f32).T
    wr_hi = wr_t.astype(bf16)
    prm = {
        "ln_emb_g": row(ln_emb_g), "ln_emb_b": row(ln_emb_b),
        "rel_bias": rel_bias,
        "lam": lam,
        "subln_g": subln_g[l].astype(f32).reshape(-1, 1),
        "w_out": w_out[l].astype(bf16),
        "ln1_g": row(ln1_g[l]), "ln1_b": row(ln1_b[l]),
        "wr_hi": wr_hi, "wr_lo": (wr_t - wr_hi.astype(f32)).astype(bf16),
        "router_bias": router_bias[l].astype(f32).reshape(-1, 1),
        "w_gate": w_gate[l], "w_up": w_up[l], "w_down": w_down[l],
        "ws_gu": jnp.concatenate([ws_gate[l], ws_up[l]], axis=-1).astype(bf16),
        "ws_down": ws_down[l].astype(bf16),
        "ln2_g": row(ln2_g[l]), "ln2_b": row(ln2_b[l]),
    }
    prm["w_in"], prm["w_vt"] = _prep_w_in(w_in[l])
    prm["proj_meta"], prm["vt_meta"], vta_meta = _ln_inproj(meta_tokens.astype(f32), prm["ln_emb_g"], prm["ln_emb_b"],
                                                           prm["w_in"], prm["w_vt"])
    meta_pad = NK_A - 3 * TQ_A - N_META
    prm["km_a"] = jnp.pad(prm["proj_meta"][:, KA_BLK * LANES:(KA_BLK + 2) * LANES], ((0, meta_pad), (0, 0)))
    prm["vtm_a"] = jnp.pad(vta_meta, ((0, 0), (0, meta_pad)))
    prm["tab_a"], prm["sink_rows"] = _bias_tables_a(rel_bias, attn_sink[l])
    front_p = _trunk_front(x_prompt, prm)
    front_s = _trunk_front(x_sample, prm)
    return (_trunk_back(front_p, prm), _trunk_back(front_s, prm))
```

```python
import functools
import math

import jax
import jax.numpy as jnp
from jax import lax
from jax.experimental import pallas as pl
from jax.experimental.pallas import tpu as pltpu
from jax.experimental.pallas import tpu_sc as plsc

N_META = 16
HEAD_DIM = 64
WINDOW = 128
A_Q_HEADS = 8
A_KV_HEADS = 2
B_HEADS = 4
N_BUCKETS = 32
MAX_DISTANCE = 128
TOP_K = 8
N_GROUPS = 8
TOPK_GROUPS = 4
ROUTED_SCALE = 2.5
LN_EPS = 1e-5
DEPTH = 1
ALPHA = (2 * DEPTH) ** 0.25
NEG = -1e30
LAMBDA_INIT = 0.8 - 0.6 * math.exp(-0.3 * 0)

LANES = 128
VMEM_LIMIT = 48 * 1024 * 1024

QA_BLK = 0
KA_BLK = 4
QB_BLK = 6
KB_BLK = 10
PROJ_COLS = 14 * LANES
VT_ROWS = LANES + 16
VA_ROWS = HEAD_DIM + 16
LOG2E = 1.4426950408889634

TQ_A = 128
NK_A = 4 * TQ_A
TQ_B = 512
TK_B = 512
TM_MOE = 512
SLAB = 8
SC_GRP = 32
SC_NBUF = 2
SC_WORKERS = 32
KEY_SHIFT = 20


def _cparams(sem):
    return pltpu.CompilerParams(dimension_semantics=sem, vmem_limit_bytes=VMEM_LIMIT)


def _layernorm_f32(x, g, b):
    mu = jnp.mean(x, axis=-1, keepdims=True)
    xc = x - mu
    var = jnp.mean(xc * xc, axis=-1, keepdims=True)
    return xc * lax.rsqrt(var + LN_EPS) * g + b


def _dot_nt(a, b):
    return lax.dot_general(a, b, (((1,), (1,)), ((), ())), preferred_element_type=jnp.float32)


def _dot(a, b):
    return jnp.dot(a, b, preferred_element_type=jnp.float32)


def _ln_inproj_kernel(x_ref, g_ref, b_ref, w_ref, wvt_ref, o_ref, vt_ref, vta_ref):
    h = _layernorm_f32(x_ref[...], g_ref[...], b_ref[...]).astype(jnp.bfloat16)
    o_ref[...] = _dot(h, w_ref[...]).astype(o_ref.dtype)
    vt = _dot_nt(wvt_ref[...], h).astype(vt_ref.dtype)
    ones = jnp.ones((16, vt.shape[1]), vt_ref.dtype)
    for hd in range(B_HEADS):
        vt_ref[hd * VT_ROWS:hd * VT_ROWS + LANES, :] = vt[hd * LANES:(hd + 1) * LANES, :]
        vt_ref[hd * VT_ROWS + LANES:(hd + 1) * VT_ROWS, :] = ones
    base = B_HEADS * LANES
    for g in range(A_KV_HEADS):
        vta_ref[g * VA_ROWS:g * VA_ROWS + HEAD_DIM, :] = vt[base + g * HEAD_DIM:base + (g + 1) * HEAD_DIM, :]
        vta_ref[g * VA_ROWS + HEAD_DIM:(g + 1) * VA_ROWS, :] = ones


def _ln_inproj(x2d, g, b, w, wvt):
    t, d = x2d.shape
    n = w.shape[1]
    tm = min(1024, t)
    return pl.pallas_call(
        _ln_inproj_kernel,
        out_shape=(jax.ShapeDtypeStruct((t, n), jnp.bfloat16),
                   jax.ShapeDtypeStruct((B_HEADS * VT_ROWS, t), jnp.bfloat16),
                   jax.ShapeDtypeStruct((A_KV_HEADS * VA_ROWS, t), jnp.bfloat16)),
        grid=(t // tm,),
        in_specs=[
            pl.BlockSpec((tm, d), lambda i: (i, 0)),
            pl.BlockSpec((1, d), lambda i: (0, 0)),
            pl.BlockSpec((1, d), lambda i: (0, 0)),
            pl.BlockSpec((d, n), lambda i: (0, 0)),
            pl.BlockSpec((wvt.shape[0], d), lambda i: (0, 0)),
        ],
        out_specs=(pl.BlockSpec((tm, n), lambda i: (i, 0)),
                   pl.BlockSpec((B_HEADS * VT_ROWS, tm), lambda i: (0, i)),
                   pl.BlockSpec((A_KV_HEADS * VA_ROWS, tm), lambda i: (0, i))),
        compiler_params=_cparams(("parallel",)),
        name="ln_inproj",
    )(x2d, g, b, w, wvt)


def _mixer_a_kernel(q_ref, k_ref, vt_ref, km_ref, vtm_ref, tab_ref, sink_ref, o_ref, *, nblk, sub):
    i = pl.program_id(1)
    lane = lax.broadcasted_iota(jnp.int32, (1, LANES), 1)
    lo = lane < HEAD_DIM
    for j in range(sub):
        gi = i * sub + j
        sp = pl.multiple_of(jnp.maximum(gi - 1, 0) * TQ_A, TQ_A)
        sc = pl.multiple_of(gi * TQ_A, TQ_A)
        sn = pl.multiple_of(jnp.minimum(gi + 1, nblk - 1) * TQ_A, TQ_A)
        variant = jnp.where(gi == 0, 0, jnp.where(gi == nblk - 1, 2, 1))
        rows = slice(j * TQ_A, (j + 1) * TQ_A)
        for g in range(A_KV_HEADS):
            gs = slice(g * LANES, (g + 1) * LANES)
            vr = slice(g * VA_ROWS, (g + 1) * VA_ROWS)
            k_all = jnp.concatenate([k_ref[0, pl.ds(sp, TQ_A), gs], k_ref[0, pl.ds(sc, TQ_A), gs],
                                     k_ref[0, pl.ds(sn, TQ_A), gs], km_ref[:, gs]], axis=0)
            vt_all = jnp.concatenate([vt_ref[vr, pl.ds(sp, TQ_A)], vt_ref[vr, pl.ds(sc, TQ_A)],
                                      vt_ref[vr, pl.ds(sn, TQ_A)], vtm_ref[vr, :]], axis=1)
            for pp in range(2):
                hp = 2 * g + pp
                cols = slice(hp * LANES, (hp + 1) * LANES)
                qc = q_ref[0, rows, cols]
                q2 = jnp.concatenate([jnp.where(lo, qc, jnp.zeros_like(qc)), jnp.where(lo, jnp.zeros_like(qc), qc)],
                                     axis=0)
                s = _dot_nt(k_all, q2) + tab_ref[variant, hp]
                sink = sink_ref[hp]
                m = jnp.maximum(jnp.max(s, axis=0, keepdims=True), sink)
                p = jnp.exp2(s - m)
                acc = _dot(vt_all, p.astype(jnp.bfloat16))
                o = acc[0:HEAD_DIM, :] / (acc[HEAD_DIM:HEAD_DIM + 1, :] + jnp.exp2(sink - m))
                o2 = jnp.concatenate([o[:, :TQ_A], o[:, TQ_A:]], axis=0)
                o_ref[0, rows, cols] = o2.T.astype(o_ref.dtype)


def _mixer_a(proj3, vta, km, vtm, tab, sink):
    bsz, s, _ = proj3.shape
    nblk = s // TQ_A
    assert nblk >= 2
    sub = 4 if nblk % 4 == 0 else 1
    nq = nblk // sub
    tq = sub * TQ_A
    kern = functools.partial(_mixer_a_kernel, nblk=nblk, sub=sub)
    return pl.pallas_call(
        kern,
        out_shape=jax.ShapeDtypeStruct((bsz, s, A_Q_HEADS * HEAD_DIM), jnp.bfloat16),
        grid=(bsz, nq),
        in_specs=[
            pl.BlockSpec((1, tq, 4 * LANES), lambda b, i: (b, i, QA_BLK // 4)),
            pl.BlockSpec((1, s, 2 * LANES), lambda b, i: (b, 0, KA_BLK // 2)),
            pl.BlockSpec((A_KV_HEADS * VA_ROWS, s), lambda b, i: (0, b)),
            pl.BlockSpec(km.shape, lambda b, i: (0, 0)),
            pl.BlockSpec(vtm.shape, lambda b, i: (0, 0)),
            pl.BlockSpec(tab.shape, lambda b, i: (0, 0, 0, 0)),
            pl.BlockSpec(sink.shape, lambda b, i: (0, 0, 0)),
        ],
        out_specs=pl.BlockSpec((1, tq, 4 * LANES), lambda b, i: (b, i, 0)),
        compiler_params=_cparams(("parallel", "arbitrary")),
        name="mixer_a",
    )(proj3, proj3, vta, km, vtm, tab, sink)


def _mixer_b_kernel(sc_ref, q_ref, k_ref, vt_ref, km_ref, vtm_ref, tab_ref, tabm_ref, g_ref, o_ref,
                    m_ref, acc_ref, s0_ref, s1_ref, cmax_ref, *, nk):
    i = pl.program_id(2)
    lane = lax.broadcasted_iota(jnp.int32, (1, LANES), 1)
    lo = lane < HEAD_DIM
    q = q_ref[0]
    qs = (jnp.where(lo, q, jnp.zeros_like(q)), jnp.where(lo, jnp.zeros_like(q), q))
    lam = sc_ref[0]
    sbufs = (s0_ref, s1_ref)

    km = km_ref[...]
    vtm = vtm_ref[...]
    for c in range(2):
        s = _dot_nt(km, qs[c]) + tabm_ref[0]
        m = jnp.max(s, axis=0, keepdims=True)
        p = jnp.exp2(s - m)
        m_ref[c] = m
        acc_ref[c] = _dot(vtm, p.astype(jnp.bfloat16))

    def scores(j, slot):
        start = pl.multiple_of(j * TK_B, TK_B)
        kj = k_ref[0, pl.ds(start, TK_B), :]
        bias = tab_ref[0, jnp.clip(j - i, -2, 2) + 2]
        for c in range(2):
            s = _dot_nt(kj, qs[c]) + bias
            sbufs[slot][c] = s
            cmax_ref[slot, c] = jnp.max(s, axis=0, keepdims=True)

    def accumulate(j, slot):
        start = pl.multiple_of(j * TK_B, TK_B)
        vtj = vt_ref[:, pl.ds(start, TK_B)]
        for c in range(2):
            s = sbufs[slot][c]
            m_prev = m_ref[c]
            m_new = jnp.maximum(m_prev, cmax_ref[slot, c])
            a = jnp.exp2(m_prev - m_new)
            p = jnp.exp2(s - m_new)
            m_ref[c] = m_new
            acc_ref[c] = a * acc_ref[c] + _dot(vtj, p.astype(jnp.bfloat16))

    scores(0, 0)

    def pair(jj, carry):
        j = 2 * jj
        scores(j + 1, 1)
        accumulate(j, 0)
        scores(j + 2, 0)
        accumulate(j + 1, 1)
        return carry

    lax.fori_loop(0, nk // 2 - 1, pair, 0)
    scores(nk - 1, 1)
    accumulate(nk - 2, 0)
    accumulate(nk - 1, 1)

    o0 = acc_ref[0, 0:LANES, :] / acc_ref[0, LANES:LANES + 1, :]
    o1 = acc_ref[1, 0:LANES, :] / acc_ref[1, LANES:LANES + 1, :]
    o = o0 - lam * o1
    ms = jnp.mean(o * o, axis=0, keepdims=True)
    o = o * lax.rsqrt(ms + LN_EPS) * (g_ref[...] * (1.0 - LAMBDA_INIT))
    o_ref[0] = o.T.astype(o_ref.dtype)


def _mixer_b(proj3, vt, proj_meta, vt_meta, tab, tabm, scal, subln_g):
    bsz, s, _ = proj3.shape
    nq = s // TQ_B
    nk = s // TK_B
    assert nk % 2 == 0
    kern = functools.partial(_mixer_b_kernel, nk=nk)
    return pl.pallas_call(
        kern,
        out_shape=jax.ShapeDtypeStruct((bsz, s, B_HEADS * LANES), jnp.bfloat16),
        grid=(bsz, B_HEADS, nq),
        in_specs=[
            pl.BlockSpec(memory_space=pltpu.SMEM),
            pl.BlockSpec((1, TQ_B, LANES), lambda b, h, i: (b, i, QB_BLK + h)),
            pl.BlockSpec((1, s, LANES), lambda b, h, i: (b, 0, KB_BLK + h)),
            pl.BlockSpec((VT_ROWS, s), lambda b, h, i: (h, b)),
            pl.BlockSpec((N_META, LANES), lambda b, h, i: (0, KB_BLK + h)),
            pl.BlockSpec((VT_ROWS, N_META), lambda b, h, i: (h, 0)),
            pl.BlockSpec((1, 5, TK_B, TQ_B), lambda b, h, i: (h, 0, 0, 0)),
            pl.BlockSpec((1, N_META, TQ_B), lambda b, h, i: (h, 0, i)),
            pl.BlockSpec((LANES, 1), lambda b, h, i: (0, 0)),
        ],
        out_specs=pl.BlockSpec((1, TQ_B, LANES), lambda b, h, i: (b, i, h)),
        scratch_shapes=[
            pltpu.VMEM((2, 1, TQ_B), jnp.float32),
            pltpu.VMEM((2, VT_ROWS, TQ_B), jnp.float32),
            pltpu.VMEM((2, TK_B, TQ_B), jnp.float32),
            pltpu.VMEM((2, TK_B, TQ_B), jnp.float32),
            pltpu.VMEM((2, 2, 1, TQ_B), jnp.float32),
        ],
        compiler_params=_cparams(("parallel", "parallel", "arbitrary")),
        name="mixer_b",
    )(scal, proj3, proj3, vt, proj_meta, vt_meta, tab, tabm, subln_g)


def _slab_load(ref, lead, rows):
    return jnp.concatenate([ref[lead + (pl.ds(s, rows, stride=SLAB), slice(None))] for s in range(SLAB)], axis=1)


def _slab_store(ref, lead, val):
    rows = val.shape[0]
    for s in range(SLAB):
        ref[lead + (pl.ds(s, rows, stride=SLAB), slice(None))] = val[:, s * LANES:(s + 1) * LANES]


def _outproj_kernel(x_ref, oa_ref, ob_ref, w_ref, eg_ref, eb_ref, g_ref, b_ref, o_ref, os_ref):
    h0 = _layernorm_f32(x_ref[...], eg_ref[...], eb_ref[...])
    half = oa_ref.shape[1]
    mix = _dot(oa_ref[...], w_ref[0:half, :]) + _dot(ob_ref[...], w_ref[half:, :])
    h1 = _layernorm_f32(ALPHA * h0 + mix, g_ref[...], b_ref[...])
    o_ref[...] = h1
    _slab_store(os_ref, (), h1)


def _outproj_ln1(x2d, oa, ob, w_out, eg, eb, g, b):
    t, d = x2d.shape
    tm = min(1024, t)
    wa = oa.shape[1]
    wb = ob.shape[1]
    vec = pl.BlockSpec((1, d), lambda i: (0, 0))
    return pl.pallas_call(
        _outproj_kernel,
        out_shape=(jax.ShapeDtypeStruct((t, d), jnp.float32),
                   jax.ShapeDtypeStruct((t * SLAB, LANES), jnp.float32)),
        grid=(t // tm,),
        in_specs=[
            pl.BlockSpec((tm, d), lambda i: (i, 0)),
            pl.BlockSpec((tm, wa), lambda i: (i, 0)),
            pl.BlockSpec((tm, wb), lambda i: (i, 0)),
            pl.BlockSpec((wa + wb, d), lambda i: (0, 0)),
            vec, vec, vec, vec,
        ],
        out_specs=(pl.BlockSpec((tm, d), lambda i: (i, 0)),
                   pl.BlockSpec((tm * SLAB, LANES), lambda i: (i, 0))),
        compiler_params=_cparams(("parallel",)),
        name="outproj_ln1",
    )(x2d, oa, ob, w_out, eg, eb, g, b)


def _router_kernel(h_ref, wh_ref, wl_ref, rb_ref, e_ref, w_ref, cnt_ref, carry_ref, *, n_exp):
    i = pl.program_id(0)
    tn = h_ref.shape[0]
    gsz = n_exp // N_GROUPS

    @pl.when(i == 0)
    def _():
        carry_ref[...] = jnp.zeros_like(carry_ref)

    x = h_ref[...]
    xh = x.astype(jnp.bfloat16)
    xl = (x - xh.astype(jnp.float32)).astype(jnp.bfloat16)
    logits = _dot_nt(wh_ref[...], xh) + (_dot_nt(wh_ref[...], xl) + _dot_nt(wl_ref[...], xh))
    scores = 1.0 / (1.0 + jnp.exp(-logits))
    biased = scores + rb_ref[...]

    g3 = biased.reshape(N_GROUPS, gsz, tn)
    it3 = lax.broadcasted_iota(jnp.int32, (N_GROUPS, gsz, tn), 1)
    mx1 = jnp.max(g3, axis=1, keepdims=True)
    first = jnp.min(jnp.where(g3 == mx1, it3, gsz), axis=1, keepdims=True)
    mx2 = jnp.max(jnp.where(it3 == first, -jnp.inf, g3), axis=1, keepdims=True)
    gscore = (mx1 + mx2).reshape(N_GROUPS, tn)

    itg = lax.broadcasted_iota(jnp.int32, (N_GROUPS, tn), 0)
    gsel = jnp.zeros((N_GROUPS, tn), jnp.bool_)
    cur = gscore
    for _ in range(TOPK_GROUPS):
        mx = jnp.max(cur, axis=0, keepdims=True)
        fi = jnp.min(jnp.where(cur == mx, itg, N_GROUPS), axis=0, keepdims=True)
        hit = itg == fi
        gsel = jnp.logical_or(gsel, hit)
        cur = jnp.where(hit, -jnp.inf, cur)
    emask = jnp.broadcast_to(gsel.reshape(N_GROUPS, 1, tn), (N_GROUPS, gsz, tn)).reshape(n_exp, tn)
    cur = jnp.where(emask, biased, NEG)

    ite = lax.broadcasted_iota(jnp.int32, (n_exp, tn), 0)
    hits = []
    eidx = []
    wsel = []
    for _ in range(TOP_K):
        mx = jnp.max(cur, axis=0, keepdims=True)
        fi = jnp.min(jnp.where(cur == mx, ite, n_exp), axis=0, keepdims=True)
        hit = ite == fi
        hits.append(hit)
        eidx.append(fi)
        wsel.append(jnp.sum(jnp.where(hit, scores, 0.0), axis=0, keepdims=True))
        cur = jnp.where(hit, -jnp.inf, cur)
    sel = hits[0]
    for hit in hits[1:]:
        sel = jnp.logical_or(sel, hit)
    self32 = jnp.where(sel, 1.0, 0.0)

    carry_ref[...] = carry_ref[...] + jnp.sum(self32, axis=1, keepdims=True)
    cnt_ref[...] = carry_ref[...]

    wcat = jnp.concatenate(wsel, axis=0)
    wcat = wcat / jnp.sum(wcat, axis=0, keepdims=True) * ROUTED_SCALE
    tok = i * tn + lax.broadcasted_iota(jnp.int32, (TOP_K, tn), 1)
    slot = lax.broadcasted_iota(jnp.int32, (TOP_K, tn), 0)
    e_ref[...] = jnp.concatenate(eidx, axis=0) * (1 << KEY_SHIFT) + (tok * TOP_K + slot)
    w_ref[...] = wcat


def _router(h1, wr_hi, wr_lo, rbias):
    t, d = h1.shape
    n_exp = wr_hi.shape[0]
    tn = min(512, t)
    kern = functools.partial(_router_kernel, n_exp=n_exp)
    row = pl.BlockSpec((TOP_K, tn), lambda i: (0, i))
    return pl.pallas_call(
        kern,
        out_shape=(
            jax.ShapeDtypeStruct((TOP_K, t), jnp.int32),
            jax.ShapeDtypeStruct((TOP_K, t), jnp.float32),
            jax.ShapeDtypeStruct((n_exp, 1), jnp.float32),
        ),
        grid=(t // tn,),
        in_specs=[
            pl.BlockSpec((tn, d), lambda i: (i, 0)),
            pl.BlockSpec((n_exp, d), lambda i: (0, 0)),
            pl.BlockSpec((n_exp, d), lambda i: (0, 0)),
            pl.BlockSpec((n_exp, 1), lambda i: (0, 0)),
        ],
        out_specs=(row, row, pl.BlockSpec((n_exp, 1), lambda i: (0, 0))),
        scratch_shapes=[pltpu.VMEM((n_exp, 1), jnp.float32)],
        compiler_params=_cparams(("arbitrary",)),
        name="router",
    )(h1, wr_hi, wr_lo, rbias)


def _sc_move_rows(x_slab, idx, n_out, scatter):
    m = idx.shape[0]
    rnd = SC_GRP * SC_NBUF
    per = m // SC_WORKERS
    assert m % (SC_WORKERS * rnd) == 0
    row = x_slab.shape[1:]
    mesh = plsc.VectorSubcoreMesh(core_axis_name="c", subcore_axis_name="s")

    @pl.kernel(out_type=jax.ShapeDtypeStruct((n_out,) + row, x_slab.dtype), mesh=mesh,
               scratch_types=[pltpu.VMEM((rnd,), jnp.int32), pltpu.VMEM((SC_NBUF, SC_GRP) + row, x_slab.dtype),
                              pltpu.SemaphoreType.DMA((SC_NBUF,)), pltpu.SemaphoreType.DMA((SC_NBUF,))])
    def kern(x_hbm, i_hbm, o_hbm, ibuf, buf, lsem, ssem):
        worker = lax.axis_index("c") * (SC_WORKERS // 2) + lax.axis_index("s")
        base = worker * per

        @pl.loop(0, per // rnd)
        def _(r):
            off = base + r * rnd
            pltpu.sync_copy(i_hbm.at[pl.ds(off, rnd)], ibuf)
            loads = []
            stores = []
            for b in range(SC_NBUF):
                indexed = ibuf.at[pl.ds(b * SC_GRP, SC_GRP)]
                linear = pl.ds(off + b * SC_GRP, SC_GRP)
                src = x_hbm.at[linear] if scatter else x_hbm.at[indexed]
                dst = o_hbm.at[indexed] if scatter else o_hbm.at[linear]
                loads.append(pltpu.make_async_copy(src, buf.at[b], lsem.at[b]))
                stores.append(pltpu.make_async_copy(buf.at[b], dst, ssem.at[b]))
            for ld in loads:
                ld.start()
            for b in range(SC_NBUF):
                loads[b].wait()
                stores[b].start()
            for st in stores:
                st.wait()

    return kern(x_slab, idx)


X_BUFS = 3
Y_BUFS = 2


def _experts_kernel(te_ref, tv_ref, tu_ref, na_ref, xs_hbm, wg_ref, wu_ref, wd_ref, ys_hbm,
                    xb0, xb1, xb2, yb0, yb1, wgu_s, wd_s, xsem, ysem):
    i = pl.program_id(0)
    nt = pl.num_programs(0)
    n_act = na_ref[0]
    xbufs = (xb0, xb1, xb2)
    ybufs = (yb0, yb1)
    tile_rows = TM_MOE * SLAB

    def x_copy(slot, tile):
        tile = jnp.minimum(tile, nt - 1)
        start = pl.multiple_of(tu_ref[tile] * SLAB, SLAB)
        return pltpu.make_async_copy(xs_hbm.at[pl.ds(start, tile_rows), :], xbufs[slot], xsem.at[slot])

    def y_pieces(slot, tile, go):
        nvalid = tv_ref[tile]
        base = tu_ref[tile]
        size = TM_MOE
        while size >= 1:
            @pl.when((nvalid & size) != 0)
            def _(size=size):
                off = nvalid & ~(2 * size - 1)
                src = ybufs[slot].at[pl.ds(pl.multiple_of(off * SLAB, SLAB), size * SLAB), :]
                dst = ys_hbm.at[pl.ds(pl.multiple_of((base + off) * SLAB, SLAB), size * SLAB), :]
                go(pltpu.make_async_copy(src, dst, ysem.at[slot]))
            size //= 2

    @pl.when(i == 0)
    def _():
        x_copy(0, 0).start()
        x_copy(1, 1).start()

    iprev = jnp.maximum(i - 1, 0)

    @pl.when(jnp.logical_and(i < n_act, jnp.logical_or(i == 0, te_ref[i] != te_ref[iprev])))
    def _():
        de = wg_ref.shape[2]
        wgu_s[:, 0:de] = wg_ref[0].astype(wgu_s.dtype)
        wgu_s[:, de:] = wu_ref[0].astype(wgu_s.dtype)
        wd_s[...] = wd_ref[0].astype(wd_s.dtype)

    phase = i % (X_BUFS * Y_BUFS)
    for c in range(X_BUFS * Y_BUFS):
        xs, ys = c % X_BUFS, c % Y_BUFS

        @pl.when(jnp.logical_and(i < n_act, phase == c))
        def _(xs=xs, ys=ys):
            x_copy(xs, i).wait()
            x_copy((xs + 2) % X_BUFS, i + 2).start()
            x = _slab_load(xbufs[xs], (), TM_MOE).astype(jnp.bfloat16)
            gu = _dot(x, wgu_s[...])
            de = gu.shape[1] // 2
            gate = gu[:, :de]
            hid = (gate / (1.0 + jnp.exp(-gate))) * gu[:, de:]
            y = _dot(hid.astype(jnp.bfloat16), wd_s[...])

            @pl.when(i >= Y_BUFS)
            def _():
                y_pieces(ys, i - Y_BUFS, lambda cp: cp.wait())

            _slab_store(ybufs[ys], (), y)
            y_pieces(ys, i, lambda cp: cp.start())

        @pl.when(jnp.logical_and(i == n_act, phase == c))
        def _(xs=xs, ys=ys):
            x_copy(xs, i).wait()
            x_copy((xs + 1) % X_BUFS, i + 1).wait()

            @pl.when(i >= 2)
            def _():
                y_pieces(ys, i - 2, lambda cp: cp.wait())

            @pl.when(i >= 1)
            def _():
                y_pieces(1 - ys, i - 1, lambda cp: cp.wait())


def _experts(xs_slab, wg, wu, wd, tile_e, tile_valid, tile_u, n_active):
    nt = tile_e.shape[0]
    d = wg.shape[1]
    de = wg.shape[2]
    buf = pltpu.VMEM((TM_MOE * SLAB, LANES), jnp.float32)
    return pl.pallas_call(
        _experts_kernel,
        out_shape=jax.ShapeDtypeStruct(xs_slab.shape, jnp.float32),
        grid_spec=pltpu.PrefetchScalarGridSpec(
            num_scalar_prefetch=4,
            grid=(nt,),
            in_specs=[
                pl.BlockSpec(memory_space=pl.ANY),
                pl.BlockSpec((1, d, de), lambda i, te, tv, tu, na: (te[i], 0, 0)),
                pl.BlockSpec((1, d, de), lambda i, te, tv, tu, na: (te[i], 0, 0)),
                pl.BlockSpec((1, de, d), lambda i, te, tv, tu, na: (te[i], 0, 0)),
            ],
            out_specs=pl.BlockSpec(memory_space=pl.ANY),
            scratch_shapes=[buf] * (X_BUFS + Y_BUFS) + [
                pltpu.VMEM((d, 2 * de), jnp.bfloat16), pltpu.VMEM((de, d), jnp.bfloat16),
                pltpu.SemaphoreType.DMA((X_BUFS,)), pltpu.SemaphoreType.DMA((Y_BUFS,))],
        ),
        compiler_params=_cparams(("arbitrary",)),
        name="experts",
    )(tile_e, tile_valid, tile_u, n_active, xs_slab, wg, wu, wd)


def _combine_kernel(h_ref, w_ref, *rest):
    y_refs = rest[:TOP_K]
    sgu_ref, sd_ref, g_ref, b_ref, o_ref = rest[TOP_K:]
    tn = h_ref.shape[0]
    h = h_ref[...]
    gu = _dot(h.astype(jnp.bfloat16), sgu_ref[...])
    ds = gu.shape[1] // 2
    gate = gu[:, :ds]
    hid = (gate / (1.0 + jnp.exp(-gate))) * gu[:, ds:]
    shared = _dot(hid.astype(jnp.bfloat16), sd_ref[...])

    w = w_ref[...]
    routed = _slab_load(y_refs[0], (), tn) * w[:, 0:1]
    for k in range(1, TOP_K):
        routed = routed + _slab_load(y_refs[k], (), tn) * w[:, k:k + 1]
    o_ref[...] = _layernorm_f32(ALPHA * h + (routed + shared), g_ref[...], b_ref[...])


def _combine(h1, w_tok, y_slab, sgu, sd, g, b):
    t, d = h1.shape
    tn = min(256, t)
    nblk = t // tn
    ds2 = sgu.shape[1]
    vec = pl.BlockSpec((1, d), lambda i: (0, 0))

    def slot_spec(k):
        return pl.BlockSpec((tn * SLAB, LANES), lambda i: (k * nblk + i, 0))

    return pl.pallas_call(
        _combine_kernel,
        out_shape=jax.ShapeDtypeStruct((t, d), jnp.float32),
        grid=(nblk,),
        in_specs=[
            pl.BlockSpec((tn, d), lambda i: (i, 0)),
            pl.BlockSpec((tn, TOP_K), lambda i: (i, 0)),
            *[slot_spec(k) for k in range(TOP_K)],
            pl.BlockSpec((d, ds2), lambda i: (0, 0)),
            pl.BlockSpec((ds2 // 2, d), lambda i: (0, 0)),
            vec, vec,
        ],
        out_specs=pl.BlockSpec((tn, d), lambda i: (i, 0)),
        compiler_params=_cparams(("parallel",)),
        name="combine_ln2",
    )(h1, w_tok, *([y_slab] * TOP_K), sgu, sd, g, b)


def _rel_bucket(rel):
    nb = N_BUCKETS // 2
    max_exact = nb // 2
    ret = jnp.where(rel > 0, nb, 0)
    n = jnp.abs(rel)
    nf = jnp.maximum(n, 1).astype(jnp.float32)
    large = max_exact + (jnp.log(nf / max_exact) / math.log(MAX_DISTANCE / max_exact) * (nb - max_exact)).astype(jnp.int32)
    large = jnp.minimum(large, nb - 1)
    return ret + jnp.where(n < max_exact, n, large)


def _bias_of_rel(rel_bias, rel):
    return rel_bias.astype(jnp.float32)[_rel_bucket(rel)]


def _toeplitz(vec, nrow, ncol, off):
    lo = off - (nrow - 1)
    v = vec[lo:off + ncol]
    p = v.shape[0] + 1
    v = jnp.concatenate([v, v[:1]], axis=0)
    flat = jnp.tile(v, (nrow + 1, 1))
    base = off - lo
    out = flat[base:base + nrow * (p - 1)].reshape(nrow, p - 1, vec.shape[1])
    return out[:, :ncol]


def _bias_tables_a(rel_bias, sink):
    bias_a = rel_bias[:, :A_Q_HEADS]
    m = jnp.arange(3 * TQ_A + TQ_A - 1, dtype=jnp.int32)
    rel = 2 * TQ_A - 1 - m
    vec = jnp.where((jnp.abs(rel) <= WINDOW)[:, None], _bias_of_rel(bias_a, rel), NEG)
    band = _toeplitz(vec, 3 * TQ_A, TQ_A, 3 * TQ_A - 1)
    mm = jnp.arange(TQ_A + N_META - 1, dtype=jnp.int32)
    meta_first = _toeplitz(_bias_of_rel(bias_a, -1 - mm), N_META, TQ_A, N_META - 1)
    meta_far = jnp.broadcast_to(_bias_of_rel(bias_a, jnp.int32(-2 * MAX_DISTANCE)), (N_META, TQ_A, A_Q_HEADS))
    pad = jnp.full((NK_A - 3 * TQ_A - N_META, TQ_A, A_Q_HEADS), NEG, jnp.float32)
    blocked = jnp.full((TQ_A, TQ_A, A_Q_HEADS), NEG, jnp.float32)
    first = jnp.concatenate([blocked, band[TQ_A:], meta_first, pad])
    middle = jnp.concatenate([band, meta_far, pad])
    last = jnp.concatenate([band[:2 * TQ_A], blocked, meta_far, pad])
    tab = jnp.stack([first, middle, last])
    tab = tab.reshape(3, NK_A, TQ_A, A_Q_HEADS // 2, 2)
    tab = jnp.transpose(tab, (0, 3, 1, 4, 2)).reshape(3, A_Q_HEADS // 2, NK_A, 2 * TQ_A) * LOG2E
    sink_rows = jnp.repeat(sink.astype(jnp.float32).reshape(A_Q_HEADS // 2, 1, 2), TQ_A, axis=2) * LOG2E
    return tab, sink_rows


def _bias_tables_b(rel_bias, s):
    bias_b = rel_bias[:, A_Q_HEADS:]
    near = []
    for d in (-1, 0, 1):
        m = jnp.arange(TQ_B + TK_B - 1, dtype=jnp.int32)
        vec = _bias_of_rel(bias_b, TK_B * d + TK_B - 1 - m)
        near.append(_toeplitz(vec, TK_B, TQ_B, TK_B - 1))
    far_l = jnp.broadcast_to(_bias_of_rel(bias_b, jnp.int32(-TK_B - 1)), (TK_B, TQ_B, B_HEADS))
    far_r = jnp.broadcast_to(_bias_of_rel(bias_b, jnp.int32(TK_B + 1)), (TK_B, TQ_B, B_HEADS))
    tabs = jnp.transpose(jnp.stack([far_l] + near + [far_r]), (3, 0, 1, 2)) * LOG2E
    m = jnp.arange(s + N_META - 1, dtype=jnp.int32)
    vec = _bias_of_rel(bias_b, -1 - m)
    meta = jnp.transpose(_toeplitz(vec, N_META, s, N_META - 1), (2, 0, 1)) * LOG2E
    return tabs, meta


def _prep_w_in(w_in):
    a_w = A_Q_HEADS * HEAD_DIM
    kv = A_KV_HEADS * HEAD_DIM
    bqk = B_HEADS * 2 * HEAD_DIM
    scale = HEAD_DIM ** -0.5
    qa = w_in[:, :a_w] * (scale * LOG2E)
    ka = w_in[:, a_w:a_w + kv]
    va = w_in[:, a_w + kv:a_w + 2 * kv]
    o = a_w + 2 * kv
    qb = w_in[:, o:o + bqk] * (scale * LOG2E)
    kb = w_in[:, o + bqk:o + 2 * bqk]
    vb = w_in[:, o + 2 * bqk:]

    def dup(w):
        return jnp.concatenate([w[:, g * HEAD_DIM:(g + 1) * HEAD_DIM] for g in range(A_KV_HEADS) for _ in range(2)], axis=1)

    w = jnp.concatenate([qa, dup(ka), qb, kb], axis=1).astype(jnp.bfloat16)
    return w, jnp.concatenate([vb, va], axis=1).T.astype(jnp.bfloat16)


def _trunk_front(x, prm):
    bsz, s, d = x.shape
    t = bsz * s
    x2d = x.reshape(t, d)
    proj, vt, vta = _ln_inproj(x2d, prm["ln_emb_g"], prm["ln_emb_b"], prm["w_in"], prm["w_vt"])
    proj3 = proj.reshape(bsz, s, PROJ_COLS)
    oa = _mixer_a(proj3, vta, prm["km_a"], prm["vtm_a"], prm["tab_a"], prm["sink_rows"])
    tabs_b, meta_b = _bias_tables_b(prm["rel_bias"], s)
    ob = _mixer_b(proj3, vt, prm["proj_meta"], prm["vt_meta"], tabs_b, meta_b, prm["lam"].reshape(1).astype(jnp.float32),
                  prm["subln_g"])
    h1, h1_slab = _outproj_ln1(x2d, oa.reshape(t, -1), ob.reshape(t, -1), prm["w_out"], prm["ln_emb_g"],
                               prm["ln_emb_b"], prm["ln1_g"], prm["ln1_b"])

    keys, wts, counts = _router(h1, prm["wr_hi"], prm["wr_lo"], prm["router_bias"])
    n_exp = counts.shape[0]
    n_asg = t * TOP_K
    assert n_asg <= (1 << KEY_SHIFT) and n_asg % LANES == 0
    order = jnp.sort(keys.reshape(n_asg)) & ((1 << KEY_SHIFT) - 1)
    tok = order >> 3
    dst = (order & (TOP_K - 1)) * t + tok
    sc_pad = SC_WORKERS * SC_GRP * SC_NBUF
    assert sc_pad >= TM_MOE and n_asg % sc_pad == 0
    xs = _sc_move_rows(h1_slab.reshape(t, SLAB, LANES), jnp.concatenate([tok, jnp.zeros((sc_pad,), jnp.int32)]),
                       n_asg + sc_pad, scatter=False)
    xs_slab = xs.reshape((n_asg + sc_pad) * SLAB, LANES)
    counts = counts[:, 0].astype(jnp.int32)
    tiles_e = (counts + TM_MOE - 1) // TM_MOE
    tend = jnp.cumsum(tiles_e)
    tstart = tend - tiles_e
    ustart = jnp.cumsum(counts) - counts
    nt = n_asg // TM_MOE + n_exp + 1
    tid = jnp.arange(nt, dtype=jnp.int32)
    tile_e = jnp.minimum(jnp.sum((tend[None, :] <= tid[:, None]).astype(jnp.int32), axis=1), n_exp - 1)
    onehot = (tile_e[:, None] == jnp.arange(n_exp, dtype=jnp.int32)[None, :]).astype(jnp.int32)
    in_e = (tid - jnp.sum(onehot * tstart[None, :], axis=1)) * TM_MOE
    active = tid < tend[-1]
    tile_valid = jnp.where(active, jnp.clip(jnp.sum(onehot * counts[None, :], axis=1) - in_e, 0, TM_MOE), 0)
    tile_u = jnp.where(active, jnp.sum(onehot * ustart[None, :], axis=1) + in_e, 0)

    tiles = (tile_e.astype(jnp.int32), tile_valid.astype(jnp.int32), tile_u.astype(jnp.int32),
             tend[-1:].astype(jnp.int32))
    return h1, wts.T, xs_slab, dst, tiles, x.shape


def _trunk_back(front, prm):
    h1, w_tok, xs_slab, dst, tiles, shape = front
    n_asg = dst.shape[0]
    ys_slab = _experts(xs_slab, prm["w_gate"], prm["w_up"], prm["w_down"], *tiles)
    y = _sc_move_rows(ys_slab.reshape(-1, SLAB, LANES), dst, n_asg, scatter=True)
    out = _combine(h1, w_tok, y.reshape(n_asg * SLAB, LANES), prm["ws_gu"], prm["ws_down"], prm["ln2_g"], prm["ln2_b"])
    return out.reshape(shape)


def kernel(x_prompt, x_sample, meta_tokens, ln_emb_g, ln_emb_b, rel_bias, w_in, attn_sink, lambda_q1, lambda_k1, lambda_q2, lambda_k2, subln_g, w_out, ln1_g, ln1_b, w_router, router_bias, w_gate, w_up, w_down, ws_gate, ws_up, ws_down, ln2_g, ln2_b):
    f32 = jnp.float32
    bf16 = jnp.bfloat16
    l = 0
    row = lambda v: v.reshape(1, -1).astype(f32)
    lam = (jnp.exp(jnp.sum(lambda_q1[l].astype(f32) * lambda_k1[l].astype(f32)))
           - jnp.exp(jnp.sum(lambda_q2[l].astype(f32) * lambda_k2[l].astype(f32))) + LAMBDA_INIT)
    wr_t = w_router[l].astype(f32).T
    wr_hi = wr_t.astype(bf16)
    prm = {
        "ln_emb_g": row(ln_emb_g), "ln_emb_b": row(ln_emb_b),
        "rel_bias": rel_bias,
        "lam": lam,
        "subln_g": subln_g[l].astype(f32).reshape(-1, 1),
        "w_out": w_out[l].astype(bf16),
        "ln1_g": row(ln1_g[l]), "ln1_b": row(ln1_b[l]),
        "wr_hi": wr_hi, "wr_lo": (wr_t - wr_hi.astype(f32)).astype(bf16),
        "router_bias": router_bias[l].astype(f32).reshape(-1, 1),
        "w_gate": w_gate[l], "w_up": w_up[l], "w_down": w_down[l],
        "ws_gu": jnp.concatenate([ws_gate[l], ws_up[l]], axis=-1).astype(bf16),
        "ws_down": ws_down[l].astype(bf16),
        "ln2_g": row(ln2_g[l]), "ln2_b": row(ln2_b[l]),
    }
    prm["w_in"], prm["w_vt"] = _prep_w_in(w_in[l])
    prm["proj_meta"], prm["vt_meta"], vta_meta = _ln_inproj(meta_tokens.astype(f32), prm["ln_emb_g"], prm["ln_emb_b"],
                                                           prm["w_in"], prm["w_vt"])
    meta_pad = NK_A - 3 * TQ_A - N_META
    prm["km_a"] = jnp.pad(prm["proj_meta"][:, KA_BLK * LANES:(KA_BLK + 2) * LANES], ((0, meta_pad), (0, 0)))
    prm["vtm_a"] = jnp.pad(vta_meta, ((0, 0), (0, meta_pad)))
    prm["tab_a"], prm["sink_rows"] = _bias_tables_a(rel_bias, attn_sink[l])
    front_p = _trunk_front(x_prompt, prm)
    front_s = _trunk_front(x_sample, prm)
    return (_trunk_back(front_p, prm), _trunk_back(front_s, prm))
```
